```python
import math
import jax, jax.numpy as jnp
from jax import lax
import numpy as np

D_MODEL = 1024
BATCH = 4
SEQ = 8192
DEPTH = 1
DEC_BATCH = 32
DEC_SEQ = 4
PAST_LEN = 16384
PAGE_SIZE = 128

HEAD_DIM = 64
N_ATT_HEADS = 16
N_KV_HEADS = 8
KV_GROUP = N_ATT_HEADS // N_KV_HEADS
ATT_WIDTH = N_ATT_HEADS * HEAD_DIM
KV_WIDTH = N_KV_HEADS * HEAD_DIM
DILATED_BRANCHES = ((128, 1), (512, 4), (2048, 16))
MAX_WINDOW = 2048
ATT_QBLOCK = 128
ROPE_THETA = 10000.0

SSD_HEADS = 16
SSD_HEAD_DIM = 64
SSD_WIDTH = SSD_HEADS * SSD_HEAD_DIM
SSD_GROUPS = 2
SSD_STATE = 128
SSD_CHUNK = 128
CONV_W = 4
CONV_DIM = SSD_WIDTH + 2 * SSD_GROUPS * SSD_STATE

MIX_WIDTH = ATT_WIDTH + SSD_WIDTH
SPLITS = (ATT_WIDTH,
          ATT_WIDTH + KV_WIDTH,
          ATT_WIDTH + 2 * KV_WIDTH,
          ATT_WIDTH + 2 * KV_WIDTH + SSD_WIDTH,
          ATT_WIDTH + 2 * KV_WIDTH + SSD_WIDTH + CONV_DIM)
IN_COLS = SPLITS[-1] + SSD_HEADS

N_EXPERTS = 256
TOP_K = 8
N_EXPERT_GROUPS = 8
TOPK_GROUPS = 4
EXPERT_FF = 256
SHARED_FF = 256
ROUTED_SCALE = 2.5
MOE_BLOCK = 128
MOE_BLOCK_SMALL = 8

DEEPNORM_ALPHA = (2.0 * DEPTH) ** 0.25
DEEPNORM_BETA = (8.0 * DEPTH) ** -0.25
NORM_EPS = 1e-5

kernel_name = 'hymba_ssd_dilated_moe_step'

F32 = jnp.float32


def layer_norm(x, g, b):
    xf = x.astype(F32)
    mu = jnp.mean(xf, axis=-1, keepdims=True)
    var = jnp.mean(jnp.square(xf - mu), axis=-1, keepdims=True)
    return ((xf - mu) * lax.rsqrt(var + NORM_EPS) * g + b).astype(x.dtype)


def rope(x, pos):
    half = HEAD_DIM // 2
    inv = ROPE_THETA ** (-jnp.arange(half, dtype=F32) / half)
    ang = pos[:, None] * inv[None, :]
    cos = jnp.cos(ang)[None, :, None, :]
    sin = jnp.sin(ang)[None, :, None, :]
    xf = x.astype(F32)
    x1, x2 = xf[..., :half], xf[..., half:]
    return jnp.concatenate([x1 * cos - x2 * sin, x2 * cos + x1 * sin], axis=-1).astype(x.dtype)


def dilated_attention(q, k_all, v_all, off):
    b, l = q.shape[0], q.shape[1]

    def block(qb, start):
        tq = qb.shape[1]
        qi = off + start + jnp.arange(tq, dtype=jnp.int32)
        outs, lses = [], []
        for window, dil in DILATED_BRANCHES:
            dist = dil * jnp.arange(window // dil + 1, dtype=jnp.int32)
            idx = qi[:, None] - dist[None, :]
            valid = idx >= 0
            idx = jnp.maximum(idx, 0)
            kg = jnp.take(k_all, idx, axis=1)
            vg = jnp.take(v_all, idx, axis=1)
            s = jnp.einsum('bqhgd,bqnhd->bqhgn', qb, kg, preferred_element_type=F32)
            s = jnp.where(valid[None, :, None, None, :], s, -jnp.inf)
            lse = jax.nn.logsumexp(s, axis=-1)
            p = jnp.exp(s - lse[..., None])
            outs.append(jnp.einsum('bqhgn,bqnhd->bqhgd', p, vg.astype(F32)))
            lses.append(lse)
        wts = jax.nn.softmax(jnp.stack(lses, axis=0), axis=0)
        o = jnp.sum(wts[..., None] * jnp.stack(outs, axis=0), axis=0)
        return o.astype(qb.dtype)

    if l > ATT_QBLOCK and l % ATT_QBLOCK == 0:
        nb = l // ATT_QBLOCK
        qb = jnp.moveaxis(q.reshape(b, nb, ATT_QBLOCK, N_KV_HEADS, KV_GROUP, HEAD_DIM), 1, 0)
        starts = jnp.arange(nb, dtype=jnp.int32) * ATT_QBLOCK
        o = lax.map(lambda a: block(a[0], a[1]), (qb, starts))
        return jnp.moveaxis(o, 0, 1).reshape(b, l, ATT_WIDTH)
    return block(q, 0).reshape(b, l, ATT_WIDTH)


def causal_conv(xbc, buf, w, bias):
    l = xbc.shape[1]
    xp = jnp.concatenate([buf.astype(xbc.dtype), xbc], axis=1)
    out = xp[:, 0:l] * w[0]
    for k in range(1, CONV_W):
        out = out + xp[:, k:k + l] * w[k]
    return jax.nn.silu(out + bias), xp[:, -(CONV_W - 1):]


def ssd_scan(x, dt, a, bm, cm, h0):
    b, l, h, p = x.shape
    g, n = bm.shape[2], bm.shape[3]
    r = h // g
    q = SSD_CHUNK if l % SSD_CHUNK == 0 else l
    c = l // q
    xdt = (x.astype(F32) * dt[..., None]).reshape(b, c, q, g, r, p)
    a_cs = jnp.cumsum((dt * a).reshape(b, c, q, g, r), axis=2)
    bc = bm.astype(F32).reshape(b, c, q, g, n)
    cc = cm.astype(F32).reshape(b, c, q, g, n)
    causal = jnp.tril(jnp.ones((q, q), dtype=bool))
    seg = a_cs[:, :, :, None] - a_cs[:, :, None, :]
    decay = jnp.exp(jnp.where(causal[None, None, :, :, None, None], seg, -jnp.inf))
    cb = jnp.einsum('bctgn,bcsgn->bctsg', cc, bc)
    y_diag = jnp.einsum('bctsg,bctsgr,bcsgrp->bctgrp', cb, decay, xdt)
    to_end = jnp.exp(a_cs[:, :, -1:] - a_cs)
    states = jnp.einsum('bcsgn,bcsgr,bcsgrp->bcgrpn', bc, to_end, xdt)
    chunk_decay = jnp.exp(a_cs[:, :, -1])

    def step(hc, inp):
        dec, st = inp
        return dec[..., None, None] * hc + st, hc

    h_last, h_in = lax.scan(step, h0.astype(F32).reshape(b, g, r, p, n),
                            (jnp.moveaxis(chunk_decay, 1, 0), jnp.moveaxis(states, 1, 0)))
    h_in = jnp.moveaxis(h_in, 0, 1)
    y_off = jnp.einsum('bctgn,bcgrpn,bctgr->bctgrp', cc, h_in, jnp.exp(a_cs))
    y = (y_diag + y_off).reshape(b, l, h, p)
    return y, h_last.reshape(b, h, p, n)


def gated_group_rmsnorm(y, z, w):
    b, l, _ = y.shape
    hg = (y.astype(F32) * jax.nn.silu(z.astype(F32))).reshape(b, l, SSD_GROUPS, SSD_WIDTH // SSD_GROUPS)
    hg = hg * lax.rsqrt(jnp.mean(jnp.square(hg), axis=-1, keepdims=True) + NORM_EPS)
    return (hg.reshape(b, l, SSD_WIDTH) * w).astype(z.dtype)


def token_mixers(x, pos, k_past, v_past, ssm0, conv0, w_in, conv_w, conv_b,
                 dt_bias, a_log, d_skip, ssd_norm_w, w_out):
    b, l, _ = x.shape
    proj = jnp.einsum('bld,dc->blc', x, w_in)
    q, k, v, z, xbc, dt_raw = jnp.split(proj, list(SPLITS), axis=-1)
    q = rope(q.reshape(b, l, N_ATT_HEADS, HEAD_DIM), pos) * (HEAD_DIM ** -0.5)
    k = rope(k.reshape(b, l, N_KV_HEADS, HEAD_DIM), pos)
    v = v.reshape(b, l, N_KV_HEADS, HEAD_DIM)
    if k_past is None:
        k_all, v_all, off = k, v, 0
    else:
        k_all = jnp.concatenate([k_past.astype(k.dtype), k], axis=1)
        v_all = jnp.concatenate([v_past.astype(v.dtype), v], axis=1)
        off = k_past.shape[1]
    att = dilated_attention(q.reshape(b, l, N_KV_HEADS, KV_GROUP, HEAD_DIM), k_all, v_all, off)
    xbc_c, conv_new = causal_conv(xbc, conv0, conv_w, conv_b)
    xs = xbc_c[..., :SSD_WIDTH].reshape(b, l, SSD_HEADS, SSD_HEAD_DIM)
    bm = xbc_c[..., SSD_WIDTH:SSD_WIDTH + SSD_GROUPS * SSD_STATE].reshape(b, l, SSD_GROUPS, SSD_STATE)
    cm = xbc_c[..., SSD_WIDTH + SSD_GROUPS * SSD_STATE:].reshape(b, l, SSD_GROUPS, SSD_STATE)
    dt = jax.nn.softplus(dt_raw.astype(F32) + dt_bias.astype(F32))
    a = -jnp.exp(a_log.astype(F32))
    y, ssm_new = ssd_scan(xs, dt, a, bm, cm, ssm0)
    y = y + d_skip.astype(F32)[None, None, :, None] * xs.astype(F32)
    ssd_out = gated_group_rmsnorm(y.reshape(b, l, SSD_WIDTH), z, ssd_norm_w)
    mixed = jnp.concatenate([att, ssd_out], axis=-1)
    return jnp.einsum('blc,cd->bld', mixed, w_out), k, v, ssm_new, conv_new


def swiglu(x, wg, wu, wd):
    return jnp.dot(jax.nn.silu(jnp.dot(x, wg)) * jnp.dot(x, wu), wd)


def moe_ffn(x2d, w_router, router_bias, w_exp_gate, w_exp_up, w_exp_down, w_sh_gate, w_sh_up, w_sh_down):
    t, d = x2d.shape
    s = jax.nn.sigmoid(jnp.dot(x2d.astype(F32), w_router.astype(F32)))
    sel = s + router_bias.astype(F32)
    per_group = N_EXPERTS // N_EXPERT_GROUPS
    grp_score = jnp.sum(lax.top_k(sel.reshape(t, N_EXPERT_GROUPS, per_group), 2)[0], axis=-1)
    _, top_g = lax.top_k(grp_score, TOPK_GROUPS)
    gmask = jnp.any(top_g[..., None] == jnp.arange(N_EXPERT_GROUPS)[None, None, :], axis=1)
    emask = jnp.repeat(gmask, per_group, axis=1)
    _, top_e = lax.top_k(jnp.where(emask, sel, -jnp.inf), TOP_K)
    gsel = jnp.take_along_axis(s, top_e, axis=1)
    gates = gsel / jnp.sum(gsel, axis=-1, keepdims=True) * ROUTED_SCALE
    m = t * TOP_K
    blk = MOE_BLOCK if m >= N_EXPERTS * MOE_BLOCK else MOE_BLOCK_SMALL
    e_flat = top_e.reshape(m)
    tok = jnp.arange(m, dtype=jnp.int32) // TOP_K
    g_flat = gates.reshape(m)
    order = jnp.argsort(e_flat)
    e_s, tok_s, g_s = e_flat[order], tok[order], g_flat[order]
    counts = jnp.bincount(e_flat, length=N_EXPERTS)
    starts = jnp.cumsum(counts) - counts
    pcounts = (counts + blk - 1) // blk * blk
    pends = jnp.cumsum(pcounts)
    pstarts = pends - pcounts
    dest = pstarts[e_s] + (jnp.arange(m, dtype=jnp.int32) - starts[e_s])
    n_rows = (m + N_EXPERTS * (blk - 1) + blk - 1) // blk * blk
    n_blocks = n_rows // blk
    tok_pad = jnp.full((n_rows,), t, dtype=jnp.int32).at[dest].set(tok_s)
    g_pad = jnp.zeros((n_rows,), F32).at[dest].set(g_s)
    blk_expert = jnp.minimum(jnp.searchsorted(pends, jnp.arange(n_blocks, dtype=jnp.int32) * blk, side='right'),
                             N_EXPERTS - 1)
    x_ext = jnp.concatenate([x2d, jnp.zeros((1, d), x2d.dtype)], axis=0)

    def block_fn(args):
        tb, gb, e = args
        xb = x_ext[tb]
        yb = swiglu(xb, w_exp_gate[e], w_exp_up[e], w_exp_down[e])
        return yb * gb[:, None].astype(yb.dtype)

    y = lax.map(block_fn, (tok_pad.reshape(n_blocks, blk), g_pad.reshape(n_blocks, blk), blk_expert))
    routed = jax.ops.segment_sum(y.reshape(n_rows, d), tok_pad, num_segments=t + 1)[:t]
    return routed + swiglu(x2d, w_sh_gate, w_sh_up, w_sh_down)


def layer(x, pos, k_past, v_past, ssm0, conv0, w_in, conv_w, conv_b, dt_bias, a_log, d_skip, ssd_norm_w,
          w_out, ln1_g, ln1_b, w_router, router_bias, w_exp_gate, w_exp_up, w_exp_down,
          w_sh_gate, w_sh_up, w_sh_down, ln2_g, ln2_b):
    mix, k, v, ssm_new, conv_new = token_mixers(x, pos, k_past, v_past, ssm0, conv0, w_in, conv_w, conv_b,
                                                dt_bias, a_log, d_skip, ssd_norm_w, w_out)
    h = layer_norm(DEEPNORM_ALPHA * x + mix, ln1_g, ln1_b)
    b, l, d = h.shape
    f = moe_ffn(h.reshape(b * l, d), w_router, router_bias, w_exp_gate, w_exp_up, w_exp_down,
                w_sh_gate, w_sh_up, w_sh_down).reshape(b, l, d)
    out = layer_norm(DEEPNORM_ALPHA * h + f.astype(h.dtype), ln2_g, ln2_b)
    return out, k, v, ssm_new, conv_new


def setup_inputs(seed: int = 0) -> dict:
    key = jax.random.key(seed)
    ks = jax.random.split(key, 32)
    win_buf = min(MAX_WINDOW, PAST_LEN)

    def nrm(k, shape, scale):
        return jax.random.normal(k, shape, F32) * scale

    dt0 = jnp.exp(jax.random.uniform(ks[10], (DEPTH, SSD_HEADS), F32, math.log(1e-3), math.log(1e-1)))
    return {
        'x_prompt': nrm(ks[0], (BATCH, SEQ, D_MODEL), 1.0),
        'x_sample': nrm(ks[1], (DEC_BATCH, DEC_SEQ, D_MODEL), 1.0),
        'cache_k': nrm(ks[2], (DEPTH, DEC_BATCH, win_buf, N_KV_HEADS, HEAD_DIM), 1.0),
        'cache_v': nrm(ks[3], (DEPTH, DEC_BATCH, win_buf, N_KV_HEADS, HEAD_DIM), 1.0),
        'state_ssm': nrm(ks[4], (DEPTH, DEC_BATCH, SSD_HEADS, SSD_HEAD_DIM, SSD_STATE), 0.5),
        'state_conv': nrm(ks[5], (DEPTH, DEC_BATCH, CONV_W - 1, CONV_DIM), 1.0),
        'w_in': nrm(ks[6], (DEPTH, D_MODEL, IN_COLS), D_MODEL ** -0.5),
        'conv_w': nrm(ks[7], (DEPTH, CONV_W, CONV_DIM), CONV_W ** -0.5),
        'conv_b': nrm(ks[8], (DEPTH, CONV_DIM), 0.01),
        'dt_bias': dt0 + jnp.log(-jnp.expm1(-dt0)),
        'a_log': jnp.log(jax.random.uniform(ks[11], (DEPTH, SSD_HEADS), F32, 1.0, 16.0)),
        'd_skip': 1.0 + nrm(ks[12], (DEPTH, SSD_HEADS), 0.1),
        'ssd_norm_w': 1.0 + nrm(ks[13], (DEPTH, SSD_WIDTH), 0.1),
        'w_out': nrm(ks[14], (DEPTH, MIX_WIDTH, D_MODEL), MIX_WIDTH ** -0.5 * DEEPNORM_BETA),
        'ln1_g': 1.0 + nrm(ks[15], (DEPTH, D_MODEL), 0.1),
        'ln1_b': nrm(ks[16], (DEPTH, D_MODEL), 0.01),
        'w_router': nrm(ks[17], (DEPTH, D_MODEL, N_EXPERTS), D_MODEL ** -0.5),
        'router_bias': nrm(ks[18], (DEPTH, N_EXPERTS), 0.01),
        'w_exp_gate': nrm(ks[19], (DEPTH, N_EXPERTS, D_MODEL, EXPERT_FF), D_MODEL ** -0.5),
        'w_exp_up': nrm(ks[20], (DEPTH, N_EXPERTS, D_MODEL, EXPERT_FF), D_MODEL ** -0.5),
        'w_exp_down': nrm(ks[21], (DEPTH, N_EXPERTS, EXPERT_FF, D_MODEL), EXPERT_FF ** -0.5 * DEEPNORM_BETA),
        'w_sh_gate': nrm(ks[22], (DEPTH, D_MODEL, SHARED_FF), D_MODEL ** -0.5),
        'w_sh_up': nrm(ks[23], (DEPTH, D_MODEL, SHARED_FF), D_MODEL ** -0.5),
        'w_sh_down': nrm(ks[24], (DEPTH, SHARED_FF, D_MODEL), SHARED_FF ** -0.5 * DEEPNORM_BETA),
        'ln2_g': 1.0 + nrm(ks[25], (DEPTH, D_MODEL), 0.1),
        'ln2_b': nrm(ks[26], (DEPTH, D_MODEL), 0.01),
    }


def reference(x_prompt, x_sample, cache_k, cache_v, state_ssm, state_conv, w_in, conv_w, conv_b,
              dt_bias, a_log, d_skip, ssd_norm_w, w_out, ln1_g, ln1_b, w_router, router_bias,
              w_exp_gate, w_exp_up, w_exp_down, w_sh_gate, w_sh_up, w_sh_down, ln2_g, ln2_b):
    bp, lp, _ = x_prompt.shape
    ls = x_sample.shape[1]
    pos_p = jnp.arange(lp, dtype=F32)
    pos_s = (PAST_LEN + jnp.arange(ls, dtype=jnp.int32)).astype(F32)
    keep = min(MAX_WINDOW, lp)
    yp, ys = x_prompt, x_sample
    kp_l, vp_l, sp_l, cp_l, ks_l, vs_l, ss_l, cs_l = [], [], [], [], [], [], [], []
    for i in range(DEPTH):
        lw = (w_in[i], conv_w[i], conv_b[i], dt_bias[i], a_log[i], d_skip[i], ssd_norm_w[i], w_out[i],
              ln1_g[i], ln1_b[i], w_router[i], router_bias[i], w_exp_gate[i], w_exp_up[i], w_exp_down[i],
              w_sh_gate[i], w_sh_up[i], w_sh_down[i], ln2_g[i], ln2_b[i])
        ssm0 = jnp.zeros((bp, SSD_HEADS, SSD_HEAD_DIM, SSD_STATE), F32)
        conv0 = jnp.zeros((bp, CONV_W - 1, CONV_DIM), x_prompt.dtype)
        yp, kp, vp, sp, cp = layer(yp, pos_p, None, None, ssm0, conv0, *lw)
        ys, kn, vn, sn, cn = layer(ys, pos_s, cache_k[i], cache_v[i], state_ssm[i], state_conv[i], *lw)
        kp_l.append(kp[:, lp - keep:])
        vp_l.append(vp[:, lp - keep:])
        sp_l.append(sp)
        cp_l.append(cp)
        ks_l.append(kn)
        vs_l.append(vn)
        ss_l.append(sn)
        cs_l.append(cn)
    return (yp, ys, jnp.stack(kp_l), jnp.stack(vp_l), jnp.stack(sp_l), jnp.stack(cp_l),
            jnp.stack(ks_l), jnp.stack(vs_l), jnp.stack(ss_l), jnp.stack(cs_l))
```

```python
import functools
import math

import jax
import jax.numpy as jnp
import numpy as np
from jax import lax
from jax.experimental import pallas as pl
from jax.experimental.pallas import tpu as pltpu

F32 = jnp.float32
BF16 = jnp.bfloat16
I32 = jnp.int32

D_MODEL = 1024
PAST_LEN = 16384
HEAD_DIM = 64
N_ATT_HEADS = 16
N_KV_HEADS = 8
ATT_WIDTH = N_ATT_HEADS * HEAD_DIM
KV_WIDTH = N_KV_HEADS * HEAD_DIM
DILATED_BRANCHES = ((128, 1), (512, 4), (2048, 16))
MAX_WINDOW = 2048
ROPE_THETA = 10000.0
SSD_HEADS = 16
SSD_HEAD_DIM = 64
SSD_WIDTH = SSD_HEADS * SSD_HEAD_DIM
SSD_GROUPS = 2
SSD_STATE = 128
SSD_CHUNK = 128
CONV_W = 4
CONV_DIM = SSD_WIDTH + 2 * SSD_GROUPS * SSD_STATE
MIX_WIDTH = ATT_WIDTH + SSD_WIDTH
IN_COLS = ATT_WIDTH + 2 * KV_WIDTH + SSD_WIDTH + CONV_DIM + SSD_HEADS
N_EXPERTS = 256
TOP_K = 8
N_EXPERT_GROUPS = 8
TOPK_GROUPS = 4
EXPERT_FF = 256
SHARED_FF = 256
ROUTED_SCALE = 2.5
DEPTH = 1
DEEPNORM_ALPHA = (2.0 * DEPTH) ** 0.25
NORM_EPS = 1e-5

LANES = 128
SUBLANES = 8
VMEM_LIMIT = 56 * 1024 * 1024

IN_COLS_PAD = ATT_WIDTH + 2 * KV_WIDTH + SSD_WIDTH + CONV_DIM + LANES
KEY_TILE = 128
N_KEY_TILES = MAX_WINDOW // KEY_TILE + 1
EXPERT_BLOCK = 256
ROUTER_TILE = 128
DISPATCH_TILE = 256
COMBINE_TILE = 128
NEG_BIG = -1e30
HIGHEST = lax.Precision.HIGHEST


def _cparams(sem):
    return pltpu.CompilerParams(dimension_semantics=sem, vmem_limit_bytes=VMEM_LIMIT)


def _silu(x):
    return x * (1.0 / (1.0 + jnp.exp(-x)))


def _inproj_kernel(x_ref, w_ref, cq_ref, sq_ref, ck_ref, sk_ref,
                   q_ref, kf_ref, vf_ref, kb_ref, vb_ref, z_ref, xbc_ref, dt_ref):
    tm = x_ref.shape[0]
    xb = x_ref[...].astype(BF16)
    lane = lax.broadcasted_iota(I32, (tm, LANES), 1)
    first_half = (lane % HEAD_DIM) < (HEAD_DIM // 2)

    def rope(a, c, s):
        partner = jnp.where(first_half, pltpu.roll(a, LANES - HEAD_DIM // 2, 1), pltpu.roll(a, HEAD_DIM // 2, 1))
        return a * c + partner * s

    c0 = 0
    acc = jnp.dot(xb, w_ref[:, c0:c0 + ATT_WIDTH], preferred_element_type=F32)
    cq, sq = cq_ref[...], sq_ref[...]
    for j in range(ATT_WIDTH // LANES):
        q_ref[:, j * LANES:(j + 1) * LANES] = rope(acc[:, j * LANES:(j + 1) * LANES], cq, sq).astype(BF16)
    c0 += ATT_WIDTH
    acc = jnp.dot(xb, w_ref[:, c0:c0 + KV_WIDTH], preferred_element_type=F32)
    ck, sk = ck_ref[...], sk_ref[...]
    for j in range(KV_WIDTH // LANES):
        r = rope(acc[:, j * LANES:(j + 1) * LANES], ck, sk)
        kf_ref[:, j * LANES:(j + 1) * LANES] = r
        kb_ref[:, j * LANES:(j + 1) * LANES] = r.astype(BF16)
    c0 += KV_WIDTH
    acc = jnp.dot(xb, w_ref[:, c0:c0 + KV_WIDTH], preferred_element_type=F32)
    vf_ref[...] = acc
    vb_ref[...] = acc.astype(BF16)
    c0 += KV_WIDTH
    z_ref[...] = jnp.dot(xb, w_ref[:, c0:c0 + SSD_WIDTH], preferred_element_type=F32)
    c0 += SSD_WIDTH
    xbc_ref[...] = jnp.dot(xb, w_ref[:, c0:c0 + CONV_DIM], preferred_element_type=F32)
    c0 += CONV_DIM
    dt_ref[...] = jnp.dot(xb, w_ref[:, c0:c0 + LANES], preferred_element_type=F32)


def _rope_tables(pos):
    half = HEAD_DIM // 2
    inv = ROPE_THETA ** (-jnp.arange(half, dtype=F32) / half)
    ang = pos[:, None] * inv[None, :]
    cos, sin = jnp.cos(ang), jnp.sin(ang)
    c = jnp.concatenate([cos, cos, cos, cos], axis=1)
    s = jnp.concatenate([-sin, sin, -sin, sin], axis=1)
    return c, s


def _inproj(x2d, w_bf, cos_t, sin_t, tm, tiles_per_seq):
    t = x2d.shape[0]
    scale = HEAD_DIM ** -0.5
    tab = pl.BlockSpec((tm, LANES), lambda i: (i % tiles_per_seq, 0))
    row = lambda w: pl.BlockSpec((tm, w), lambda i: (i, 0))
    outs = (
        jax.ShapeDtypeStruct((t, ATT_WIDTH), BF16),
        jax.ShapeDtypeStruct((t, KV_WIDTH), F32),
        jax.ShapeDtypeStruct((t, KV_WIDTH), F32),
        jax.ShapeDtypeStruct((t, KV_WIDTH), BF16),
        jax.ShapeDtypeStruct((t, KV_WIDTH), BF16),
        jax.ShapeDtypeStruct((t, SSD_WIDTH), F32),
        jax.ShapeDtypeStruct((t, CONV_DIM), F32),
        jax.ShapeDtypeStruct((t, LANES), F32),
    )
    return pl.pallas_call(
        _inproj_kernel,
        grid=(t // tm,),
        in_specs=[row(D_MODEL), pl.BlockSpec((D_MODEL, IN_COLS_PAD), lambda i: (0, 0)), tab, tab, tab, tab],
        out_specs=(row(ATT_WIDTH), row(KV_WIDTH), row(KV_WIDTH), row(KV_WIDTH), row(KV_WIDTH),
                   row(SSD_WIDTH), row(CONV_DIM), row(LANES)),
        out_shape=outs,
        compiler_params=_cparams(("parallel",)),
        name="inproj",
    )(x2d, w_bf, cos_t * scale, sin_t * scale, cos_t, sin_t)


def _key_weight_table():
    t = np.arange(N_KEY_TILES)[:, None, None]
    r = np.arange(KEY_TILE)[None, :, None]
    c = np.arange(KEY_TILE)[None, None, :]
    d = KEY_TILE * t + r - c
    w = np.zeros(d.shape, np.float32)
    for window, dil in DILATED_BRANCHES:
        w += ((d >= 0) & (d <= window) & (d % dil == 0)).astype(np.float32)
    return jnp.asarray(w)


def _attn_kernel(q_ref, k_ref, v_ref, wt_ref, o_ref, *, off_tiles):
    tq = q_ref.shape[1]
    qi = pl.program_id(2)
    lane = lax.broadcasted_iota(I32, (tq, LANES), 1)
    lo = lane < HEAD_DIM
    n_tiles = jnp.minimum(qi * (tq // KEY_TILE) + off_tiles, N_KEY_TILES - 1) + 1
    for half in (0, 1):
        q = q_ref[0, :, half * LANES:(half + 1) * LANES].astype(F32)
        qs = pltpu.roll(q, HEAD_DIM, 1)
        valid = lo if half == 0 else jnp.logical_not(lo)
        qa_src, qb_src = (q, qs) if half == 0 else (qs, q)
        qq = jnp.concatenate([jnp.where(valid, qa_src, 0.0), jnp.where(valid, qb_src, 0.0)], axis=0).astype(BF16)

        def body(d, carry):
            m, l, acc = carry
            kt = qi * (tq // KEY_TILE) + off_tiles - d
            start = pl.multiple_of(kt * KEY_TILE, KEY_TILE)
            kt_tile = k_ref[0, pl.ds(start, KEY_TILE), :]
            s = lax.dot_general(qq, kt_tile, (((1,), (1,)), ((), ())), preferred_element_type=F32)
            wt = wt_ref[d, 0:tq, :]
            wt2 = jnp.concatenate([wt, wt], axis=0)
            s = jnp.where(wt2 > 0.0, s, NEG_BIG)
            m_new = jnp.maximum(m, jnp.max(s, axis=1, keepdims=True))
            alpha = jnp.exp(m - m_new)
            p = jnp.exp(s - m_new) * wt2
            l = alpha * l + jnp.sum(p, axis=1, keepdims=True)
            vt = v_ref[0, pl.ds(start, KEY_TILE), :]
            acc = alpha * acc + jnp.dot(p.astype(BF16), vt, preferred_element_type=F32)
            return m_new, l, acc

        m0 = jnp.full((2 * tq, 1), NEG_BIG, F32)
        l0 = jnp.zeros((2 * tq, 1), F32)
        a0 = jnp.zeros((2 * tq, LANES), F32)
        _, l, acc = lax.fori_loop(0, n_tiles, body, (m0, l0, a0))
        o = acc / l
        oa, ob = o[:tq], o[tq:]
        if half == 0:
            out = jnp.where(lo, oa, pltpu.roll(ob, HEAD_DIM, 1))
        else:
            out = jnp.where(lo, pltpu.roll(oa, HEAD_DIM, 1), ob)
        o_ref[0, :, half * LANES:(half + 1) * LANES] = out.astype(BF16)


def _attention(q, k, v, wt, tq, off_tiles):
    b, lq, _ = q.shape
    lk = k.shape[1]
    pair = 2 * LANES
    return pl.pallas_call(
        functools.partial(_attn_kernel, off_tiles=off_tiles),
        grid=(b, ATT_WIDTH // pair, lq // tq),
        in_specs=[pl.BlockSpec((1, tq, pair), lambda bi, pi, qi: (bi, qi, pi)),
                  pl.BlockSpec((1, lk, LANES), lambda bi, pi, qi: (bi, 0, pi)),
                  pl.BlockSpec((1, lk, LANES), lambda bi, pi, qi: (bi, 0, pi)),
                  pl.BlockSpec((N_KEY_TILES, KEY_TILE, KEY_TILE), lambda bi, pi, qi: (0, 0, 0))],
        out_specs=pl.BlockSpec((1, tq, pair), lambda bi, pi, qi: (bi, qi, pi)),
        out_shape=jax.ShapeDtypeStruct((b, lq, ATT_WIDTH), BF16),
        compiler_params=_cparams(("parallel", "parallel", "arbitrary")),
        name="dilated_attn",
    )(q, k, v, wt)


def _ssd_kernel(xbc_ref, dt_ref, z_ref, ssm0_ref, conv0_ref, cw_ref, cb_ref, dtb_ref, alog_ref, dsk_ref,
                nw_ref, ex_ref, y_ref, ssm_ref, conv_ref, xpad, dtpad, s_scr, *, n_valid):
    q = SSD_CHUNK
    lb = xbc_ref.shape[1]
    ci = pl.program_id(1)
    nc = pl.num_programs(1)
    gw = SSD_WIDTH // SSD_GROUPS
    hpg = SSD_HEADS // SSD_GROUPS

    @pl.when(ci == 0)
    def _():
        xpad[0:SUBLANES, :] = jnp.zeros((SUBLANES, CONV_DIM), F32)
        xpad[SUBLANES - (CONV_W - 1):SUBLANES, :] = conv0_ref[0]
        for g in range(SSD_GROUPS):
            s_scr[g] = jnp.transpose(ssm0_ref[0, g * hpg:(g + 1) * hpg].reshape(gw, SSD_STATE))

    xpad[SUBLANES:SUBLANES + lb, :] = xbc_ref[0]
    dtpad[0:lb, :] = dt_ref[0]
    if lb < q:
        xpad[SUBLANES + lb:SUBLANES + q, :] = jnp.zeros((q - lb, CONV_DIM), F32)
        dtpad[lb:q, :] = jnp.zeros((q - lb, LANES), F32)

    conv = cb_ref[...]
    for k in range(CONV_W):
        sh = CONV_W - 1 - k
        conv = conv + xpad[SUBLANES - sh:SUBLANES - sh + q, :] * cw_ref[k:k + 1, :]
    act = _silu(conv)

    @pl.when(ci == nc - 1)
    def _():
        conv_ref[0] = xpad[SUBLANES + n_valid - (CONV_W - 1):SUBLANES + n_valid, :]

    xpad[0:SUBLANES, :] = xpad[q:q + SUBLANES, :]

    xs = act[:, :SSD_WIDTH]
    row = lax.broadcasted_iota(I32, (q, LANES), 0)
    dtr = dtpad[...] + dtb_ref[...]
    dt = jnp.maximum(dtr, 0.0) + jnp.log(1.0 + jnp.exp(-jnp.abs(dtr)))
    dt = jnp.where(row < n_valid, dt, 0.0)
    a = -jnp.exp(alog_ref[...])
    da = dt * a
    r2 = lax.broadcasted_iota(I32, (q, q), 0)
    c2 = lax.broadcasted_iota(I32, (q, q), 1)
    causal = r2 >= c2
    a_cs = jnp.dot(causal.astype(F32), da, precision=HIGHEST, preferred_element_type=F32)
    a_cst = jnp.transpose(a_cs)
    a_last = a_cs[q - 1:q, :]
    ex = ex_ref[...]
    expand = lambda t: jnp.dot(t, ex, precision=HIGHEST, preferred_element_type=F32)
    dt_x = expand(dt)
    ea_x = expand(jnp.exp(a_cs))
    te_x = expand(jnp.exp(a_last - a_cs))
    cd_x = expand(jnp.broadcast_to(jnp.exp(a_last), (SUBLANES, LANES)))[0:1, :]
    xdt = xs * dt_x
    xdt_b = xdt.astype(BF16)
    xw_b = (xdt * te_x).astype(BF16)
    lo = lax.broadcasted_iota(I32, (q, LANES), 1) < SSD_HEAD_DIM

    ys = []
    for g in range(SSD_GROUPS):
        bm = act[:, SSD_WIDTH + g * SSD_STATE:SSD_WIDTH + (g + 1) * SSD_STATE]
        cm = act[:, SSD_WIDTH + (SSD_GROUPS + g) * SSD_STATE:SSD_WIDTH + (SSD_GROUPS + g + 1) * SSD_STATE]
        bm_b, cm_b = bm.astype(BF16), cm.astype(BF16)
        cb = lax.dot_general(cm_b, bm_b, (((1,), (1,)), ((), ())), preferred_element_type=F32)
        s_old = s_scr[g]
        y_off = jnp.dot(cm_b, s_old.astype(BF16), preferred_element_type=F32)
        for jp in range(hpg // 2):
            pair = g * (hpg // 2) + jp
            yp = []
            for hh in (0, 1):
                h = 2 * pair + hh
                seg = a_cs[:, h:h + 1] - a_cst[h:h + 1, :]
                dec = jnp.exp(jnp.where(causal, seg, NEG_BIG))
                mm = (cb * dec).astype(BF16)
                yp.append(jnp.dot(mm, xdt_b[:, pair * LANES:(pair + 1) * LANES], preferred_element_type=F32))
            ys.append(jnp.where(lo, yp[0], yp[1]) + y_off[:, jp * LANES:(jp + 1) * LANES]
                      * ea_x[:, pair * LANES:(pair + 1) * LANES])
        bmt_b = jnp.transpose(bm).astype(BF16)
        s_new = s_old * cd_x[:, g * gw:(g + 1) * gw] + jnp.dot(bmt_b, xw_b[:, g * gw:(g + 1) * gw],
                                                              preferred_element_type=F32)
        s_scr[g] = s_new

    y = jnp.concatenate(ys, axis=1) + dsk_ref[...] * xs

    @pl.when(ci == nc - 1)
    def _():
        for g in range(SSD_GROUPS):
            ssm_ref[0, g * hpg:(g + 1) * hpg] = jnp.transpose(s_scr[g]).reshape(hpg, SSD_HEAD_DIM, SSD_STATE)

    hg = y[:lb] * _silu(z_ref[0])
    outs = []
    for g in range(SSD_GROUPS):
        part = hg[:, g * gw:(g + 1) * gw]
        ms = jnp.mean(part * part, axis=1, keepdims=True)
        outs.append(part * lax.rsqrt(ms + NORM_EPS))
    y_ref[0] = (jnp.concatenate(outs, axis=1) * nw_ref[...]).astype(BF16)


def _ssd(xbc, dt_raw, z, ssm0, conv0, prm, lb, n_valid):
    b, l, _ = xbc.shape
    nc = l // lb
    cw, cbias, dtb, alog, dsk, nw, ex = prm
    full = lambda a: pl.BlockSpec(a.shape, lambda bi, ci: (0,) * a.ndim)
    seq = lambda w: pl.BlockSpec((1, lb, w), lambda bi, ci: (bi, ci, 0))
    return pl.pallas_call(
        functools.partial(_ssd_kernel, n_valid=n_valid),
        grid=(b, nc),
        in_specs=[seq(CONV_DIM), seq(LANES), seq(SSD_WIDTH),
                  pl.BlockSpec((1, SSD_HEADS, SSD_HEAD_DIM, SSD_STATE), lambda bi, ci: (bi, 0, 0, 0)),
                  pl.BlockSpec((1, CONV_W - 1, CONV_DIM), lambda bi, ci: (bi, 0, 0)),
                  full(cw), full(cbias), full(dtb), full(alog), full(dsk), full(nw), full(ex)],
        out_specs=(seq(SSD_WIDTH),
                   pl.BlockSpec((1, SSD_HEADS, SSD_HEAD_DIM, SSD_STATE), lambda bi, ci: (bi, 0, 0, 0)),
                   pl.BlockSpec((1, CONV_W - 1, CONV_DIM), lambda bi, ci: (bi, 0, 0))),
        out_shape=(jax.ShapeDtypeStruct((b, l, SSD_WIDTH), BF16),
                   jax.ShapeDtypeStruct((b, SSD_HEADS, SSD_HEAD_DIM, SSD_STATE), F32),
                   jax.ShapeDtypeStruct((b, CONV_W - 1, CONV_DIM), F32)),
        scratch_shapes=[pltpu.VMEM((SSD_CHUNK + 2 * SUBLANES, CONV_DIM), F32),
                        pltpu.VMEM((SSD_CHUNK, LANES), F32),
                        pltpu.VMEM((SSD_GROUPS, SSD_STATE, SSD_WIDTH // SSD_GROUPS), F32)],
        compiler_params=_cparams(("parallel", "arbitrary")),
        name="conv_ssd",
    )(xbc, dt_raw, z, ssm0, conv0, cw, cbias, dtb, alog, dsk, nw, ex)


def _layer_norm(r, g, b):
    mu = jnp.mean(r, axis=1, keepdims=True)
    d = r - mu
    var = jnp.mean(d * d, axis=1, keepdims=True)
    return d * lax.rsqrt(var + NORM_EPS) * g + b


def _mix_kernel(att_ref, ssd_ref, x_ref, wo_ref, g_ref, b_ref, h_ref):
    mix = jnp.dot(att_ref[...], wo_ref[0:ATT_WIDTH, :], preferred_element_type=F32)
    mix = mix + jnp.dot(ssd_ref[...], wo_ref[ATT_WIDTH:MIX_WIDTH, :], preferred_element_type=F32)
    h_ref[...] = _layer_norm(DEEPNORM_ALPHA * x_ref[...] + mix, g_ref[...], b_ref[...])


def _mix(att, ssd, x2d, wo_bf, g, b, tm):
    t = x2d.shape[0]
    row = lambda w: pl.BlockSpec((tm, w), lambda i: (i, 0))
    vec = pl.BlockSpec((1, D_MODEL), lambda i: (0, 0))
    return pl.pallas_call(
        _mix_kernel,
        grid=(t // tm,),
        in_specs=[row(ATT_WIDTH), row(SSD_WIDTH), row(D_MODEL),
                  pl.BlockSpec((MIX_WIDTH, D_MODEL), lambda i: (0, 0)), vec, vec],
        out_specs=row(D_MODEL),
        out_shape=jax.ShapeDtypeStruct((t, D_MODEL), F32),
        compiler_params=_cparams(("parallel",)),
        name="outproj_ln1",
    )(att, ssd, x2d, wo_bf, g, b)


def _router_kernel(h_ref, wr_ref, bias_ref, tope_ref, rank_ref, gt_ref, cnt_ref, gscr):
    tm = h_ref.shape[0]
    i = pl.program_id(0)
    per_group = N_EXPERTS // N_EXPERT_GROUPS

    @pl.when(i == 0)
    def _():
        cnt_ref[...] = jnp.zeros_like(cnt_ref)

    logits = lax.dot_general(wr_ref[...], h_ref[...], (((1,), (1,)), ((), ())),
                             precision=HIGHEST, preferred_element_type=F32)
    s = 1.0 / (1.0 + jnp.exp(-logits))
    sel = s + bias_ref[...]
    neg_inf = -jnp.inf
    e_iota = lax.broadcasted_iota(I32, (N_EXPERTS, tm), 0).astype(F32)
    g_iota = lax.broadcasted_iota(I32, (per_group, tm), 0).astype(F32)

    gscore = []
    for g in range(N_EXPERT_GROUPS):
        blk = sel[g * per_group:(g + 1) * per_group]
        m1 = jnp.max(blk, axis=0, keepdims=True)
        a1 = jnp.min(jnp.where(blk == m1, g_iota, float(per_group)), axis=0, keepdims=True)
        m2 = jnp.max(jnp.where(g_iota == a1, neg_inf, blk), axis=0, keepdims=True)
        gscore.append(m1 + m2)
    blocks = []
    for g in range(N_EXPERT_GROUPS):
        beaten = jnp.zeros((1, tm), F32)
        for o in range(N_EXPERT_GROUPS):
            if o < g:
                beaten = beaten + jnp.where(gscore[o] >= gscore[g], 1.0, 0.0)
            elif o > g:
                beaten = beaten + jnp.where(gscore[o] > gscore[g], 1.0, 0.0)
        blocks.append(jnp.where(beaten < float(TOPK_GROUPS), sel[g * per_group:(g + 1) * per_group], neg_inf))
    cand = jnp.concatenate(blocks, axis=0)

    tops, gsel = [], []
    onehot = jnp.zeros((N_EXPERTS, tm), F32)
    for _ in range(TOP_K):
        mx = jnp.max(cand, axis=0, keepdims=True)
        ix = jnp.min(jnp.where(cand == mx, e_iota, float(N_EXPERTS)), axis=0, keepdims=True)
        hit = e_iota == ix
        tops.append(ix)
        gsel.append(jnp.sum(jnp.where(hit, s, 0.0), axis=0, keepdims=True))
        onehot = jnp.where(hit, 1.0, onehot)
        cand = jnp.where(hit, neg_inf, cand)
    den = gsel[0]
    for k in range(1, TOP_K):
        den = den + gsel[k]

    t_r = lax.broadcasted_iota(I32, (tm, tm), 0)
    t_c = lax.broadcasted_iota(I32, (tm, tm), 1)
    before = (t_r < t_c).astype(BF16)
    oh_b = onehot.astype(BF16)
    base = cnt_ref[...] + jnp.dot(oh_b, before, preferred_element_type=F32)
    cnt_ref[...] = cnt_ref[...] + jnp.dot(oh_b, jnp.ones((tm, LANES), BF16), preferred_element_type=F32)

    gscr[...] = jnp.zeros_like(gscr)
    for k in range(TOP_K):
        tope_ref[k:k + 1, :] = tops[k].astype(I32)
        rank_ref[k:k + 1, :] = jnp.sum(jnp.where(e_iota == tops[k], base, 0.0), axis=0, keepdims=True).astype(I32)
        gscr[k:k + 1, :] = gsel[k] / den * ROUTED_SCALE
    gt_ref[...] = jnp.transpose(gscr[...])


def _router(h_all, wr_t, bias_b):
    t = h_all.shape[0]
    tm = ROUTER_TILE
    return pl.pallas_call(
        _router_kernel,
        grid=(t // tm,),
        in_specs=[pl.BlockSpec((tm, D_MODEL), lambda i: (i, 0)),
                  pl.BlockSpec((N_EXPERTS, D_MODEL), lambda i: (0, 0)),
                  pl.BlockSpec((N_EXPERTS, tm), lambda i: (0, 0))],
        out_specs=(pl.BlockSpec((TOP_K, tm), lambda i: (0, i)),
                   pl.BlockSpec((TOP_K, tm), lambda i: (0, i)),
                   pl.BlockSpec((tm, LANES), lambda i: (i, 0)),
                   pl.BlockSpec((N_EXPERTS, LANES), lambda i: (0, 0))),
        out_shape=(jax.ShapeDtypeStruct((TOP_K, t), I32),
                   jax.ShapeDtypeStruct((TOP_K, t), I32),
                   jax.ShapeDtypeStruct((t, LANES), F32),
                   jax.ShapeDtypeStruct((N_EXPERTS, LANES), F32)),
        scratch_shapes=[pltpu.VMEM((tm, LANES), F32)],
        compiler_params=_cparams(("arbitrary",)),
        name="router",
    )(h_all, wr_t, bias_b)


_PAD_SIZES = tuple(2 ** p for p in range(int(math.log2(EXPERT_BLOCK)) - 1, -1, -1))


ROW_SLAB = D_MODEL // LANES


def _slab(ref, row, n=1):
    return ref.at[pl.ds(pl.multiple_of(row * ROW_SLAB, ROW_SLAB), n * ROW_SLAB)]


def _padfill_kernel(pstart_ref, pcnt_ref, tail_ref, xs_ref, zero_scr, sem):
    zero_scr[...] = jnp.zeros_like(zero_scr)
    half = EXPERT_BLOCK // 2
    tail_copy = lambda j: pltpu.make_async_copy(_slab(zero_scr, 0, half), _slab(xs_ref, tail_ref[0] + j * half, half),
                                                sem)

    def tail_start(j, c):
        tail_copy(j).start()
        return c

    def tail_wait(j, c):
        tail_copy(j).wait()
        return c

    lax.fori_loop(0, tail_ref[1], tail_start, 0)

    def copies(e):
        base = pstart_ref[e]
        cnt = pcnt_ref[e]
        out = []
        for sz in _PAD_SIZES:
            out.append(((cnt & sz) != 0, pltpu.make_async_copy(_slab(zero_scr, 0, sz), _slab(xs_ref, base, sz), sem)))
            base = base + (cnt & sz)
        return out

    def start(e, c):
        for pred, cp in copies(e):
            @pl.when(pred)
            def _():
                cp.start()
        return c

    def wait(e, c):
        for pred, cp in copies(e):
            @pl.when(pred)
            def _():
                cp.wait()
        return c

    lax.fori_loop(0, N_EXPERTS, start, 0)
    lax.fori_loop(0, N_EXPERTS, wait, 0)
    lax.fori_loop(0, tail_ref[1], tail_wait, 0)


def _padfill(pad_start, pad_cnt, tail, n_rows):
    return pl.pallas_call(
        _padfill_kernel,
        grid_spec=pltpu.PrefetchScalarGridSpec(
            num_scalar_prefetch=3, grid=(1,), in_specs=[],
            out_specs=pl.BlockSpec(memory_space=pl.ANY),
            scratch_shapes=[pltpu.VMEM((EXPERT_BLOCK // 2 * ROW_SLAB, LANES), F32), pltpu.SemaphoreType.DMA]),
        out_shape=jax.ShapeDtypeStruct((n_rows * ROW_SLAB, LANES), F32),
        compiler_params=_cparams(("arbitrary",)),
        name="dispatch_padfill",
    )(pad_start, pad_cnt, tail)


def _dispatch_kernel(dest_ref, h_ref, xs_in_ref, xs_ref, idx_smem, isem, sem):
    del xs_in_ref
    i = pl.program_id(0)
    tm = DISPATCH_TILE
    cp = pltpu.make_async_copy(dest_ref.at[i], idx_smem, isem)
    cp.start()
    cp.wait()

    def row_copy(t, k):
        return pltpu.make_async_copy(_slab(h_ref, i * tm + t), _slab(xs_ref, idx_smem[t * TOP_K + k]), sem)

    def start(t, c):
        for k in range(TOP_K):
            row_copy(t, k).start()
        return c

    def wait(t, c):
        for k in range(TOP_K):
            row_copy(t, k).wait()
        return c

    lax.fori_loop(0, tm, start, 0)
    lax.fori_loop(0, tm, wait, 0)


def _dispatch(dest_tiles, h_slab, xs):
    tm = DISPATCH_TILE
    return pl.pallas_call(
        _dispatch_kernel,
        grid=(dest_tiles.shape[0],),
        in_specs=[pl.BlockSpec(dest_tiles.shape, lambda i: (0, 0)),
                  pl.BlockSpec(memory_space=pl.ANY),
                  pl.BlockSpec(memory_space=pl.ANY)],
        out_specs=pl.BlockSpec(memory_space=pl.ANY),
        out_shape=jax.ShapeDtypeStruct(xs.shape, xs.dtype),
        scratch_shapes=[pltpu.SMEM((tm * TOP_K,), I32), pltpu.SemaphoreType.DMA, pltpu.SemaphoreType.DMA],
        input_output_aliases={2: 0},
        compiler_params=_cparams(("arbitrary",)),
        name="dispatch_rows",
    )(dest_tiles, h_slab, xs)


def _expert_kernel(bi_ref, be_ref, nu_ref, x_ref, wg_ref, wu_ref, wd_ref, y_ref, wg_b, wu_b, wd_b):
    del bi_ref
    i = pl.program_id(0)

    @pl.when(i < nu_ref[0])
    def _():
        prev = be_ref[jnp.maximum(i - 1, 0)]

        @pl.when(jnp.logical_or(i == 0, be_ref[i] != prev))
        def _():
            wg_b[...] = wg_ref[0].astype(BF16)
            wu_b[...] = wu_ref[0].astype(BF16)
            wd_b[...] = wd_ref[0].astype(BF16)

        xb = jnp.concatenate([x_ref[pl.ds(j, EXPERT_BLOCK, stride=ROW_SLAB), :] for j in range(ROW_SLAB)],
                             axis=1).astype(BF16)
        g = jnp.dot(xb, wg_b[...], preferred_element_type=F32)
        u = jnp.dot(xb, wu_b[...], preferred_element_type=F32)
        a = (_silu(g) * u).astype(BF16)
        y = jnp.dot(a, wd_b[...], preferred_element_type=F32)
        for j in range(ROW_SLAB):
            y_ref[pl.ds(j, EXPERT_BLOCK, stride=ROW_SLAB), :] = y[:, j * LANES:(j + 1) * LANES]

    @pl.when(i >= nu_ref[0])
    def _():
        y_ref[...] = jnp.zeros_like(y_ref)


def _experts(blk_idx, blk_exp, n_used, xs, wg, wu, wd):
    n_rows = xs.shape[0] // ROW_SLAB
    nb = n_rows // EXPERT_BLOCK
    rows = pl.BlockSpec((EXPERT_BLOCK * ROW_SLAB, LANES), lambda i, bi, be, nu: (bi[i], 0))
    out_rows = pl.BlockSpec((EXPERT_BLOCK * ROW_SLAB, LANES), lambda i, bi, be, nu: (i, 0))
    return pl.pallas_call(
        _expert_kernel,
        grid_spec=pltpu.PrefetchScalarGridSpec(
            num_scalar_prefetch=3, grid=(nb,),
            in_specs=[rows,
                      pl.BlockSpec((1, D_MODEL, EXPERT_FF), lambda i, bi, be, nu: (be[i], 0, 0)),
                      pl.BlockSpec((1, D_MODEL, EXPERT_FF), lambda i, bi, be, nu: (be[i], 0, 0)),
                      pl.BlockSpec((1, EXPERT_FF, D_MODEL), lambda i, bi, be, nu: (be[i], 0, 0))],
            out_specs=out_rows,
            scratch_shapes=[pltpu.VMEM((D_MODEL, EXPERT_FF), BF16), pltpu.VMEM((D_MODEL, EXPERT_FF), BF16),
                            pltpu.VMEM((EXPERT_FF, D_MODEL), BF16)]),
        out_shape=jax.ShapeDtypeStruct(xs.shape, F32),
        compiler_params=_cparams(("arbitrary",)),
        name="expert_ffn",
    )(blk_idx, blk_exp, n_used, xs, wg, wu, wd)


def _combine_kernel(dest_ref, h_ref, gt_ref, ys_ref, sg_ref, su_ref, sd_ref, g_ref, b_ref, o_ref,
                    idx_smem, gbuf, isem, sem):
    i = pl.program_id(0)
    tm = h_ref.shape[0]
    cp = pltpu.make_async_copy(dest_ref.at[i], idx_smem, isem)
    cp.start()
    cp.wait()

    def row_copy(t, k):
        return pltpu.make_async_copy(_slab(ys_ref, idx_smem[t * TOP_K + k]), _slab(gbuf.at[k], t), sem)

    def start(t, c):
        for k in range(TOP_K):
            row_copy(t, k).start()
        return c

    def wait(t, c):
        for k in range(TOP_K):
            row_copy(t, k).wait()
        return c

    lax.fori_loop(0, tm, start, 0)
    h = h_ref[...]
    hb = h.astype(BF16)
    g = jnp.dot(hb, sg_ref[...], preferred_element_type=F32)
    u = jnp.dot(hb, su_ref[...], preferred_element_type=F32)
    f = jnp.dot((_silu(g) * u).astype(BF16), sd_ref[...], preferred_element_type=F32)
    lax.fori_loop(0, tm, wait, 0)
    gt = gt_ref[...]
    cols = []
    for j in range(ROW_SLAB):
        fj = f[:, j * LANES:(j + 1) * LANES]
        for k in range(TOP_K):
            fj = fj + gbuf[k, pl.ds(j, tm, stride=ROW_SLAB), :] * gt[:, k:k + 1]
        cols.append(fj)
    f = jnp.concatenate(cols, axis=1)
    o_ref[...] = _layer_norm(DEEPNORM_ALPHA * h + f, g_ref[...], b_ref[...])


def _combine(dest_tiles, h_all, gates_t, ys, sg_bf, su_bf, sd_bf, g, b):
    t = h_all.shape[0]
    tm = COMBINE_TILE
    vec = pl.BlockSpec((1, D_MODEL), lambda i: (0, 0))
    return pl.pallas_call(
        _combine_kernel,
        grid=(t // tm,),
        in_specs=[pl.BlockSpec(dest_tiles.shape, lambda i: (0, 0)),
                  pl.BlockSpec((tm, D_MODEL), lambda i: (i, 0)),
                  pl.BlockSpec((tm, LANES), lambda i: (i, 0)),
                  pl.BlockSpec(memory_space=pl.ANY),
                  pl.BlockSpec((D_MODEL, SHARED_FF), lambda i: (0, 0)),
                  pl.BlockSpec((D_MODEL, SHARED_FF), lambda i: (0, 0)),
                  pl.BlockSpec((SHARED_FF, D_MODEL), lambda i: (0, 0)), vec, vec],
        out_specs=pl.BlockSpec((tm, D_MODEL), lambda i: (i, 0)),
        out_shape=jax.ShapeDtypeStruct((t, D_MODEL), F32),
        scratch_shapes=[pltpu.SMEM((tm * TOP_K,), I32), pltpu.VMEM((TOP_K, tm * ROW_SLAB, LANES), F32),
                        pltpu.SemaphoreType.DMA, pltpu.SemaphoreType.DMA],
        compiler_params=_cparams(("arbitrary",)),
        name="combine_ln2",
    )(dest_tiles, h_all, gates_t, ys, sg_bf, su_bf, sd_bf, g, b)


def _moe(h_all, wr, rbias, wg, wu, wd, sg, su, sd, g2, b2):
    t = h_all.shape[0]
    top_e, rank, gates_t, cnt = _router(h_all, jnp.transpose(wr),
                                        jnp.broadcast_to(rbias[:, None], (N_EXPERTS, ROUTER_TILE)))
    counts = cnt[:, 0].astype(I32)
    pcounts = (counts + EXPERT_BLOCK - 1) // EXPERT_BLOCK * EXPERT_BLOCK
    pends = jnp.cumsum(pcounts)
    pstarts = pends - pcounts
    n_rows = (t * TOP_K // EXPERT_BLOCK + N_EXPERTS - 1) * EXPERT_BLOCK
    nb = n_rows // EXPERT_BLOCK
    n_used = (pends[-1] // EXPERT_BLOCK).astype(I32)
    blk_idx = jnp.minimum(jnp.arange(nb, dtype=I32), n_used - 1)
    blk_exp = jnp.minimum(jnp.searchsorted(pends, blk_idx * EXPERT_BLOCK, side='right'), N_EXPERTS - 1).astype(I32)
    dest = pstarts[top_e] + rank
    dest_tm = jnp.transpose(dest)
    tail = jnp.stack([pends[-1], (n_rows - pends[-1]) // (EXPERT_BLOCK // 2)]).astype(I32)
    xs = _padfill((pstarts + counts).astype(I32), (pcounts - counts).astype(I32), tail, n_rows)
    xs = _dispatch(dest_tm.reshape(t // DISPATCH_TILE, DISPATCH_TILE * TOP_K),
                   h_all.reshape(t * ROW_SLAB, LANES), xs)
    ys = _experts(blk_idx, blk_exp, n_used.reshape(1), xs, wg, wu, wd)
    return _combine(dest_tm.reshape(t // COMBINE_TILE, COMBINE_TILE * TOP_K), h_all, gates_t, ys,
                    sg.astype(BF16), su.astype(BF16), sd.astype(BF16), g2[None, :], b2[None, :])


def _expand_matrix():
    h = np.arange(LANES)[:, None]
    c = np.arange(SSD_WIDTH)[None, :]
    return jnp.asarray((c // SSD_HEAD_DIM == h).astype(np.float32))


def kernel(x_prompt, x_sample, cache_k, cache_v, state_ssm, state_conv, w_in, conv_w, conv_b, dt_bias, a_log, d_skip, ssd_norm_w, w_out, ln1_g, ln1_b, w_router, router_bias, w_exp_gate, w_exp_up, w_exp_down, w_sh_gate, w_sh_up, w_sh_down, ln2_g, ln2_b):
    bp, lp, _ = x_prompt.shape
    bs, ls, _ = x_sample.shape
    win = cache_k.shape[2]
    keep = min(MAX_WINDOW, lp)
    assert lp % SSD_CHUNK == 0 and ls <= SUBLANES and win % KEY_TILE == 0 and win >= MAX_WINDOW

    w_bf = jnp.pad(w_in[0], ((0, 0), (0, IN_COLS_PAD - IN_COLS))).astype(BF16)
    wo_bf = w_out[0].astype(BF16)
    pad_l = lambda v: jnp.pad(v, (0, LANES - v.shape[0]))[None, :]
    ssd_prm = (conv_w[0], conv_b[0][None, :], pad_l(dt_bias[0]), pad_l(a_log[0]),
               jnp.repeat(d_skip[0], SSD_HEAD_DIM)[None, :], ssd_norm_w[0][None, :], _expand_matrix())
    wt = _key_weight_table()

    tp = bp * lp
    tm_p = 256 if lp % 256 == 0 else SSD_CHUNK
    tm_mix = 512 if tp % 512 == 0 else SSD_CHUNK
    cos_p, sin_p = _rope_tables(jnp.arange(lp, dtype=F32))
    xp2 = x_prompt.reshape(tp, D_MODEL)
    q, kf, vf, kb, vb, z, xbc, dtr = _inproj(xp2, w_bf, cos_p, sin_p, tm_p, lp // tm_p)
    att = _attention(q.reshape(bp, lp, ATT_WIDTH), kb.reshape(bp, lp, KV_WIDTH), vb.reshape(bp, lp, KV_WIDTH),
                     wt, KEY_TILE, 0)
    ssd_y, ssm_p, conv_p = _ssd(xbc.reshape(bp, lp, CONV_DIM), dtr.reshape(bp, lp, LANES),
                                z.reshape(bp, lp, SSD_WIDTH),
                                jnp.zeros((bp, SSD_HEADS, SSD_HEAD_DIM, SSD_STATE), F32),
                                jnp.zeros((bp, CONV_W - 1, CONV_DIM), F32), ssd_prm, SSD_CHUNK, SSD_CHUNK)
    h_p = _mix(att.reshape(tp, ATT_WIDTH), ssd_y.reshape(tp, SSD_WIDTH), xp2, wo_bf, ln1_g, ln1_b, tm_mix)

    ts = bs * ls
    pos_s = (PAST_LEN + jnp.arange(ls, dtype=jnp.int32)).astype(F32)
    cos_s, sin_s = _rope_tables(jnp.tile(pos_s, bs))
    xs2 = x_sample.reshape(ts, D_MODEL)
    q_s, kf_s, vf_s, _, _, z_s, xbc_s, dtr_s = _inproj(xs2, w_bf, cos_s, sin_s, ts, 1)
    rows8 = lambda a, w: jnp.pad(a.reshape(bs, ls, w), ((0, 0), (0, SUBLANES - ls), (0, 0)))
    tail = jnp.zeros((bs, KEY_TILE - ls, KV_WIDTH), F32)
    k_all = jnp.concatenate([cache_k[0].reshape(bs, win, KV_WIDTH), kf_s.reshape(bs, ls, KV_WIDTH), tail], axis=1)
    v_all = jnp.concatenate([cache_v[0].reshape(bs, win, KV_WIDTH), vf_s.reshape(bs, ls, KV_WIDTH), tail], axis=1)
    att_s = _attention(rows8(q_s, ATT_WIDTH), k_all.astype(BF16), v_all.astype(BF16), wt, SUBLANES,
                       win // KEY_TILE)[:, :ls]
    ssd_s, ssm_s, conv_s = _ssd(rows8(xbc_s, CONV_DIM), rows8(dtr_s, LANES), rows8(z_s, SSD_WIDTH),
                                state_ssm[0], state_conv[0], ssd_prm, SUBLANES, ls)
    h_s = _mix(att_s.reshape(ts, ATT_WIDTH), ssd_s[:, :ls].reshape(ts, SSD_WIDTH), xs2, wo_bf, ln1_g, ln1_b, ts)

    t_all = tp + ts
    t_pad = -(-t_all // DISPATCH_TILE) * DISPATCH_TILE
    h_all = jnp.concatenate([h_p, h_s, jnp.zeros((t_pad - t_all, D_MODEL), F32)], axis=0)
    y_all = _moe(h_all, w_router[0], router_bias[0], w_exp_gate[0], w_exp_up[0], w_exp_down[0],
                 w_sh_gate[0], w_sh_up[0], w_sh_down[0], ln2_g[0], ln2_b[0])

    kv5 = lambda a, b, l: a.reshape(1, b, l, N_KV_HEADS, HEAD_DIM)
    return (y_all[:tp].reshape(bp, lp, D_MODEL), y_all[tp:t_all].reshape(bs, ls, D_MODEL),
            kv5(kf.reshape(bp, lp, KV_WIDTH)[:, lp - keep:], bp, keep),
            kv5(vf.reshape(bp, lp, KV_WIDTH)[:, lp - keep:], bp, keep),
            ssm_p[None], conv_p[None],
            kv5(kf_s, bs, ls), kv5(vf_s, bs, ls), ssm_s[None], conv_s[None])
```

```python
import functools
import math

import jax
import jax.numpy as jnp
import numpy as np
from jax import lax
from jax.experimental import pallas as pl
from jax.experimental.pallas import tpu as pltpu

F32 = jnp.float32
BF16 = jnp.bfloat16
I32 = jnp.int32

D_MODEL = 1024
PAST_LEN = 16384
HEAD_DIM = 64
N_ATT_HEADS = 16
N_KV_HEADS = 8
ATT_WIDTH = N_ATT_HEADS * HEAD_DIM
KV_WIDTH = N_KV_HEADS * HEAD_DIM
DILATED_BRANCHES = ((128, 1), (512, 4), (2048, 16))
MAX_WINDOW = 2048
ROPE_THETA = 10000.0
SSD_HEADS = 16
SSD_HEAD_DIM = 64
SSD_WIDTH = SSD_HEADS * SSD_HEAD_DIM
SSD_GROUPS = 2
SSD_STATE = 128
SSD_CHUNK = 128
CONV_W = 4
CONV_DIM = SSD_WIDTH + 2 * SSD_GROUPS * SSD_STATE
MIX_WIDTH = ATT_WIDTH + SSD_WIDTH
IN_COLS = ATT_WIDTH + 2 * KV_WIDTH + SSD_WIDTH + CONV_DIM + SSD_HEADS
N_EXPERTS = 256
TOP_K = 8
N_EXPERT_GROUPS = 8
TOPK_GROUPS = 4
EXPERT_FF = 256
SHARED_FF = 256
ROUTED_SCALE = 2.5
DEPTH = 1
DEEPNORM_ALPHA = (2.0 * DEPTH) ** 0.25
NORM_EPS = 1e-5

LANES = 128
SUBLANES = 8
VMEM_LIMIT = 56 * 1024 * 1024

IN_COLS_PAD = ATT_WIDTH + 2 * KV_WIDTH + SSD_WIDTH + CONV_DIM + LANES
KEY_TILE = 128
N_KEY_TILES = MAX_WINDOW // KEY_TILE + 1
EXPERT_BLOCK = 256
ROUTER_TILE = 128
DISPATCH_TILE = 256
COMBINE_TILE = 128
NEG_BIG = -1e30
HIGHEST = lax.Precision.HIGHEST


def _cparams(sem):
    return pltpu.CompilerParams(dimension_semantics=sem, vmem_limit_bytes=VMEM_LIMIT)


def _silu(x):
    return x * (1.0 / (1.0 + jnp.exp(-x)))


def _inproj_kernel(x_ref, w_ref, cq_ref, sq_ref, ck_ref, sk_ref,
                   q_ref, kf_ref, vf_ref, kb_ref, vb_ref, z_ref, xbc_ref, dt_ref):
    tm = x_ref.shape[0]
    xb = x_ref[...].astype(BF16)
    lane = lax.broadcasted_iota(I32, (tm, LANES), 1)
    first_half = (lane % HEAD_DIM) < (HEAD_DIM // 2)

    def rope(a, c, s):
        partner = jnp.where(first_half, pltpu.roll(a, LANES - HEAD_DIM // 2, 1), pltpu.roll(a, HEAD_DIM // 2, 1))
        return a * c + partner * s

    c0 = 0
    acc = jnp.dot(xb, w_ref[:, c0:c0 + ATT_WIDTH], preferred_element_type=F32)
    cq, sq = cq_ref[...], sq_ref[...]
    for j in range(ATT_WIDTH // LANES):
        q_ref[:, j * LANES:(j + 1) * LANES] = rope(acc[:, j * LANES:(j + 1) * LANES], cq, sq).astype(BF16)
    c0 += ATT_WIDTH
    acc = jnp.dot(xb, w_ref[:, c0:c0 + KV_WIDTH], preferred_element_type=F32)
    ck, sk = ck_ref[...], sk_ref[...]
    for j in range(KV_WIDTH // LANES):
        r = rope(acc[:, j * LANES:(j + 1) * LANES], ck, sk)
        kf_ref[:, j * LANES:(j + 1) * LANES] = r
        kb_ref[:, j * LANES:(j + 1) * LANES] = r.astype(BF16)
    c0 += KV_WIDTH
    acc = jnp.dot(xb, w_ref[:, c0:c0 + KV_WIDTH], preferred_element_type=F32)
    vf_ref[...] = acc
    vb_ref[...] = acc.astype(BF16)
    c0 += KV_WIDTH
    z_ref[...] = jnp.dot(xb, w_ref[:, c0:c0 + SSD_WIDTH], preferred_element_type=F32)
    c0 += SSD_WIDTH
    xbc_ref[...] = jnp.dot(xb, w_ref[:, c0:c0 + CONV_DIM], preferred_element_type=F32)
    c0 += CONV_DIM
    dt_ref[...] = jnp.dot(xb, w_ref[:, c0:c0 + LANES], preferred_element_type=F32)


def _rope_tables(pos):
    half = HEAD_DIM // 2
    inv = ROPE_THETA ** (-jnp.arange(half, dtype=F32) / half)
    ang = pos[:, None] * inv[None, :]
    cos, sin = jnp.cos(ang), jnp.sin(ang)
    c = jnp.concatenate([cos, cos, cos, cos], axis=1)
    s = jnp.concatenate([-sin, sin, -sin, sin], axis=1)
    return c, s


def _inproj(x2d, w_bf, cos_t, sin_t, tm, tiles_per_seq):
    t = x2d.shape[0]
    scale = HEAD_DIM ** -0.5
    tab = pl.BlockSpec((tm, LANES), lambda i: (i % tiles_per_seq, 0))
    row = lambda w: pl.BlockSpec((tm, w), lambda i: (i, 0))
    outs = (
        jax.ShapeDtypeStruct((t, ATT_WIDTH), BF16),
        jax.ShapeDtypeStruct((t, KV_WIDTH), F32),
        jax.ShapeDtypeStruct((t, KV_WIDTH), F32),
        jax.ShapeDtypeStruct((t, KV_WIDTH), BF16),
        jax.ShapeDtypeStruct((t, KV_WIDTH), BF16),
        jax.ShapeDtypeStruct((t, SSD_WIDTH), F32),
        jax.ShapeDtypeStruct((t, CONV_DIM), F32),
        jax.ShapeDtypeStruct((t, LANES), F32),
    )
    return pl.pallas_call(
        _inproj_kernel,
        grid=(t // tm,),
        in_specs=[row(D_MODEL), pl.BlockSpec((D_MODEL, IN_COLS_PAD), lambda i: (0, 0)), tab, tab, tab, tab],
        out_specs=(row(ATT_WIDTH), row(KV_WIDTH), row(KV_WIDTH), row(KV_WIDTH), row(KV_WIDTH),
                   row(SSD_WIDTH), row(CONV_DIM), row(LANES)),
        out_shape=outs,
        compiler_params=_cparams(("parallel",)),
        name="inproj",
    )(x2d, w_bf, cos_t * scale, sin_t * scale, cos_t, sin_t)


def _key_weight_table():
    t = np.arange(N_KEY_TILES)[:, None, None]
    r = np.arange(KEY_TILE)[None, :, None]
    c = np.arange(KEY_TILE)[None, None, :]
    d = KEY_TILE * t + r - c
    w = np.zeros(d.shape, np.float32)
    for window, dil in DILATED_BRANCHES:
        w += ((d >= 0) & (d <= window) & (d % dil == 0)).astype(np.float32)
    return jnp.asarray(w)


def _attn_kernel(q_ref, k_ref, v_ref, wt_ref, o_ref, *, off_tiles):
    tq = q_ref.shape[1]
    qi = pl.program_id(2)
    lane = lax.broadcasted_iota(I32, (tq, LANES), 1)
    lo = lane < HEAD_DIM
    n_tiles = jnp.minimum(qi * (tq // KEY_TILE) + off_tiles, N_KEY_TILES - 1) + 1
    for half in (0, 1):
        q = q_ref[0, :, half * LANES:(half + 1) * LANES].astype(F32)
        qs = pltpu.roll(q, HEAD_DIM, 1)
        valid = lo if half == 0 else jnp.logical_not(lo)
        qa_src, qb_src = (q, qs) if half == 0 else (qs, q)
        qq = jnp.concatenate([jnp.where(valid, qa_src, 0.0), jnp.where(valid, qb_src, 0.0)], axis=0).astype(BF16)

        def body(d, carry):
            m, l, acc = carry
            kt = qi * (tq // KEY_TILE) + off_tiles - d
            start = pl.multiple_of(kt * KEY_TILE, KEY_TILE)
            kt_tile = k_ref[0, pl.ds(start, KEY_TILE), :]
            s = lax.dot_general(qq, kt_tile, (((1,), (1,)), ((), ())), preferred_element_type=F32)
            wt = wt_ref[d, 0:tq, :]
            wt2 = jnp.concatenate([wt, wt], axis=0)
            s = jnp.where(wt2 > 0.0, s, NEG_BIG)
            m_new = jnp.maximum(m, jnp.max(s, axis=1, keepdims=True))
            alpha = jnp.exp(m - m_new)
            p = jnp.exp(s - m_new) * wt2
            l = alpha * l + jnp.sum(p, axis=1, keepdims=True)
            vt = v_ref[0, pl.ds(start, KEY_TILE), :]
            acc = alpha * acc + jnp.dot(p.astype(BF16), vt, preferred_element_type=F32)
            return m_new, l, acc

        m0 = jnp.full((2 * tq, 1), NEG_BIG, F32)
        l0 = jnp.zeros((2 * tq, 1), F32)
        a0 = jnp.zeros((2 * tq, LANES), F32)
        _, l, acc = lax.fori_loop(0, n_tiles, body, (m0, l0, a0))
        o = acc / l
        oa, ob = o[:tq], o[tq:]
        if half == 0:
            out = jnp.where(lo, oa, pltpu.roll(ob, HEAD_DIM, 1))
        else:
            out = jnp.where(lo, pltpu.roll(oa, HEAD_DIM, 1), ob)
        o_ref[0, :, half * LANES:(half + 1) * LANES] = out.astype(BF16)


def _attention(q, k, v, wt, tq, off_tiles):
    b, lq, _ = q.shape
    lk = k.shape[1]
    pair = 2 * LANES
    return pl.pallas_call(
        functools.partial(_attn_kernel, off_tiles=off_tiles),
        grid=(b, ATT_WIDTH // pair, lq // tq),
        in_specs=[pl.BlockSpec((1, tq, pair), lambda bi, pi, qi: (bi, qi, pi)),
                  pl.BlockSpec((1, lk, LANES), lambda bi, pi, qi: (bi, 0, pi)),
                  pl.BlockSpec((1, lk, LANES), lambda bi, pi, qi: (bi, 0, pi)),
                  pl.BlockSpec((N_KEY_TILES, KEY_TILE, KEY_TILE), lambda bi, pi, qi: (0, 0, 0))],
        out_specs=pl.BlockSpec((1, tq, pair), lambda bi, pi, qi: (bi, qi, pi)),
        out_shape=jax.ShapeDtypeStruct((b, lq, ATT_WIDTH), BF16),
        compiler_params=_cparams(("parallel", "parallel", "arbitrary")),
        name="dilated_attn",
    )(q, k, v, wt)


def _ssd_kernel(xbc_ref, dt_ref, z_ref, ssm0_ref, conv0_ref, cw_ref, cb_ref, dtb_ref, alog_ref, dsk_ref,
                nw_ref, ex_ref, y_ref, ssm_ref, conv_ref, xpad, dtpad, s_scr, *, n_valid):
    q = SSD_CHUNK
    lb = xbc_ref.shape[1]
    ci = pl.program_id(1)
    nc = pl.num_programs(1)
    gw = SSD_WIDTH // SSD_GROUPS
    hpg = SSD_HEADS // SSD_GROUPS

    @pl.when(ci == 0)
    def _():
        xpad[0:SUBLANES, :] = jnp.zeros((SUBLANES, CONV_DIM), F32)
        xpad[SUBLANES - (CONV_W - 1):SUBLANES, :] = conv0_ref[0]
        for g in range(SSD_GROUPS):
            s_scr[g] = jnp.transpose(ssm0_ref[0, g * hpg:(g + 1) * hpg].reshape(gw, SSD_STATE))

    xpad[SUBLANES:SUBLANES + lb, :] = xbc_ref[0]
    dtpad[0:lb, :] = dt_ref[0]
    if lb < q:
        xpad[SUBLANES + lb:SUBLANES + q, :] = jnp.zeros((q - lb, CONV_DIM), F32)
        dtpad[lb:q, :] = jnp.zeros((q - lb, LANES), F32)

    conv = cb_ref[...]
    for k in range(CONV_W):
        sh = CONV_W - 1 - k
        conv = conv + xpad[SUBLANES - sh:SUBLANES - sh + q, :] * cw_ref[k:k + 1, :]
    act = _silu(conv)

    @pl.when(ci == nc - 1)
    def _():
        conv_ref[0] = xpad[SUBLANES + n_valid - (CONV_W - 1):SUBLANES + n_valid, :]

    xpad[0:SUBLANES, :] = xpad[q:q + SUBLANES, :]

    xs = act[:, :SSD_WIDTH]
    row = lax.broadcasted_iota(I32, (q, LANES), 0)
    dtr = dtpad[...] + dtb_ref[...]
    dt = jnp.maximum(dtr, 0.0) + jnp.log(1.0 + jnp.exp(-jnp.abs(dtr)))
    dt = jnp.where(row < n_valid, dt, 0.0)
    a = -jnp.exp(alog_ref[...])
    da = dt * a
    r2 = lax.broadcasted_iota(I32, (q, q), 0)
    c2 = lax.broadcasted_iota(I32, (q, q), 1)
    causal = r2 >= c2
    a_cs = jnp.dot(causal.astype(F32), da, precision=HIGHEST, preferred_element_type=F32)
    a_cst = jnp.transpose(a_cs)
    a_last = a_cs[q - 1:q, :]
    ex = ex_ref[...]
    expand = lambda t: jnp.dot(t, ex, precision=HIGHEST, preferred_element_type=F32)
    dt_x = expand(dt)
    ea_x = expand(jnp.exp(a_cs))
    te_x = expand(jnp.exp(a_last - a_cs))
    cd_x = expand(jnp.broadcast_to(jnp.exp(a_last), (SUBLANES, LANES)))[0:1, :]
    xdt = xs * dt_x
    xdt_b = xdt.astype(BF16)
    xw_b = (xdt * te_x).astype(BF16)
    lo = lax.broadcasted_iota(I32, (q, LANES), 1) < SSD_HEAD_DIM

    ys = []
    for g in range(SSD_GROUPS):
        bm = act[:, SSD_WIDTH + g * SSD_STATE:SSD_WIDTH + (g + 1) * SSD_STATE]
        cm = act[:, SSD_WIDTH + (SSD_GROUPS + g) * SSD_STATE:SSD_WIDTH + (SSD_GROUPS + g + 1) * SSD_STATE]
        bm_b, cm_b = bm.astype(BF16), cm.astype(BF16)
        cb = lax.dot_general(cm_b, bm_b, (((1,), (1,)), ((), ())), preferred_element_type=F32)
        s_old = s_scr[g]
        y_off = jnp.dot(cm_b, s_old.astype(BF16), preferred_element_type=F32)
        for jp in range(hpg // 2):
            pair = g * (hpg // 2) + jp
            yp = []
            for hh in (0, 1):
                h = 2 * pair + hh
                seg = a_cs[:, h:h + 1] - a_cst[h:h + 1, :]
                dec = jnp.exp(jnp.where(causal, seg, NEG_BIG))
                mm = (cb * dec).astype(BF16)
                yp.append(jnp.dot(mm, xdt_b[:, pair * LANES:(pair + 1) * LANES], preferred_element_type=F32))
            ys.append(jnp.where(lo, yp[0], yp[1]) + y_off[:, jp * LANES:(jp + 1) * LANES]
                      * ea_x[:, pair * LANES:(pair + 1) * LANES])
        bmt_b = jnp.transpose(bm).astype(BF16)
        s_new = s_old * cd_x[:, g * gw:(g + 1) * gw] + jnp.dot(bmt_b, xw_b[:, g * gw:(g + 1) * gw],
                                                              preferred_element_type=F32)
        s_scr[g] = s_new

    y = jnp.concatenate(ys, axis=1) + dsk_ref[...] * xs

    @pl.when(ci == nc - 1)
    def _():
        for g in range(SSD_GROUPS):
            ssm_ref[0, g * hpg:(g + 1) * hpg] = jnp.transpose(s_scr[g]).reshape(hpg, SSD_HEAD_DIM, SSD_STATE)

    hg = y[:lb] * _silu(z_ref[0])
    outs = []
    for g in range(SSD_GROUPS):
        part = hg[:, g * gw:(g + 1) * gw]
        ms = jnp.mean(part * part, axis=1, keepdims=True)
        outs.append(part * lax.rsqrt(ms + NORM_EPS))
    y_ref[0] = (jnp.concatenate(outs, axis=1) * nw_ref[...]).astype(BF16)


def _ssd(xbc, dt_raw, z, ssm0, conv0, prm, lb, n_valid):
    b, l, _ = xbc.shape
    nc = l // lb
    cw, cbias, dtb, alog, dsk, nw, ex = prm
    full = lambda a: pl.BlockSpec(a.shape, lambda bi, ci: (0,) * a.ndim)
    seq = lambda w: pl.BlockSpec((1, lb, w), lambda bi, ci: (bi, ci, 0))
    return pl.pallas_call(
        functools.partial(_ssd_kernel, n_valid=n_valid),
        grid=(b, nc),
        in_specs=[seq(CONV_DIM), seq(LANES), seq(SSD_WIDTH),
                  pl.BlockSpec((1, SSD_HEADS, SSD_HEAD_DIM, SSD_STATE), lambda bi, ci: (bi, 0, 0, 0)),
                  pl.BlockSpec((1, CONV_W - 1, CONV_DIM), lambda bi, ci: (bi, 0, 0)),
                  full(cw), full(cbias), full(dtb), full(alog), full(dsk), full(nw), full(ex)],
        out_specs=(seq(SSD_WIDTH),
                   pl.BlockSpec((1, SSD_HEADS, SSD_HEAD_DIM, SSD_STATE), lambda bi, ci: (bi, 0, 0, 0)),
                   pl.BlockSpec((1, CONV_W - 1, CONV_DIM), lambda bi, ci: (bi, 0, 0))),
        out_shape=(jax.ShapeDtypeStruct((b, l, SSD_WIDTH), BF16),
                   jax.ShapeDtypeStruct((b, SSD_HEADS, SSD_HEAD_DIM, SSD_STATE), F32),
                   jax.ShapeDtypeStruct((b, CONV_W - 1, CONV_DIM), F32)),
        scratch_shapes=[pltpu.VMEM((SSD_CHUNK + 2 * SUBLANES, CONV_DIM), F32),
                        pltpu.VMEM((SSD_CHUNK, LANES), F32),
                        pltpu.VMEM((SSD_GROUPS, SSD_STATE, SSD_WIDTH // SSD_GROUPS), F32)],
        compiler_params=_cparams(("parallel", "arbitrary")),
        name="conv_ssd",
    )(xbc, dt_raw, z, ssm0, conv0, cw, cbias, dtb, alog, dsk, nw, ex)


def _layer_norm(r, g, b):
    mu = jnp.mean(r, axis=1, keepdims=True)
    d = r - mu
    var = jnp.mean(d * d, axis=1, keepdims=True)
    return d * lax.rsqrt(var + NORM_EPS) * g + b


def _mix_kernel(att_ref, ssd_ref, x_ref, wo_ref, g_ref, b_ref, h_ref):
    mix = jnp.dot(att_ref[...], wo_ref[0:ATT_WIDTH, :], preferred_element_type=F32)
    mix = mix + jnp.dot(ssd_ref[...], wo_ref[ATT_WIDTH:MIX_WIDTH, :], preferred_element_type=F32)
    h_ref[...] = _layer_norm(DEEPNORM_ALPHA * x_ref[...] + mix, g_ref[...], b_ref[...])


def _mix(att, ssd, x2d, wo_bf, g, b, tm):
    t = x2d.shape[0]
    row = lambda w: pl.BlockSpec((tm, w), lambda i: (i, 0))
    vec = pl.BlockSpec((1, D_MODEL), lambda i: (0, 0))
    return pl.pallas_call(
        _mix_kernel,
        grid=(t // tm,),
        in_specs=[row(ATT_WIDTH), row(SSD_WIDTH), row(D_MODEL),
                  pl.BlockSpec((MIX_WIDTH, D_MODEL), lambda i: (0, 0)), vec, vec],
        out_specs=row(D_MODEL),
        out_shape=jax.ShapeDtypeStruct((t, D_MODEL), F32),
        compiler_params=_cparams(("parallel",)),
        name="outproj_ln1",
    )(att, ssd, x2d, wo_bf, g, b)


def _router_kernel(h_ref, wr_ref, bias_ref, tope_ref, rank_ref, gt_ref, cnt_ref, gscr):
    tm = h_ref.shape[0]
    i = pl.program_id(0)
    per_group = N_EXPERTS // N_EXPERT_GROUPS

    @pl.when(i == 0)
    def _():
        cnt_ref[...] = jnp.zeros_like(cnt_ref)

    logits = lax.dot_general(wr_ref[...], h_ref[...], (((1,), (1,)), ((), ())),
                             precision=HIGHEST, preferred_element_type=F32)
    s = 1.0 / (1.0 + jnp.exp(-logits))
    sel = s + bias_ref[...]
    neg_inf = -jnp.inf
    e_iota = lax.broadcasted_iota(I32, (N_EXPERTS, tm), 0).astype(F32)
    g_iota = lax.broadcasted_iota(I32, (per_group, tm), 0).astype(F32)

    gscore = []
    for g in range(N_EXPERT_GROUPS):
        blk = sel[g * per_group:(g + 1) * per_group]
        m1 = jnp.max(blk, axis=0, keepdims=True)
        a1 = jnp.min(jnp.where(blk == m1, g_iota, float(per_group)), axis=0, keepdims=True)
        m2 = jnp.max(jnp.where(g_iota == a1, neg_inf, blk), axis=0, keepdims=True)
        gscore.append(m1 + m2)
    blocks = []
    for g in range(N_EXPERT_GROUPS):
        beaten = jnp.zeros((1, tm), F32)
        for o in range(N_EXPERT_GROUPS):
            if o < g:
                beaten = beaten + jnp.where(gscore[o] >= gscore[g], 1.0, 0.0)
            elif o > g:
                beaten = beaten + jnp.where(gscore[o] > gscore[g], 1.0, 0.0)
        blocks.append(jnp.where(beaten < float(TOPK_GROUPS), sel[g * per_group:(g + 1) * per_group], neg_inf))
    cand = jnp.concatenate(blocks, axis=0)

    tops, gsel = [], []
    onehot = jnp.zeros((N_EXPERTS, tm), F32)
    for _ in range(TOP_K):
        mx = jnp.max(cand, axis=0, keepdims=True)
        ix = jnp.min(jnp.where(cand == mx, e_iota, float(N_EXPERTS)), axis=0, keepdims=True)
        hit = e_iota == ix
        tops.append(ix)
        gsel.append(jnp.sum(jnp.where(hit, s, 0.0), axis=0, keepdims=True))
        onehot = jnp.where(hit, 1.0, onehot)
        cand = jnp.where(hit, neg_inf, cand)
    den = gsel[0]
    for k in range(1, TOP_K):
        den = den + gsel[k]

    t_r = lax.broadcasted_iota(I32, (tm, tm), 0)
    t_c = lax.broadcasted_iota(I32, (tm, tm), 1)
    before = (t_r < t_c).astype(BF16)
    oh_b = onehot.astype(BF16)
    base = cnt_ref[...] + jnp.dot(oh_b, before, preferred_element_type=F32)
    cnt_ref[...] = cnt_ref[...] + jnp.dot(oh_b, jnp.ones((tm, LANES), BF16), preferred_element_type=F32)

    gscr[...] = jnp.zeros_like(gscr)
    for k in range(TOP_K):
        tope_ref[k:k + 1, :] = tops[k].astype(I32)
        rank_ref[k:k + 1, :] = jnp.sum(jnp.where(e_iota == tops[k], base, 0.0), axis=0, keepdims=True).astype(I32)
        gscr[k:k + 1, :] = gsel[k] / den * ROUTED_SCALE
    gt_ref[...] = jnp.transpose(gscr[...])


def _router(h_all, wr_t, bias_b):
    t = h_all.shape[0]
    tm = ROUTER_TILE
    return pl.pallas_call(
        _router_kernel,
        grid=(t // tm,),
        in_specs=[pl.BlockSpec((tm, D_MODEL), lambda i: (i, 0)),
                  pl.BlockSpec((N_EXPERTS, D_MODEL), lambda i: (0, 0)),
                  pl.BlockSpec((N_EXPERTS, tm), lambda i: (0, 0))],
        out_specs=(pl.BlockSpec((TOP_K, tm), lambda i: (0, i)),
                   pl.BlockSpec((TOP_K, tm), lambda i: (0, i)),
                   pl.BlockSpec((tm, LANES), lambda i: (i, 0)),
                   pl.BlockSpec((N_EXPERTS, LANES), lambda i: (0, 0))),
        out_shape=(jax.ShapeDtypeStruct((TOP_K, t), I32),
                   jax.ShapeDtypeStruct((TOP_K, t), I32),
                   jax.ShapeDtypeStruct((t, LANES), F32),
                   jax.ShapeDtypeStruct((N_EXPERTS, LANES), F32)),
        scratch_shapes=[pltpu.VMEM((tm, LANES), F32)],
        compiler_params=_cparams(("arbitrary",)),
        name="router",
    )(h_all, wr_t, bias_b)


_PAD_SIZES = tuple(2 ** p for p in range(int(math.log2(EXPERT_BLOCK)) - 1, -1, -1))


ROW_SLAB = D_MODEL // LANES


def _slab(ref, row, n=1):
    return ref.at[pl.ds(pl.multiple_of(row * ROW_SLAB, ROW_SLAB), n * ROW_SLAB)]


def _padfill_kernel(pstart_ref, pcnt_ref, tail_ref, xs_ref, zero_scr, sem):
    zero_scr[...] = jnp.zeros_like(zero_scr)
    half = EXPERT_BLOCK // 2
    tail_copy = lambda j: pltpu.make_async_copy(_slab(zero_scr, 0, half), _slab(xs_ref, tail_ref[0] + j * half, half),
                                                sem)

    def tail_start(j, c):
        tail_copy(j).start()
        return c

    def tail_wait(j, c):
        tail_copy(j).wait()
        return c

    lax.fori_loop(0, tail_ref[1], tail_start, 0)

    def copies(e):
        base = pstart_ref[e]
        cnt = pcnt_ref[e]
        out = []
        for sz in _PAD_SIZES:
            out.append(((cnt & sz) != 0, pltpu.make_async_copy(_slab(zero_scr, 0, sz), _slab(xs_ref, base, sz), sem)))
            base = base + (cnt & sz)
        return out

    def start(e, c):
        for pred, cp in copies(e):
            @pl.when(pred)
            def _():
                cp.start()
        return c

    def wait(e, c):
        for pred, cp in copies(e):
            @pl.when(pred)
            def _():
                cp.wait()
        return c

    lax.fori_loop(0, N_EXPERTS, start, 0)
    lax.fori_loop(0, N_EXPERTS, wait, 0)
    lax.fori_loop(0, tail_ref[1], tail_wait, 0)


def _padfill(pad_start, pad_cnt, tail, n_rows):
    return pl.pallas_call(
        _padfill_kernel,
        grid_spec=pltpu.PrefetchScalarGridSpec(
            num_scalar_prefetch=3, grid=(1,), in_specs=[],
            out_specs=pl.BlockSpec(memory_space=pl.ANY),
            scratch_shapes=[pltpu.VMEM((EXPERT_BLOCK // 2 * ROW_SLAB, LANES), F32), pltpu.SemaphoreType.DMA]),
        out_shape=jax.ShapeDtypeStruct((n_rows * ROW_SLAB, LANES), F32),
        compiler_params=_cparams(("arbitrary",)),
        name="dispatch_padfill",
    )(pad_start, pad_cnt, tail)


def _dest_kernel(tope_ref, rank_ref, pstart_ref, dest_ref):
    tm = tope_ref.shape[1]
    e_iota = lax.broadcasted_iota(I32, (N_EXPERTS, tm), 0)
    ps = pstart_ref[...]
    for k in range(TOP_K):
        base = jnp.sum(jnp.where(e_iota == tope_ref[k:k + 1, :], ps, 0.0), axis=0, keepdims=True)
        dest_ref[k:k + 1, :] = base.astype(I32) + rank_ref[k:k + 1, :]


def _dest(top_e, rank, pstarts):
    t = top_e.shape[1]
    tm = DISPATCH_TILE
    blk = pl.BlockSpec((TOP_K, tm), lambda i: (0, i))
    return pl.pallas_call(
        _dest_kernel,
        grid=(t // tm,),
        in_specs=[blk, blk, pl.BlockSpec((N_EXPERTS, tm), lambda i: (0, 0))],
        out_specs=blk,
        out_shape=jax.ShapeDtypeStruct((TOP_K, t), I32),
        compiler_params=_cparams(("parallel",)),
        name="dispatch_dest",
    )(top_e, rank, jnp.broadcast_to(pstarts.astype(F32)[:, None], (N_EXPERTS, tm)))


def _dest_tiles(dest, tm):
    t = dest.shape[1]
    return jnp.transpose(dest.reshape(TOP_K, t // tm, tm), (1, 0, 2)).reshape(t // tm, TOP_K * tm)


def _dispatch_kernel(dest_ref, h_ref, xs_in_ref, xs_ref, idx_smem, isem, sem):
    del xs_in_ref
    i = pl.program_id(0)
    tm = DISPATCH_TILE
    cp = pltpu.make_async_copy(dest_ref.at[i], idx_smem, isem)
    cp.start()
    cp.wait()

    def row_copy(t, k):
        return pltpu.make_async_copy(_slab(h_ref, t), _slab(xs_ref, idx_smem[k * tm + t]), sem)

    def start(t, c):
        for k in range(TOP_K):
            row_copy(t, k).start()
        return c

    def wait(t, c):
        for k in range(TOP_K):
            row_copy(t, k).wait()
        return c

    lax.fori_loop(0, tm, start, 0)
    lax.fori_loop(0, tm, wait, 0)


def _dispatch(dest_tiles, h_slab, xs):
    tm = DISPATCH_TILE
    return pl.pallas_call(
        _dispatch_kernel,
        grid=(dest_tiles.shape[0],),
        in_specs=[pl.BlockSpec(dest_tiles.shape, lambda i: (0, 0)),
                  pl.BlockSpec((tm * ROW_SLAB, LANES), lambda i: (i, 0)),
                  pl.BlockSpec(memory_space=pl.ANY)],
        out_specs=pl.BlockSpec(memory_space=pl.ANY),
        out_shape=jax.ShapeDtypeStruct(xs.shape, xs.dtype),
        scratch_shapes=[pltpu.SMEM((tm * TOP_K,), I32), pltpu.SemaphoreType.DMA, pltpu.SemaphoreType.DMA],
        input_output_aliases={2: 0},
        compiler_params=_cparams(("arbitrary",)),
        name="dispatch_rows",
    )(dest_tiles, h_slab, xs)


def _expert_kernel(bi_ref, be_ref, nu_ref, x_ref, wg_ref, wu_ref, wd_ref, y_ref, wg_b, wu_b, wd_b):
    del bi_ref
    i = pl.program_id(0)

    @pl.when(i < nu_ref[0])
    def _():
        prev = be_ref[jnp.maximum(i - 1, 0)]

        @pl.when(jnp.logical_or(i == 0, be_ref[i] != prev))
        def _():
            wg_b[...] = wg_ref[0].astype(BF16)
            wu_b[...] = wu_ref[0].astype(BF16)
            wd_b[...] = wd_ref[0].astype(BF16)

        xb = jnp.concatenate([x_ref[pl.ds(j, EXPERT_BLOCK, stride=ROW_SLAB), :] for j in range(ROW_SLAB)],
                             axis=1).astype(BF16)
        g = jnp.dot(xb, wg_b[...], preferred_element_type=F32)
        u = jnp.dot(xb, wu_b[...], preferred_element_type=F32)
        a = (_silu(g) * u).astype(BF16)
        y = jnp.dot(a, wd_b[...], preferred_element_type=F32)
        for j in range(ROW_SLAB):
            y_ref[pl.ds(j, EXPERT_BLOCK, stride=ROW_SLAB), :] = y[:, j * LANES:(j + 1) * LANES]

    @pl.when(i >= nu_ref[0])
    def _():
        y_ref[...] = jnp.zeros_like(y_ref)


def _experts(blk_idx, blk_exp, n_used, xs, wg, wu, wd):
    n_rows = xs.shape[0] // ROW_SLAB
    nb = n_rows // EXPERT_BLOCK
    rows = pl.BlockSpec((EXPERT_BLOCK * ROW_SLAB, LANES), lambda i, bi, be, nu: (bi[i], 0))
    out_rows = pl.BlockSpec((EXPERT_BLOCK * ROW_SLAB, LANES), lambda i, bi, be, nu: (i, 0))
    return pl.pallas_call(
        _expert_kernel,
        grid_spec=pltpu.PrefetchScalarGridSpec(
            num_scalar_prefetch=3, grid=(nb,),
            in_specs=[rows,
                      pl.BlockSpec((1, D_MODEL, EXPERT_FF), lambda i, bi, be, nu: (be[i], 0, 0)),
                      pl.BlockSpec((1, D_MODEL, EXPERT_FF), lambda i, bi, be, nu: (be[i], 0, 0)),
                      pl.BlockSpec((1, EXPERT_FF, D_MODEL), lambda i, bi, be, nu: (be[i], 0, 0))],
            out_specs=out_rows,
            scratch_shapes=[pltpu.VMEM((D_MODEL, EXPERT_FF), BF16), pltpu.VMEM((D_MODEL, EXPERT_FF), BF16),
                            pltpu.VMEM((EXPERT_FF, D_MODEL), BF16)]),
        out_shape=jax.ShapeDtypeStruct(xs.shape, F32),
        compiler_params=_cparams(("arbitrary",)),
        name="expert_ffn",
    )(blk_idx, blk_exp, n_used, xs, wg, wu, wd)


def _combine_kernel(dest_ref, h_ref, gt_ref, ys_ref, sg_ref, su_ref, sd_ref, g_ref, b_ref, o_ref,
                    idx_smem, gbuf, isem, sem):
    i = pl.program_id(0)
    tm = h_ref.shape[0]
    cp = pltpu.make_async_copy(dest_ref.at[i], idx_smem, isem)
    cp.start()
    cp.wait()

    def row_copy(t, k):
        return pltpu.make_async_copy(_slab(ys_ref, idx_smem[k * tm + t]), _slab(gbuf.at[k], t), sem)

    def start(t, c):
        for k in range(TOP_K):
            row_copy(t, k).start()
        return c

    def wait(t, c):
        for k in range(TOP_K):
            row_copy(t, k).wait()
        return c

    lax.fori_loop(0, tm, start, 0)
    h = h_ref[...]
    hb = h.astype(BF16)
    g = jnp.dot(hb, sg_ref[...], preferred_element_type=F32)
    u = jnp.dot(hb, su_ref[...], preferred_element_type=F32)
    f = jnp.dot((_silu(g) * u).astype(BF16), sd_ref[...], preferred_element_type=F32)
    lax.fori_loop(0, tm, wait, 0)
    gt = gt_ref[...]
    cols = []
    for j in range(ROW_SLAB):
        fj = f[:, j * LANES:(j + 1) * LANES]
        for k in range(TOP_K):
            fj = fj + gbuf[k, pl.ds(j, tm, stride=ROW_SLAB), :] * gt[:, k:k + 1]
        cols.append(fj)
    f = jnp.concatenate(cols, axis=1)
    o_ref[...] = _layer_norm(DEEPNORM_ALPHA * h + f, g_ref[...], b_ref[...])


def _combine(dest_tiles, h_all, gates_t, ys, sg_bf, su_bf, sd_bf, g, b):
    t = h_all.shape[0]
    tm = COMBINE_TILE
    vec = pl.BlockSpec((1, D_MODEL), lambda i: (0, 0))
    return pl.pallas_call(
        _combine_kernel,
        grid=(t // tm,),
        in_specs=[pl.BlockSpec(dest_tiles.shape, lambda i: (0, 0)),
                  pl.BlockSpec((tm, D_MODEL), lambda i: (i, 0)),
                  pl.BlockSpec((tm, LANES), lambda i: (i, 0)),
                  pl.BlockSpec(memory_space=pl.ANY),
                  pl.BlockSpec((D_MODEL, SHARED_FF), lambda i: (0, 0)),
                  pl.BlockSpec((D_MODEL, SHARED_FF), lambda i: (0, 0)),
                  pl.BlockSpec((SHARED_FF, D_MODEL), lambda i: (0, 0)), vec, vec],
        out_specs=pl.BlockSpec((tm, D_MODEL), lambda i: (i, 0)),
        out_shape=jax.ShapeDtypeStruct((t, D_MODEL), F32),
        scratch_shapes=[pltpu.SMEM((tm * TOP_K,), I32), pltpu.VMEM((TOP_K, tm * ROW_SLAB, LANES), F32),
                        pltpu.SemaphoreType.DMA, pltpu.SemaphoreType.DMA],
        compiler_params=_cparams(("arbitrary",)),
        name="combine_ln2",
    )(dest_tiles, h_all, gates_t, ys, sg_bf, su_bf, sd_bf, g, b)


def _moe(h_all, wr, rbias, wg, wu, wd, sg, su, sd, g2, b2):
    t = h_all.shape[0]
    top_e, rank, gates_t, cnt = _router(h_all, jnp.transpose(wr),
                                        jnp.broadcast_to(rbias[:, None], (N_EXPERTS, ROUTER_TILE)))
    counts = cnt[:, 0].astype(I32)
    pcounts = (counts + EXPERT_BLOCK - 1) // EXPERT_BLOCK * EXPERT_BLOCK
    pends = jnp.cumsum(pcounts)
    pstarts = pends - pcounts
    n_rows = (t * TOP_K // EXPERT_BLOCK + N_EXPERTS - 1) * EXPERT_BLOCK
    nb = n_rows // EXPERT_BLOCK
    n_used = (pends[-1] // EXPERT_BLOCK).astype(I32)
    blk_idx = jnp.minimum(jnp.arange(nb, dtype=I32), n_used - 1)
    blk_exp = jnp.minimum(jnp.sum((pends[None, :] <= (blk_idx * EXPERT_BLOCK)[:, None]).astype(I32), axis=1),
                          N_EXPERTS - 1)
    dest = _dest(top_e, rank, pstarts)
    tail = jnp.stack([pends[-1], (n_rows - pends[-1]) // (EXPERT_BLOCK // 2)]).astype(I32)
    xs = _padfill((pstarts + counts).astype(I32), (pcounts - counts).astype(I32), tail, n_rows)
    xs = _dispatch(_dest_tiles(dest, DISPATCH_TILE), h_all.reshape(t * ROW_SLAB, LANES), xs)
    ys = _experts(blk_idx, blk_exp, n_used.reshape(1), xs, wg, wu, wd)
    return _combine(_dest_tiles(dest, COMBINE_TILE), h_all, gates_t, ys,
                    sg.astype(BF16), su.astype(BF16), sd.astype(BF16), g2[None, :], b2[None, :])


def _expand_matrix():
    h = np.arange(LANES)[:, None]
    c = np.arange(SSD_WIDTH)[None, :]
    return jnp.asarray((c // SSD_HEAD_DIM == h).astype(np.float32))


def kernel(x_prompt, x_sample, cache_k, cache_v, state_ssm, state_conv, w_in, conv_w, conv_b, dt_bias, a_log, d_skip, ssd_norm_w, w_out, ln1_g, ln1_b, w_router, router_bias, w_exp_gate, w_exp_up, w_exp_down, w_sh_gate, w_sh_up, w_sh_down, ln2_g, ln2_b):
    bp, lp, _ = x_prompt.shape
    bs, ls, _ = x_sample.shape
    win = cache_k.shape[2]
    keep = min(MAX_WINDOW, lp)
    assert lp % SSD_CHUNK == 0 and ls <= SUBLANES and win % KEY_TILE == 0 and win >= MAX_WINDOW

    w_bf = jnp.pad(w_in[0], ((0, 0), (0, IN_COLS_PAD - IN_COLS))).astype(BF16)
    wo_bf = w_out[0].astype(BF16)
    pad_l = lambda v: jnp.pad(v, (0, LANES - v.shape[0]))[None, :]
    ssd_prm = (conv_w[0], conv_b[0][None, :], pad_l(dt_bias[0]), pad_l(a_log[0]),
               jnp.repeat(d_skip[0], SSD_HEAD_DIM)[None, :], ssd_norm_w[0][None, :], _expand_matrix())
    wt = _key_weight_table()

    tp = bp * lp
    tm_p = 256 if lp % 256 == 0 else SSD_CHUNK
    tm_mix = 512 if tp % 512 == 0 else SSD_CHUNK
    cos_p, sin_p = _rope_tables(jnp.arange(lp, dtype=F32))
    xp2 = x_prompt.reshape(tp, D_MODEL)
    q, kf, vf, kb, vb, z, xbc, dtr = _inproj(xp2, w_bf, cos_p, sin_p, tm_p, lp // tm_p)
    att = _attention(q.reshape(bp, lp, ATT_WIDTH), kb.reshape(bp, lp, KV_WIDTH), vb.reshape(bp, lp, KV_WIDTH),
                     wt, KEY_TILE, 0)
    ssd_y, ssm_p, conv_p = _ssd(xbc.reshape(bp, lp, CONV_DIM), dtr.reshape(bp, lp, LANES),
                                z.reshape(bp, lp, SSD_WIDTH),
                                jnp.zeros((bp, SSD_HEADS, SSD_HEAD_DIM, SSD_STATE), F32),
                                jnp.zeros((bp, CONV_W - 1, CONV_DIM), F32), ssd_prm, SSD_CHUNK, SSD_CHUNK)
    h_p = _mix(att.reshape(tp, ATT_WIDTH), ssd_y.reshape(tp, SSD_WIDTH), xp2, wo_bf, ln1_g, ln1_b, tm_mix)

    ts = bs * ls
    pos_s = (PAST_LEN + jnp.arange(ls, dtype=jnp.int32)).astype(F32)
    cos_s, sin_s = _rope_tables(jnp.tile(pos_s, bs))
    xs2 = x_sample.reshape(ts, D_MODEL)
    q_s, kf_s, vf_s, _, _, z_s, xbc_s, dtr_s = _inproj(xs2, w_bf, cos_s, sin_s, ts, 1)
    rows8 = lambda a, w: jnp.pad(a.reshape(bs, ls, w), ((0, 0), (0, SUBLANES - ls), (0, 0)))
    tail = jnp.zeros((bs, KEY_TILE - ls, KV_WIDTH), F32)
    k_all = jnp.concatenate([cache_k[0].reshape(bs, win, KV_WIDTH), kf_s.reshape(bs, ls, KV_WIDTH), tail], axis=1)
    v_all = jnp.concatenate([cache_v[0].reshape(bs, win, KV_WIDTH), vf_s.reshape(bs, ls, KV_WIDTH), tail], axis=1)
    att_s = _attention(rows8(q_s, ATT_WIDTH), k_all.astype(BF16), v_all.astype(BF16), wt, SUBLANES,
                       win // KEY_TILE)[:, :ls]
    ssd_s, ssm_s, conv_s = _ssd(rows8(xbc_s, CONV_DIM), rows8(dtr_s, LANES), rows8(z_s, SSD_WIDTH),
                                state_ssm[0], state_conv[0], ssd_prm, SUBLANES, ls)
    h_s = _mix(att_s.reshape(ts, ATT_WIDTH), ssd_s[:, :ls].reshape(ts, SSD_WIDTH), xs2, wo_bf, ln1_g, ln1_b, ts)

    t_all = tp + ts
    t_pad = -(-t_all // DISPATCH_TILE) * DISPATCH_TILE
    h_all = jnp.concatenate([h_p, h_s, jnp.zeros((t_pad - t_all, D_MODEL), F32)], axis=0)
    y_all = _moe(h_all, w_router[0], router_bias[0], w_exp_gate[0], w_exp_up[0], w_exp_down[0],
                 w_sh_gate[0], w_sh_up[0], w_sh_down[0], ln2_g[0], ln2_b[0])

    kv5 = lambda a, b, l: a.reshape(1, b, l, N_KV_HEADS, HEAD_DIM)
    return (y_all[:tp].reshape(bp, lp, D_MODEL), y_all[tp:t_all].reshape(bs, ls, D_MODEL),
            kv5(kf.reshape(bp, lp, KV_WIDTH)[:, lp - keep:], bp, keep),
            kv5(vf.reshape(bp, lp, KV_WIDTH)[:, lp - keep:], bp, keep),
            ssm_p[None], conv_p[None],
            kv5(kf_s, bs, ls), kv5(vf_s, bs, ls), ssm_s[None], conv_s[None])
```

```python
import functools
import math

import jax
import jax.numpy as jnp
import numpy as np
from jax import lax
from jax.experimental import pallas as pl
from jax.experimental.pallas import tpu as pltpu

F32 = jnp.float32
BF16 = jnp.bfloat16
I32 = jnp.int32

D_MODEL = 1024
PAST_LEN = 16384
HEAD_DIM = 64
N_ATT_HEADS = 16
N_KV_HEADS = 8
ATT_WIDTH = N_ATT_HEADS * HEAD_DIM
KV_WIDTH = N_KV_HEADS * HEAD_DIM
DILATED_BRANCHES = ((128, 1), (512, 4), (2048, 16))
MAX_WINDOW = 2048
ROPE_THETA = 10000.0
SSD_HEADS = 16
SSD_HEAD_DIM = 64
SSD_WIDTH = SSD_HEADS * SSD_HEAD_DIM
SSD_GROUPS = 2
SSD_STATE = 128
SSD_CHUNK = 128
CONV_W = 4
CONV_DIM = SSD_WIDTH + 2 * SSD_GROUPS * SSD_STATE
MIX_WIDTH = ATT_WIDTH + SSD_WIDTH
IN_COLS = ATT_WIDTH + 2 * KV_WIDTH + SSD_WIDTH + CONV_DIM + SSD_HEADS
N_EXPERTS = 256
TOP_K = 8
N_EXPERT_GROUPS = 8
TOPK_GROUPS = 4
EXPERT_FF = 256
SHARED_FF = 256
ROUTED_SCALE = 2.5
DEPTH = 1
DEEPNORM_ALPHA = (2.0 * DEPTH) ** 0.25
NORM_EPS = 1e-5

LANES = 128
SUBLANES = 8
VMEM_LIMIT = 56 * 1024 * 1024

IN_COLS_PAD = ATT_WIDTH + 2 * KV_WIDTH + SSD_WIDTH + CONV_DIM + LANES
KEY_TILE = 128
N_KEY_TILES = MAX_WINDOW // KEY_TILE + 1
EXPERT_BLOCK = 256
ROUTER_TILE = 128
DISPATCH_TILE = 256
COMBINE_TILE = 128
NEG_BIG = -1e30
HIGHEST = lax.Precision.HIGHEST


def _cparams(sem):
    return pltpu.CompilerParams(dimension_semantics=sem, vmem_limit_bytes=VMEM_LIMIT)


def _silu(x):
    return x * (1.0 / (1.0 + jnp.exp(-x)))


def _inproj_kernel(x_ref, w_ref, cq_ref, sq_ref, ck_ref, sk_ref,
                   q_ref, kf_ref, vf_ref, z_ref, xbc_ref, dt_ref):
    tm = x_ref.shape[0]
    xb = x_ref[...].astype(BF16)
    lane = lax.broadcasted_iota(I32, (tm, LANES), 1)
    first_half = (lane % HEAD_DIM) < (HEAD_DIM // 2)

    def rope(a, c, s):
        partner = jnp.where(first_half, pltpu.roll(a, LANES - HEAD_DIM // 2, 1), pltpu.roll(a, HEAD_DIM // 2, 1))
        return a * c + partner * s

    c0 = 0
    acc = jnp.dot(xb, w_ref[:, c0:c0 + ATT_WIDTH], preferred_element_type=F32)
    cq, sq = cq_ref[...], sq_ref[...]
    for j in range(ATT_WIDTH // LANES):
        q_ref[:, j * LANES:(j + 1) * LANES] = rope(acc[:, j * LANES:(j + 1) * LANES], cq, sq)
    c0 += ATT_WIDTH
    acc = jnp.dot(xb, w_ref[:, c0:c0 + KV_WIDTH], preferred_element_type=F32)
    ck, sk = ck_ref[...], sk_ref[...]
    for j in range(KV_WIDTH // LANES):
        kf_ref[:, j * LANES:(j + 1) * LANES] = rope(acc[:, j * LANES:(j + 1) * LANES], ck, sk)
    c0 += KV_WIDTH
    vf_ref[...] = jnp.dot(xb, w_ref[:, c0:c0 + KV_WIDTH], preferred_element_type=F32)
    c0 += KV_WIDTH
    z_ref[...] = jnp.dot(xb, w_ref[:, c0:c0 + SSD_WIDTH], preferred_element_type=F32)
    c0 += SSD_WIDTH
    xbc_ref[...] = jnp.dot(xb, w_ref[:, c0:c0 + CONV_DIM], preferred_element_type=F32)
    c0 += CONV_DIM
    dt_ref[...] = jnp.dot(xb, w_ref[:, c0:c0 + LANES], preferred_element_type=F32)


def _rope_tables(pos):
    half = HEAD_DIM // 2
    inv = ROPE_THETA ** (-jnp.arange(half, dtype=F32) / half)
    ang = pos[:, None] * inv[None, :]
    cos, sin = jnp.cos(ang), jnp.sin(ang)
    c = jnp.concatenate([cos, cos, cos, cos], axis=1)
    s = jnp.concatenate([-sin, sin, -sin, sin], axis=1)
    return c, s


def _inproj(x2d, w_bf, cos_t, sin_t, tm, tiles_per_seq):
    t = x2d.shape[0]
    scale = HEAD_DIM ** -0.5
    tab = pl.BlockSpec((tm, LANES), lambda i: (i % tiles_per_seq, 0))
    row = lambda w: pl.BlockSpec((tm, w), lambda i: (i, 0))
    outs = (
        jax.ShapeDtypeStruct((t, ATT_WIDTH), F32),
        jax.ShapeDtypeStruct((t, KV_WIDTH), F32),
        jax.ShapeDtypeStruct((t, KV_WIDTH), F32),
        jax.ShapeDtypeStruct((t, SSD_WIDTH), F32),
        jax.ShapeDtypeStruct((t, CONV_DIM), F32),
        jax.ShapeDtypeStruct((t, LANES), F32),
    )
    return pl.pallas_call(
        _inproj_kernel,
        grid=(t // tm,),
        in_specs=[row(D_MODEL), pl.BlockSpec((D_MODEL, IN_COLS_PAD), lambda i: (0, 0)), tab, tab, tab, tab],
        out_specs=(row(ATT_WIDTH), row(KV_WIDTH), row(KV_WIDTH), row(SSD_WIDTH), row(CONV_DIM), row(LANES)),
        out_shape=outs,
        compiler_params=_cparams(("parallel",)),
        name="inproj",
    )(x2d, w_bf, cos_t * scale, sin_t * scale, cos_t, sin_t)


def _key_weight_table():
    t = np.arange(N_KEY_TILES)[:, None, None]
    r = np.arange(KEY_TILE)[None, :, None]
    c = np.arange(KEY_TILE)[None, None, :]
    d = KEY_TILE * t + r - c
    w = np.zeros(d.shape, np.float32)
    for window, dil in DILATED_BRANCHES:
        w += ((d >= 0) & (d <= window) & (d % dil == 0)).astype(np.float32)
    return jnp.asarray(w)


def _attn_kernel(q_ref, k_ref, v_ref, wt_ref, o_ref, *, off_tiles):
    tq = q_ref.shape[1]
    qi = pl.program_id(2)
    lane = lax.broadcasted_iota(I32, (tq, LANES), 1)
    lo = lane < HEAD_DIM
    n_tiles = jnp.minimum(qi * (tq // KEY_TILE) + off_tiles, N_KEY_TILES - 1) + 1
    for half in (0, 1):
        q = q_ref[0, :, half * LANES:(half + 1) * LANES].astype(F32)
        qs = pltpu.roll(q, HEAD_DIM, 1)
        valid = lo if half == 0 else jnp.logical_not(lo)
        qa_src, qb_src = (q, qs) if half == 0 else (qs, q)
        qq = jnp.concatenate([jnp.where(valid, qa_src, 0.0), jnp.where(valid, qb_src, 0.0)], axis=0).astype(BF16)

        def body(d, carry):
            m, l, acc = carry
            kt = qi * (tq // KEY_TILE) + off_tiles - d
            start = pl.multiple_of(kt * KEY_TILE, KEY_TILE)
            kt_tile = k_ref[0, pl.ds(start, KEY_TILE), :]
            s = lax.dot_general(qq, kt_tile, (((1,), (1,)), ((), ())), preferred_element_type=F32)
            wt = wt_ref[d, 0:tq, :]
            wt2 = jnp.concatenate([wt, wt], axis=0)
            s = jnp.where(wt2 > 0.0, s, NEG_BIG)
            m_new = jnp.maximum(m, jnp.max(s, axis=1, keepdims=True))
            alpha = jnp.exp(m - m_new)
            p = jnp.exp(s - m_new) * wt2
            l = alpha * l + jnp.sum(p, axis=1, keepdims=True)
            vt = v_ref[0, pl.ds(start, KEY_TILE), :]
            acc = alpha * acc + jnp.dot(p.astype(BF16), vt, preferred_element_type=F32)
            return m_new, l, acc

        m0 = jnp.full((2 * tq, 1), NEG_BIG, F32)
        l0 = jnp.zeros((2 * tq, 1), F32)
        a0 = jnp.zeros((2 * tq, LANES), F32)
        _, l, acc = lax.fori_loop(0, n_tiles, body, (m0, l0, a0))
        o = acc / l
        oa, ob = o[:tq], o[tq:]
        if half == 0:
            out = jnp.where(lo, oa, pltpu.roll(ob, HEAD_DIM, 1))
        else:
            out = jnp.where(lo, pltpu.roll(oa, HEAD_DIM, 1), ob)
        o_ref[0, :, half * LANES:(half + 1) * LANES] = out.astype(BF16)


def _attention(q, k, v, wt, tq, off_tiles):
    b, lq, _ = q.shape
    lk = k.shape[1]
    pair = 2 * LANES
    return pl.pallas_call(
        functools.partial(_attn_kernel, off_tiles=off_tiles),
        grid=(b, ATT_WIDTH // pair, lq // tq),
        in_specs=[pl.BlockSpec((1, tq, pair), lambda bi, pi, qi: (bi, qi, pi)),
                  pl.BlockSpec((1, lk, LANES), lambda bi, pi, qi: (bi, 0, pi)),
                  pl.BlockSpec((1, lk, LANES), lambda bi, pi, qi: (bi, 0, pi)),
                  pl.BlockSpec((N_KEY_TILES, KEY_TILE, KEY_TILE), lambda bi, pi, qi: (0, 0, 0))],
        out_specs=pl.BlockSpec((1, tq, pair), lambda bi, pi, qi: (bi, qi, pi)),
        out_shape=jax.ShapeDtypeStruct((b, lq, ATT_WIDTH), BF16),
        compiler_params=_cparams(("parallel", "parallel", "arbitrary")),
        name="dilated_attn",
    )(q, k, v, wt)


ATT_SUPER = 2048
ATT_UNIT = 128
ATT_SPAN = max(w // d for w, d in DILATED_BRANCHES)


def _attn_window_kernel(q0_ref, q1_ref, k_ref, v_ref, o_ref, acc_s, m_s, l_s):
    sb = q0_ref.shape[1]
    sbi = pl.program_id(2)
    u_rows, span = ATT_UNIT, ATT_SPAN
    nkeys = u_rows + span
    lane = lax.broadcasted_iota(I32, (u_rows, LANES), 1)
    lo = lane < HEAD_DIM
    row_i = lax.broadcasted_iota(I32, (2 * u_rows, nkeys), 0) & (u_rows - 1)
    rc = row_i - lax.broadcasted_iota(I32, (2 * u_rows, nkeys), 1)
    units = sb // u_rows

    for half, qh_ref in enumerate((q0_ref, q1_ref)):
        valid = lo if half == 0 else jnp.logical_not(lo)

        for bi, (window, r) in enumerate(DILATED_BRANCHES):
            assert window // r == span and units % r == 0
            shift = r.bit_length() - 1

            def unit(idx, c, r=r, shift=shift, first=(bi == 0)):
                rho = idx & (r - 1)
                u = idx >> shift
                qrow0 = rho + r * u_rows * u
                q = qh_ref[0, pl.ds(qrow0, u_rows, stride=r), :]
                qs = pltpu.roll(q, HEAD_DIM, 1)
                qa_src, qb_src = (q, qs) if half == 0 else (qs, q)
                qq = jnp.concatenate([jnp.where(valid, qa_src, 0.0), jnp.where(valid, qb_src, 0.0)],
                                     axis=0).astype(BF16)
                qclass0 = sbi * (sb // r) + u_rows * u
                kclass0 = jnp.maximum(qclass0 - span, 0)
                krows = pl.ds(rho + r * kclass0, nkeys, stride=r)
                kt = k_ref[0, krows, :].astype(BF16)
                vt = v_ref[0, krows, :].astype(BF16)
                s = lax.dot_general(qq, kt, (((1,), (1,)), ((), ())), preferred_element_type=F32)
                dist = rc + (qclass0 - kclass0)
                s = jnp.where(pltpu.bitcast(dist, jnp.uint32) <= jnp.uint32(span), s, NEG_BIG)
                m = jnp.max(s, axis=1, keepdims=True)
                p = jnp.exp(s - m)
                l = jnp.sum(p, axis=1, keepdims=True)
                acc = jnp.dot(p.astype(BF16), vt, preferred_element_type=F32)
                m = jnp.broadcast_to(m, (2 * u_rows, LANES))
                l = jnp.broadcast_to(l, (2 * u_rows, LANES))
                rows = (pl.ds(qrow0, u_rows, stride=r), pl.ds(sb + qrow0, u_rows, stride=r))
                if not first:
                    ld = lambda ref: jnp.concatenate([ref[rows[0], :], ref[rows[1], :]], axis=0)
                    m_old, l_old, acc_old = ld(m_s), ld(l_s), ld(acc_s)
                    m_new = jnp.maximum(m_old, m)
                    a_old, a_new = jnp.exp(m_old - m_new), jnp.exp(m - m_new)
                    l = a_old * l_old + a_new * l
                    acc = a_old * acc_old + a_new * acc
                    m = m_new
                for h2 in (0, 1):
                    sl = slice(h2 * u_rows, (h2 + 1) * u_rows)
                    m_s[rows[h2], :] = m[sl]
                    l_s[rows[h2], :] = l[sl]
                    acc_s[rows[h2], :] = acc[sl]
                return c

            lax.fori_loop(0, units, unit, 0, unroll=2)

        chunk = 2 * u_rows
        for c in range(sb // chunk):
            ra, rb = slice(c * chunk, (c + 1) * chunk), slice(sb + c * chunk, sb + (c + 1) * chunk)
            oa = acc_s[ra, :] / l_s[ra, :]
            ob = acc_s[rb, :] / l_s[rb, :]
            lo2 = jnp.concatenate([lo, lo], axis=0)
            if half == 0:
                out = jnp.where(lo2, oa, pltpu.roll(ob, HEAD_DIM, 1))
            else:
                out = jnp.where(lo2, pltpu.roll(oa, HEAD_DIM, 1), ob)
            o_ref[0, ra, half * LANES:(half + 1) * LANES] = out.astype(BF16)


def _attention_window(q, k, v):
    b, l, _ = q.shape
    sb = ATT_SUPER
    max_dil = max(d for _, d in DILATED_BRANCHES)
    assert l % sb == 0 and l // max_dil >= ATT_UNIT + ATT_SPAN
    qspec = lambda h: pl.BlockSpec((1, sb, LANES), lambda bi, pi, si: (bi, si, 2 * pi + h))
    kspec = pl.BlockSpec((1, l, LANES), lambda bi, pi, si: (bi, 0, pi))
    return pl.pallas_call(
        _attn_window_kernel,
        grid=(b, KV_WIDTH // LANES, l // sb),
        in_specs=[qspec(0), qspec(1), kspec, kspec],
        out_specs=pl.BlockSpec((1, sb, 2 * LANES), lambda bi, pi, si: (bi, si, pi)),
        out_shape=jax.ShapeDtypeStruct((b, l, ATT_WIDTH), BF16),
        scratch_shapes=[pltpu.VMEM((2 * sb, LANES), F32)] * 3,
        compiler_params=_cparams(("parallel", "parallel", "arbitrary")),
        name="window_attn",
    )(q, q, k, v)


def _ssd_kernel(xbc_ref, dt_ref, z_ref, ssm0_ref, conv0_ref, cw_ref, cb_ref, dtb_ref, alog_ref, dsk_ref,
                nw_ref, ex_ref, y_ref, ssm_ref, conv_ref, xpad, dtpad, s_scr, *, n_valid):
    q = SSD_CHUNK
    lb = xbc_ref.shape[1]
    ci = pl.program_id(1)
    nc = pl.num_programs(1)
    gw = SSD_WIDTH // SSD_GROUPS
    hpg = SSD_HEADS // SSD_GROUPS

    @pl.when(ci == 0)
    def _():
        xpad[0:SUBLANES, :] = jnp.zeros((SUBLANES, CONV_DIM), F32)
        xpad[SUBLANES - (CONV_W - 1):SUBLANES, :] = conv0_ref[0]
        for g in range(SSD_GROUPS):
            s_scr[g] = jnp.transpose(ssm0_ref[0, g * hpg:(g + 1) * hpg].reshape(gw, SSD_STATE))

    xpad[SUBLANES:SUBLANES + lb, :] = xbc_ref[0]
    dtpad[0:lb, :] = dt_ref[0]
    if lb < q:
        xpad[SUBLANES + lb:SUBLANES + q, :] = jnp.zeros((q - lb, CONV_DIM), F32)
        dtpad[lb:q, :] = jnp.zeros((q - lb, LANES), F32)

    conv = cb_ref[...]
    for k in range(CONV_W):
        sh = CONV_W - 1 - k
        conv = conv + xpad[SUBLANES - sh:SUBLANES - sh + q, :] * cw_ref[k:k + 1, :]
    act = _silu(conv)

    @pl.when(ci == nc - 1)
    def _():
        conv_ref[0] = xpad[SUBLANES + n_valid - (CONV_W - 1):SUBLANES + n_valid, :]

    xpad[0:SUBLANES, :] = xpad[q:q + SUBLANES, :]

    xs = act[:, :SSD_WIDTH]
    row = lax.broadcasted_iota(I32, (q, LANES), 0)
    dtr = dtpad[...] + dtb_ref[...]
    dt = jnp.maximum(dtr, 0.0) + jnp.log(1.0 + jnp.exp(-jnp.abs(dtr)))
    dt = jnp.where(row < n_valid, dt, 0.0)
    a = -jnp.exp(alog_ref[...])
    da = dt * a
    r2 = lax.broadcasted_iota(I32, (q, q), 0)
    c2 = lax.broadcasted_iota(I32, (q, q), 1)
    causal = r2 >= c2
    a_cs = jnp.dot(causal.astype(F32), da, precision=HIGHEST, preferred_element_type=F32)
    a_cst = jnp.transpose(a_cs)
    a_last = a_cs[q - 1:q, :]
    ex = ex_ref[...]
    expand = lambda t: jnp.dot(t, ex, precision=HIGHEST, preferred_element_type=F32)
    dt_x = expand(dt)
    ea_x = expand(jnp.exp(a_cs))
    te_x = expand(jnp.exp(a_last - a_cs))
    cd_x = expand(jnp.broadcast_to(jnp.exp(a_last), (SUBLANES, LANES)))[0:1, :]
    xdt = xs * dt_x
    xdt_b = xdt.astype(BF16)
    xw_b = (xdt * te_x).astype(BF16)
    lo = lax.broadcasted_iota(I32, (q, LANES), 1) < SSD_HEAD_DIM

    ys = []
    for g in range(SSD_GROUPS):
        bm = act[:, SSD_WIDTH + g * SSD_STATE:SSD_WIDTH + (g + 1) * SSD_STATE]
        cm = act[:, SSD_WIDTH + (SSD_GROUPS + g) * SSD_STATE:SSD_WIDTH + (SSD_GROUPS + g + 1) * SSD_STATE]
        bm_b, cm_b = bm.astype(BF16), cm.astype(BF16)
        cb = lax.dot_general(cm_b, bm_b, (((1,), (1,)), ((), ())), preferred_element_type=F32)
        s_old = s_scr[g]
        y_off = jnp.dot(cm_b, s_old.astype(BF16), preferred_element_type=F32)
        for jp in range(hpg // 2):
            pair = g * (hpg // 2) + jp
            yp = []
            for hh in (0, 1):
                h = 2 * pair + hh
                seg = a_cs[:, h:h + 1] - a_cst[h:h + 1, :]
                dec = jnp.exp(jnp.where(causal, seg, NEG_BIG))
                mm = (cb * dec).astype(BF16)
                yp.append(jnp.dot(mm, xdt_b[:, pair * LANES:(pair + 1) * LANES], preferred_element_type=F32))
            ys.append(jnp.where(lo, yp[0], yp[1]) + y_off[:, jp * LANES:(jp + 1) * LANES]
                      * ea_x[:, pair * LANES:(pair + 1) * LANES])
        bmt_b = jnp.transpose(bm).astype(BF16)
        s_new = s_old * cd_x[:, g * gw:(g + 1) * gw] + jnp.dot(bmt_b, xw_b[:, g * gw:(g + 1) * gw],
                                                              preferred_element_type=F32)
        s_scr[g] = s_new

    y = jnp.concatenate(ys, axis=1) + dsk_ref[...] * xs

    @pl.when(ci == nc - 1)
    def _():
        for g in range(SSD_GROUPS):
            ssm_ref[0, g * hpg:(g + 1) * hpg] = jnp.transpose(s_scr[g]).reshape(hpg, SSD_HEAD_DIM, SSD_STATE)

    hg = y[:lb] * _silu(z_ref[0])
    outs = []
    for g in range(SSD_GROUPS):
        part = hg[:, g * gw:(g + 1) * gw]
        ms = jnp.mean(part * part, axis=1, keepdims=True)
        outs.append(part * lax.rsqrt(ms + NORM_EPS))
    y_ref[0] = (jnp.concatenate(outs, axis=1) * nw_ref[...]).astype(BF16)


def _ssd(xbc, dt_raw, z, ssm0, conv0, prm, lb, n_valid):
    b, l, _ = xbc.shape
    nc = l // lb
    cw, cbias, dtb, alog, dsk, nw, ex = prm
    full = lambda a: pl.BlockSpec(a.shape, lambda bi, ci: (0,) * a.ndim)
    seq = lambda w: pl.BlockSpec((1, lb, w), lambda bi, ci: (bi, ci, 0))
    return pl.pallas_call(
        functools.partial(_ssd_kernel, n_valid=n_valid),
        grid=(b, nc),
        in_specs=[seq(CONV_DIM), seq(LANES), seq(SSD_WIDTH),
                  pl.BlockSpec((1, SSD_HEADS, SSD_HEAD_DIM, SSD_STATE), lambda bi, ci: (bi, 0, 0, 0)),
                  pl.BlockSpec((1, CONV_W - 1, CONV_DIM), lambda bi, ci: (bi, 0, 0)),
                  full(cw), full(cbias), full(dtb), full(alog), full(dsk), full(nw), full(ex)],
        out_specs=(seq(SSD_WIDTH),
                   pl.BlockSpec((1, SSD_HEADS, SSD_HEAD_DIM, SSD_STATE), lambda bi, ci: (bi, 0, 0, 0)),
                   pl.BlockSpec((1, CONV_W - 1, CONV_DIM), lambda bi, ci: (bi, 0, 0))),
        out_shape=(jax.ShapeDtypeStruct((b, l, SSD_WIDTH), BF16),
                   jax.ShapeDtypeStruct((b, SSD_HEADS, SSD_HEAD_DIM, SSD_STATE), F32),
                   jax.ShapeDtypeStruct((b, CONV_W - 1, CONV_DIM), F32)),
        scratch_shapes=[pltpu.VMEM((SSD_CHUNK + 2 * SUBLANES, CONV_DIM), F32),
                        pltpu.VMEM((SSD_CHUNK, LANES), F32),
                        pltpu.VMEM((SSD_GROUPS, SSD_STATE, SSD_WIDTH // SSD_GROUPS), F32)],
        compiler_params=_cparams(("parallel", "arbitrary")),
        name="conv_ssd",
    )(xbc, dt_raw, z, ssm0, conv0, cw, cbias, dtb, alog, dsk, nw, ex)


def _layer_norm(r, g, b):
    mu = jnp.mean(r, axis=1, keepdims=True)
    d = r - mu
    var = jnp.mean(d * d, axis=1, keepdims=True)
    return d * lax.rsqrt(var + NORM_EPS) * g + b


def _mix_kernel(att_ref, ssd_ref, x_ref, wo_ref, g_ref, b_ref, h_ref):
    mix = jnp.dot(att_ref[...], wo_ref[0:ATT_WIDTH, :], preferred_element_type=F32)
    mix = mix + jnp.dot(ssd_ref[...], wo_ref[ATT_WIDTH:MIX_WIDTH, :], preferred_element_type=F32)
    h_ref[...] = _layer_norm(DEEPNORM_ALPHA * x_ref[...] + mix, g_ref[...], b_ref[...])


def _mix(att, ssd, x2d, wo_bf, g, b, tm):
    t = x2d.shape[0]
    row = lambda w: pl.BlockSpec((tm, w), lambda i: (i, 0))
    vec = pl.BlockSpec((1, D_MODEL), lambda i: (0, 0))
    return pl.pallas_call(
        _mix_kernel,
        grid=(t // tm,),
        in_specs=[row(ATT_WIDTH), row(SSD_WIDTH), row(D_MODEL),
                  pl.BlockSpec((MIX_WIDTH, D_MODEL), lambda i: (0, 0)), vec, vec],
        out_specs=row(D_MODEL),
        out_shape=jax.ShapeDtypeStruct((t, D_MODEL), F32),
        compiler_params=_cparams(("parallel",)),
        name="outproj_ln1",
    )(att, ssd, x2d, wo_bf, g, b)


def _router_kernel(h_ref, wr_ref, bias_ref, tope_ref, rank_ref, gt_ref, cnt_ref, gscr):
    tm = h_ref.shape[0]
    i = pl.program_id(0)
    per_group = N_EXPERTS // N_EXPERT_GROUPS

    @pl.when(i == 0)
    def _():
        cnt_ref[...] = jnp.zeros_like(cnt_ref)

    logits = lax.dot_general(wr_ref[...], h_ref[...], (((1,), (1,)), ((), ())),
                             precision=HIGHEST, preferred_element_type=F32)
    s = 1.0 / (1.0 + jnp.exp(-logits))
    sel = s + bias_ref[...]
    neg_inf = -jnp.inf
    e_iota = lax.broadcasted_iota(I32, (N_EXPERTS, tm), 0).astype(F32)
    g_iota = lax.broadcasted_iota(I32, (per_group, tm), 0).astype(F32)

    gscore = []
    for g in range(N_EXPERT_GROUPS):
        blk = sel[g * per_group:(g + 1) * per_group]
        m1 = jnp.max(blk, axis=0, keepdims=True)
        a1 = jnp.min(jnp.where(blk == m1, g_iota, float(per_group)), axis=0, keepdims=True)
        m2 = jnp.max(jnp.where(g_iota == a1, neg_inf, blk), axis=0, keepdims=True)
        gscore.append(m1 + m2)
    blocks = []
    for g in range(N_EXPERT_GROUPS):
        beaten = jnp.zeros((1, tm), F32)
        for o in range(N_EXPERT_GROUPS):
            if o < g:
                beaten = beaten + jnp.where(gscore[o] >= gscore[g], 1.0, 0.0)
            elif o > g:
                beaten = beaten + jnp.where(gscore[o] > gscore[g], 1.0, 0.0)
        blocks.append(jnp.where(beaten < float(TOPK_GROUPS), sel[g * per_group:(g + 1) * per_group], neg_inf))
    cand = jnp.concatenate(blocks, axis=0)

    tops, gsel = [], []
    onehot = jnp.zeros((N_EXPERTS, tm), F32)
    for _ in range(TOP_K):
        mx = jnp.max(cand, axis=0, keepdims=True)
        ix = jnp.min(jnp.where(cand == mx, e_iota, float(N_EXPERTS)), axis=0, keepdims=True)
        hit = e_iota == ix
        tops.append(ix)
        gsel.append(jnp.sum(jnp.where(hit, s, 0.0), axis=0, keepdims=True))
        onehot = jnp.where(hit, 1.0, onehot)
        cand = jnp.where(hit, neg_inf, cand)
    den = gsel[0]
    for k in range(1, TOP_K):
        den = den + gsel[k]

    t_r = lax.broadcasted_iota(I32, (tm, tm), 0)
    t_c = lax.broadcasted_iota(I32, (tm, tm), 1)
    before = (t_r < t_c).astype(BF16)
    oh_b = onehot.astype(BF16)
    base = cnt_ref[...] + jnp.dot(oh_b, before, preferred_element_type=F32)
    cnt_ref[...] = cnt_ref[...] + jnp.dot(oh_b, jnp.ones((tm, LANES), BF16), preferred_element_type=F32)

    gscr[...] = jnp.zeros_like(gscr)
    for k in range(TOP_K):
        tope_ref[k:k + 1, :] = tops[k].astype(I32)
        rank_ref[k:k + 1, :] = jnp.sum(jnp.where(e_iota == tops[k], base, 0.0), axis=0, keepdims=True).astype(I32)
        gscr[k:k + 1, :] = gsel[k] / den * ROUTED_SCALE
    gt_ref[...] = jnp.transpose(gscr[...])


def _router(h_all, wr_t, bias_b):
    t = h_all.shape[0]
    tm = ROUTER_TILE
    return pl.pallas_call(
        _router_kernel,
        grid=(t // tm,),
        in_specs=[pl.BlockSpec((tm, D_MODEL), lambda i: (i, 0)),
                  pl.BlockSpec((N_EXPERTS, D_MODEL), lambda i: (0, 0)),
                  pl.BlockSpec((N_EXPERTS, tm), lambda i: (0, 0))],
        out_specs=(pl.BlockSpec((TOP_K, tm), lambda i: (0, i)),
                   pl.BlockSpec((TOP_K, tm), lambda i: (0, i)),
                   pl.BlockSpec((tm, LANES), lambda i: (i, 0)),
                   pl.BlockSpec((N_EXPERTS, LANES), lambda i: (0, 0))),
        out_shape=(jax.ShapeDtypeStruct((TOP_K, t), I32),
                   jax.ShapeDtypeStruct((TOP_K, t), I32),
                   jax.ShapeDtypeStruct((t, LANES), F32),
                   jax.ShapeDtypeStruct((N_EXPERTS, LANES), F32)),
        scratch_shapes=[pltpu.VMEM((tm, LANES), F32)],
        compiler_params=_cparams(("arbitrary",)),
        name="router",
    )(h_all, wr_t, bias_b)


_PAD_SIZES = tuple(2 ** p for p in range(int(math.log2(EXPERT_BLOCK)) - 1, -1, -1))


ROW_SLAB = D_MODEL // LANES


def _slab(ref, row, n=1):
    return ref.at[pl.ds(pl.multiple_of(row * ROW_SLAB, ROW_SLAB), n * ROW_SLAB)]


def _padfill_kernel(pstart_ref, pcnt_ref, tail_ref, xs_ref, zero_scr, sem):
    zero_scr[...] = jnp.zeros_like(zero_scr)
    half = EXPERT_BLOCK // 2
    tail_copy = lambda j: pltpu.make_async_copy(_slab(zero_scr, 0, half), _slab(xs_ref, tail_ref[0] + j * half, half),
                                                sem)

    def tail_start(j, c):
        tail_copy(j).start()
        return c

    def tail_wait(j, c):
        tail_copy(j).wait()
        return c

    lax.fori_loop(0, tail_ref[1], tail_start, 0)

    def copies(e):
        base = pstart_ref[e]
        cnt = pcnt_ref[e]
        out = []
        for sz in _PAD_SIZES:
            out.append(((cnt & sz) != 0, pltpu.make_async_copy(_slab(zero_scr, 0, sz), _slab(xs_ref, base, sz), sem)))
            base = base + (cnt & sz)
        return out

    def start(e, c):
        for pred, cp in copies(e):
            @pl.when(pred)
            def _():
                cp.start()
        return c

    def wait(e, c):
        for pred, cp in copies(e):
            @pl.when(pred)
            def _():
                cp.wait()
        return c

    lax.fori_loop(0, N_EXPERTS, start, 0)
    lax.fori_loop(0, N_EXPERTS, wait, 0)
    lax.fori_loop(0, tail_ref[1], tail_wait, 0)


def _padfill(pad_start, pad_cnt, tail, n_rows):
    return pl.pallas_call(
        _padfill_kernel,
        grid_spec=pltpu.PrefetchScalarGridSpec(
            num_scalar_prefetch=3, grid=(1,), in_specs=[],
            out_specs=pl.BlockSpec(memory_space=pl.ANY),
            scratch_shapes=[pltpu.VMEM((EXPERT_BLOCK // 2 * ROW_SLAB, LANES), F32), pltpu.SemaphoreType.DMA]),
        out_shape=jax.ShapeDtypeStruct((n_rows * ROW_SLAB, LANES), F32),
        compiler_params=_cparams(("arbitrary",)),
        name="dispatch_padfill",
    )(pad_start, pad_cnt, tail)


def _dest_kernel(tope_ref, rank_ref, pstart_ref, dest_ref):
    tm = tope_ref.shape[1]
    e_iota = lax.broadcasted_iota(I32, (N_EXPERTS, tm), 0)
    ps = pstart_ref[...]
    for k in range(TOP_K):
        base = jnp.sum(jnp.where(e_iota == tope_ref[k:k + 1, :], ps, 0.0), axis=0, keepdims=True)
        dest_ref[k:k + 1, :] = base.astype(I32) + rank_ref[k:k + 1, :]


def _dest(top_e, rank, pstarts):
    t = top_e.shape[1]
    tm = DISPATCH_TILE
    blk = pl.BlockSpec((TOP_K, tm), lambda i: (0, i))
    return pl.pallas_call(
        _dest_kernel,
        grid=(t // tm,),
        in_specs=[blk, blk, pl.BlockSpec((N_EXPERTS, tm), lambda i: (0, 0))],
        out_specs=blk,
        out_shape=jax.ShapeDtypeStruct((TOP_K, t), I32),
        compiler_params=_cparams(("parallel",)),
        name="dispatch_dest",
    )(top_e, rank, jnp.broadcast_to(pstarts.astype(F32)[:, None], (N_EXPERTS, tm)))


def _dest_tiles(dest, tm):
    t = dest.shape[1]
    return jnp.transpose(dest.reshape(TOP_K, t // tm, tm), (1, 0, 2)).reshape(t // tm, TOP_K * tm)


def _dispatch_kernel(dest_ref, h_ref, xs_in_ref, xs_ref, idx_smem, isem, sem):
    del xs_in_ref
    i = pl.program_id(0)
    tm = DISPATCH_TILE
    cp = pltpu.make_async_copy(dest_ref.at[i], idx_smem, isem)
    cp.start()
    cp.wait()

    def row_copy(t, k):
        return pltpu.make_async_copy(_slab(h_ref, t), _slab(xs_ref, idx_smem[k * tm + t]), sem)

    def start(t, c):
        for k in range(TOP_K):
            row_copy(t, k).start()
        return c

    def wait(t, c):
        for k in range(TOP_K):
            row_copy(t, k).wait()
        return c

    lax.fori_loop(0, tm, start, 0)
    lax.fori_loop(0, tm, wait, 0)


def _dispatch(dest_tiles, h_slab, xs):
    tm = DISPATCH_TILE
    return pl.pallas_call(
        _dispatch_kernel,
        grid=(dest_tiles.shape[0],),
        in_specs=[pl.BlockSpec(dest_tiles.shape, lambda i: (0, 0)),
                  pl.BlockSpec((tm * ROW_SLAB, LANES), lambda i: (i, 0)),
                  pl.BlockSpec(memory_space=pl.ANY)],
        out_specs=pl.BlockSpec(memory_space=pl.ANY),
        out_shape=jax.ShapeDtypeStruct(xs.shape, xs.dtype),
        scratch_shapes=[pltpu.SMEM((tm * TOP_K,), I32), pltpu.SemaphoreType.DMA, pltpu.SemaphoreType.DMA],
        input_output_aliases={2: 0},
        compiler_params=_cparams(("arbitrary",)),
        name="dispatch_rows",
    )(dest_tiles, h_slab, xs)


def _expert_kernel(bi_ref, be_ref, nu_ref, x_ref, wg_ref, wu_ref, wd_ref, y_ref, wg_b, wu_b, wd_b):
    del bi_ref
    i = pl.program_id(0)

    @pl.when(i < nu_ref[0])
    def _():
        prev = be_ref[jnp.maximum(i - 1, 0)]

        @pl.when(jnp.logical_or(i == 0, be_ref[i] != prev))
        def _():
            wg_b[...] = wg_ref[0].astype(BF16)
            wu_b[...] = wu_ref[0].astype(BF16)
            wd_b[...] = wd_ref[0].astype(BF16)

        xb = jnp.concatenate([x_ref[pl.ds(j, EXPERT_BLOCK, stride=ROW_SLAB), :] for j in range(ROW_SLAB)],
                             axis=1).astype(BF16)
        g = jnp.dot(xb, wg_b[...], preferred_element_type=F32)
        u = jnp.dot(xb, wu_b[...], preferred_element_type=F32)
        a = (_silu(g) * u).astype(BF16)
        y = jnp.dot(a, wd_b[...], preferred_element_type=F32)
        for j in range(ROW_SLAB):
            y_ref[pl.ds(j, EXPERT_BLOCK, stride=ROW_SLAB), :] = y[:, j * LANES:(j + 1) * LANES]

    @pl.when(i >= nu_ref[0])
    def _():
        y_ref[...] = jnp.zeros_like(y_ref)


def _experts(blk_idx, blk_exp, n_used, xs, wg, wu, wd):
    n_rows = xs.shape[0] // ROW_SLAB
    nb = n_rows // EXPERT_BLOCK
    rows = pl.BlockSpec((EXPERT_BLOCK * ROW_SLAB, LANES), lambda i, bi, be, nu: (bi[i], 0))
    out_rows = pl.BlockSpec((EXPERT_BLOCK * ROW_SLAB, LANES), lambda i, bi, be, nu: (i, 0))
    return pl.pallas_call(
        _expert_kernel,
        grid_spec=pltpu.PrefetchScalarGridSpec(
            num_scalar_prefetch=3, grid=(nb,),
            in_specs=[rows,
                      pl.BlockSpec((1, D_MODEL, EXPERT_FF), lambda i, bi, be, nu: (be[i], 0, 0)),
                      pl.BlockSpec((1, D_MODEL, EXPERT_FF), lambda i, bi, be, nu: (be[i], 0, 0)),
                      pl.BlockSpec((1, EXPERT_FF, D_MODEL), lambda i, bi, be, nu: (be[i], 0, 0))],
            out_specs=out_rows,
            scratch_shapes=[pltpu.VMEM((D_MODEL, EXPERT_FF), BF16), pltpu.VMEM((D_MODEL, EXPERT_FF), BF16),
                            pltpu.VMEM((EXPERT_FF, D_MODEL), BF16)]),
        out_shape=jax.ShapeDtypeStruct(xs.shape, F32),
        compiler_params=_cparams(("arbitrary",)),
        name="expert_ffn",
    )(blk_idx, blk_exp, n_used, xs, wg, wu, wd)


def _combine_kernel(dest_ref, h_ref, gt_ref, ys_ref, sg_ref, su_ref, sd_ref, g_ref, b_ref, o_ref,
                    idx_smem, gbuf, isem, sem):
    i = pl.program_id(0)
    tm = h_ref.shape[0]
    cp = pltpu.make_async_copy(dest_ref.at[i], idx_smem, isem)
    cp.start()
    cp.wait()

    def row_copy(t, k):
        return pltpu.make_async_copy(_slab(ys_ref, idx_smem[k * tm + t]), _slab(gbuf.at[k], t), sem)

    def start(t, c):
        for k in range(TOP_K):
            row_copy(t, k).start()
        return c

    def wait(t, c):
        for k in range(TOP_K):
            row_copy(t, k).wait()
        return c

    lax.fori_loop(0, tm, start, 0)
    h = h_ref[...]
    hb = h.astype(BF16)
    g = jnp.dot(hb, sg_ref[...], preferred_element_type=F32)
    u = jnp.dot(hb, su_ref[...], preferred_element_type=F32)
    f = jnp.dot((_silu(g) * u).astype(BF16), sd_ref[...], preferred_element_type=F32)
    lax.fori_loop(0, tm, wait, 0)
    gt = gt_ref[...]
    cols = []
    for j in range(ROW_SLAB):
        fj = f[:, j * LANES:(j + 1) * LANES]
        for k in range(TOP_K):
            fj = fj + gbuf[k, pl.ds(j, tm, stride=ROW_SLAB), :] * gt[:, k:k + 1]
        cols.append(fj)
    f = jnp.concatenate(cols, axis=1)
    o_ref[...] = _layer_norm(DEEPNORM_ALPHA * h + f, g_ref[...], b_ref[...])


def _combine(dest_tiles, h_all, gates_t, ys, sg_bf, su_bf, sd_bf, g, b):
    t = h_all.shape[0]
    tm = COMBINE_TILE
    vec = pl.BlockSpec((1, D_MODEL), lambda i: (0, 0))
    return pl.pallas_call(
        _combine_kernel,
        grid=(t // tm,),
        in_specs=[pl.BlockSpec(dest_tiles.shape, lambda i: (0, 0)),
                  pl.BlockSpec((tm, D_MODEL), lambda i: (i, 0)),
                  pl.BlockSpec((tm, LANES), lambda i: (i, 0)),
                  pl.BlockSpec(memory_space=pl.ANY),
                  pl.BlockSpec((D_MODEL, SHARED_FF), lambda i: (0, 0)),
                  pl.BlockSpec((D_MODEL, SHARED_FF), lambda i: (0, 0)),
                  pl.BlockSpec((SHARED_FF, D_MODEL), lambda i: (0, 0)), vec, vec],
        out_specs=pl.BlockSpec((tm, D_MODEL), lambda i: (i, 0)),
        out_shape=jax.ShapeDtypeStruct((t, D_MODEL), F32),
        scratch_shapes=[pltpu.SMEM((tm * TOP_K,), I32), pltpu.VMEM((TOP_K, tm * ROW_SLAB, LANES), F32),
                        pltpu.SemaphoreType.DMA, pltpu.SemaphoreType.DMA],
        compiler_params=_cparams(("arbitrary",)),
        name="combine_ln2",
    )(dest_tiles, h_all, gates_t, ys, sg_bf, su_bf, sd_bf, g, b)


def _moe(h_all, wr, rbias, wg, wu, wd, sg, su, sd, g2, b2):
    t = h_all.shape[0]
    top_e, rank, gates_t, cnt = _router(h_all, jnp.transpose(wr),
                                        jnp.broadcast_to(rbias[:, None], (N_EXPERTS, ROUTER_TILE)))
    counts = cnt[:, 0].astype(I32)
    pcounts = (counts + EXPERT_BLOCK - 1) // EXPERT_BLOCK * EXPERT_BLOCK
    pends = jnp.cumsum(pcounts)
    pstarts = pends - pcounts
    n_rows = (t * TOP_K // EXPERT_BLOCK + N_EXPERTS - 1) * EXPERT_BLOCK
    nb = n_rows // EXPERT_BLOCK
    n_used = (pends[-1] // EXPERT_BLOCK).astype(I32)
    blk_idx = jnp.minimum(jnp.arange(nb, dtype=I32), n_used - 1)
    blk_exp = jnp.minimum(jnp.sum((pends[None, :] <= (blk_idx * EXPERT_BLOCK)[:, None]).astype(I32), axis=1),
                          N_EXPERTS - 1)
    dest = _dest(top_e, rank, pstarts)
    tail = jnp.stack([pends[-1], (n_rows - pends[-1]) // (EXPERT_BLOCK // 2)]).astype(I32)
    xs = _padfill((pstarts + counts).astype(I32), (pcounts - counts).astype(I32), tail, n_rows)
    xs = _dispatch(_dest_tiles(dest, DISPATCH_TILE), h_all.reshape(t * ROW_SLAB, LANES), xs)
    ys = _experts(blk_idx, blk_exp, n_used.reshape(1), xs, wg, wu, wd)
    return _combine(_dest_tiles(dest, COMBINE_TILE), h_all, gates_t, ys,
                    sg.astype(BF16), su.astype(BF16), sd.astype(BF16), g2[None, :], b2[None, :])


def _expand_matrix():
    h = np.arange(LANES)[:, None]
    c = np.arange(SSD_WIDTH)[None, :]
    return jnp.asarray((c // SSD_HEAD_DIM == h).astype(np.float32))


def kernel(x_prompt, x_sample, cache_k, cache_v, state_ssm, state_conv, w_in, conv_w, conv_b, dt_bias, a_log, d_skip, ssd_norm_w, w_out, ln1_g, ln1_b, w_router, router_bias, w_exp_gate, w_exp_up, w_exp_down, w_sh_gate, w_sh_up, w_sh_down, ln2_g, ln2_b):
    bp, lp, _ = x_prompt.shape
    bs, ls, _ = x_sample.shape
    win = cache_k.shape[2]
    keep = min(MAX_WINDOW, lp)
    assert lp % SSD_CHUNK == 0 and ls <= SUBLANES and win % KEY_TILE == 0 and win >= MAX_WINDOW

    w_bf = jnp.pad(w_in[0], ((0, 0), (0, IN_COLS_PAD - IN_COLS))).astype(BF16)
    wo_bf = w_out[0].astype(BF16)
    pad_l = lambda v: jnp.pad(v, (0, LANES - v.shape[0]))[None, :]
    ssd_prm = (conv_w[0], conv_b[0][None, :], pad_l(dt_bias[0]), pad_l(a_log[0]),
               jnp.repeat(d_skip[0], SSD_HEAD_DIM)[None, :], ssd_norm_w[0][None, :], _expand_matrix())
    wt = _key_weight_table()

    tp = bp * lp
    tm_p = 256 if lp % 256 == 0 else SSD_CHUNK
    tm_mix = 512 if tp % 512 == 0 else SSD_CHUNK
    cos_p, sin_p = _rope_tables(jnp.arange(lp, dtype=F32))
    xp2 = x_prompt.reshape(tp, D_MODEL)
    q, kf, vf, z, xbc, dtr = _inproj(xp2, w_bf, cos_p, sin_p, tm_p, lp // tm_p)
    att = _attention_window(q.reshape(bp, lp, ATT_WIDTH), kf.reshape(bp, lp, KV_WIDTH), vf.reshape(bp, lp, KV_WIDTH))
    ssd_y, ssm_p, conv_p = _ssd(xbc.reshape(bp, lp, CONV_DIM), dtr.reshape(bp, lp, LANES),
                                z.reshape(bp, lp, SSD_WIDTH),
                                jnp.zeros((bp, SSD_HEADS, SSD_HEAD_DIM, SSD_STATE), F32),
                                jnp.zeros((bp, CONV_W - 1, CONV_DIM), F32), ssd_prm, SSD_CHUNK, SSD_CHUNK)
    h_p = _mix(att.reshape(tp, ATT_WIDTH), ssd_y.reshape(tp, SSD_WIDTH), xp2, wo_bf, ln1_g, ln1_b, tm_mix)

    ts = bs * ls
    pos_s = (PAST_LEN + jnp.arange(ls, dtype=jnp.int32)).astype(F32)
    cos_s, sin_s = _rope_tables(jnp.tile(pos_s, bs))
    xs2 = x_sample.reshape(ts, D_MODEL)
    q_s, kf_s, vf_s, z_s, xbc_s, dtr_s = _inproj(xs2, w_bf, cos_s, sin_s, ts, 1)
    rows8 = lambda a, w: jnp.pad(a.reshape(bs, ls, w), ((0, 0), (0, SUBLANES - ls), (0, 0)))
    tail = jnp.zeros((bs, KEY_TILE - ls, KV_WIDTH), F32)
    k_all = jnp.concatenate([cache_k[0].reshape(bs, win, KV_WIDTH), kf_s.reshape(bs, ls, KV_WIDTH), tail], axis=1)
    v_all = jnp.concatenate([cache_v[0].reshape(bs, win, KV_WIDTH), vf_s.reshape(bs, ls, KV_WIDTH), tail], axis=1)
    att_s = _attention(rows8(q_s, ATT_WIDTH).astype(BF16), k_all.astype(BF16), v_all.astype(BF16), wt, SUBLANES,
                       win // KEY_TILE)[:, :ls]
    ssd_s, ssm_s, conv_s = _ssd(rows8(xbc_s, CONV_DIM), rows8(dtr_s, LANES), rows8(z_s, SSD_WIDTH),
                                state_ssm[0], state_conv[0], ssd_prm, SUBLANES, ls)
    h_s = _mix(att_s.reshape(ts, ATT_WIDTH), ssd_s[:, :ls].reshape(ts, SSD_WIDTH), xs2, wo_bf, ln1_g, ln1_b, ts)

    t_all = tp + ts
    t_pad = -(-t_all // DISPATCH_TILE) * DISPATCH_TILE
    h_all = jnp.concatenate([h_p, h_s, jnp.zeros((t_pad - t_all, D_MODEL), F32)], axis=0)
    y_all = _moe(h_all, w_router[0], router_bias[0], w_exp_gate[0], w_exp_up[0], w_exp_down[0],
                 w_sh_gate[0], w_sh_up[0], w_sh_down[0], ln2_g[0], ln2_b[0])

    kv5 = lambda a, b, l: a.reshape(1, b, l, N_KV_HEADS, HEAD_DIM)
    return (y_all[:tp].reshape(bp, lp, D_MODEL), y_all[tp:t_all].reshape(bs, ls, D_MODEL),
            kv5(kf.reshape(bp, lp, KV_WIDTH)[:, lp - keep:], bp, keep),
            kv5(vf.reshape(bp, lp, KV_WIDTH)[:, lp - keep:], bp, keep),
            ssm_p[None], conv_p[None],
            kv5(kf_s, bs, ls), kv5(vf_s, bs, ls), ssm_s[None], conv_s[None])
```

```python
import functools
import math

import jax
import jax.numpy as jnp
import numpy as np
from jax import lax
from jax.experimental import pallas as pl
from jax.experimental.pallas import tpu as pltpu

F32 = jnp.float32
BF16 = jnp.bfloat16
I32 = jnp.int32

D_MODEL = 1024
PAST_LEN = 16384
HEAD_DIM = 64
N_ATT_HEADS = 16
N_KV_HEADS = 8
ATT_WIDTH = N_ATT_HEADS * HEAD_DIM
KV_WIDTH = N_KV_HEADS * HEAD_DIM
DILATED_BRANCHES = ((128, 1), (512, 4), (2048, 16))
MAX_WINDOW = 2048
ROPE_THETA = 10000.0
SSD_HEADS = 16
SSD_HEAD_DIM = 64
SSD_WIDTH = SSD_HEADS * SSD_HEAD_DIM
SSD_GROUPS = 2
SSD_STATE = 128
SSD_CHUNK = 128
CONV_W = 4
CONV_DIM = SSD_WIDTH + 2 * SSD_GROUPS * SSD_STATE
MIX_WIDTH = ATT_WIDTH + SSD_WIDTH
IN_COLS = ATT_WIDTH + 2 * KV_WIDTH + SSD_WIDTH + CONV_DIM + SSD_HEADS
N_EXPERTS = 256
TOP_K = 8
N_EXPERT_GROUPS = 8
TOPK_GROUPS = 4
EXPERT_FF = 256
SHARED_FF = 256
ROUTED_SCALE = 2.5
DEPTH = 1
DEEPNORM_ALPHA = (2.0 * DEPTH) ** 0.25
NORM_EPS = 1e-5

LANES = 128
SUBLANES = 8
VMEM_LIMIT = 56 * 1024 * 1024

IN_COLS_PAD = ATT_WIDTH + 2 * KV_WIDTH + SSD_WIDTH + CONV_DIM + LANES
KEY_TILE = 128
EXPERT_BLOCK = 256
ROUTER_TILE = 128
DISPATCH_TILE = 256
COMBINE_TILE = 128
NEG_BIG = -1e30
HIGHEST = lax.Precision.HIGHEST


def _cparams(sem):
    return pltpu.CompilerParams(dimension_semantics=sem, vmem_limit_bytes=VMEM_LIMIT)


def _silu(x):
    return x * (1.0 / (1.0 + jnp.exp(-x)))


def _inproj_kernel(x_ref, w_ref, cq_ref, sq_ref, ck_ref, sk_ref,
                   q_ref, kf_ref, vf_ref, z_ref, xbc_ref, dt_ref):
    tm = x_ref.shape[0]
    xb = x_ref[...].astype(BF16)
    lane = lax.broadcasted_iota(I32, (tm, LANES), 1)
    first_half = (lane % HEAD_DIM) < (HEAD_DIM // 2)

    def rope(a, c, s):
        partner = jnp.where(first_half, pltpu.roll(a, LANES - HEAD_DIM // 2, 1), pltpu.roll(a, HEAD_DIM // 2, 1))
        return a * c + partner * s

    c0 = 0
    acc = jnp.dot(xb, w_ref[:, c0:c0 + ATT_WIDTH], preferred_element_type=F32)
    cq, sq = cq_ref[...], sq_ref[...]
    for j in range(ATT_WIDTH // LANES):
        q_ref[:, j * LANES:(j + 1) * LANES] = rope(acc[:, j * LANES:(j + 1) * LANES], cq, sq)
    c0 += ATT_WIDTH
    acc = jnp.dot(xb, w_ref[:, c0:c0 + KV_WIDTH], preferred_element_type=F32)
    ck, sk = ck_ref[...], sk_ref[...]
    for j in range(KV_WIDTH // LANES):
        kf_ref[:, j * LANES:(j + 1) * LANES] = rope(acc[:, j * LANES:(j + 1) * LANES], ck, sk)
    c0 += KV_WIDTH
    vf_ref[...] = jnp.dot(xb, w_ref[:, c0:c0 + KV_WIDTH], preferred_element_type=F32)
    c0 += KV_WIDTH
    z_ref[...] = jnp.dot(xb, w_ref[:, c0:c0 + SSD_WIDTH], preferred_element_type=F32)
    c0 += SSD_WIDTH
    xbc_ref[...] = jnp.dot(xb, w_ref[:, c0:c0 + CONV_DIM], preferred_element_type=F32)
    c0 += CONV_DIM
    dt_ref[...] = jnp.dot(xb, w_ref[:, c0:c0 + LANES], preferred_element_type=F32)


def _rope_tables(pos):
    half = HEAD_DIM // 2
    inv = ROPE_THETA ** (-jnp.arange(half, dtype=F32) / half)
    ang = pos[:, None] * inv[None, :]
    cos, sin = jnp.cos(ang), jnp.sin(ang)
    c = jnp.concatenate([cos, cos, cos, cos], axis=1)
    s = jnp.concatenate([-sin, sin, -sin, sin], axis=1)
    return c, s


def _inproj(x2d, w_bf, cos_t, sin_t, tm, tiles_per_seq):
    t = x2d.shape[0]
    scale = HEAD_DIM ** -0.5
    tab = pl.BlockSpec((tm, LANES), lambda i: (i % tiles_per_seq, 0))
    row = lambda w: pl.BlockSpec((tm, w), lambda i: (i, 0))
    outs = (
        jax.ShapeDtypeStruct((t, ATT_WIDTH), F32),
        jax.ShapeDtypeStruct((t, KV_WIDTH), F32),
        jax.ShapeDtypeStruct((t, KV_WIDTH), F32),
        jax.ShapeDtypeStruct((t, SSD_WIDTH), F32),
        jax.ShapeDtypeStruct((t, CONV_DIM), F32),
        jax.ShapeDtypeStruct((t, LANES), F32),
    )
    return pl.pallas_call(
        _inproj_kernel,
        grid=(t // tm,),
        in_specs=[row(D_MODEL), pl.BlockSpec((D_MODEL, IN_COLS_PAD), lambda i: (0, 0)), tab, tab, tab, tab],
        out_specs=(row(ATT_WIDTH), row(KV_WIDTH), row(KV_WIDTH), row(SSD_WIDTH), row(CONV_DIM), row(LANES)),
        out_shape=outs,
        compiler_params=_cparams(("parallel",)),
        name="inproj",
    )(x2d, w_bf, cos_t * scale, sin_t * scale, cos_t, sin_t)


def _branch_weight(d):
    w = np.zeros(d.shape, np.float32)
    for window, dil in DILATED_BRANCHES:
        w += ((d >= 0) & (d <= window) & (d % dil == 0)).astype(np.float32)
    return w


def _attn_decode_kernel(q_ref, kc_ref, vc_ref, kn_ref, vn_ref, wc_ref, wn_ref, o_ref, ktail, vtail):
    nq = q_ref.shape[1]
    lane = lax.broadcasted_iota(I32, (nq, LANES), 1)
    lo = lane < HEAD_DIM
    ktail[...] = jnp.zeros_like(ktail)
    vtail[...] = jnp.zeros_like(vtail)
    ktail[0:nq, :] = kn_ref[0]
    vtail[0:nq, :] = vn_ref[0]
    wc, wn = wc_ref[...], wn_ref[...]
    nt = (((1,), (1,)), ((), ()))
    for g in range(N_KV_HEADS):
        pair, half = divmod(g, 2)
        cols = slice(pair * LANES, (pair + 1) * LANES)
        q = q_ref[0, :, g * LANES:(g + 1) * LANES]
        qs = pltpu.roll(q, HEAD_DIM, 1)
        valid = lo if half == 0 else jnp.logical_not(lo)
        qa_src, qb_src = (q, qs) if half == 0 else (qs, q)
        qq = jnp.concatenate([jnp.where(valid, qa_src, 0.0), jnp.where(valid, qb_src, 0.0)], axis=0).astype(BF16)
        s_c = lax.dot_general(qq, kc_ref[0, :, cols].astype(BF16), nt, preferred_element_type=F32)
        s_n = lax.dot_general(qq, ktail[:, cols].astype(BF16), nt, preferred_element_type=F32)
        s_c = jnp.where(wc > 0.0, s_c, NEG_BIG)
        s_n = jnp.where(wn > 0.0, s_n, NEG_BIG)
        m = jnp.maximum(jnp.max(s_c, axis=1, keepdims=True), jnp.max(s_n, axis=1, keepdims=True))
        p_c = jnp.exp(s_c - m) * wc
        p_n = jnp.exp(s_n - m) * wn
        l = jnp.sum(p_c, axis=1, keepdims=True) + jnp.sum(p_n, axis=1, keepdims=True)
        acc = jnp.dot(p_c.astype(BF16), vc_ref[0, :, cols].astype(BF16), preferred_element_type=F32)
        acc = acc + jnp.dot(p_n.astype(BF16), vtail[:, cols].astype(BF16), preferred_element_type=F32)
        o = acc / l
        oa, ob = o[:nq], o[nq:]
        if half == 0:
            out = jnp.where(lo, oa, pltpu.roll(ob, HEAD_DIM, 1))
        else:
            out = jnp.where(lo, pltpu.roll(oa, HEAD_DIM, 1), ob)
        o_ref[0, :, g * LANES:(g + 1) * LANES] = out.astype(BF16)


def _attention_decode(q, k_cache, v_cache, k_new, v_new):
    b, nq, _ = q.shape
    win = k_cache.shape[1]
    i = np.arange(nq)[:, None]
    wc = _branch_weight(win + i - np.arange(win)[None, :])
    wn = _branch_weight(i - np.arange(KEY_TILE)[None, :])
    wn[:, nq:] = 0.0
    stack = lambda w: jnp.asarray(np.concatenate([w, w], axis=0))
    seq = lambda rows, w: pl.BlockSpec((1, rows, w), lambda bi: (bi, 0, 0))
    full = lambda shp: pl.BlockSpec(shp, lambda bi: (0, 0))
    return pl.pallas_call(
        _attn_decode_kernel,
        grid=(b,),
        in_specs=[seq(nq, ATT_WIDTH), seq(win, KV_WIDTH), seq(win, KV_WIDTH), seq(nq, KV_WIDTH), seq(nq, KV_WIDTH),
                  full((2 * nq, win)), full((2 * nq, KEY_TILE))],
        out_specs=seq(nq, ATT_WIDTH),
        out_shape=jax.ShapeDtypeStruct((b, nq, ATT_WIDTH), BF16),
        scratch_shapes=[pltpu.VMEM((KEY_TILE, KV_WIDTH), F32), pltpu.VMEM((KEY_TILE, KV_WIDTH), F32)],
        compiler_params=_cparams(("arbitrary",)),
        name="decode_attn",
    )(q, k_cache, v_cache, k_new, v_new, stack(wc), stack(wn))


ATT_SUPER = 2048
ATT_UNIT = 128
ATT_SPAN = max(w // d for w, d in DILATED_BRANCHES)


def _attn_window_kernel(q0_ref, q1_ref, k_ref, v_ref, o_ref, acc_s, m_s, l_s):
    sb = q0_ref.shape[1]
    sbi = pl.program_id(2)
    u_rows, span = ATT_UNIT, ATT_SPAN
    nkeys = u_rows + span
    lane = lax.broadcasted_iota(I32, (u_rows, LANES), 1)
    lo = lane < HEAD_DIM
    row_i = lax.broadcasted_iota(I32, (2 * u_rows, nkeys), 0) & (u_rows - 1)
    rc = row_i - lax.broadcasted_iota(I32, (2 * u_rows, nkeys), 1)
    units = sb // u_rows

    for half, qh_ref in enumerate((q0_ref, q1_ref)):
        valid = lo if half == 0 else jnp.logical_not(lo)

        for bi, (window, r) in enumerate(DILATED_BRANCHES):
            assert window // r == span and units % r == 0
            shift = r.bit_length() - 1

            def unit(idx, c, r=r, shift=shift, first=(bi == 0)):
                rho = idx & (r - 1)
                u = idx >> shift
                qrow0 = rho + r * u_rows * u
                q = qh_ref[0, pl.ds(qrow0, u_rows, stride=r), :]
                qs = pltpu.roll(q, HEAD_DIM, 1)
                qa_src, qb_src = (q, qs) if half == 0 else (qs, q)
                qq = jnp.concatenate([jnp.where(valid, qa_src, 0.0), jnp.where(valid, qb_src, 0.0)],
                                     axis=0).astype(BF16)
                qclass0 = sbi * (sb // r) + u_rows * u
                kclass0 = jnp.maximum(qclass0 - span, 0)
                krows = pl.ds(rho + r * kclass0, nkeys, stride=r)
                kt = k_ref[0, krows, :].astype(BF16)
                vt = v_ref[0, krows, :].astype(BF16)
                s = lax.dot_general(qq, kt, (((1,), (1,)), ((), ())), preferred_element_type=F32)
                dist = rc + (qclass0 - kclass0)
                s = jnp.where(pltpu.bitcast(dist, jnp.uint32) <= jnp.uint32(span), s, NEG_BIG)
                m = jnp.max(s, axis=1, keepdims=True)
                p = jnp.exp(s - m)
                l = jnp.sum(p, axis=1, keepdims=True)
                acc = jnp.dot(p.astype(BF16), vt, preferred_element_type=F32)
                m = jnp.broadcast_to(m, (2 * u_rows, LANES))
                l = jnp.broadcast_to(l, (2 * u_rows, LANES))
                rows = (pl.ds(qrow0, u_rows, stride=r), pl.ds(sb + qrow0, u_rows, stride=r))
                if not first:
                    ld = lambda ref: jnp.concatenate([ref[rows[0], :], ref[rows[1], :]], axis=0)
                    m_old, l_old, acc_old = ld(m_s), ld(l_s), ld(acc_s)
                    m_new = jnp.maximum(m_old, m)
                    a_old, a_new = jnp.exp(m_old - m_new), jnp.exp(m - m_new)
                    l = a_old * l_old + a_new * l
                    acc = a_old * acc_old + a_new * acc
                    m = m_new
                for h2 in (0, 1):
                    sl = slice(h2 * u_rows, (h2 + 1) * u_rows)
                    m_s[rows[h2], :] = m[sl]
                    l_s[rows[h2], :] = l[sl]
                    acc_s[rows[h2], :] = acc[sl]
                return c

            lax.fori_loop(0, units, unit, 0, unroll=2)

        chunk = 2 * u_rows
        for c in range(sb // chunk):
            ra, rb = slice(c * chunk, (c + 1) * chunk), slice(sb + c * chunk, sb + (c + 1) * chunk)
            oa = acc_s[ra, :] / l_s[ra, :]
            ob = acc_s[rb, :] / l_s[rb, :]
            lo2 = jnp.concatenate([lo, lo], axis=0)
            if half == 0:
                out = jnp.where(lo2, oa, pltpu.roll(ob, HEAD_DIM, 1))
            else:
                out = jnp.where(lo2, pltpu.roll(oa, HEAD_DIM, 1), ob)
            o_ref[0, ra, half * LANES:(half + 1) * LANES] = out.astype(BF16)


def _attention_window(q, k, v):
    b, l, _ = q.shape
    sb = ATT_SUPER
    max_dil = max(d for _, d in DILATED_BRANCHES)
    assert l % sb == 0 and l // max_dil >= ATT_UNIT + ATT_SPAN
    qspec = lambda h: pl.BlockSpec((1, sb, LANES), lambda bi, pi, si: (bi, si, 2 * pi + h))
    kspec = pl.BlockSpec((1, l, LANES), lambda bi, pi, si: (bi, 0, pi))
    return pl.pallas_call(
        _attn_window_kernel,
        grid=(b, KV_WIDTH // LANES, l // sb),
        in_specs=[qspec(0), qspec(1), kspec, kspec],
        out_specs=pl.BlockSpec((1, sb, 2 * LANES), lambda bi, pi, si: (bi, si, pi)),
        out_shape=jax.ShapeDtypeStruct((b, l, ATT_WIDTH), BF16),
        scratch_shapes=[pltpu.VMEM((2 * sb, LANES), F32)] * 3,
        compiler_params=_cparams(("parallel", "parallel", "arbitrary")),
        name="window_attn",
    )(q, q, k, v)


def _ssd_kernel(xbc_ref, dt_ref, z_ref, ssm0_ref, conv0_ref, cw_ref, cb_ref, dtb_ref, alog_ref, dsk_ref,
                nw_ref, ex_ref, y_ref, ssm_ref, conv_ref, xpad, dtpad, s_scr, *, n_valid):
    q = SSD_CHUNK
    lb = xbc_ref.shape[1]
    ci = pl.program_id(1)
    nc = pl.num_programs(1)
    gw = SSD_WIDTH // SSD_GROUPS
    hpg = SSD_HEADS // SSD_GROUPS

    @pl.when(ci == 0)
    def _():
        xpad[0:SUBLANES, :] = jnp.zeros((SUBLANES, CONV_DIM), F32)
        xpad[SUBLANES - (CONV_W - 1):SUBLANES, :] = conv0_ref[0]
        for g in range(SSD_GROUPS):
            s_scr[g] = jnp.transpose(ssm0_ref[0, g * hpg:(g + 1) * hpg].reshape(gw, SSD_STATE))

    xpad[SUBLANES:SUBLANES + lb, :] = xbc_ref[0]
    dtpad[0:lb, :] = dt_ref[0]
    if lb < q:
        xpad[SUBLANES + lb:SUBLANES + q, :] = jnp.zeros((q - lb, CONV_DIM), F32)
        dtpad[lb:q, :] = jnp.zeros((q - lb, LANES), F32)

    conv = cb_ref[...]
    for k in range(CONV_W):
        sh = CONV_W - 1 - k
        conv = conv + xpad[SUBLANES - sh:SUBLANES - sh + q, :] * cw_ref[k:k + 1, :]
    act = _silu(conv)

    @pl.when(ci == nc - 1)
    def _():
        conv_ref[0] = xpad[SUBLANES + n_valid - (CONV_W - 1):SUBLANES + n_valid, :]

    xpad[0:SUBLANES, :] = xpad[q:q + SUBLANES, :]

    xs = act[:, :SSD_WIDTH]
    row = lax.broadcasted_iota(I32, (q, LANES), 0)
    dtr = dtpad[...] + dtb_ref[...]
    dt = jnp.maximum(dtr, 0.0) + jnp.log(1.0 + jnp.exp(-jnp.abs(dtr)))
    dt = jnp.where(row < n_valid, dt, 0.0)
    a = -jnp.exp(alog_ref[...])
    da = dt * a
    r2 = lax.broadcasted_iota(I32, (q, q), 0)
    c2 = lax.broadcasted_iota(I32, (q, q), 1)
    causal = r2 >= c2
    a_cs = jnp.dot(causal.astype(F32), da, precision=HIGHEST, preferred_element_type=F32)
    a_cst = jnp.transpose(a_cs)
    a_last = a_cs[q - 1:q, :]
    ex = ex_ref[...]
    expand = lambda t: jnp.dot(t, ex, precision=HIGHEST, preferred_element_type=F32)
    dt_x = expand(dt)
    ea_x = expand(jnp.exp(a_cs))
    te_x = expand(jnp.exp(a_last - a_cs))
    cd_x = expand(jnp.broadcast_to(jnp.exp(a_last), (SUBLANES, LANES)))[0:1, :]
    xdt = xs * dt_x
    xdt_b = xdt.astype(BF16)
    xw_b = (xdt * te_x).astype(BF16)
    lo = lax.broadcasted_iota(I32, (q, LANES), 1) < SSD_HEAD_DIM

    ys = []
    for g in range(SSD_GROUPS):
        bm = act[:, SSD_WIDTH + g * SSD_STATE:SSD_WIDTH + (g + 1) * SSD_STATE]
        cm = act[:, SSD_WIDTH + (SSD_GROUPS + g) * SSD_STATE:SSD_WIDTH + (SSD_GROUPS + g + 1) * SSD_STATE]
        bm_b, cm_b = bm.astype(BF16), cm.astype(BF16)
        cb = lax.dot_general(cm_b, bm_b, (((1,), (1,)), ((), ())), preferred_element_type=F32)
        s_old = s_scr[g]
        y_off = jnp.dot(cm_b, s_old.astype(BF16), preferred_element_type=F32)
        for jp in range(hpg // 2):
            pair = g * (hpg // 2) + jp
            yp = []
            for hh in (0, 1):
                h = 2 * pair + hh
                seg = a_cs[:, h:h + 1] - a_cst[h:h + 1, :]
                dec = jnp.exp(jnp.where(causal, seg, NEG_BIG))
                mm = (cb * dec).astype(BF16)
                yp.append(jnp.dot(mm, xdt_b[:, pair * LANES:(pair + 1) * LANES], preferred_element_type=F32))
            ys.append(jnp.where(lo, yp[0], yp[1]) + y_off[:, jp * LANES:(jp + 1) * LANES]
                      * ea_x[:, pair * LANES:(pair + 1) * LANES])
        bmt_b = jnp.transpose(bm).astype(BF16)
        s_new = s_old * cd_x[:, g * gw:(g + 1) * gw] + jnp.dot(bmt_b, xw_b[:, g * gw:(g + 1) * gw],
                                                              preferred_element_type=F32)
        s_scr[g] = s_new

    y = jnp.concatenate(ys, axis=1) + dsk_ref[...] * xs

    @pl.when(ci == nc - 1)
    def _():
        for g in range(SSD_GROUPS):
            ssm_ref[0, g * hpg:(g + 1) * hpg] = jnp.transpose(s_scr[g]).reshape(hpg, SSD_HEAD_DIM, SSD_STATE)

    hg = y[:lb] * _silu(z_ref[0])
    outs = []
    for g in range(SSD_GROUPS):
        part = hg[:, g * gw:(g + 1) * gw]
        ms = jnp.mean(part * part, axis=1, keepdims=True)
        outs.append(part * lax.rsqrt(ms + NORM_EPS))
    y_ref[0] = (jnp.concatenate(outs, axis=1) * nw_ref[...]).astype(BF16)


def _ssd(xbc, dt_raw, z, ssm0, conv0, prm, lb, n_valid):
    b, l, _ = xbc.shape
    nc = l // lb
    cw, cbias, dtb, alog, dsk, nw, ex = prm
    full = lambda a: pl.BlockSpec(a.shape, lambda bi, ci: (0,) * a.ndim)
    seq = lambda w: pl.BlockSpec((1, lb, w), lambda bi, ci: (bi, ci, 0))
    return pl.pallas_call(
        functools.partial(_ssd_kernel, n_valid=n_valid),
        grid=(b, nc),
        in_specs=[seq(CONV_DIM), seq(LANES), seq(SSD_WIDTH),
                  pl.BlockSpec((1, SSD_HEADS, SSD_HEAD_DIM, SSD_STATE), lambda bi, ci: (bi, 0, 0, 0)),
                  pl.BlockSpec((1, CONV_W - 1, CONV_DIM), lambda bi, ci: (bi, 0, 0)),
                  full(cw), full(cbias), full(dtb), full(alog), full(dsk), full(nw), full(ex)],
        out_specs=(seq(SSD_WIDTH),
                   pl.BlockSpec((1, SSD_HEADS, SSD_HEAD_DIM, SSD_STATE), lambda bi, ci: (bi, 0, 0, 0)),
                   pl.BlockSpec((1, CONV_W - 1, CONV_DIM), lambda bi, ci: (bi, 0, 0))),
        out_shape=(jax.ShapeDtypeStruct((b, l, SSD_WIDTH), BF16),
                   jax.ShapeDtypeStruct((b, SSD_HEADS, SSD_HEAD_DIM, SSD_STATE), F32),
                   jax.ShapeDtypeStruct((b, CONV_W - 1, CONV_DIM), F32)),
        scratch_shapes=[pltpu.VMEM((SSD_CHUNK + 2 * SUBLANES, CONV_DIM), F32),
                        pltpu.VMEM((SSD_CHUNK, LANES), F32),
                        pltpu.VMEM((SSD_GROUPS, SSD_STATE, SSD_WIDTH // SSD_GROUPS), F32)],
        compiler_params=_cparams(("parallel", "arbitrary")),
        name="conv_ssd",
    )(xbc, dt_raw, z, ssm0, conv0, cw, cbias, dtb, alog, dsk, nw, ex)


def _layer_norm(r, g, b):
    mu = jnp.mean(r, axis=1, keepdims=True)
    d = r - mu
    var = jnp.mean(d * d, axis=1, keepdims=True)
    return d * lax.rsqrt(var + NORM_EPS) * g + b


def _mix_kernel(att_ref, ssd_ref, x_ref, wo_ref, g_ref, b_ref, h_ref):
    mix = jnp.dot(att_ref[...], wo_ref[0:ATT_WIDTH, :], preferred_element_type=F32)
    mix = mix + jnp.dot(ssd_ref[...], wo_ref[ATT_WIDTH:MIX_WIDTH, :], preferred_element_type=F32)
    h_ref[...] = _layer_norm(DEEPNORM_ALPHA * x_ref[...] + mix, g_ref[...], b_ref[...])


def _mix(att, ssd, x2d, wo_bf, g, b, tm):
    t = x2d.shape[0]
    row = lambda w: pl.BlockSpec((tm, w), lambda i: (i, 0))
    vec = pl.BlockSpec((1, D_MODEL), lambda i: (0, 0))
    return pl.pallas_call(
        _mix_kernel,
        grid=(t // tm,),
        in_specs=[row(ATT_WIDTH), row(SSD_WIDTH), row(D_MODEL),
                  pl.BlockSpec((MIX_WIDTH, D_MODEL), lambda i: (0, 0)), vec, vec],
        out_specs=row(D_MODEL),
        out_shape=jax.ShapeDtypeStruct((t, D_MODEL), F32),
        compiler_params=_cparams(("parallel",)),
        name="outproj_ln1",
    )(att, ssd, x2d, wo_bf, g, b)


def _router_kernel(h_ref, wr_ref, bias_ref, tope_ref, rank_ref, gt_ref, cnt_ref, gscr):
    tm = h_ref.shape[0]
    i = pl.program_id(0)
    per_group = N_EXPERTS // N_EXPERT_GROUPS

    @pl.when(i == 0)
    def _():
        cnt_ref[...] = jnp.zeros_like(cnt_ref)

    logits = lax.dot_general(wr_ref[...], h_ref[...], (((1,), (1,)), ((), ())),
                             precision=HIGHEST, preferred_element_type=F32)
    s = 1.0 / (1.0 + jnp.exp(-logits))
    sel = s + bias_ref[...]
    neg_inf = -jnp.inf
    e_iota = lax.broadcasted_iota(I32, (N_EXPERTS, tm), 0).astype(F32)
    g_iota = lax.broadcasted_iota(I32, (per_group, tm), 0).astype(F32)

    gscore = []
    for g in range(N_EXPERT_GROUPS):
        blk = sel[g * per_group:(g + 1) * per_group]
        m1 = jnp.max(blk, axis=0, keepdims=True)
        a1 = jnp.min(jnp.where(blk == m1, g_iota, float(per_group)), axis=0, keepdims=True)
        m2 = jnp.max(jnp.where(g_iota == a1, neg_inf, blk), axis=0, keepdims=True)
        gscore.append(m1 + m2)
    blocks = []
    for g in range(N_EXPERT_GROUPS):
        beaten = jnp.zeros((1, tm), F32)
        for o in range(N_EXPERT_GROUPS):
            if o < g:
                beaten = beaten + jnp.where(gscore[o] >= gscore[g], 1.0, 0.0)
            elif o > g:
                beaten = beaten + jnp.where(gscore[o] > gscore[g], 1.0, 0.0)
        blocks.append(jnp.where(beaten < float(TOPK_GROUPS), sel[g * per_group:(g + 1) * per_group], neg_inf))
    cand = jnp.concatenate(blocks, axis=0)

    tops, gsel = [], []
    onehot = jnp.zeros((N_EXPERTS, tm), F32)
    for _ in range(TOP_K):
        mx = jnp.max(cand, axis=0, keepdims=True)
        ix = jnp.min(jnp.where(cand == mx, e_iota, float(N_EXPERTS)), axis=0, keepdims=True)
        hit = e_iota == ix
        tops.append(ix)
        gsel.append(jnp.sum(jnp.where(hit, s, 0.0), axis=0, keepdims=True))
        onehot = jnp.where(hit, 1.0, onehot)
        cand = jnp.where(hit, neg_inf, cand)
    den = gsel[0]
    for k in range(1, TOP_K):
        den = den + gsel[k]

    t_r = lax.broadcasted_iota(I32, (tm, tm), 0)
    t_c = lax.broadcasted_iota(I32, (tm, tm), 1)
    before = (t_r < t_c).astype(BF16)
    oh_b = onehot.astype(BF16)
    base = cnt_ref[...] + jnp.dot(oh_b, before, preferred_element_type=F32)
    cnt_ref[...] = cnt_ref[...] + jnp.dot(oh_b, jnp.ones((tm, LANES), BF16), preferred_element_type=F32)

    gscr[...] = jnp.zeros_like(gscr)
    for k in range(TOP_K):
        tope_ref[k:k + 1, :] = tops[k].astype(I32)
        rank_ref[k:k + 1, :] = jnp.sum(jnp.where(e_iota == tops[k], base, 0.0), axis=0, keepdims=True).astype(I32)
        gscr[k:k + 1, :] = gsel[k] / den * ROUTED_SCALE
    gt_ref[...] = jnp.transpose(gscr[...])


def _router(h_all, wr_t, bias_b):
    t = h_all.shape[0]
    tm = ROUTER_TILE
    return pl.pallas_call(
        _router_kernel,
        grid=(t // tm,),
        in_specs=[pl.BlockSpec((tm, D_MODEL), lambda i: (i, 0)),
                  pl.BlockSpec((N_EXPERTS, D_MODEL), lambda i: (0, 0)),
                  pl.BlockSpec((N_EXPERTS, tm), lambda i: (0, 0))],
        out_specs=(pl.BlockSpec((TOP_K, tm), lambda i: (0, i)),
                   pl.BlockSpec((TOP_K, tm), lambda i: (0, i)),
                   pl.BlockSpec((tm, LANES), lambda i: (i, 0)),
                   pl.BlockSpec((N_EXPERTS, LANES), lambda i: (0, 0))),
        out_shape=(jax.ShapeDtypeStruct((TOP_K, t), I32),
                   jax.ShapeDtypeStruct((TOP_K, t), I32),
                   jax.ShapeDtypeStruct((t, LANES), F32),
                   jax.ShapeDtypeStruct((N_EXPERTS, LANES), F32)),
        scratch_shapes=[pltpu.VMEM((tm, LANES), F32)],
        compiler_params=_cparams(("arbitrary",)),
        name="router",
    )(h_all, wr_t, bias_b)


_PAD_SIZES = tuple(2 ** p for p in range(int(math.log2(EXPERT_BLOCK)) - 1, -1, -1))


ROW_SLAB = D_MODEL // LANES


def _slab(ref, row, n=1):
    return ref.at[pl.ds(pl.multiple_of(row * ROW_SLAB, ROW_SLAB), n * ROW_SLAB)]


def _padfill_kernel(pstart_ref, pcnt_ref, tail_ref, xs_ref, zero_scr, sem):
    zero_scr[...] = jnp.zeros_like(zero_scr)
    half = EXPERT_BLOCK // 2
    tail_copy = lambda j: pltpu.make_async_copy(_slab(zero_scr, 0, half), _slab(xs_ref, tail_ref[0] + j * half, half),
                                                sem)

    def tail_start(j, c):
        tail_copy(j).start()
        return c

    def tail_wait(j, c):
        tail_copy(j).wait()
        return c

    lax.fori_loop(0, tail_ref[1], tail_start, 0)

    def copies(e):
        base = pstart_ref[e]
        cnt = pcnt_ref[e]
        out = []
        for sz in _PAD_SIZES:
            out.append(((cnt & sz) != 0, pltpu.make_async_copy(_slab(zero_scr, 0, sz), _slab(xs_ref, base, sz), sem)))
            base = base + (cnt & sz)
        return out

    def start(e, c):
        for pred, cp in copies(e):
            @pl.when(pred)
            def _():
                cp.start()
        return c

    def wait(e, c):
        for pred, cp in copies(e):
            @pl.when(pred)
            def _():
                cp.wait()
        return c

    lax.fori_loop(0, N_EXPERTS, start, 0)
    lax.fori_loop(0, N_EXPERTS, wait, 0)
    lax.fori_loop(0, tail_ref[1], tail_wait, 0)


def _padfill(pad_start, pad_cnt, tail, n_rows):
    return pl.pallas_call(
        _padfill_kernel,
        grid_spec=pltpu.PrefetchScalarGridSpec(
            num_scalar_prefetch=3, grid=(1,), in_specs=[],
            out_specs=pl.BlockSpec(memory_space=pl.ANY),
            scratch_shapes=[pltpu.VMEM((EXPERT_BLOCK // 2 * ROW_SLAB, LANES), F32), pltpu.SemaphoreType.DMA]),
        out_shape=jax.ShapeDtypeStruct((n_rows * ROW_SLAB, LANES), F32),
        compiler_params=_cparams(("arbitrary",)),
        name="dispatch_padfill",
    )(pad_start, pad_cnt, tail)


def _dest_kernel(tope_ref, rank_ref, pstart_ref, dest_ref):
    tm = tope_ref.shape[1]
    e_iota = lax.broadcasted_iota(I32, (N_EXPERTS, tm), 0)
    ps = pstart_ref[...]
    for k in range(TOP_K):
        base = jnp.sum(jnp.where(e_iota == tope_ref[k:k + 1, :], ps, 0.0), axis=0, keepdims=True)
        dest_ref[k:k + 1, :] = base.astype(I32) + rank_ref[k:k + 1, :]


def _dest(top_e, rank, pstarts):
    t = top_e.shape[1]
    tm = DISPATCH_TILE
    blk = pl.BlockSpec((TOP_K, tm), lambda i: (0, i))
    return pl.pallas_call(
        _dest_kernel,
        grid=(t // tm,),
        in_specs=[blk, blk, pl.BlockSpec((N_EXPERTS, tm), lambda i: (0, 0))],
        out_specs=blk,
        out_shape=jax.ShapeDtypeStruct((TOP_K, t), I32),
        compiler_params=_cparams(("parallel",)),
        name="dispatch_dest",
    )(top_e, rank, jnp.broadcast_to(pstarts.astype(F32)[:, None], (N_EXPERTS, tm)))


def _dest_tiles(dest, tm):
    t = dest.shape[1]
    return jnp.transpose(dest.reshape(TOP_K, t // tm, tm), (1, 0, 2)).reshape(t // tm, TOP_K * tm)


def _dispatch_kernel(dest_ref, h_ref, xs_in_ref, xs_ref, idx_smem, isem, sem):
    del xs_in_ref
    i = pl.program_id(0)
    tm = DISPATCH_TILE
    cp = pltpu.make_async_copy(dest_ref.at[i], idx_smem, isem)
    cp.start()
    cp.wait()

    def row_copy(t, k):
        return pltpu.make_async_copy(_slab(h_ref, t), _slab(xs_ref, idx_smem[k * tm + t]), sem)

    def start(t, c):
        for k in range(TOP_K):
            row_copy(t, k).start(priority=k % 2)
        return c

    def wait(t, c):
        for k in range(TOP_K):
            row_copy(t, k).wait()
        return c

    lax.fori_loop(0, tm, start, 0)
    lax.fori_loop(0, tm, wait, 0)


def _dispatch(dest_tiles, h_slab, xs):
    tm = DISPATCH_TILE
    return pl.pallas_call(
        _dispatch_kernel,
        grid=(dest_tiles.shape[0],),
        in_specs=[pl.BlockSpec(dest_tiles.shape, lambda i: (0, 0)),
                  pl.BlockSpec((tm * ROW_SLAB, LANES), lambda i: (i, 0)),
                  pl.BlockSpec(memory_space=pl.ANY)],
        out_specs=pl.BlockSpec(memory_space=pl.ANY),
        out_shape=jax.ShapeDtypeStruct(xs.shape, xs.dtype),
        scratch_shapes=[pltpu.SMEM((tm * TOP_K,), I32), pltpu.SemaphoreType.DMA, pltpu.SemaphoreType.DMA],
        input_output_aliases={2: 0},
        compiler_params=_cparams(("arbitrary",)),
        name="dispatch_rows",
    )(dest_tiles, h_slab, xs)


def _expert_kernel(bi_ref, be_ref, nu_ref, x_ref, wg_ref, wu_ref, wd_ref, y_ref, wg_b, wu_b, wd_b):
    del bi_ref
    i = pl.program_id(0)

    @pl.when(i < nu_ref[0])
    def _():
        prev = be_ref[jnp.maximum(i - 1, 0)]

        @pl.when(jnp.logical_or(i == 0, be_ref[i] != prev))
        def _():
            wg_b[...] = wg_ref[0].astype(BF16)
            wu_b[...] = wu_ref[0].astype(BF16)
            wd_b[...] = wd_ref[0].astype(BF16)

        xb = jnp.concatenate([x_ref[pl.ds(j, EXPERT_BLOCK, stride=ROW_SLAB), :] for j in range(ROW_SLAB)],
                             axis=1).astype(BF16)
        g = jnp.dot(xb, wg_b[...], preferred_element_type=F32)
        u = jnp.dot(xb, wu_b[...], preferred_element_type=F32)
        a = (_silu(g) * u).astype(BF16)
        y = jnp.dot(a, wd_b[...], preferred_element_type=F32)
        for j in range(ROW_SLAB):
            y_ref[pl.ds(j, EXPERT_BLOCK, stride=ROW_SLAB), :] = y[:, j * LANES:(j + 1) * LANES]

    @pl.when(i >= nu_ref[0])
    def _():
        y_ref[...] = jnp.zeros_like(y_ref)


def _experts(blk_idx, blk_exp, n_used, xs, wg, wu, wd):
    n_rows = xs.shape[0] // ROW_SLAB
    nb = n_rows // EXPERT_BLOCK
    rows = pl.BlockSpec((EXPERT_BLOCK * ROW_SLAB, LANES), lambda i, bi, be, nu: (bi[i], 0))
    out_rows = pl.BlockSpec((EXPERT_BLOCK * ROW_SLAB, LANES), lambda i, bi, be, nu: (i, 0))
    return pl.pallas_call(
        _expert_kernel,
        grid_spec=pltpu.PrefetchScalarGridSpec(
            num_scalar_prefetch=3, grid=(nb,),
            in_specs=[rows,
                      pl.BlockSpec((1, D_MODEL, EXPERT_FF), lambda i, bi, be, nu: (be[i], 0, 0)),
                      pl.BlockSpec((1, D_MODEL, EXPERT_FF), lambda i, bi, be, nu: (be[i], 0, 0)),
                      pl.BlockSpec((1, EXPERT_FF, D_MODEL), lambda i, bi, be, nu: (be[i], 0, 0))],
            out_specs=out_rows,
            scratch_shapes=[pltpu.VMEM((D_MODEL, EXPERT_FF), BF16), pltpu.VMEM((D_MODEL, EXPERT_FF), BF16),
                            pltpu.VMEM((EXPERT_FF, D_MODEL), BF16)]),
        out_shape=jax.ShapeDtypeStruct(xs.shape, F32),
        compiler_params=_cparams(("arbitrary",)),
        name="expert_ffn",
    )(blk_idx, blk_exp, n_used, xs, wg, wu, wd)


def _combine_kernel(dest_ref, h_ref, gt_ref, ys_ref, sg_ref, su_ref, sd_ref, g_ref, b_ref, o_ref,
                    idx_smem, gbuf, isem, sem):
    i = pl.program_id(0)
    tm = h_ref.shape[0]
    cp = pltpu.make_async_copy(dest_ref.at[i], idx_smem, isem)
    cp.start()
    cp.wait()

    def row_copy(t, k):
        return pltpu.make_async_copy(_slab(ys_ref, idx_smem[k * tm + t]), _slab(gbuf.at[k], t), sem)

    def start(t, c):
        for k in range(TOP_K):
            row_copy(t, k).start(priority=k % 2)
        return c

    def wait(t, c):
        for k in range(TOP_K):
            row_copy(t, k).wait()
        return c

    lax.fori_loop(0, tm, start, 0)
    h = h_ref[...]
    hb = h.astype(BF16)
    g = jnp.dot(hb, sg_ref[...], preferred_element_type=F32)
    u = jnp.dot(hb, su_ref[...], preferred_element_type=F32)
    f = jnp.dot((_silu(g) * u).astype(BF16), sd_ref[...], preferred_element_type=F32)
    lax.fori_loop(0, tm, wait, 0)
    gt = gt_ref[...]
    cols = []
    for j in range(ROW_SLAB):
        fj = f[:, j * LANES:(j + 1) * LANES]
        for k in range(TOP_K):
            fj = fj + gbuf[k, pl.ds(j, tm, stride=ROW_SLAB), :] * gt[:, k:k + 1]
        cols.append(fj)
    f = jnp.concatenate(cols, axis=1)
    o_ref[...] = _layer_norm(DEEPNORM_ALPHA * h + f, g_ref[...], b_ref[...])


def _combine(dest_tiles, h_all, gates_t, ys, sg_bf, su_bf, sd_bf, g, b):
    t = h_all.shape[0]
    tm = COMBINE_TILE
    vec = pl.BlockSpec((1, D_MODEL), lambda i: (0, 0))
    return pl.pallas_call(
        _combine_kernel,
        grid=(t // tm,),
        in_specs=[pl.BlockSpec(dest_tiles.shape, lambda i: (0, 0)),
                  pl.BlockSpec((tm, D_MODEL), lambda i: (i, 0)),
                  pl.BlockSpec((tm, LANES), lambda i: (i, 0)),
                  pl.BlockSpec(memory_space=pl.ANY),
                  pl.BlockSpec((D_MODEL, SHARED_FF), lambda i: (0, 0)),
                  pl.BlockSpec((D_MODEL, SHARED_FF), lambda i: (0, 0)),
                  pl.BlockSpec((SHARED_FF, D_MODEL), lambda i: (0, 0)), vec, vec],
        out_specs=pl.BlockSpec((tm, D_MODEL), lambda i: (i, 0)),
        out_shape=jax.ShapeDtypeStruct((t, D_MODEL), F32),
        scratch_shapes=[pltpu.SMEM((tm * TOP_K,), I32), pltpu.VMEM((TOP_K, tm * ROW_SLAB, LANES), F32),
                        pltpu.SemaphoreType.DMA, pltpu.SemaphoreType.DMA],
        compiler_params=_cparams(("arbitrary",)),
        name="combine_ln2",
    )(dest_tiles, h_all, gates_t, ys, sg_bf, su_bf, sd_bf, g, b)


def _moe(h_all, wr, rbias, wg, wu, wd, sg, su, sd, g2, b2):
    t = h_all.shape[0]
    top_e, rank, gates_t, cnt = _router(h_all, jnp.transpose(wr),
                                        jnp.broadcast_to(rbias[:, None], (N_EXPERTS, ROUTER_TILE)))
    counts = cnt[:, 0].astype(I32)
    pcounts = (counts + EXPERT_BLOCK - 1) // EXPERT_BLOCK * EXPERT_BLOCK
    pends = jnp.cumsum(pcounts)
    pstarts = pends - pcounts
    n_rows = (t * TOP_K // EXPERT_BLOCK + N_EXPERTS - 1) * EXPERT_BLOCK
    nb = n_rows // EXPERT_BLOCK
    n_used = (pends[-1] // EXPERT_BLOCK).astype(I32)
    blk_idx = jnp.minimum(jnp.arange(nb, dtype=I32), n_used - 1)
    blk_exp = jnp.minimum(jnp.sum((pends[None, :] <= (blk_idx * EXPERT_BLOCK)[:, None]).astype(I32), axis=1),
                          N_EXPERTS - 1)
    dest = _dest(top_e, rank, pstarts)
    tail = jnp.stack([pends[-1], (n_rows - pends[-1]) // (EXPERT_BLOCK // 2)]).astype(I32)
    xs = _padfill((pstarts + counts).astype(I32), (pcounts - counts).astype(I32), tail, n_rows)
    xs = _dispatch(_dest_tiles(dest, DISPATCH_TILE), h_all.reshape(t * ROW_SLAB, LANES), xs)
    ys = _experts(blk_idx, blk_exp, n_used.reshape(1), xs, wg, wu, wd)
    return _combine(_dest_tiles(dest, COMBINE_TILE), h_all, gates_t, ys,
                    sg.astype(BF16), su.astype(BF16), sd.astype(BF16), g2[None, :], b2[None, :])


def _expand_matrix():
    h = np.arange(LANES)[:, None]
    c = np.arange(SSD_WIDTH)[None, :]
    return jnp.asarray((c // SSD_HEAD_DIM == h).astype(np.float32))


def kernel(x_prompt, x_sample, cache_k, cache_v, state_ssm, state_conv, w_in, conv_w, conv_b, dt_bias, a_log, d_skip, ssd_norm_w, w_out, ln1_g, ln1_b, w_router, router_bias, w_exp_gate, w_exp_up, w_exp_down, w_sh_gate, w_sh_up, w_sh_down, ln2_g, ln2_b):
    bp, lp, _ = x_prompt.shape
    bs, ls, _ = x_sample.shape
    win = cache_k.shape[2]
    keep = min(MAX_WINDOW, lp)
    assert lp % SSD_CHUNK == 0 and ls <= SUBLANES and win % KEY_TILE == 0 and win >= MAX_WINDOW

    w_bf = jnp.pad(w_in[0], ((0, 0), (0, IN_COLS_PAD - IN_COLS))).astype(BF16)
    wo_bf = w_out[0].astype(BF16)
    pad_l = lambda v: jnp.pad(v, (0, LANES - v.shape[0]))[None, :]
    ssd_prm = (conv_w[0], conv_b[0][None, :], pad_l(dt_bias[0]), pad_l(a_log[0]),
               jnp.repeat(d_skip[0], SSD_HEAD_DIM)[None, :], ssd_norm_w[0][None, :], _expand_matrix())

    tp = bp * lp
    tm_p = 256 if lp % 256 == 0 else SSD_CHUNK
    tm_mix = 512 if tp % 512 == 0 else SSD_CHUNK
    cos_p, sin_p = _rope_tables(jnp.arange(lp, dtype=F32))
    xp2 = x_prompt.reshape(tp, D_MODEL)
    q, kf, vf, z, xbc, dtr = _inproj(xp2, w_bf, cos_p, sin_p, tm_p, lp // tm_p)
    att = _attention_window(q.reshape(bp, lp, ATT_WIDTH), kf.reshape(bp, lp, KV_WIDTH), vf.reshape(bp, lp, KV_WIDTH))
    ssd_y, ssm_p, conv_p = _ssd(xbc.reshape(bp, lp, CONV_DIM), dtr.reshape(bp, lp, LANES),
                                z.reshape(bp, lp, SSD_WIDTH),
                                jnp.zeros((bp, SSD_HEADS, SSD_HEAD_DIM, SSD_STATE), F32),
                                jnp.zeros((bp, CONV_W - 1, CONV_DIM), F32), ssd_prm, SSD_CHUNK, SSD_CHUNK)
    h_p = _mix(att.reshape(tp, ATT_WIDTH), ssd_y.reshape(tp, SSD_WIDTH), xp2, wo_bf, ln1_g, ln1_b, tm_mix)

    ts = bs * ls
    pos_s = (PAST_LEN + jnp.arange(ls, dtype=jnp.int32)).astype(F32)
    cos_s, sin_s = _rope_tables(jnp.tile(pos_s, bs))
    xs2 = x_sample.reshape(ts, D_MODEL)
    q_s, kf_s, vf_s, z_s, xbc_s, dtr_s = _inproj(xs2, w_bf, cos_s, sin_s, ts, 1)
    rows8 = lambda a, w: jnp.pad(a.reshape(bs, ls, w), ((0, 0), (0, SUBLANES - ls), (0, 0)))
    att_s = _attention_decode(rows8(q_s, ATT_WIDTH), cache_k[0].reshape(bs, win, KV_WIDTH),
                              cache_v[0].reshape(bs, win, KV_WIDTH), rows8(kf_s, KV_WIDTH),
                              rows8(vf_s, KV_WIDTH))[:, :ls]
    ssd_s, ssm_s, conv_s = _ssd(rows8(xbc_s, CONV_DIM), rows8(dtr_s, LANES), rows8(z_s, SSD_WIDTH),
                                state_ssm[0], state_conv[0], ssd_prm, SUBLANES, ls)
    h_s = _mix(att_s.reshape(ts, ATT_WIDTH), ssd_s[:, :ls].reshape(ts, SSD_WIDTH), xs2, wo_bf, ln1_g, ln1_b, ts)

    t_all = tp + ts
    t_pad = -(-t_all // DISPATCH_TILE) * DISPATCH_TILE
    h_all = jnp.concatenate([h_p, h_s, jnp.zeros((t_pad - t_all, D_MODEL), F32)], axis=0)
    y_all = _moe(h_all, w_router[0], router_bias[0], w_exp_gate[0], w_exp_up[0], w_exp_down[0],
                 w_sh_gate[0], w_sh_up[0], w_sh_down[0], ln2_g[0], ln2_b[0])

    kv5 = lambda a, b, l: a.reshape(1, b, l, N_KV_HEADS, HEAD_DIM)
    return (y_all[:tp].reshape(bp, lp, D_MODEL), y_all[tp:t_all].reshape(bs, ls, D_MODEL),
            kv5(kf.reshape(bp, lp, KV_WIDTH)[:, lp - keep:], bp, keep),
            kv5(vf.reshape(bp, lp, KV_WIDTH)[:, lp - keep:], bp, keep),
            ssm_p[None], conv_p[None],
            kv5(kf_s, bs, ls), kv5(vf_s, bs, ls), ssm_s[None], conv_s[None])
```

```python
import functools
import math

import jax
import jax.numpy as jnp
import numpy as np
from jax import lax
from jax.experimental import pallas as pl
from jax.experimental.pallas import tpu as pltpu

F32 = jnp.float32
BF16 = jnp.bfloat16
I32 = jnp.int32

D_MODEL = 1024
PAST_LEN = 16384
HEAD_DIM = 64
N_ATT_HEADS = 16
N_KV_HEADS = 8
ATT_WIDTH = N_ATT_HEADS * HEAD_DIM
KV_WIDTH = N_KV_HEADS * HEAD_DIM
DILATED_BRANCHES = ((128, 1), (512, 4), (2048, 16))
MAX_WINDOW = 2048
ROPE_THETA = 10000.0
SSD_HEADS = 16
SSD_HEAD_DIM = 64
SSD_WIDTH = SSD_HEADS * SSD_HEAD_DIM
SSD_GROUPS = 2
SSD_STATE = 128
SSD_CHUNK = 128
CONV_W = 4
CONV_DIM = SSD_WIDTH + 2 * SSD_GROUPS * SSD_STATE
MIX_WIDTH = ATT_WIDTH + SSD_WIDTH
IN_COLS = ATT_WIDTH + 2 * KV_WIDTH + SSD_WIDTH + CONV_DIM + SSD_HEADS
N_EXPERTS = 256
TOP_K = 8
N_EXPERT_GROUPS = 8
TOPK_GROUPS = 4
EXPERT_FF = 256
SHARED_FF = 256
ROUTED_SCALE = 2.5
DEPTH = 1
DEEPNORM_ALPHA = (2.0 * DEPTH) ** 0.25
NORM_EPS = 1e-5

LANES = 128
SUBLANES = 8
VMEM_LIMIT = 56 * 1024 * 1024

IN_COLS_PAD = ATT_WIDTH + 2 * KV_WIDTH + SSD_WIDTH + CONV_DIM + LANES
KEY_TILE = 128
EXPERT_BLOCK = 512
EXPERT_CHUNKS = 1
ROUTER_TILE = 128
DISPATCH_TILE = 256
COMBINE_TILE = 128
ROW_SLAB = D_MODEL // LANES
NEG_BIG = -1e30
HIGHEST = lax.Precision.HIGHEST


def _cparams(sem):
    return pltpu.CompilerParams(dimension_semantics=sem, vmem_limit_bytes=VMEM_LIMIT)


def _silu(x):
    return x * (1.0 / (1.0 + jnp.exp(-x)))


def _inproj_kernel(x_ref, w_ref, cq_ref, sq_ref, ck_ref, sk_ref,
                   q_ref, kf_ref, vf_ref, z_ref, xbc_ref, dt_ref):
    tm = x_ref.shape[0]
    xb = x_ref[...].astype(BF16)
    lane = lax.broadcasted_iota(I32, (tm, LANES), 1)
    first_half = (lane % HEAD_DIM) < (HEAD_DIM // 2)

    def rope(a, c, s):
        partner = jnp.where(first_half, pltpu.roll(a, LANES - HEAD_DIM // 2, 1), pltpu.roll(a, HEAD_DIM // 2, 1))
        return a * c + partner * s

    c0 = 0
    acc = jnp.dot(xb, w_ref[:, c0:c0 + ATT_WIDTH], preferred_element_type=F32)
    cq, sq = cq_ref[...], sq_ref[...]
    for j in range(ATT_WIDTH // LANES):
        q_ref[:, j * LANES:(j + 1) * LANES] = rope(acc[:, j * LANES:(j + 1) * LANES], cq, sq)
    c0 += ATT_WIDTH
    acc = jnp.dot(xb, w_ref[:, c0:c0 + KV_WIDTH], preferred_element_type=F32)
    ck, sk = ck_ref[...], sk_ref[...]
    for j in range(KV_WIDTH // LANES):
        kf_ref[:, j * LANES:(j + 1) * LANES] = rope(acc[:, j * LANES:(j + 1) * LANES], ck, sk)
    c0 += KV_WIDTH
    vf_ref[...] = jnp.dot(xb, w_ref[:, c0:c0 + KV_WIDTH], preferred_element_type=F32)
    c0 += KV_WIDTH
    z_ref[...] = jnp.dot(xb, w_ref[:, c0:c0 + SSD_WIDTH], preferred_element_type=F32)
    c0 += SSD_WIDTH
    xbc_ref[...] = jnp.dot(xb, w_ref[:, c0:c0 + CONV_DIM], preferred_element_type=F32)
    c0 += CONV_DIM
    dt_ref[...] = jnp.dot(xb, w_ref[:, c0:c0 + LANES], preferred_element_type=F32)


def _rope_tables(pos):
    half = HEAD_DIM // 2
    inv = ROPE_THETA ** (-jnp.arange(half, dtype=F32) / half)
    ang = pos[:, None] * inv[None, :]
    cos, sin = jnp.cos(ang), jnp.sin(ang)
    c = jnp.concatenate([cos, cos, cos, cos], axis=1)
    s = jnp.concatenate([-sin, sin, -sin, sin], axis=1)
    return c, s


def _inproj(x2d, w_bf, cos_t, sin_t, tm, tiles_per_seq):
    t = x2d.shape[0]
    scale = HEAD_DIM ** -0.5
    tab = pl.BlockSpec((tm, LANES), lambda i: (i % tiles_per_seq, 0))
    row = lambda w: pl.BlockSpec((tm, w), lambda i: (i, 0))
    outs = (
        jax.ShapeDtypeStruct((t, ATT_WIDTH), F32),
        jax.ShapeDtypeStruct((t, KV_WIDTH), F32),
        jax.ShapeDtypeStruct((t, KV_WIDTH), F32),
        jax.ShapeDtypeStruct((t, SSD_WIDTH), F32),
        jax.ShapeDtypeStruct((t, CONV_DIM), F32),
        jax.ShapeDtypeStruct((t, LANES), F32),
    )
    return pl.pallas_call(
        _inproj_kernel,
        grid=(t // tm,),
        in_specs=[row(D_MODEL), pl.BlockSpec((D_MODEL, IN_COLS_PAD), lambda i: (0, 0)), tab, tab, tab, tab],
        out_specs=(row(ATT_WIDTH), row(KV_WIDTH), row(KV_WIDTH), row(SSD_WIDTH), row(CONV_DIM), row(LANES)),
        out_shape=outs,
        compiler_params=_cparams(("parallel",)),
        name="inproj",
    )(x2d, w_bf, cos_t * scale, sin_t * scale, cos_t, sin_t)


def _branch_weight(d):
    w = np.zeros(d.shape, np.float32)
    for window, dil in DILATED_BRANCHES:
        w += ((d >= 0) & (d <= window) & (d % dil == 0)).astype(np.float32)
    return w


def _attn_decode_kernel(q_ref, kc_ref, vc_ref, kn_ref, vn_ref, wc_ref, wn_ref, o_ref, ktail, vtail):
    nq = q_ref.shape[1]
    lane = lax.broadcasted_iota(I32, (nq, LANES), 1)
    lo = lane < HEAD_DIM
    ktail[...] = jnp.zeros_like(ktail)
    vtail[...] = jnp.zeros_like(vtail)
    ktail[0:nq, :] = kn_ref[0]
    vtail[0:nq, :] = vn_ref[0]
    wc, wn = wc_ref[...], wn_ref[...]
    nt = (((1,), (1,)), ((), ()))
    for g in range(N_KV_HEADS):
        pair, half = divmod(g, 2)
        cols = slice(pair * LANES, (pair + 1) * LANES)
        q = q_ref[0, :, g * LANES:(g + 1) * LANES]
        qs = pltpu.roll(q, HEAD_DIM, 1)
        valid = lo if half == 0 else jnp.logical_not(lo)
        qa_src, qb_src = (q, qs) if half == 0 else (qs, q)
        qq = jnp.concatenate([jnp.where(valid, qa_src, 0.0), jnp.where(valid, qb_src, 0.0)], axis=0).astype(BF16)
        s_c = lax.dot_general(qq, kc_ref[0, :, cols].astype(BF16), nt, preferred_element_type=F32)
        s_n = lax.dot_general(qq, ktail[:, cols].astype(BF16), nt, preferred_element_type=F32)
        s_c = jnp.where(wc > 0.0, s_c, NEG_BIG)
        s_n = jnp.where(wn > 0.0, s_n, NEG_BIG)
        m = jnp.maximum(jnp.max(s_c, axis=1, keepdims=True), jnp.max(s_n, axis=1, keepdims=True))
        p_c = jnp.exp(s_c - m) * wc
        p_n = jnp.exp(s_n - m) * wn
        l = jnp.sum(p_c, axis=1, keepdims=True) + jnp.sum(p_n, axis=1, keepdims=True)
        acc = jnp.dot(p_c.astype(BF16), vc_ref[0, :, cols].astype(BF16), preferred_element_type=F32)
        acc = acc + jnp.dot(p_n.astype(BF16), vtail[:, cols].astype(BF16), preferred_element_type=F32)
        o = acc / l
        oa, ob = o[:nq], o[nq:]
        if half == 0:
            out = jnp.where(lo, oa, pltpu.roll(ob, HEAD_DIM, 1))
        else:
            out = jnp.where(lo, pltpu.roll(oa, HEAD_DIM, 1), ob)
        o_ref[0, :, g * LANES:(g + 1) * LANES] = out.astype(BF16)


def _attention_decode(q, k_cache, v_cache, k_new, v_new):
    b, nq, _ = q.shape
    win = k_cache.shape[1]
    i = np.arange(nq)[:, None]
    wc = _branch_weight(win + i - np.arange(win)[None, :])
    wn = _branch_weight(i - np.arange(KEY_TILE)[None, :])
    wn[:, nq:] = 0.0
    stack = lambda w: jnp.asarray(np.concatenate([w, w], axis=0))
    seq = lambda rows, w: pl.BlockSpec((1, rows, w), lambda bi: (bi, 0, 0))
    full = lambda shp: pl.BlockSpec(shp, lambda bi: (0, 0))
    return pl.pallas_call(
        _attn_decode_kernel,
        grid=(b,),
        in_specs=[seq(nq, ATT_WIDTH), seq(win, KV_WIDTH), seq(win, KV_WIDTH), seq(nq, KV_WIDTH), seq(nq, KV_WIDTH),
                  full((2 * nq, win)), full((2 * nq, KEY_TILE))],
        out_specs=seq(nq, ATT_WIDTH),
        out_shape=jax.ShapeDtypeStruct((b, nq, ATT_WIDTH), BF16),
        scratch_shapes=[pltpu.VMEM((KEY_TILE, KV_WIDTH), F32), pltpu.VMEM((KEY_TILE, KV_WIDTH), F32)],
        compiler_params=_cparams(("arbitrary",)),
        name="decode_attn",
    )(q, k_cache, v_cache, k_new, v_new, stack(wc), stack(wn))


ATT_SUPER = 2048
ATT_UNIT = 128
ATT_SPAN = max(w // d for w, d in DILATED_BRANCHES)


def _attn_window_kernel(q0_ref, q1_ref, k_ref, v_ref, o_ref, acc_s, m_s, l_s):
    sb = q0_ref.shape[1]
    sbi = pl.program_id(2)
    u_rows, span = ATT_UNIT, ATT_SPAN
    nkeys = u_rows + span
    lane = lax.broadcasted_iota(I32, (u_rows, LANES), 1)
    lo = lane < HEAD_DIM
    row_i = lax.broadcasted_iota(I32, (2 * u_rows, nkeys), 0) & (u_rows - 1)
    rc = row_i - lax.broadcasted_iota(I32, (2 * u_rows, nkeys), 1)
    units = sb // u_rows

    for half, qh_ref in enumerate((q0_ref, q1_ref)):
        valid = lo if half == 0 else jnp.logical_not(lo)

        for bi, (window, r) in enumerate(DILATED_BRANCHES):
            assert window // r == span and units % r == 0
            shift = r.bit_length() - 1

            def unit(idx, c, r=r, shift=shift, first=(bi == 0)):
                rho = idx & (r - 1)
                u = idx >> shift
                qrow0 = rho + r * u_rows * u
                q = qh_ref[0, pl.ds(qrow0, u_rows, stride=r), :]
                qs = pltpu.roll(q, HEAD_DIM, 1)
                qa_src, qb_src = (q, qs) if half == 0 else (qs, q)
                qq = jnp.concatenate([jnp.where(valid, qa_src, 0.0), jnp.where(valid, qb_src, 0.0)],
                                     axis=0).astype(BF16)
                qclass0 = sbi * (sb // r) + u_rows * u
                kclass0 = jnp.maximum(qclass0 - span, 0)
                krows = pl.ds(rho + r * kclass0, nkeys, stride=r)
                kt = k_ref[0, krows, :].astype(BF16)
                vt = v_ref[0, krows, :].astype(BF16)
                s = lax.dot_general(qq, kt, (((1,), (1,)), ((), ())), preferred_element_type=F32)
                dist = rc + (qclass0 - kclass0)
                s = jnp.where(pltpu.bitcast(dist, jnp.uint32) <= jnp.uint32(span), s, NEG_BIG)
                m = jnp.max(s, axis=1, keepdims=True)
                p = jnp.exp(s - m)
                l = jnp.sum(p, axis=1, keepdims=True)
                acc = jnp.dot(p.astype(BF16), vt, preferred_element_type=F32)
                m = jnp.broadcast_to(m, (2 * u_rows, LANES))
                l = jnp.broadcast_to(l, (2 * u_rows, LANES))
                rows = (pl.ds(qrow0, u_rows, stride=r), pl.ds(sb + qrow0, u_rows, stride=r))
                if not first:
                    ld = lambda ref: jnp.concatenate([ref[rows[0], :], ref[rows[1], :]], axis=0)
                    m_old, l_old, acc_old = ld(m_s), ld(l_s), ld(acc_s)
                    m_new = jnp.maximum(m_old, m)
                    a_old, a_new = jnp.exp(m_old - m_new), jnp.exp(m - m_new)
                    l = a_old * l_old + a_new * l
                    acc = a_old * acc_old + a_new * acc
                    m = m_new
                for h2 in (0, 1):
                    sl = slice(h2 * u_rows, (h2 + 1) * u_rows)
                    m_s[rows[h2], :] = m[sl]
                    l_s[rows[h2], :] = l[sl]
                    acc_s[rows[h2], :] = acc[sl]
                return c

            lax.fori_loop(0, units, unit, 0, unroll=4)

        chunk = 2 * u_rows
        for c in range(sb // chunk):
            ra, rb = slice(c * chunk, (c + 1) * chunk), slice(sb + c * chunk, sb + (c + 1) * chunk)
            oa = acc_s[ra, :] / l_s[ra, :]
            ob = acc_s[rb, :] / l_s[rb, :]
            lo2 = jnp.concatenate([lo, lo], axis=0)
            if half == 0:
                out = jnp.where(lo2, oa, pltpu.roll(ob, HEAD_DIM, 1))
            else:
                out = jnp.where(lo2, pltpu.roll(oa, HEAD_DIM, 1), ob)
            o_ref[0, ra, half * LANES:(half + 1) * LANES] = out.astype(BF16)


def _attention_window(q, k, v):
    b, l, _ = q.shape
    sb = ATT_SUPER
    max_dil = max(d for _, d in DILATED_BRANCHES)
    assert l % sb == 0 and l // max_dil >= ATT_UNIT + ATT_SPAN
    qspec = lambda h: pl.BlockSpec((1, sb, LANES), lambda bi, pi, si: (bi, si, 2 * pi + h))
    kspec = pl.BlockSpec((1, l, LANES), lambda bi, pi, si: (bi, 0, pi))
    return pl.pallas_call(
        _attn_window_kernel,
        grid=(b, KV_WIDTH // LANES, l // sb),
        in_specs=[qspec(0), qspec(1), kspec, kspec],
        out_specs=pl.BlockSpec((1, sb, 2 * LANES), lambda bi, pi, si: (bi, si, pi)),
        out_shape=jax.ShapeDtypeStruct((b, l, ATT_WIDTH), BF16),
        scratch_shapes=[pltpu.VMEM((2 * sb, LANES), F32)] * 3,
        compiler_params=_cparams(("parallel", "parallel", "arbitrary")),
        name="window_attn",
    )(q, q, k, v)


def _ssd_kernel(xbc_ref, dt_ref, z_ref, ssm0_ref, conv0_ref, cw_ref, cb_ref, dtb_ref, alog_ref, dsk_ref,
                nw_ref, ex_ref, y_ref, ssm_ref, conv_ref, xpad, dtpad, s_scr, *, n_valid):
    q = SSD_CHUNK
    lb = xbc_ref.shape[1]
    ci = pl.program_id(1)
    nc = pl.num_programs(1)
    gw = SSD_WIDTH // SSD_GROUPS
    hpg = SSD_HEADS // SSD_GROUPS

    @pl.when(ci == 0)
    def _():
        xpad[0:SUBLANES, :] = jnp.zeros((SUBLANES, CONV_DIM), F32)
        xpad[SUBLANES - (CONV_W - 1):SUBLANES, :] = conv0_ref[0]
        for g in range(SSD_GROUPS):
            s_scr[g] = jnp.transpose(ssm0_ref[0, g * hpg:(g + 1) * hpg].reshape(gw, SSD_STATE))

    xpad[SUBLANES:SUBLANES + lb, :] = xbc_ref[0]
    dtpad[0:lb, :] = dt_ref[0]
    if lb < q:
        xpad[SUBLANES + lb:SUBLANES + q, :] = jnp.zeros((q - lb, CONV_DIM), F32)
        dtpad[lb:q, :] = jnp.zeros((q - lb, LANES), F32)

    conv = cb_ref[...]
    for k in range(CONV_W):
        sh = CONV_W - 1 - k
        conv = conv + xpad[SUBLANES - sh:SUBLANES - sh + q, :] * cw_ref[k:k + 1, :]
    act = _silu(conv)

    @pl.when(ci == nc - 1)
    def _():
        conv_ref[0] = xpad[SUBLANES + n_valid - (CONV_W - 1):SUBLANES + n_valid, :]

    xpad[0:SUBLANES, :] = xpad[q:q + SUBLANES, :]

    xs = act[:, :SSD_WIDTH]
    row = lax.broadcasted_iota(I32, (q, LANES), 0)
    dtr = dtpad[...] + dtb_ref[...]
    dt = jnp.maximum(dtr, 0.0) + jnp.log(1.0 + jnp.exp(-jnp.abs(dtr)))
    dt = jnp.where(row < n_valid, dt, 0.0)
    a = -jnp.exp(alog_ref[...])
    da = dt * a
    r2 = lax.broadcasted_iota(I32, (q, q), 0)
    c2 = lax.broadcasted_iota(I32, (q, q), 1)
    causal = r2 >= c2
    a_cs = jnp.dot(causal.astype(F32), da, precision=HIGHEST, preferred_element_type=F32)
    a_cst = jnp.transpose(a_cs)
    a_last = a_cs[q - 1:q, :]
    ex = ex_ref[...]
    expand = lambda t: jnp.dot(t, ex, precision=HIGHEST, preferred_element_type=F32)
    dt_x = expand(dt)
    ea_x = expand(jnp.exp(a_cs))
    te_x = expand(jnp.exp(a_last - a_cs))
    cd_x = expand(jnp.broadcast_to(jnp.exp(a_last), (SUBLANES, LANES)))[0:1, :]
    xdt = xs * dt_x
    xdt_b = xdt.astype(BF16)
    xw_b = (xdt * te_x).astype(BF16)
    lo = lax.broadcasted_iota(I32, (q, LANES), 1) < SSD_HEAD_DIM

    ys = []
    for g in range(SSD_GROUPS):
        bm = act[:, SSD_WIDTH + g * SSD_STATE:SSD_WIDTH + (g + 1) * SSD_STATE]
        cm = act[:, SSD_WIDTH + (SSD_GROUPS + g) * SSD_STATE:SSD_WIDTH + (SSD_GROUPS + g + 1) * SSD_STATE]
        bm_b, cm_b = bm.astype(BF16), cm.astype(BF16)
        cb = lax.dot_general(cm_b, bm_b, (((1,), (1,)), ((), ())), preferred_element_type=F32)
        s_old = s_scr[g]
        y_off = jnp.dot(cm_b, s_old.astype(BF16), preferred_element_type=F32)
        for jp in range(hpg // 2):
            pair = g * (hpg // 2) + jp
            yp = []
            for hh in (0, 1):
                h = 2 * pair + hh
                seg = a_cs[:, h:h + 1] - a_cst[h:h + 1, :]
                dec = jnp.exp(jnp.where(causal, seg, NEG_BIG))
                mm = (cb * dec).astype(BF16)
                yp.append(jnp.dot(mm, xdt_b[:, pair * LANES:(pair + 1) * LANES], preferred_element_type=F32))
            ys.append(jnp.where(lo, yp[0], yp[1]) + y_off[:, jp * LANES:(jp + 1) * LANES]
                      * ea_x[:, pair * LANES:(pair + 1) * LANES])
        bmt_b = jnp.transpose(bm).astype(BF16)
        s_new = s_old * cd_x[:, g * gw:(g + 1) * gw] + jnp.dot(bmt_b, xw_b[:, g * gw:(g + 1) * gw],
                                                              preferred_element_type=F32)
        s_scr[g] = s_new

    y = jnp.concatenate(ys, axis=1) + dsk_ref[...] * xs

    @pl.when(ci == nc - 1)
    def _():
        for g in range(SSD_GROUPS):
            ssm_ref[0, g * hpg:(g + 1) * hpg] = jnp.transpose(s_scr[g]).reshape(hpg, SSD_HEAD_DIM, SSD_STATE)

    hg = y[:lb] * _silu(z_ref[0])
    outs = []
    for g in range(SSD_GROUPS):
        part = hg[:, g * gw:(g + 1) * gw]
        ms = jnp.mean(part * part, axis=1, keepdims=True)
        outs.append(part * lax.rsqrt(ms + NORM_EPS))
    y_ref[0] = (jnp.concatenate(outs, axis=1) * nw_ref[...]).astype(BF16)


def _ssd(xbc, dt_raw, z, ssm0, conv0, prm, lb, n_valid):
    b, l, _ = xbc.shape
    nc = l // lb
    cw, cbias, dtb, alog, dsk, nw, ex = prm
    full = lambda a: pl.BlockSpec(a.shape, lambda bi, ci: (0,) * a.ndim)
    seq = lambda w: pl.BlockSpec((1, lb, w), lambda bi, ci: (bi, ci, 0))
    return pl.pallas_call(
        functools.partial(_ssd_kernel, n_valid=n_valid),
        grid=(b, nc),
        in_specs=[seq(CONV_DIM), seq(LANES), seq(SSD_WIDTH),
                  pl.BlockSpec((1, SSD_HEADS, SSD_HEAD_DIM, SSD_STATE), lambda bi, ci: (bi, 0, 0, 0)),
                  pl.BlockSpec((1, CONV_W - 1, CONV_DIM), lambda bi, ci: (bi, 0, 0)),
                  full(cw), full(cbias), full(dtb), full(alog), full(dsk), full(nw), full(ex)],
        out_specs=(seq(SSD_WIDTH),
                   pl.BlockSpec((1, SSD_HEADS, SSD_HEAD_DIM, SSD_STATE), lambda bi, ci: (bi, 0, 0, 0)),
                   pl.BlockSpec((1, CONV_W - 1, CONV_DIM), lambda bi, ci: (bi, 0, 0))),
        out_shape=(jax.ShapeDtypeStruct((b, l, SSD_WIDTH), BF16),
                   jax.ShapeDtypeStruct((b, SSD_HEADS, SSD_HEAD_DIM, SSD_STATE), F32),
                   jax.ShapeDtypeStruct((b, CONV_W - 1, CONV_DIM), F32)),
        scratch_shapes=[pltpu.VMEM((SSD_CHUNK + 2 * SUBLANES, CONV_DIM), F32),
                        pltpu.VMEM((SSD_CHUNK, LANES), F32),
                        pltpu.VMEM((SSD_GROUPS, SSD_STATE, SSD_WIDTH // SSD_GROUPS), F32)],
        compiler_params=_cparams(("parallel", "arbitrary")),
        name="conv_ssd",
    )(xbc, dt_raw, z, ssm0, conv0, cw, cbias, dtb, alog, dsk, nw, ex)


def _layer_norm(r, g, b):
    mu = jnp.mean(r, axis=1, keepdims=True)
    d = r - mu
    var = jnp.mean(d * d, axis=1, keepdims=True)
    return d * lax.rsqrt(var + NORM_EPS) * g + b


def _mix_kernel(att_ref, ssd_ref, x_ref, wo_ref, g_ref, b_ref, *rest):
    h_ref, hs_ref = rest[-2:]
    tm = x_ref.shape[0]
    mix = jnp.dot(att_ref[...], wo_ref[0:ATT_WIDTH, :], preferred_element_type=F32)
    mix = mix + jnp.dot(ssd_ref[...], wo_ref[ATT_WIDTH:MIX_WIDTH, :], preferred_element_type=F32)
    h = _layer_norm(DEEPNORM_ALPHA * x_ref[...] + mix, g_ref[...], b_ref[...])
    h_ref[...] = h
    for j in range(ROW_SLAB):
        hs_ref[pl.ds(j, tm, stride=ROW_SLAB), :] = h[:, j * LANES:(j + 1) * LANES]


def _mix(att, ssd, x2d, wo_bf, g, b, tm, t_total, row0=0, into=None):
    t = x2d.shape[0]
    b0 = row0 // tm
    row = lambda w: pl.BlockSpec((tm, w), lambda i: (i, 0))
    vec = pl.BlockSpec((1, D_MODEL), lambda i: (0, 0))
    hbm = pl.BlockSpec(memory_space=pl.ANY)
    extra = () if into is None else tuple(into)
    return pl.pallas_call(
        _mix_kernel,
        grid=(t // tm,),
        in_specs=[row(ATT_WIDTH), row(SSD_WIDTH), row(D_MODEL),
                  pl.BlockSpec((MIX_WIDTH, D_MODEL), lambda i: (0, 0)), vec, vec] + [hbm] * len(extra),
        out_specs=(pl.BlockSpec((tm, D_MODEL), lambda i: (b0 + i, 0)),
                   pl.BlockSpec((tm * ROW_SLAB, LANES), lambda i: (b0 + i, 0))),
        out_shape=(jax.ShapeDtypeStruct((t_total, D_MODEL), F32),
                   jax.ShapeDtypeStruct((t_total * ROW_SLAB, LANES), F32)),
        input_output_aliases={} if into is None else {6: 0, 7: 1},
        compiler_params=_cparams(("parallel",)),
        name="outproj_ln1",
    )(att, ssd, x2d, wo_bf, g, b, *extra)


def _router_kernel(h_ref, wr_ref, bias_ref, tope_ref, rank_ref, gt_ref, cnt_ref, gscr):
    tm = h_ref.shape[0]
    i = pl.program_id(0)
    per_group = N_EXPERTS // N_EXPERT_GROUPS

    @pl.when(i == 0)
    def _():
        cnt_ref[...] = jnp.zeros_like(cnt_ref)

    logits = lax.dot_general(wr_ref[...], h_ref[...], (((1,), (1,)), ((), ())),
                             precision=HIGHEST, preferred_element_type=F32)
    s = 1.0 / (1.0 + jnp.exp(-logits))
    sel = s + bias_ref[...]
    neg_inf = -jnp.inf
    e_iota = lax.broadcasted_iota(I32, (N_EXPERTS, tm), 0).astype(F32)
    g_iota = lax.broadcasted_iota(I32, (per_group, tm), 0).astype(F32)

    gscore = []
    for g in range(N_EXPERT_GROUPS):
        blk = sel[g * per_group:(g + 1) * per_group]
        m1 = jnp.max(blk, axis=0, keepdims=True)
        a1 = jnp.min(jnp.where(blk == m1, g_iota, float(per_group)), axis=0, keepdims=True)
        m2 = jnp.max(jnp.where(g_iota == a1, neg_inf, blk), axis=0, keepdims=True)
        gscore.append(m1 + m2)
    blocks = []
    for g in range(N_EXPERT_GROUPS):
        beaten = jnp.zeros((1, tm), F32)
        for o in range(N_EXPERT_GROUPS):
            if o < g:
                beaten = beaten + jnp.where(gscore[o] >= gscore[g], 1.0, 0.0)
            elif o > g:
                beaten = beaten + jnp.where(gscore[o] > gscore[g], 1.0, 0.0)
        blocks.append(jnp.where(beaten < float(TOPK_GROUPS), sel[g * per_group:(g + 1) * per_group], neg_inf))
    cand = jnp.concatenate(blocks, axis=0)

    tops, gsel = [], []
    onehot = jnp.zeros((N_EXPERTS, tm), F32)
    for _ in range(TOP_K):
        mx = jnp.max(cand, axis=0, keepdims=True)
        ix = jnp.min(jnp.where(cand == mx, e_iota, float(N_EXPERTS)), axis=0, keepdims=True)
        hit = e_iota == ix
        tops.append(ix)
        gsel.append(jnp.sum(jnp.where(hit, s, 0.0), axis=0, keepdims=True))
        onehot = jnp.where(hit, 1.0, onehot)
        cand = jnp.where(hit, neg_inf, cand)
    den = gsel[0]
    for k in range(1, TOP_K):
        den = den + gsel[k]

    t_r = lax.broadcasted_iota(I32, (tm, tm), 0)
    t_c = lax.broadcasted_iota(I32, (tm, tm), 1)
    before = (t_r < t_c).astype(BF16)
    oh_b = onehot.astype(BF16)
    base = cnt_ref[...] + jnp.dot(oh_b, before, preferred_element_type=F32)
    cnt_ref[...] = cnt_ref[...] + jnp.dot(oh_b, jnp.ones((tm, LANES), BF16), preferred_element_type=F32)

    gscr[...] = jnp.zeros_like(gscr)
    for k in range(TOP_K):
        tope_ref[k:k + 1, :] = tops[k].astype(I32)
        rank_ref[k:k + 1, :] = jnp.sum(jnp.where(e_iota == tops[k], base, 0.0), axis=0, keepdims=True).astype(I32)
        gscr[k:k + 1, :] = gsel[k] / den * ROUTED_SCALE
    gt_ref[...] = jnp.transpose(gscr[...])


def _router(h_all, wr_t, bias_b):
    t = h_all.shape[0]
    tm = ROUTER_TILE
    return pl.pallas_call(
        _router_kernel,
        grid=(t // tm,),
        in_specs=[pl.BlockSpec((tm, D_MODEL), lambda i: (i, 0)),
                  pl.BlockSpec((N_EXPERTS, D_MODEL), lambda i: (0, 0)),
                  pl.BlockSpec((N_EXPERTS, tm), lambda i: (0, 0))],
        out_specs=(pl.BlockSpec((TOP_K, tm), lambda i: (0, i)),
                   pl.BlockSpec((TOP_K, tm), lambda i: (0, i)),
                   pl.BlockSpec((tm, LANES), lambda i: (i, 0)),
                   pl.BlockSpec((N_EXPERTS, LANES), lambda i: (0, 0))),
        out_shape=(jax.ShapeDtypeStruct((TOP_K, t), I32),
                   jax.ShapeDtypeStruct((TOP_K, t), I32),
                   jax.ShapeDtypeStruct((t, LANES), F32),
                   jax.ShapeDtypeStruct((N_EXPERTS, LANES), F32)),
        scratch_shapes=[pltpu.VMEM((tm, LANES), F32)],
        compiler_params=_cparams(("arbitrary",)),
        name="router",
    )(h_all, wr_t, bias_b)


_PAD_SIZES = tuple(2 ** p for p in range(int(math.log2(EXPERT_BLOCK)) - 1, -1, -1))


def _slab(ref, row, n=1):
    return ref.at[pl.ds(pl.multiple_of(row * ROW_SLAB, ROW_SLAB), n * ROW_SLAB)]


def _padfill_kernel(pstart_ref, pcnt_ref, tail_ref, xs_ref, zero_scr, sem):
    zero_scr[...] = jnp.zeros_like(zero_scr)
    half = EXPERT_BLOCK // 2
    tail_copy = lambda j: pltpu.make_async_copy(_slab(zero_scr, 0, half), _slab(xs_ref, tail_ref[0] + j * half, half),
                                                sem)

    def tail_start(j, c):
        tail_copy(j).start()
        return c

    def tail_wait(j, c):
        tail_copy(j).wait()
        return c

    lax.fori_loop(0, tail_ref[1], tail_start, 0)

    def copies(e):
        base = pstart_ref[e]
        cnt = pcnt_ref[e]
        out = []
        for sz in _PAD_SIZES:
            out.append(((cnt & sz) != 0, pltpu.make_async_copy(_slab(zero_scr, 0, sz), _slab(xs_ref, base, sz), sem)))
            base = base + (cnt & sz)
        return out

    def start(e, c):
        for pred, cp in copies(e):
            @pl.when(pred)
            def _():
                cp.start()
        return c

    def wait(e, c):
        for pred, cp in copies(e):
            @pl.when(pred)
            def _():
                cp.wait()
        return c

    lax.fori_loop(0, N_EXPERTS, start, 0)
    lax.fori_loop(0, N_EXPERTS, wait, 0)
    lax.fori_loop(0, tail_ref[1], tail_wait, 0)


def _padfill(pad_start, pad_cnt, tail, n_rows):
    return pl.pallas_call(
        _padfill_kernel,
        grid_spec=pltpu.PrefetchScalarGridSpec(
            num_scalar_prefetch=3, grid=(1,), in_specs=[],
            out_specs=pl.BlockSpec(memory_space=pl.ANY),
            scratch_shapes=[pltpu.VMEM((EXPERT_BLOCK // 2 * ROW_SLAB, LANES), F32), pltpu.SemaphoreType.DMA]),
        out_shape=jax.ShapeDtypeStruct((n_rows * ROW_SLAB, LANES), F32),
        compiler_params=_cparams(("arbitrary",)),
        name="dispatch_padfill",
    )(pad_start, pad_cnt, tail)


def _dest_kernel(tope_ref, rank_ref, pstart_ref, dest_ref):
    tm = tope_ref.shape[1]
    e_iota = lax.broadcasted_iota(I32, (N_EXPERTS, tm), 0)
    ps = pstart_ref[...]
    for k in range(TOP_K):
        base = jnp.sum(jnp.where(e_iota == tope_ref[k:k + 1, :], ps, 0.0), axis=0, keepdims=True)
        dest_ref[k:k + 1, :] = base.astype(I32) + rank_ref[k:k + 1, :]


def _dest(top_e, rank, pstarts):
    t = top_e.shape[1]
    tm = DISPATCH_TILE
    blk = pl.BlockSpec((TOP_K, tm), lambda i: (0, i))
    return pl.pallas_call(
        _dest_kernel,
        grid=(t // tm,),
        in_specs=[blk, blk, pl.BlockSpec((N_EXPERTS, tm), lambda i: (0, 0))],
        out_specs=blk,
        out_shape=jax.ShapeDtypeStruct((TOP_K, t), I32),
        compiler_params=_cparams(("parallel",)),
        name="dispatch_dest",
    )(top_e, rank, jnp.broadcast_to(pstarts.astype(F32)[:, None], (N_EXPERTS, tm)))


def _dest_tiles(dest, tm):
    t = dest.shape[1]
    return jnp.transpose(dest.reshape(TOP_K, t // tm, tm), (1, 0, 2)).reshape(t // tm, TOP_K * tm)


def _dispatch_kernel(dest_ref, h_ref, xs_in_ref, xs_ref, idx_smem, isem, sem):
    del xs_in_ref
    i = pl.program_id(0)
    tm = DISPATCH_TILE
    cp = pltpu.make_async_copy(dest_ref.at[i], idx_smem, isem)
    cp.start()
    cp.wait()

    def row_copy(t, k):
        return pltpu.make_async_copy(_slab(h_ref, t), _slab(xs_ref, idx_smem[k * tm + t]), sem)

    def start(t, c):
        for k in range(TOP_K):
            row_copy(t, k).start(priority=k % 2)
        return c

    def wait(t, c):
        for k in range(TOP_K):
            row_copy(t, k).wait()
        return c

    lax.fori_loop(0, tm, start, 0)
    lax.fori_loop(0, tm, wait, 0)


def _dispatch(dest_tiles, h_slab, xs):
    tm = DISPATCH_TILE
    return pl.pallas_call(
        _dispatch_kernel,
        grid=(dest_tiles.shape[0],),
        in_specs=[pl.BlockSpec(dest_tiles.shape, lambda i: (0, 0)),
                  pl.BlockSpec((tm * ROW_SLAB, LANES), lambda i: (i, 0)),
                  pl.BlockSpec(memory_space=pl.ANY)],
        out_specs=pl.BlockSpec(memory_space=pl.ANY),
        out_shape=jax.ShapeDtypeStruct(xs.shape, xs.dtype),
        scratch_shapes=[pltpu.SMEM((tm * TOP_K,), I32), pltpu.SemaphoreType.DMA, pltpu.SemaphoreType.DMA],
        input_output_aliases={2: 0},
        compiler_params=_cparams(("arbitrary",)),
        name="dispatch_rows",
    )(dest_tiles, h_slab, xs)


def _expert_kernel(bi_ref, be_ref, nu_ref, x_ref, wg_ref, wu_ref, wd_ref, y_ref, wg_b, wu_b, wd_b):
    del bi_ref
    i = pl.program_id(0)

    @pl.when(i < nu_ref[0])
    def _():
        prev = be_ref[jnp.maximum(i - 1, 0)]

        @pl.when(jnp.logical_or(i == 0, be_ref[i] != prev))
        def _():
            wg_b[...] = wg_ref[0].astype(BF16)
            wu_b[...] = wu_ref[0].astype(BF16)
            wd_b[...] = wd_ref[0].astype(BF16)

        rows = EXPERT_BLOCK // EXPERT_CHUNKS
        for c in range(EXPERT_CHUNKS):
            r0 = c * rows * ROW_SLAB
            xb = jnp.concatenate([x_ref[pl.ds(r0 + j, rows, stride=ROW_SLAB), :] for j in range(ROW_SLAB)],
                                 axis=1).astype(BF16)
            g = jnp.dot(xb, wg_b[...], preferred_element_type=F32)
            u = jnp.dot(xb, wu_b[...], preferred_element_type=F32)
            a = (_silu(g) * u).astype(BF16)
            y = jnp.dot(a, wd_b[...], preferred_element_type=F32)
            for j in range(ROW_SLAB):
                y_ref[pl.ds(r0 + j, rows, stride=ROW_SLAB), :] = y[:, j * LANES:(j + 1) * LANES]

    @pl.when(i >= nu_ref[0])
    def _():
        y_ref[...] = jnp.zeros_like(y_ref)


def _experts(blk_idx, blk_exp, n_used, xs, wg, wu, wd):
    n_rows = xs.shape[0] // ROW_SLAB
    nb = n_rows // EXPERT_BLOCK
    rows = pl.BlockSpec((EXPERT_BLOCK * ROW_SLAB, LANES), lambda i, bi, be, nu: (bi[i], 0))
    out_rows = pl.BlockSpec((EXPERT_BLOCK * ROW_SLAB, LANES), lambda i, bi, be, nu: (i, 0))
    return pl.pallas_call(
        _expert_kernel,
        grid_spec=pltpu.PrefetchScalarGridSpec(
            num_scalar_prefetch=3, grid=(nb,),
            in_specs=[rows,
                      pl.BlockSpec((1, D_MODEL, EXPERT_FF), lambda i, bi, be, nu: (be[i], 0, 0)),
                      pl.BlockSpec((1, D_MODEL, EXPERT_FF), lambda i, bi, be, nu: (be[i], 0, 0)),
                      pl.BlockSpec((1, EXPERT_FF, D_MODEL), lambda i, bi, be, nu: (be[i], 0, 0))],
            out_specs=out_rows,
            scratch_shapes=[pltpu.VMEM((D_MODEL, EXPERT_FF), BF16), pltpu.VMEM((D_MODEL, EXPERT_FF), BF16),
                            pltpu.VMEM((EXPERT_FF, D_MODEL), BF16)]),
        out_shape=jax.ShapeDtypeStruct(xs.shape, F32),
        compiler_params=_cparams(("arbitrary",)),
        name="expert_ffn",
    )(blk_idx, blk_exp, n_used, xs, wg, wu, wd)


def _combine_kernel(dest_ref, h_ref, gt_ref, ys_ref, sg_ref, su_ref, sd_ref, g_ref, b_ref, o_ref, o2_ref,
                    idx_smem, gbuf, isem, sem, *, n_first):
    i = pl.program_id(0)
    tm = h_ref.shape[0]
    cp = pltpu.make_async_copy(dest_ref.at[i], idx_smem, isem)
    cp.start()
    cp.wait()

    def row_copy(t, k):
        return pltpu.make_async_copy(_slab(ys_ref, idx_smem[k * tm + t]), _slab(gbuf.at[k], t), sem)

    def start(t, c):
        for k in range(TOP_K):
            row_copy(t, k).start(priority=k % 2)
        return c

    def wait(t, c):
        for k in range(TOP_K):
            row_copy(t, k).wait()
        return c

    lax.fori_loop(0, tm, start, 0)
    h = h_ref[...]
    hb = h.astype(BF16)
    g = jnp.dot(hb, sg_ref[...], preferred_element_type=F32)
    u = jnp.dot(hb, su_ref[...], preferred_element_type=F32)
    f = jnp.dot((_silu(g) * u).astype(BF16), sd_ref[...], preferred_element_type=F32)
    lax.fori_loop(0, tm, wait, 0)
    gt = gt_ref[...]
    cols = []
    for j in range(ROW_SLAB):
        fj = f[:, j * LANES:(j + 1) * LANES]
        for k in range(TOP_K):
            fj = fj + gbuf[k, pl.ds(j, tm, stride=ROW_SLAB), :] * gt[:, k:k + 1]
        cols.append(fj)
    f = jnp.concatenate(cols, axis=1)
    out = _layer_norm(DEEPNORM_ALPHA * h + f, g_ref[...], b_ref[...])

    @pl.when(i < n_first)
    def _():
        o_ref[...] = out

    @pl.when(i == n_first)
    def _():
        o2_ref[...] = out


def _combine(dest_tiles, h_all, gates_t, ys, sg_bf, su_bf, sd_bf, g, b, n_first):
    t = h_all.shape[0]
    tm = COMBINE_TILE
    vec = pl.BlockSpec((1, D_MODEL), lambda i: (0, 0))
    return pl.pallas_call(
        functools.partial(_combine_kernel, n_first=n_first),
        grid=(t // tm,),
        in_specs=[pl.BlockSpec(dest_tiles.shape, lambda i: (0, 0)),
                  pl.BlockSpec((tm, D_MODEL), lambda i: (i, 0)),
                  pl.BlockSpec((tm, LANES), lambda i: (i, 0)),
                  pl.BlockSpec(memory_space=pl.ANY),
                  pl.BlockSpec((D_MODEL, SHARED_FF), lambda i: (0, 0)),
                  pl.BlockSpec((D_MODEL, SHARED_FF), lambda i: (0, 0)),
                  pl.BlockSpec((SHARED_FF, D_MODEL), lambda i: (0, 0)), vec, vec],
        out_specs=(pl.BlockSpec((tm, D_MODEL), lambda i: (jnp.minimum(i, n_first - 1), 0)),
                   pl.BlockSpec((tm, D_MODEL), lambda i: (0, 0))),
        out_shape=(jax.ShapeDtypeStruct((n_first * tm, D_MODEL), F32), jax.ShapeDtypeStruct((tm, D_MODEL), F32)),
        scratch_shapes=[pltpu.SMEM((tm * TOP_K,), I32), pltpu.VMEM((TOP_K, tm * ROW_SLAB, LANES), F32),
                        pltpu.SemaphoreType.DMA, pltpu.SemaphoreType.DMA],
        compiler_params=_cparams(("arbitrary",)),
        name="combine_ln2",
    )(dest_tiles, h_all, gates_t, ys, sg_bf, su_bf, sd_bf, g, b)


def _moe(h_all, h_slab, n_first, wr, rbias, wg, wu, wd, sg, su, sd, g2, b2):
    t = h_all.shape[0]
    top_e, rank, gates_t, cnt = _router(h_all, jnp.transpose(wr),
                                        jnp.broadcast_to(rbias[:, None], (N_EXPERTS, ROUTER_TILE)))
    counts = cnt[:, 0].astype(I32)
    pcounts = (counts + EXPERT_BLOCK - 1) // EXPERT_BLOCK * EXPERT_BLOCK
    pends = jnp.cumsum(pcounts)
    pstarts = pends - pcounts
    n_rows = (t * TOP_K // EXPERT_BLOCK + N_EXPERTS - 1) * EXPERT_BLOCK
    nb = n_rows // EXPERT_BLOCK
    n_used = (pends[-1] // EXPERT_BLOCK).astype(I32)
    blk_idx = jnp.minimum(jnp.arange(nb, dtype=I32), n_used - 1)
    blk_exp = jnp.minimum(jnp.sum((pends[None, :] <= (blk_idx * EXPERT_BLOCK)[:, None]).astype(I32), axis=1),
                          N_EXPERTS - 1)
    dest = _dest(top_e, rank, pstarts)
    tail = jnp.stack([pends[-1], (n_rows - pends[-1]) // (EXPERT_BLOCK // 2)]).astype(I32)
    xs = _padfill((pstarts + counts).astype(I32), (pcounts - counts).astype(I32), tail, n_rows)
    xs = _dispatch(_dest_tiles(dest, DISPATCH_TILE), h_slab, xs)
    ys = _experts(blk_idx, blk_exp, n_used.reshape(1), xs, wg, wu, wd)
    return _combine(_dest_tiles(dest, COMBINE_TILE), h_all, gates_t, ys,
                    sg.astype(BF16), su.astype(BF16), sd.astype(BF16), g2[None, :], b2[None, :], n_first)


def _expand_matrix():
    h = np.arange(LANES)[:, None]
    c = np.arange(SSD_WIDTH)[None, :]
    return jnp.asarray((c // SSD_HEAD_DIM == h).astype(np.float32))


def kernel(x_prompt, x_sample, cache_k, cache_v, state_ssm, state_conv, w_in, conv_w, conv_b, dt_bias, a_log, d_skip, ssd_norm_w, w_out, ln1_g, ln1_b, w_router, router_bias, w_exp_gate, w_exp_up, w_exp_down, w_sh_gate, w_sh_up, w_sh_down, ln2_g, ln2_b):
    bp, lp, _ = x_prompt.shape
    bs, ls, _ = x_sample.shape
    win = cache_k.shape[2]
    keep = min(MAX_WINDOW, lp)
    assert lp % SSD_CHUNK == 0 and ls <= SUBLANES and win % KEY_TILE == 0 and win >= MAX_WINDOW

    w_bf = jnp.pad(w_in[0], ((0, 0), (0, IN_COLS_PAD - IN_COLS))).astype(BF16)
    wo_bf = w_out[0].astype(BF16)
    pad_l = lambda v: jnp.pad(v, (0, LANES - v.shape[0]))[None, :]
    ssd_prm = (conv_w[0], conv_b[0][None, :], pad_l(dt_bias[0]), pad_l(a_log[0]),
               jnp.repeat(d_skip[0], SSD_HEAD_DIM)[None, :], ssd_norm_w[0][None, :], _expand_matrix())

    tp = bp * lp
    tm_p = 256 if lp % 256 == 0 else SSD_CHUNK
    tm_mix = 512 if tp % 512 == 0 else SSD_CHUNK
    cos_p, sin_p = _rope_tables(jnp.arange(lp, dtype=F32))
    xp2 = x_prompt.reshape(tp, D_MODEL)
    q, kf, vf, z, xbc, dtr = _inproj(xp2, w_bf, cos_p, sin_p, tm_p, lp // tm_p)
    att = _attention_window(q.reshape(bp, lp, ATT_WIDTH), kf.reshape(bp, lp, KV_WIDTH), vf.reshape(bp, lp, KV_WIDTH))
    ssd_y, ssm_p, conv_p = _ssd(xbc.reshape(bp, lp, CONV_DIM), dtr.reshape(bp, lp, LANES),
                                z.reshape(bp, lp, SSD_WIDTH),
                                jnp.zeros((bp, SSD_HEADS, SSD_HEAD_DIM, SSD_STATE), F32),
                                jnp.zeros((bp, CONV_W - 1, CONV_DIM), F32), ssd_prm, SSD_CHUNK, SSD_CHUNK)
    ts = bs * ls
    t_pad = -(-(tp + ts) // DISPATCH_TILE) * DISPATCH_TILE
    tail = t_pad - tp
    assert tp % tail == 0 and ts <= COMBINE_TILE and tp % COMBINE_TILE == 0
    h_bufs = _mix(att.reshape(tp, ATT_WIDTH), ssd_y.reshape(tp, SSD_WIDTH), xp2, wo_bf, ln1_g, ln1_b, tm_mix, t_pad)

    pos_s = (PAST_LEN + jnp.arange(ls, dtype=jnp.int32)).astype(F32)
    cos_s, sin_s = _rope_tables(jnp.tile(pos_s, bs))
    xs2 = x_sample.reshape(ts, D_MODEL)
    q_s, kf_s, vf_s, z_s, xbc_s, dtr_s = _inproj(xs2, w_bf, cos_s, sin_s, ts, 1)
    rows8 = lambda a, w: jnp.pad(a.reshape(bs, ls, w), ((0, 0), (0, SUBLANES - ls), (0, 0)))
    att_s = _attention_decode(rows8(q_s, ATT_WIDTH), cache_k[0].reshape(bs, win, KV_WIDTH),
                              cache_v[0].reshape(bs, win, KV_WIDTH), rows8(kf_s, KV_WIDTH),
                              rows8(vf_s, KV_WIDTH))[:, :ls]
    ssd_s, ssm_s, conv_s = _ssd(rows8(xbc_s, CONV_DIM), rows8(dtr_s, LANES), rows8(z_s, SSD_WIDTH),
                                state_ssm[0], state_conv[0], ssd_prm, SUBLANES, ls)
    tail_rows = lambda a: jnp.pad(a, ((0, tail - ts), (0, 0)))
    h_all, h_slab = _mix(tail_rows(att_s.reshape(ts, ATT_WIDTH)), tail_rows(ssd_s[:, :ls].reshape(ts, SSD_WIDTH)),
                         tail_rows(xs2), wo_bf, ln1_g, ln1_b, tail, t_pad, row0=tp, into=h_bufs)

    y_p, y_s = _moe(h_all, h_slab, tp // COMBINE_TILE, w_router[0], router_bias[0], w_exp_gate[0], w_exp_up[0],
                    w_exp_down[0], w_sh_gate[0], w_sh_up[0], w_sh_down[0], ln2_g[0], ln2_b[0])

    kv5 = lambda a, b, l: a.reshape(1, b, l, N_KV_HEADS, HEAD_DIM)
    return (y_p.reshape(bp, lp, D_MODEL), y_s[:ts].reshape(bs, ls, D_MODEL),
            kv5(kf.reshape(bp, lp, KV_WIDTH)[:, lp - keep:], bp, keep),
            kv5(vf.reshape(bp, lp, KV_WIDTH)[:, lp - keep:], bp, keep),
            ssm_p[None], conv_p[None],
            kv5(kf_s, bs, ls), kv5(vf_s, bs, ls), ssm_s[None], conv_s[None])
```

```python
import functools
import math

import jax
import jax.numpy as jnp
import numpy as np
from jax import lax
from jax.experimental import pallas as pl
from jax.experimental.pallas import tpu as pltpu

F32 = jnp.float32
BF16 = jnp.bfloat16
I32 = jnp.int32

D_MODEL = 1024
PAST_LEN = 16384
HEAD_DIM = 64
N_ATT_HEADS = 16
N_KV_HEADS = 8
ATT_WIDTH = N_ATT_HEADS * HEAD_DIM
KV_WIDTH = N_KV_HEADS * HEAD_DIM
DILATED_BRANCHES = ((128, 1), (512, 4), (2048, 16))
MAX_WINDOW = 2048
ROPE_THETA = 10000.0
SSD_HEADS = 16
SSD_HEAD_DIM = 64
SSD_WIDTH = SSD_HEADS * SSD_HEAD_DIM
SSD_GROUPS = 2
SSD_STATE = 128
SSD_CHUNK = 128
CONV_W = 4
CONV_DIM = SSD_WIDTH + 2 * SSD_GROUPS * SSD_STATE
MIX_WIDTH = ATT_WIDTH + SSD_WIDTH
IN_COLS = ATT_WIDTH + 2 * KV_WIDTH + SSD_WIDTH + CONV_DIM + SSD_HEADS
N_EXPERTS = 256
TOP_K = 8
N_EXPERT_GROUPS = 8
TOPK_GROUPS = 4
EXPERT_FF = 256
SHARED_FF = 256
ROUTED_SCALE = 2.5
DEPTH = 1
DEEPNORM_ALPHA = (2.0 * DEPTH) ** 0.25
NORM_EPS = 1e-5

LANES = 128
SUBLANES = 8
VMEM_LIMIT = 56 * 1024 * 1024

IN_COLS_PAD = ATT_WIDTH + 2 * KV_WIDTH + SSD_WIDTH + CONV_DIM + LANES
KEY_TILE = 128
EXPERT_BLOCK = 512
EXPERT_CHUNKS = 1
ROUTER_TILE = 128
DISPATCH_TILE = 256
COMBINE_TILE = 128
ROW_SLAB = D_MODEL // LANES
NEG_BIG = -1e30
HIGHEST = lax.Precision.HIGHEST


def _cparams(sem):
    return pltpu.CompilerParams(dimension_semantics=sem, vmem_limit_bytes=VMEM_LIMIT)


def _silu(x):
    return x * (1.0 / (1.0 + jnp.exp(-x)))


def _inproj_kernel(x_ref, w_ref, cq_ref, sq_ref, ck_ref, sk_ref,
                   q_ref, kf_ref, vf_ref, z_ref, xbc_ref, dt_ref):
    tm = x_ref.shape[0]
    xb = x_ref[...].astype(BF16)
    lane = lax.broadcasted_iota(I32, (tm, LANES), 1)
    first_half = (lane % HEAD_DIM) < (HEAD_DIM // 2)

    def rope(a, c, s):
        partner = jnp.where(first_half, pltpu.roll(a, LANES - HEAD_DIM // 2, 1), pltpu.roll(a, HEAD_DIM // 2, 1))
        return a * c + partner * s

    c0 = 0
    acc = jnp.dot(xb, w_ref[:, c0:c0 + ATT_WIDTH], preferred_element_type=F32)
    cq, sq = cq_ref[...], sq_ref[...]
    for j in range(ATT_WIDTH // LANES):
        q_ref[:, j * LANES:(j + 1) * LANES] = rope(acc[:, j * LANES:(j + 1) * LANES], cq, sq)
    c0 += ATT_WIDTH
    acc = jnp.dot(xb, w_ref[:, c0:c0 + KV_WIDTH], preferred_element_type=F32)
    ck, sk = ck_ref[...], sk_ref[...]
    for j in range(KV_WIDTH // LANES):
        kf_ref[:, j * LANES:(j + 1) * LANES] = rope(acc[:, j * LANES:(j + 1) * LANES], ck, sk)
    c0 += KV_WIDTH
    vf_ref[...] = jnp.dot(xb, w_ref[:, c0:c0 + KV_WIDTH], preferred_element_type=F32)
    c0 += KV_WIDTH
    z_ref[...] = jnp.dot(xb, w_ref[:, c0:c0 + SSD_WIDTH], preferred_element_type=F32)
    c0 += SSD_WIDTH
    xbc_ref[...] = jnp.dot(xb, w_ref[:, c0:c0 + CONV_DIM], preferred_element_type=F32)
    c0 += CONV_DIM
    dt_ref[...] = jnp.dot(xb, w_ref[:, c0:c0 + LANES], preferred_element_type=F32)


def _rope_tables(pos):
    half = HEAD_DIM // 2
    inv = ROPE_THETA ** (-jnp.arange(half, dtype=F32) / half)
    ang = pos[:, None] * inv[None, :]
    cos, sin = jnp.cos(ang), jnp.sin(ang)
    c = jnp.concatenate([cos, cos, cos, cos], axis=1)
    s = jnp.concatenate([-sin, sin, -sin, sin], axis=1)
    return c, s


def _inproj(x2d, w_bf, cos_t, sin_t, tm, tiles_per_seq):
    t = x2d.shape[0]
    scale = HEAD_DIM ** -0.5
    tab = pl.BlockSpec((tm, LANES), lambda i: (i % tiles_per_seq, 0))
    row = lambda w: pl.BlockSpec((tm, w), lambda i: (i, 0))
    outs = (
        jax.ShapeDtypeStruct((t, ATT_WIDTH), F32),
        jax.ShapeDtypeStruct((t, KV_WIDTH), F32),
        jax.ShapeDtypeStruct((t, KV_WIDTH), F32),
        jax.ShapeDtypeStruct((t, SSD_WIDTH), F32),
        jax.ShapeDtypeStruct((t, CONV_DIM), F32),
        jax.ShapeDtypeStruct((t, LANES), F32),
    )
    return pl.pallas_call(
        _inproj_kernel,
        grid=(t // tm,),
        in_specs=[row(D_MODEL), pl.BlockSpec((D_MODEL, IN_COLS_PAD), lambda i: (0, 0)), tab, tab, tab, tab],
        out_specs=(row(ATT_WIDTH), row(KV_WIDTH), row(KV_WIDTH), row(SSD_WIDTH), row(CONV_DIM), row(LANES)),
        out_shape=outs,
        compiler_params=_cparams(("parallel",)),
        name="inproj",
    )(x2d, w_bf, cos_t * scale, sin_t * scale, cos_t, sin_t)


def _branch_weight(d):
    w = np.zeros(d.shape, np.float32)
    for window, dil in DILATED_BRANCHES:
        w += ((d >= 0) & (d <= window) & (d % dil == 0)).astype(np.float32)
    return w


def _attn_decode_kernel(q_ref, kc_ref, vc_ref, kn_ref, vn_ref, wc_ref, wn_ref, o_ref, ktail, vtail):
    nq = q_ref.shape[1]
    lane = lax.broadcasted_iota(I32, (nq, LANES), 1)
    lo = lane < HEAD_DIM
    ktail[...] = jnp.zeros_like(ktail)
    vtail[...] = jnp.zeros_like(vtail)
    ktail[0:nq, :] = kn_ref[0]
    vtail[0:nq, :] = vn_ref[0]
    wc, wn = wc_ref[...], wn_ref[...]
    nt = (((1,), (1,)), ((), ()))
    for g in range(N_KV_HEADS):
        pair, half = divmod(g, 2)
        cols = slice(pair * LANES, (pair + 1) * LANES)
        q = q_ref[0, :, g * LANES:(g + 1) * LANES]
        qs = pltpu.roll(q, HEAD_DIM, 1)
        valid = lo if half == 0 else jnp.logical_not(lo)
        qa_src, qb_src = (q, qs) if half == 0 else (qs, q)
        qq = jnp.concatenate([jnp.where(valid, qa_src, 0.0), jnp.where(valid, qb_src, 0.0)], axis=0).astype(BF16)
        s_c = lax.dot_general(qq, kc_ref[0, :, cols].astype(BF16), nt, preferred_element_type=F32)
        s_n = lax.dot_general(qq, ktail[:, cols].astype(BF16), nt, preferred_element_type=F32)
        s_c = jnp.where(wc > 0.0, s_c, NEG_BIG)
        s_n = jnp.where(wn > 0.0, s_n, NEG_BIG)
        m = jnp.maximum(jnp.max(s_c, axis=1, keepdims=True), jnp.max(s_n, axis=1, keepdims=True))
        p_c = jnp.exp(s_c - m) * wc
        p_n = jnp.exp(s_n - m) * wn
        l = jnp.sum(p_c, axis=1, keepdims=True) + jnp.sum(p_n, axis=1, keepdims=True)
        acc = jnp.dot(p_c.astype(BF16), vc_ref[0, :, cols].astype(BF16), preferred_element_type=F32)
        acc = acc + jnp.dot(p_n.astype(BF16), vtail[:, cols].astype(BF16), preferred_element_type=F32)
        o = acc / l
        oa, ob = o[:nq], o[nq:]
        if half == 0:
            out = jnp.where(lo, oa, pltpu.roll(ob, HEAD_DIM, 1))
        else:
            out = jnp.where(lo, pltpu.roll(oa, HEAD_DIM, 1), ob)
        o_ref[0, :, g * LANES:(g + 1) * LANES] = out.astype(BF16)


def _attention_decode(q, k_cache, v_cache, k_new, v_new):
    b, nq, _ = q.shape
    win = k_cache.shape[1]
    i = np.arange(nq)[:, None]
    wc = _branch_weight(win + i - np.arange(win)[None, :])
    wn = _branch_weight(i - np.arange(KEY_TILE)[None, :])
    wn[:, nq:] = 0.0
    stack = lambda w: jnp.asarray(np.concatenate([w, w], axis=0))
    seq = lambda rows, w: pl.BlockSpec((1, rows, w), lambda bi: (bi, 0, 0))
    full = lambda shp: pl.BlockSpec(shp, lambda bi: (0, 0))
    return pl.pallas_call(
        _attn_decode_kernel,
        grid=(b,),
        in_specs=[seq(nq, ATT_WIDTH), seq(win, KV_WIDTH), seq(win, KV_WIDTH), seq(nq, KV_WIDTH), seq(nq, KV_WIDTH),
                  full((2 * nq, win)), full((2 * nq, KEY_TILE))],
        out_specs=seq(nq, ATT_WIDTH),
        out_shape=jax.ShapeDtypeStruct((b, nq, ATT_WIDTH), BF16),
        scratch_shapes=[pltpu.VMEM((KEY_TILE, KV_WIDTH), F32), pltpu.VMEM((KEY_TILE, KV_WIDTH), F32)],
        compiler_params=_cparams(("arbitrary",)),
        name="decode_attn",
    )(q, k_cache, v_cache, k_new, v_new, stack(wc), stack(wn))


ATT_SUPER = 2048
ATT_UNIT = 128
ATT_SPAN = max(w // d for w, d in DILATED_BRANCHES)


ATT_R4 = 4


def _attn_window_kernel(q0_ref, q1_ref, k_ref, v_ref, o_ref, k4_s, v4_s, q4_s, acc_s, m_s, l_s, tmp_s, nat_s):
    sb = q0_ref.shape[1]
    sbi = pl.program_id(2)
    u_rows, span, r4 = ATT_UNIT, ATT_SPAN, ATT_R4
    nkeys = u_rows + span
    cls = sb // r4
    look = MAX_WINDOW // r4
    units = sb // u_rows
    assert [d for _, d in DILATED_BRANCHES] == [1, r4, r4 * r4] and sb // (r4 * r4) == u_rows
    lane = lax.broadcasted_iota(I32, (u_rows, LANES), 1)
    lo = lane < HEAD_DIM
    row_i = lax.broadcasted_iota(I32, (2 * u_rows, nkeys), 0) & (u_rows - 1)
    rc = row_i - lax.broadcasted_iota(I32, (2 * u_rows, nkeys), 1)
    span_all = jnp.full((2 * u_rows, nkeys), span, I32)
    piece = 2 * u_rows

    def regroup(src_ref, dst, nat0, loc0, n):
        for c in range(r4):
            for j0 in range(0, n, piece):
                dst[c, loc0 + j0:loc0 + j0 + piece, :] = src_ref[0, pl.ds(nat0 + r4 * j0 + c, piece, stride=r4), :]

    @pl.when(sbi > 0)
    def _():
        regroup(k_ref, k4_s, sbi * sb - look * r4, 0, look)
        regroup(v_ref, v4_s, sbi * sb - look * r4, 0, look)

    @pl.when(sbi == 0)
    def _():
        k4_s[:, 0:look, :] = jnp.zeros((r4, look, LANES), F32)
        v4_s[:, 0:look, :] = jnp.zeros((r4, look, LANES), F32)

    regroup(k_ref, k4_s, sbi * sb, look, cls)
    regroup(v_ref, v4_s, sbi * sb, look, cls)
    base4 = sbi * cls - look

    for half, qh_ref in enumerate((q0_ref, q1_ref)):
        valid = lo if half == 0 else jnp.logical_not(lo)
        regroup(qh_ref, q4_s, 0, 0, cls)

        def softmax_unit(q, kt, vt, delta, limit):
            qs = pltpu.roll(q, HEAD_DIM, 1)
            qa_src, qb_src = (q, qs) if half == 0 else (qs, q)
            qq = jnp.concatenate([jnp.where(valid, qa_src, 0.0), jnp.where(valid, qb_src, 0.0)],
                                 axis=0).astype(BF16)
            s = lax.dot_general(qq, kt.astype(BF16), (((1,), (1,)), ((), ())), preferred_element_type=F32)
            dist = rc + delta
            s = jnp.where(pltpu.bitcast(dist, jnp.uint32) <= pltpu.bitcast(limit, jnp.uint32), s, NEG_BIG)
            m = jnp.max(s, axis=1, keepdims=True)
            p = jnp.exp(s - m)
            l = jnp.sum(p, axis=1, keepdims=True)
            acc = jnp.dot(p.astype(BF16), vt.astype(BF16), preferred_element_type=F32)
            return jnp.broadcast_to(m, (2 * u_rows, LANES)), jnp.broadcast_to(l, (2 * u_rows, LANES)), acc

        def merge_store(rows, m, l, acc):
            ld = lambda ref: jnp.concatenate([ref[rows[0], :], ref[rows[1], :]], axis=0)
            m_old, l_old, acc_old = ld(m_s), ld(l_s), ld(acc_s)
            m_new = jnp.maximum(m_old, m)
            a_old, a_new = jnp.exp(m_old - m_new), jnp.exp(m - m_new)
            l = a_old * l_old + a_new * l
            acc = a_old * acc_old + a_new * acc
            for h2 in (0, 1):
                sl = slice(h2 * u_rows, (h2 + 1) * u_rows)
                m_s[rows[h2], :] = m_new[sl]
                l_s[rows[h2], :] = l[sl]
                acc_s[rows[h2], :] = acc[sl]

        def unit_d1(u, c):
            q0 = pl.multiple_of(u * u_rows, u_rows)
            qpos = sbi * sb + u * u_rows
            kpos = pl.multiple_of(jnp.maximum(qpos - span, 0), u_rows)
            m, l, acc = softmax_unit(qh_ref[0, pl.ds(q0, u_rows), :], k_ref[0, pl.ds(kpos, nkeys), :],
                                     v_ref[0, pl.ds(kpos, nkeys), :], qpos - kpos, span_all)
            part = u_rows // r4
            for a, (val, dst) in enumerate(((m, m_s), (l, l_s), (acc, acc_s))):
                tmp_s[a] = val
                for h2 in (0, 1):
                    for cl in range(r4):
                        d0 = pl.multiple_of(h2 * sb + cl * cls + u * part, part)
                        dst[pl.ds(d0, part), :] = tmp_s[a, pl.ds(h2 * u_rows + cl, part, stride=r4), :]
            return c

        def unit_d4(idx, c):
            cl, u = idx & (r4 - 1), idx >> 2
            qpos = sbi * cls + u * u_rows
            kpos = jnp.maximum(qpos - span, 0)
            kloc = pl.multiple_of(kpos - base4, u_rows)
            q0 = pl.multiple_of(u * u_rows, u_rows)
            m, l, acc = softmax_unit(q4_s[cl, pl.ds(q0, u_rows), :], k4_s[cl, pl.ds(kloc, nkeys), :],
                                     v4_s[cl, pl.ds(kloc, nkeys), :], qpos - kpos, span_all)
            r0 = pl.multiple_of(cl * cls + u * u_rows, u_rows)
            merge_store((pl.ds(r0, u_rows), pl.ds(sb + r0, u_rows)), m, l, acc)
            return c

        def unit_d16(idx, c):
            cl, sg = idx & (r4 - 1), idx >> 2
            qpos = sbi * u_rows
            kloc = sg + r4 * (qpos - span) - base4
            m, l, acc = softmax_unit(q4_s[cl, pl.ds(sg, u_rows, stride=r4), :],
                                     k4_s[cl, pl.ds(kloc, nkeys, stride=r4), :],
                                     v4_s[cl, pl.ds(kloc, nkeys, stride=r4), :], span,
                                     jnp.minimum(row_i + qpos, span))
            r0 = cl * cls + sg
            merge_store((pl.ds(r0, u_rows, stride=r4), pl.ds(sb + r0, u_rows, stride=r4)), m, l, acc)
            return c

        for body in (unit_d1, unit_d4, unit_d16):
            lax.fori_loop(0, units, body, 0, unroll=4)

        for h2 in (0, 1):
            for cl in range(r4):
                for j0 in range(0, cls, piece):
                    rows = slice(h2 * sb + cl * cls + j0, h2 * sb + cl * cls + j0 + piece)
                    nat_s[h2, pl.ds(r4 * j0 + cl, piece, stride=r4), :] = acc_s[rows, :] / l_s[rows, :]
        lo2 = jnp.concatenate([lo, lo], axis=0)
        for c in range(sb // piece):
            ra = slice(c * piece, (c + 1) * piece)
            oa, ob = nat_s[0, ra, :], nat_s[1, ra, :]
            if half == 0:
                out = jnp.where(lo2, oa, pltpu.roll(ob, HEAD_DIM, 1))
            else:
                out = jnp.where(lo2, pltpu.roll(oa, HEAD_DIM, 1), ob)
            o_ref[0, ra, half * LANES:(half + 1) * LANES] = out.astype(BF16)


def _attention_window(q, k, v):
    b, l, _ = q.shape
    sb = ATT_SUPER
    max_dil = max(d for _, d in DILATED_BRANCHES)
    assert l % sb == 0 and l // max_dil >= ATT_UNIT + ATT_SPAN
    qspec = lambda h: pl.BlockSpec((1, sb, LANES), lambda bi, pi, si: (bi, si, 2 * pi + h))
    kspec = pl.BlockSpec((1, l, LANES), lambda bi, pi, si: (bi, 0, pi))
    return pl.pallas_call(
        _attn_window_kernel,
        grid=(b, KV_WIDTH // LANES, l // sb),
        in_specs=[qspec(0), qspec(1), kspec, kspec],
        out_specs=pl.BlockSpec((1, sb, 2 * LANES), lambda bi, pi, si: (bi, si, pi)),
        out_shape=jax.ShapeDtypeStruct((b, l, ATT_WIDTH), BF16),
        scratch_shapes=[pltpu.VMEM((ATT_R4, (MAX_WINDOW + sb) // ATT_R4, LANES), F32)] * 2
        + [pltpu.VMEM((ATT_R4, sb // ATT_R4, LANES), F32)]
        + [pltpu.VMEM((2 * sb, LANES), F32)] * 3
        + [pltpu.VMEM((3, 2 * ATT_UNIT, LANES), F32), pltpu.VMEM((2, sb, LANES), F32)],
        compiler_params=_cparams(("parallel", "parallel", "arbitrary")),
        name="window_attn",
    )(q, q, k, v)


def _ssd_kernel(xbc_ref, dt_ref, z_ref, ssm0_ref, conv0_ref, cw_ref, cb_ref, dtb_ref, alog_ref, dsk_ref,
                nw_ref, ex_ref, y_ref, ssm_ref, conv_ref, xpad, dtpad, s_scr, *, n_valid):
    q = SSD_CHUNK
    lb = xbc_ref.shape[1]
    ci = pl.program_id(1)
    nc = pl.num_programs(1)
    gw = SSD_WIDTH // SSD_GROUPS
    hpg = SSD_HEADS // SSD_GROUPS

    @pl.when(ci == 0)
    def _():
        xpad[0:SUBLANES, :] = jnp.zeros((SUBLANES, CONV_DIM), F32)
        xpad[SUBLANES - (CONV_W - 1):SUBLANES, :] = conv0_ref[0]
        for g in range(SSD_GROUPS):
            s_scr[g] = jnp.transpose(ssm0_ref[0, g * hpg:(g + 1) * hpg].reshape(gw, SSD_STATE))

    xpad[SUBLANES:SUBLANES + lb, :] = xbc_ref[0]
    dtpad[0:lb, :] = dt_ref[0]
    if lb < q:
        xpad[SUBLANES + lb:SUBLANES + q, :] = jnp.zeros((q - lb, CONV_DIM), F32)
        dtpad[lb:q, :] = jnp.zeros((q - lb, LANES), F32)

    conv = cb_ref[...]
    for k in range(CONV_W):
        sh = CONV_W - 1 - k
        conv = conv + xpad[SUBLANES - sh:SUBLANES - sh + q, :] * cw_ref[k:k + 1, :]
    act = _silu(conv)

    @pl.when(ci == nc - 1)
    def _():
        conv_ref[0] = xpad[SUBLANES + n_valid - (CONV_W - 1):SUBLANES + n_valid, :]

    xpad[0:SUBLANES, :] = xpad[q:q + SUBLANES, :]

    xs = act[:, :SSD_WIDTH]
    row = lax.broadcasted_iota(I32, (q, LANES), 0)
    dtr = dtpad[...] + dtb_ref[...]
    dt = jnp.maximum(dtr, 0.0) + jnp.log(1.0 + jnp.exp(-jnp.abs(dtr)))
    dt = jnp.where(row < n_valid, dt, 0.0)
    a = -jnp.exp(alog_ref[...])
    da = dt * a
    r2 = lax.broadcasted_iota(I32, (q, q), 0)
    c2 = lax.broadcasted_iota(I32, (q, q), 1)
    causal = r2 >= c2
    a_cs = jnp.dot(causal.astype(F32), da, precision=HIGHEST, preferred_element_type=F32)
    a_cst = jnp.transpose(a_cs)
    a_last = a_cs[q - 1:q, :]
    ex = ex_ref[...]
    expand = lambda t: jnp.dot(t, ex, precision=HIGHEST, preferred_element_type=F32)
    dt_x = expand(dt)
    ea_x = expand(jnp.exp(a_cs))
    te_x = expand(jnp.exp(a_last - a_cs))
    cd_x = expand(jnp.broadcast_to(jnp.exp(a_last), (SUBLANES, LANES)))[0:1, :]
    xdt = xs * dt_x
    xdt_b = xdt.astype(BF16)
    xw_b = (xdt * te_x).astype(BF16)
    lo = lax.broadcasted_iota(I32, (q, LANES), 1) < SSD_HEAD_DIM

    ys = []
    for g in range(SSD_GROUPS):
        bm = act[:, SSD_WIDTH + g * SSD_STATE:SSD_WIDTH + (g + 1) * SSD_STATE]
        cm = act[:, SSD_WIDTH + (SSD_GROUPS + g) * SSD_STATE:SSD_WIDTH + (SSD_GROUPS + g + 1) * SSD_STATE]
        bm_b, cm_b = bm.astype(BF16), cm.astype(BF16)
        cb = lax.dot_general(cm_b, bm_b, (((1,), (1,)), ((), ())), preferred_element_type=F32)
        s_old = s_scr[g]
        y_off = jnp.dot(cm_b, s_old.astype(BF16), preferred_element_type=F32)
        for jp in range(hpg // 2):
            pair = g * (hpg // 2) + jp
            yp = []
            for hh in (0, 1):
                h = 2 * pair + hh
                seg = a_cs[:, h:h + 1] - a_cst[h:h + 1, :]
                dec = jnp.exp(jnp.where(causal, seg, NEG_BIG))
                mm = (cb * dec).astype(BF16)
                yp.append(jnp.dot(mm, xdt_b[:, pair * LANES:(pair + 1) * LANES], preferred_element_type=F32))
            ys.append(jnp.where(lo, yp[0], yp[1]) + y_off[:, jp * LANES:(jp + 1) * LANES]
                      * ea_x[:, pair * LANES:(pair + 1) * LANES])
        bmt_b = jnp.transpose(bm).astype(BF16)
        s_new = s_old * cd_x[:, g * gw:(g + 1) * gw] + jnp.dot(bmt_b, xw_b[:, g * gw:(g + 1) * gw],
                                                              preferred_element_type=F32)
        s_scr[g] = s_new

    y = jnp.concatenate(ys, axis=1) + dsk_ref[...] * xs

    @pl.when(ci == nc - 1)
    def _():
        for g in range(SSD_GROUPS):
            ssm_ref[0, g * hpg:(g + 1) * hpg] = jnp.transpose(s_scr[g]).reshape(hpg, SSD_HEAD_DIM, SSD_STATE)

    hg = y[:lb] * _silu(z_ref[0])
    outs = []
    for g in range(SSD_GROUPS):
        part = hg[:, g * gw:(g + 1) * gw]
        ms = jnp.mean(part * part, axis=1, keepdims=True)
        outs.append(part * lax.rsqrt(ms + NORM_EPS))
    y_ref[0] = (jnp.concatenate(outs, axis=1) * nw_ref[...]).astype(BF16)


def _ssd(xbc, dt_raw, z, ssm0, conv0, prm, lb, n_valid):
    b, l, _ = xbc.shape
    nc = l // lb
    cw, cbias, dtb, alog, dsk, nw, ex = prm
    full = lambda a: pl.BlockSpec(a.shape, lambda bi, ci: (0,) * a.ndim)
    seq = lambda w: pl.BlockSpec((1, lb, w), lambda bi, ci: (bi, ci, 0))
    return pl.pallas_call(
        functools.partial(_ssd_kernel, n_valid=n_valid),
        grid=(b, nc),
        in_specs=[seq(CONV_DIM), seq(LANES), seq(SSD_WIDTH),
                  pl.BlockSpec((1, SSD_HEADS, SSD_HEAD_DIM, SSD_STATE), lambda bi, ci: (bi, 0, 0, 0)),
                  pl.BlockSpec((1, CONV_W - 1, CONV_DIM), lambda bi, ci: (bi, 0, 0)),
                  full(cw), full(cbias), full(dtb), full(alog), full(dsk), full(nw), full(ex)],
        out_specs=(seq(SSD_WIDTH),
                   pl.BlockSpec((1, SSD_HEADS, SSD_HEAD_DIM, SSD_STATE), lambda bi, ci: (bi, 0, 0, 0)),
                   pl.BlockSpec((1, CONV_W - 1, CONV_DIM), lambda bi, ci: (bi, 0, 0))),
        out_shape=(jax.ShapeDtypeStruct((b, l, SSD_WIDTH), BF16),
                   jax.ShapeDtypeStruct((b, SSD_HEADS, SSD_HEAD_DIM, SSD_STATE), F32),
                   jax.ShapeDtypeStruct((b, CONV_W - 1, CONV_DIM), F32)),
        scratch_shapes=[pltpu.VMEM((SSD_CHUNK + 2 * SUBLANES, CONV_DIM), F32),
                        pltpu.VMEM((SSD_CHUNK, LANES), F32),
                        pltpu.VMEM((SSD_GROUPS, SSD_STATE, SSD_WIDTH // SSD_GROUPS), F32)],
        compiler_params=_cparams(("parallel", "arbitrary")),
        name="conv_ssd",
    )(xbc, dt_raw, z, ssm0, conv0, cw, cbias, dtb, alog, dsk, nw, ex)


def _layer_norm(r, g, b):
    mu = jnp.mean(r, axis=1, keepdims=True)
    d = r - mu
    var = jnp.mean(d * d, axis=1, keepdims=True)
    return d * lax.rsqrt(var + NORM_EPS) * g + b


def _mix_kernel(att_ref, ssd_ref, x_ref, wo_ref, g_ref, b_ref, *rest):
    h_ref, hs_ref = rest[-2:]
    tm = x_ref.shape[0]
    mix = jnp.dot(att_ref[...], wo_ref[0:ATT_WIDTH, :], preferred_element_type=F32)
    mix = mix + jnp.dot(ssd_ref[...], wo_ref[ATT_WIDTH:MIX_WIDTH, :], preferred_element_type=F32)
    h = _layer_norm(DEEPNORM_ALPHA * x_ref[...] + mix, g_ref[...], b_ref[...])
    h_ref[...] = h
    for j in range(ROW_SLAB):
        hs_ref[pl.ds(j, tm, stride=ROW_SLAB), :] = h[:, j * LANES:(j + 1) * LANES]


def _mix(att, ssd, x2d, wo_bf, g, b, tm, t_total, row0=0, into=None):
    t = x2d.shape[0]
    b0 = row0 // tm
    row = lambda w: pl.BlockSpec((tm, w), lambda i: (i, 0))
    vec = pl.BlockSpec((1, D_MODEL), lambda i: (0, 0))
    hbm = pl.BlockSpec(memory_space=pl.ANY)
    extra = () if into is None else tuple(into)
    return pl.pallas_call(
        _mix_kernel,
        grid=(t // tm,),
        in_specs=[row(ATT_WIDTH), row(SSD_WIDTH), row(D_MODEL),
                  pl.BlockSpec((MIX_WIDTH, D_MODEL), lambda i: (0, 0)), vec, vec] + [hbm] * len(extra),
        out_specs=(pl.BlockSpec((tm, D_MODEL), lambda i: (b0 + i, 0)),
                   pl.BlockSpec((tm * ROW_SLAB, LANES), lambda i: (b0 + i, 0))),
        out_shape=(jax.ShapeDtypeStruct((t_total, D_MODEL), F32),
                   jax.ShapeDtypeStruct((t_total * ROW_SLAB, LANES), F32)),
        input_output_aliases={} if into is None else {6: 0, 7: 1},
        compiler_params=_cparams(("parallel",)),
        name="outproj_ln1",
    )(att, ssd, x2d, wo_bf, g, b, *extra)


def _router_kernel(h_ref, wr_ref, bias_ref, tope_ref, rank_ref, gt_ref, cnt_ref, gscr):
    tm = h_ref.shape[0]
    i = pl.program_id(0)
    per_group = N_EXPERTS // N_EXPERT_GROUPS

    @pl.when(i == 0)
    def _():
        cnt_ref[...] = jnp.zeros_like(cnt_ref)

    logits = lax.dot_general(wr_ref[...], h_ref[...], (((1,), (1,)), ((), ())),
                             precision=HIGHEST, preferred_element_type=F32)
    s = 1.0 / (1.0 + jnp.exp(-logits))
    sel = s + bias_ref[...]
    neg_inf = -jnp.inf
    e_iota = lax.broadcasted_iota(I32, (N_EXPERTS, tm), 0).astype(F32)
    g_iota = lax.broadcasted_iota(I32, (per_group, tm), 0).astype(F32)

    gscore = []
    for g in range(N_EXPERT_GROUPS):
        blk = sel[g * per_group:(g + 1) * per_group]
        m1 = jnp.max(blk, axis=0, keepdims=True)
        a1 = jnp.min(jnp.where(blk == m1, g_iota, float(per_group)), axis=0, keepdims=True)
        m2 = jnp.max(jnp.where(g_iota == a1, neg_inf, blk), axis=0, keepdims=True)
        gscore.append(m1 + m2)
    blocks = []
    for g in range(N_EXPERT_GROUPS):
        beaten = jnp.zeros((1, tm), F32)
        for o in range(N_EXPERT_GROUPS):
            if o < g:
                beaten = beaten + jnp.where(gscore[o] >= gscore[g], 1.0, 0.0)
            elif o > g:
                beaten = beaten + jnp.where(gscore[o] > gscore[g], 1.0, 0.0)
        blocks.append(jnp.where(beaten < float(TOPK_GROUPS), sel[g * per_group:(g + 1) * per_group], neg_inf))
    cand = jnp.concatenate(blocks, axis=0)

    tops, gsel = [], []
    onehot = jnp.zeros((N_EXPERTS, tm), F32)
    for _ in range(TOP_K):
        mx = jnp.max(cand, axis=0, keepdims=True)
        ix = jnp.min(jnp.where(cand == mx, e_iota, float(N_EXPERTS)), axis=0, keepdims=True)
        hit = e_iota == ix
        tops.append(ix)
        gsel.append(jnp.sum(jnp.where(hit, s, 0.0), axis=0, keepdims=True))
        onehot = jnp.where(hit, 1.0, onehot)
        cand = jnp.where(hit, neg_inf, cand)
    den = gsel[0]
    for k in range(1, TOP_K):
        den = den + gsel[k]

    t_r = lax.broadcasted_iota(I32, (tm, tm), 0)
    t_c = lax.broadcasted_iota(I32, (tm, tm), 1)
    before = (t_r < t_c).astype(BF16)
    oh_b = onehot.astype(BF16)
    base = cnt_ref[...] + jnp.dot(oh_b, before, preferred_element_type=F32)
    cnt_ref[...] = cnt_ref[...] + jnp.dot(oh_b, jnp.ones((tm, LANES), BF16), preferred_element_type=F32)

    gscr[...] = jnp.zeros_like(gscr)
    for k in range(TOP_K):
        tope_ref[k:k + 1, :] = tops[k].astype(I32)
        rank_ref[k:k + 1, :] = jnp.sum(jnp.where(e_iota == tops[k], base, 0.0), axis=0, keepdims=True).astype(I32)
        gscr[k:k + 1, :] = gsel[k] / den * ROUTED_SCALE
    gt_ref[...] = jnp.transpose(gscr[...])


def _router(h_all, wr_t, bias_b):
    t = h_all.shape[0]
    tm = ROUTER_TILE
    return pl.pallas_call(
        _router_kernel,
        grid=(t // tm,),
        in_specs=[pl.BlockSpec((tm, D_MODEL), lambda i: (i, 0)),
                  pl.BlockSpec((N_EXPERTS, D_MODEL), lambda i: (0, 0)),
                  pl.BlockSpec((N_EXPERTS, tm), lambda i: (0, 0))],
        out_specs=(pl.BlockSpec((TOP_K, tm), lambda i: (0, i)),
                   pl.BlockSpec((TOP_K, tm), lambda i: (0, i)),
                   pl.BlockSpec((tm, LANES), lambda i: (i, 0)),
                   pl.BlockSpec((N_EXPERTS, LANES), lambda i: (0, 0))),
        out_shape=(jax.ShapeDtypeStruct((TOP_K, t), I32),
                   jax.ShapeDtypeStruct((TOP_K, t), I32),
                   jax.ShapeDtypeStruct((t, LANES), F32),
                   jax.ShapeDtypeStruct((N_EXPERTS, LANES), F32)),
        scratch_shapes=[pltpu.VMEM((tm, LANES), F32)],
        compiler_params=_cparams(("arbitrary",)),
        name="router",
    )(h_all, wr_t, bias_b)


_PAD_SIZES = tuple(2 ** p for p in range(int(math.log2(EXPERT_BLOCK)) - 1, -1, -1))


def _slab(ref, row, n=1):
    return ref.at[pl.ds(pl.multiple_of(row * ROW_SLAB, ROW_SLAB), n * ROW_SLAB)]


def _padfill_kernel(pstart_ref, pcnt_ref, tail_ref, xs_ref, zero_scr, sem):
    zero_scr[...] = jnp.zeros_like(zero_scr)
    half = EXPERT_BLOCK // 2
    tail_copy = lambda j: pltpu.make_async_copy(_slab(zero_scr, 0, half), _slab(xs_ref, tail_ref[0] + j * half, half),
                                                sem)

    def tail_start(j, c):
        tail_copy(j).start()
        return c

    def tail_wait(j, c):
        tail_copy(j).wait()
        return c

    lax.fori_loop(0, tail_ref[1], tail_start, 0)

    def copies(e):
        base = pstart_ref[e]
        cnt = pcnt_ref[e]
        out = []
        for sz in _PAD_SIZES:
            out.append(((cnt & sz) != 0, pltpu.make_async_copy(_slab(zero_scr, 0, sz), _slab(xs_ref, base, sz), sem)))
            base = base + (cnt & sz)
        return out

    def start(e, c):
        for pred, cp in copies(e):
            @pl.when(pred)
            def _():
                cp.start()
        return c

    def wait(e, c):
        for pred, cp in copies(e):
            @pl.when(pred)
            def _():
                cp.wait()
        return c

    lax.fori_loop(0, N_EXPERTS, start, 0)
    lax.fori_loop(0, N_EXPERTS, wait, 0)
    lax.fori_loop(0, tail_ref[1], tail_wait, 0)


def _padfill(pad_start, pad_cnt, tail, n_rows):
    return pl.pallas_call(
        _padfill_kernel,
        grid_spec=pltpu.PrefetchScalarGridSpec(
            num_scalar_prefetch=3, grid=(1,), in_specs=[],
            out_specs=pl.BlockSpec(memory_space=pl.ANY),
            scratch_shapes=[pltpu.VMEM((EXPERT_BLOCK // 2 * ROW_SLAB, LANES), F32), pltpu.SemaphoreType.DMA]),
        out_shape=jax.ShapeDtypeStruct((n_rows * ROW_SLAB, LANES), F32),
        compiler_params=_cparams(("arbitrary",)),
        name="dispatch_padfill",
    )(pad_start, pad_cnt, tail)


def _dest_kernel(tope_ref, rank_ref, pstart_ref, dest_ref):
    tm = tope_ref.shape[1]
    e_iota = lax.broadcasted_iota(I32, (N_EXPERTS, tm), 0)
    ps = pstart_ref[...]
    for k in range(TOP_K):
        base = jnp.sum(jnp.where(e_iota == tope_ref[k:k + 1, :], ps, 0.0), axis=0, keepdims=True)
        dest_ref[k:k + 1, :] = base.astype(I32) + rank_ref[k:k + 1, :]


def _dest(top_e, rank, pstarts):
    t = top_e.shape[1]
    tm = DISPATCH_TILE
    blk = pl.BlockSpec((TOP_K, tm), lambda i: (0, i))
    return pl.pallas_call(
        _dest_kernel,
        grid=(t // tm,),
        in_specs=[blk, blk, pl.BlockSpec((N_EXPERTS, tm), lambda i: (0, 0))],
        out_specs=blk,
        out_shape=jax.ShapeDtypeStruct((TOP_K, t), I32),
        compiler_params=_cparams(("parallel",)),
        name="dispatch_dest",
    )(top_e, rank, jnp.broadcast_to(pstarts.astype(F32)[:, None], (N_EXPERTS, tm)))


def _dest_tiles(dest, tm):
    t = dest.shape[1]
    return jnp.transpose(dest.reshape(TOP_K, t // tm, tm), (1, 0, 2)).reshape(t // tm, TOP_K * tm)


def _dispatch_kernel(dest_ref, h_ref, xs_in_ref, xs_ref, idx_smem, isem, sem):
    del xs_in_ref
    i = pl.program_id(0)
    tm = DISPATCH_TILE
    cp = pltpu.make_async_copy(dest_ref.at[i], idx_smem, isem)
    cp.start()
    cp.wait()

    def row_copy(t, k):
        return pltpu.make_async_copy(_slab(h_ref, t), _slab(xs_ref, idx_smem[k * tm + t]), sem)

    def start(t, c):
        for k in range(TOP_K):
            row_copy(t, k).start(priority=k % 2)
        return c

    def wait(t, c):
        for k in range(TOP_K):
            row_copy(t, k).wait()
        return c

    lax.fori_loop(0, tm, start, 0)
    lax.fori_loop(0, tm, wait, 0)


def _dispatch(dest_tiles, h_slab, xs):
    tm = DISPATCH_TILE
    return pl.pallas_call(
        _dispatch_kernel,
        grid=(dest_tiles.shape[0],),
        in_specs=[pl.BlockSpec(dest_tiles.shape, lambda i: (0, 0)),
                  pl.BlockSpec((tm * ROW_SLAB, LANES), lambda i: (i, 0)),
                  pl.BlockSpec(memory_space=pl.ANY)],
        out_specs=pl.BlockSpec(memory_space=pl.ANY),
        out_shape=jax.ShapeDtypeStruct(xs.shape, xs.dtype),
        scratch_shapes=[pltpu.SMEM((tm * TOP_K,), I32), pltpu.SemaphoreType.DMA, pltpu.SemaphoreType.DMA],
        input_output_aliases={2: 0},
        compiler_params=_cparams(("arbitrary",)),
        name="dispatch_rows",
    )(dest_tiles, h_slab, xs)


def _expert_kernel(bi_ref, be_ref, nu_ref, x_ref, wg_ref, wu_ref, wd_ref, y_ref, wg_b, wu_b, wd_b):
    del bi_ref
    i = pl.program_id(0)

    @pl.when(i < nu_ref[0])
    def _():
        prev = be_ref[jnp.maximum(i - 1, 0)]

        @pl.when(jnp.logical_or(i == 0, be_ref[i] != prev))
        def _():
            wg_b[...] = wg_ref[0].astype(BF16)
            wu_b[...] = wu_ref[0].astype(BF16)
            wd_b[...] = wd_ref[0].astype(BF16)

        rows = EXPERT_BLOCK // EXPERT_CHUNKS
        for c in range(EXPERT_CHUNKS):
            r0 = c * rows * ROW_SLAB
            xb = jnp.concatenate([x_ref[pl.ds(r0 + j, rows, stride=ROW_SLAB), :] for j in range(ROW_SLAB)],
                                 axis=1).astype(BF16)
            g = jnp.dot(xb, wg_b[...], preferred_element_type=F32)
            u = jnp.dot(xb, wu_b[...], preferred_element_type=F32)
            a = (_silu(g) * u).astype(BF16)
            y = jnp.dot(a, wd_b[...], preferred_element_type=F32)
            for j in range(ROW_SLAB):
                y_ref[pl.ds(r0 + j, rows, stride=ROW_SLAB), :] = y[:, j * LANES:(j + 1) * LANES]

    @pl.when(i >= nu_ref[0])
    def _():
        y_ref[...] = jnp.zeros_like(y_ref)


def _experts(blk_idx, blk_exp, n_used, xs, wg, wu, wd):
    n_rows = xs.shape[0] // ROW_SLAB
    nb = n_rows // EXPERT_BLOCK
    rows = pl.BlockSpec((EXPERT_BLOCK * ROW_SLAB, LANES), lambda i, bi, be, nu: (bi[i], 0))
    out_rows = pl.BlockSpec((EXPERT_BLOCK * ROW_SLAB, LANES), lambda i, bi, be, nu: (i, 0))
    return pl.pallas_call(
        _expert_kernel,
        grid_spec=pltpu.PrefetchScalarGridSpec(
            num_scalar_prefetch=3, grid=(nb,),
            in_specs=[rows,
                      pl.BlockSpec((1, D_MODEL, EXPERT_FF), lambda i, bi, be, nu: (be[i], 0, 0)),
                      pl.BlockSpec((1, D_MODEL, EXPERT_FF), lambda i, bi, be, nu: (be[i], 0, 0)),
                      pl.BlockSpec((1, EXPERT_FF, D_MODEL), lambda i, bi, be, nu: (be[i], 0, 0))],
            out_specs=out_rows,
            scratch_shapes=[pltpu.VMEM((D_MODEL, EXPERT_FF), BF16), pltpu.VMEM((D_MODEL, EXPERT_FF), BF16),
                            pltpu.VMEM((EXPERT_FF, D_MODEL), BF16)]),
        out_shape=jax.ShapeDtypeStruct(xs.shape, F32),
        compiler_params=_cparams(("arbitrary",)),
        name="expert_ffn",
    )(blk_idx, blk_exp, n_used, xs, wg, wu, wd)


def _combine_kernel(dest_ref, h_ref, gt_ref, ys_ref, sg_ref, su_ref, sd_ref, g_ref, b_ref, o_ref, o2_ref,
                    idx_smem, gbuf, isem, sem, *, n_first):
    i = pl.program_id(0)
    n = pl.num_programs(0)
    tm = h_ref.shape[0]

    def idx_copy(step, s):
        return pltpu.make_async_copy(dest_ref.at[step], idx_smem.at[s], isem.at[s])

    def row_copy(s, t, k, src_row):
        return pltpu.make_async_copy(_slab(ys_ref, src_row), _slab(gbuf.at[s, k], t), sem.at[s])

    def issue(s):
        def body(t, c):
            for k in range(TOP_K):
                row_copy(s, t, k, idx_smem[s, k * tm + t]).start(priority=k % 2)
            return c
        lax.fori_loop(0, tm, body, 0)

    def drain(s):
        def body(t, c):
            for k in range(TOP_K):
                row_copy(s, t, k, 0).wait()
            return c
        lax.fori_loop(0, tm, body, 0)

    @pl.when(i == 0)
    def _():
        first = idx_copy(0, 0)
        first.start()
        first.wait()
        issue(0)

        @pl.when(n > 1)
        def _():
            idx_copy(1, 1).start()

    def step(slot):
        nslot = 1 - slot

        @pl.when(i + 1 < n)
        def _():
            idx_copy(i + 1, nslot).wait()
            issue(nslot)

            @pl.when(i + 2 < n)
            def _():
                idx_copy(i + 2, slot).start()

        h = h_ref[...]
        hb = h.astype(BF16)
        g = jnp.dot(hb, sg_ref[...], preferred_element_type=F32)
        u = jnp.dot(hb, su_ref[...], preferred_element_type=F32)
        f = jnp.dot((_silu(g) * u).astype(BF16), sd_ref[...], preferred_element_type=F32)
        drain(slot)
        gt = gt_ref[...]
        cols = []
        for j in range(ROW_SLAB):
            fj = f[:, j * LANES:(j + 1) * LANES]
            for k in range(TOP_K):
                fj = fj + gbuf[slot, k, pl.ds(j, tm, stride=ROW_SLAB), :] * gt[:, k:k + 1]
            cols.append(fj)
        f = jnp.concatenate(cols, axis=1)
        out = _layer_norm(DEEPNORM_ALPHA * h + f, g_ref[...], b_ref[...])

        @pl.when(i < n_first)
        def _():
            o_ref[...] = out

        @pl.when(i == n_first)
        def _():
            o2_ref[...] = out

    for parity in (0, 1):
        pl.when((i & 1) == parity)(functools.partial(step, parity))


def _combine(dest_tiles, h_all, gates_t, ys, sg_bf, su_bf, sd_bf, g, b, n_first):
    t = h_all.shape[0]
    tm = COMBINE_TILE
    vec = pl.BlockSpec((1, D_MODEL), lambda i: (0, 0))
    return pl.pallas_call(
        functools.partial(_combine_kernel, n_first=n_first),
        grid=(t // tm,),
        in_specs=[pl.BlockSpec(dest_tiles.shape, lambda i: (0, 0)),
                  pl.BlockSpec((tm, D_MODEL), lambda i: (i, 0)),
                  pl.BlockSpec((tm, LANES), lambda i: (i, 0)),
                  pl.BlockSpec(memory_space=pl.ANY),
                  pl.BlockSpec((D_MODEL, SHARED_FF), lambda i: (0, 0)),
                  pl.BlockSpec((D_MODEL, SHARED_FF), lambda i: (0, 0)),
                  pl.BlockSpec((SHARED_FF, D_MODEL), lambda i: (0, 0)), vec, vec],
        out_specs=(pl.BlockSpec((tm, D_MODEL), lambda i: (jnp.minimum(i, n_first - 1), 0)),
                   pl.BlockSpec((tm, D_MODEL), lambda i: (0, 0))),
        out_shape=(jax.ShapeDtypeStruct((n_first * tm, D_MODEL), F32), jax.ShapeDtypeStruct((tm, D_MODEL), F32)),
        scratch_shapes=[pltpu.SMEM((2, tm * TOP_K), I32), pltpu.VMEM((2, TOP_K, tm * ROW_SLAB, LANES), F32),
                        pltpu.SemaphoreType.DMA((2,)), pltpu.SemaphoreType.DMA((2,))],
        compiler_params=_cparams(("arbitrary",)),
        name="combine_ln2",
    )(dest_tiles, h_all, gates_t, ys, sg_bf, su_bf, sd_bf, g, b)


def _moe(h_all, h_slab, n_first, wr, rbias, wg, wu, wd, sg, su, sd, g2, b2):
    t = h_all.shape[0]
    top_e, rank, gates_t, cnt = _router(h_all, jnp.transpose(wr),
                                        jnp.broadcast_to(rbias[:, None], (N_EXPERTS, ROUTER_TILE)))
    counts = cnt[:, 0].astype(I32)
    pcounts = (counts + EXPERT_BLOCK - 1) // EXPERT_BLOCK * EXPERT_BLOCK
    pends = jnp.cumsum(pcounts)
    pstarts = pends - pcounts
    n_rows = (t * TOP_K // EXPERT_BLOCK + N_EXPERTS - 1) * EXPERT_BLOCK
    nb = n_rows // EXPERT_BLOCK
    n_used = (pends[-1] // EXPERT_BLOCK).astype(I32)
    blk_idx = jnp.minimum(jnp.arange(nb, dtype=I32), n_used - 1)
    blk_exp = jnp.minimum(jnp.sum((pends[None, :] <= (blk_idx * EXPERT_BLOCK)[:, None]).astype(I32), axis=1),
                          N_EXPERTS - 1)
    dest = _dest(top_e, rank, pstarts)
    tail = jnp.stack([pends[-1], (n_rows - pends[-1]) // (EXPERT_BLOCK // 2)]).astype(I32)
    xs = _padfill((pstarts + counts).astype(I32), (pcounts - counts).astype(I32), tail, n_rows)
    xs = _dispatch(_dest_tiles(dest, DISPATCH_TILE), h_slab, xs)
    ys = _experts(blk_idx, blk_exp, n_used.reshape(1), xs, wg, wu, wd)
    return _combine(_dest_tiles(dest, COMBINE_TILE), h_all, gates_t, ys,
                    sg.astype(BF16), su.astype(BF16), sd.astype(BF16), g2[None, :], b2[None, :], n_first)


def _expand_matrix():
    h = np.arange(LANES)[:, None]
    c = np.arange(SSD_WIDTH)[None, :]
    return jnp.asarray((c // SSD_HEAD_DIM == h).astype(np.float32))


def kernel(x_prompt, x_sample, cache_k, cache_v, state_ssm, state_conv, w_in, conv_w, conv_b, dt_bias, a_log, d_skip, ssd_norm_w, w_out, ln1_g, ln1_b, w_router, router_bias, w_exp_gate, w_exp_up, w_exp_down, w_sh_gate, w_sh_up, w_sh_down, ln2_g, ln2_b):
    bp, lp, _ = x_prompt.shape
    bs, ls, _ = x_sample.shape
    win = cache_k.shape[2]
    keep = min(MAX_WINDOW, lp)
    assert lp % SSD_CHUNK == 0 and ls <= SUBLANES and win % KEY_TILE == 0 and win >= MAX_WINDOW

    w_bf = jnp.pad(w_in[0], ((0, 0), (0, IN_COLS_PAD - IN_COLS))).astype(BF16)
    wo_bf = w_out[0].astype(BF16)
    pad_l = lambda v: jnp.pad(v, (0, LANES - v.shape[0]))[None, :]
    ssd_prm = (conv_w[0], conv_b[0][None, :], pad_l(dt_bias[0]), pad_l(a_log[0]),
               jnp.repeat(d_skip[0], SSD_HEAD_DIM)[None, :], ssd_norm_w[0][None, :], _expand_matrix())

    tp = bp * lp
    tm_p = 256 if lp % 256 == 0 else SSD_CHUNK
    tm_mix = 512 if tp % 512 == 0 else SSD_CHUNK
    cos_p, sin_p = _rope_tables(jnp.arange(lp, dtype=F32))
    xp2 = x_prompt.reshape(tp, D_MODEL)
    q, kf, vf, z, xbc, dtr = _inproj(xp2, w_bf, cos_p, sin_p, tm_p, lp // tm_p)
    att = _attention_window(q.reshape(bp, lp, ATT_WIDTH), kf.reshape(bp, lp, KV_WIDTH), vf.reshape(bp, lp, KV_WIDTH))
    ssd_y, ssm_p, conv_p = _ssd(xbc.reshape(bp, lp, CONV_DIM), dtr.reshape(bp, lp, LANES),
                                z.reshape(bp, lp, SSD_WIDTH),
                                jnp.zeros((bp, SSD_HEADS, SSD_HEAD_DIM, SSD_STATE), F32),
                                jnp.zeros((bp, CONV_W - 1, CONV_DIM), F32), ssd_prm, SSD_CHUNK, SSD_CHUNK)
    ts = bs * ls
    t_pad = -(-(tp + ts) // DISPATCH_TILE) * DISPATCH_TILE
    tail = t_pad - tp
    assert tp % tail == 0 and ts <= COMBINE_TILE and tp % COMBINE_TILE == 0
    h_bufs = _mix(att.reshape(tp, ATT_WIDTH), ssd_y.reshape(tp, SSD_WIDTH), xp2, wo_bf, ln1_g, ln1_b, tm_mix, t_pad)

    pos_s = (PAST_LEN + jnp.arange(ls, dtype=jnp.int32)).astype(F32)
    cos_s, sin_s = _rope_tables(jnp.tile(pos_s, bs))
    xs2 = x_sample.reshape(ts, D_MODEL)
    q_s, kf_s, vf_s, z_s, xbc_s, dtr_s = _inproj(xs2, w_bf, cos_s, sin_s, ts, 1)
    rows8 = lambda a, w: jnp.pad(a.reshape(bs, ls, w), ((0, 0), (0, SUBLANES - ls), (0, 0)))
    att_s = _attention_decode(rows8(q_s, ATT_WIDTH), cache_k[0].reshape(bs, win, KV_WIDTH),
                              cache_v[0].reshape(bs, win, KV_WIDTH), rows8(kf_s, KV_WIDTH),
                              rows8(vf_s, KV_WIDTH))[:, :ls]
    ssd_s, ssm_s, conv_s = _ssd(rows8(xbc_s, CONV_DIM), rows8(dtr_s, LANES), rows8(z_s, SSD_WIDTH),
                                state_ssm[0], state_conv[0], ssd_prm, SUBLANES, ls)
    tail_rows = lambda a: jnp.pad(a, ((0, tail - ts), (0, 0)))
    h_all, h_slab = _mix(tail_rows(att_s.reshape(ts, ATT_WIDTH)), tail_rows(ssd_s[:, :ls].reshape(ts, SSD_WIDTH)),
                         tail_rows(xs2), wo_bf, ln1_g, ln1_b, tail, t_pad, row0=tp, into=h_bufs)

    y_p, y_s = _moe(h_all, h_slab, tp // COMBINE_TILE, w_router[0], router_bias[0], w_exp_gate[0], w_exp_up[0],
                    w_exp_down[0], w_sh_gate[0], w_sh_up[0], w_sh_down[0], ln2_g[0], ln2_b[0])

    kv5 = lambda a, b, l: a.reshape(1, b, l, N_KV_HEADS, HEAD_DIM)
    return (y_p.reshape(bp, lp, D_MODEL), y_s[:ts].reshape(bs, ls, D_MODEL),
            kv5(kf.reshape(bp, lp, KV_WIDTH)[:, lp - keep:], bp, keep),
            kv5(vf.reshape(bp, lp, KV_WIDTH)[:, lp - keep:], bp, keep),
            ssm_p[None], conv_p[None],
            kv5(kf_s, bs, ls), kv5(vf_s, bs, ls), ssm_s[None], conv_s[None])
```

```python
import functools
import math

import jax
import jax.numpy as jnp
import numpy as np
from jax import lax
from jax.experimental import pallas as pl
from jax.experimental.pallas import tpu as pltpu

F32 = jnp.float32
BF16 = jnp.bfloat16
I32 = jnp.int32

D_MODEL = 1024
PAST_LEN = 16384
HEAD_DIM = 64
N_ATT_HEADS = 16
N_KV_HEADS = 8
ATT_WIDTH = N_ATT_HEADS * HEAD_DIM
KV_WIDTH = N_KV_HEADS * HEAD_DIM
DILATED_BRANCHES = ((128, 1), (512, 4), (2048, 16))
MAX_WINDOW = 2048
ROPE_THETA = 10000.0
SSD_HEADS = 16
SSD_HEAD_DIM = 64
SSD_WIDTH = SSD_HEADS * SSD_HEAD_DIM
SSD_GROUPS = 2
SSD_STATE = 128
SSD_CHUNK = 128
CONV_W = 4
CONV_DIM = SSD_WIDTH + 2 * SSD_GROUPS * SSD_STATE
MIX_WIDTH = ATT_WIDTH + SSD_WIDTH
IN_COLS = ATT_WIDTH + 2 * KV_WIDTH + SSD_WIDTH + CONV_DIM + SSD_HEADS
N_EXPERTS = 256
TOP_K = 8
N_EXPERT_GROUPS = 8
TOPK_GROUPS = 4
EXPERT_FF = 256
SHARED_FF = 256
ROUTED_SCALE = 2.5
DEPTH = 1
DEEPNORM_ALPHA = (2.0 * DEPTH) ** 0.25
NORM_EPS = 1e-5

LANES = 128
SUBLANES = 8
VMEM_LIMIT = 56 * 1024 * 1024

IN_COLS_PAD = ATT_WIDTH + 2 * KV_WIDTH + SSD_WIDTH + CONV_DIM + LANES
KEY_TILE = 128
EXPERT_BLOCK = 512
EXPERT_CHUNKS = 1
ROUTER_TILE = 128
DISPATCH_TILE = 256
COMBINE_TILE = 128
ROW_SLAB = D_MODEL // LANES
NEG_BIG = -1e30
HIGHEST = lax.Precision.HIGHEST


def _cparams(sem):
    return pltpu.CompilerParams(dimension_semantics=sem, vmem_limit_bytes=VMEM_LIMIT)


def _silu(x):
    return x * (1.0 / (1.0 + jnp.exp(-x)))


def _inproj_kernel(x_ref, w_ref, cq_ref, sq_ref, ck_ref, sk_ref,
                   q_ref, kf_ref, vf_ref, z_ref, xbc_ref, dt_ref):
    tm = x_ref.shape[0]
    xb = x_ref[...].astype(BF16)
    lane = lax.broadcasted_iota(I32, (tm, LANES), 1)
    first_half = (lane % HEAD_DIM) < (HEAD_DIM // 2)

    def rope(a, c, s):
        partner = jnp.where(first_half, pltpu.roll(a, LANES - HEAD_DIM // 2, 1), pltpu.roll(a, HEAD_DIM // 2, 1))
        return a * c + partner * s

    c0 = 0
    acc = jnp.dot(xb, w_ref[:, c0:c0 + ATT_WIDTH], preferred_element_type=F32)
    cq, sq = cq_ref[...], sq_ref[...]
    for j in range(ATT_WIDTH // LANES):
        q_ref[:, j * LANES:(j + 1) * LANES] = rope(acc[:, j * LANES:(j + 1) * LANES], cq, sq)
    c0 += ATT_WIDTH
    acc = jnp.dot(xb, w_ref[:, c0:c0 + KV_WIDTH], preferred_element_type=F32)
    ck, sk = ck_ref[...], sk_ref[...]
    for j in range(KV_WIDTH // LANES):
        kf_ref[:, j * LANES:(j + 1) * LANES] = rope(acc[:, j * LANES:(j + 1) * LANES], ck, sk)
    c0 += KV_WIDTH
    vf_ref[...] = jnp.dot(xb, w_ref[:, c0:c0 + KV_WIDTH], preferred_element_type=F32)
    c0 += KV_WIDTH
    z_ref[...] = jnp.dot(xb, w_ref[:, c0:c0 + SSD_WIDTH], preferred_element_type=F32)
    c0 += SSD_WIDTH
    xbc_ref[...] = jnp.dot(xb, w_ref[:, c0:c0 + CONV_DIM], preferred_element_type=F32)
    c0 += CONV_DIM
    dt_ref[...] = jnp.dot(xb, w_ref[:, c0:c0 + LANES], preferred_element_type=F32)


def _rope_tables(pos):
    half = HEAD_DIM // 2
    inv = ROPE_THETA ** (-jnp.arange(half, dtype=F32) / half)
    ang = pos[:, None] * inv[None, :]
    cos, sin = jnp.cos(ang), jnp.sin(ang)
    c = jnp.concatenate([cos, cos, cos, cos], axis=1)
    s = jnp.concatenate([-sin, sin, -sin, sin], axis=1)
    return c, s


def _inproj(x2d, w_bf, cos_t, sin_t, tm, tiles_per_seq):
    t = x2d.shape[0]
    scale = HEAD_DIM ** -0.5
    tab = pl.BlockSpec((tm, LANES), lambda i: (i % tiles_per_seq, 0))
    row = lambda w: pl.BlockSpec((tm, w), lambda i: (i, 0))
    outs = (
        jax.ShapeDtypeStruct((t, ATT_WIDTH), F32),
        jax.ShapeDtypeStruct((t, KV_WIDTH), F32),
        jax.ShapeDtypeStruct((t, KV_WIDTH), F32),
        jax.ShapeDtypeStruct((t, SSD_WIDTH), F32),
        jax.ShapeDtypeStruct((t, CONV_DIM), F32),
        jax.ShapeDtypeStruct((t, LANES), F32),
    )
    return pl.pallas_call(
        _inproj_kernel,
        grid=(t // tm,),
        in_specs=[row(D_MODEL), pl.BlockSpec((D_MODEL, IN_COLS_PAD), lambda i: (0, 0)), tab, tab, tab, tab],
        out_specs=(row(ATT_WIDTH), row(KV_WIDTH), row(KV_WIDTH), row(SSD_WIDTH), row(CONV_DIM), row(LANES)),
        out_shape=outs,
        compiler_params=_cparams(("parallel",)),
        name="inproj",
    )(x2d, w_bf, cos_t * scale, sin_t * scale, cos_t, sin_t)


def _branch_weight(d):
    w = np.zeros(d.shape, np.float32)
    for window, dil in DILATED_BRANCHES:
        w += ((d >= 0) & (d <= window) & (d % dil == 0)).astype(np.float32)
    return w


def _attn_decode_kernel(q_ref, kc_ref, vc_ref, kn_ref, vn_ref, wc_ref, wn_ref, o_ref, ktail, vtail):
    nq = q_ref.shape[1]
    lane = lax.broadcasted_iota(I32, (nq, LANES), 1)
    lo = lane < HEAD_DIM
    ktail[...] = jnp.zeros_like(ktail)
    vtail[...] = jnp.zeros_like(vtail)
    ktail[0:nq, :] = kn_ref[0]
    vtail[0:nq, :] = vn_ref[0]
    wc, wn = wc_ref[...], wn_ref[...]
    nt = (((1,), (1,)), ((), ()))
    for g in range(N_KV_HEADS):
        pair, half = divmod(g, 2)
        cols = slice(pair * LANES, (pair + 1) * LANES)
        q = q_ref[0, :, g * LANES:(g + 1) * LANES]
        qs = pltpu.roll(q, HEAD_DIM, 1)
        valid = lo if half == 0 else jnp.logical_not(lo)
        qa_src, qb_src = (q, qs) if half == 0 else (qs, q)
        qq = jnp.concatenate([jnp.where(valid, qa_src, 0.0), jnp.where(valid, qb_src, 0.0)], axis=0).astype(BF16)
        s_c = lax.dot_general(qq, kc_ref[0, :, cols].astype(BF16), nt, preferred_element_type=F32)
        s_n = lax.dot_general(qq, ktail[:, cols].astype(BF16), nt, preferred_element_type=F32)
        s_c = jnp.where(wc > 0.0, s_c, NEG_BIG)
        s_n = jnp.where(wn > 0.0, s_n, NEG_BIG)
        m = jnp.maximum(jnp.max(s_c, axis=1, keepdims=True), jnp.max(s_n, axis=1, keepdims=True))
        p_c = jnp.exp(s_c - m) * wc
        p_n = jnp.exp(s_n - m) * wn
        l = jnp.sum(p_c, axis=1, keepdims=True) + jnp.sum(p_n, axis=1, keepdims=True)
        acc = jnp.dot(p_c.astype(BF16), vc_ref[0, :, cols].astype(BF16), preferred_element_type=F32)
        acc = acc + jnp.dot(p_n.astype(BF16), vtail[:, cols].astype(BF16), preferred_element_type=F32)
        o = acc / l
        oa, ob = o[:nq], o[nq:]
        if half == 0:
            out = jnp.where(lo, oa, pltpu.roll(ob, HEAD_DIM, 1))
        else:
            out = jnp.where(lo, pltpu.roll(oa, HEAD_DIM, 1), ob)
        o_ref[0, :, g * LANES:(g + 1) * LANES] = out.astype(BF16)


def _attention_decode(q, k_cache, v_cache, k_new, v_new):
    b, nq, _ = q.shape
    win = k_cache.shape[1]
    i = np.arange(nq)[:, None]
    wc = _branch_weight(win + i - np.arange(win)[None, :])
    wn = _branch_weight(i - np.arange(KEY_TILE)[None, :])
    wn[:, nq:] = 0.0
    stack = lambda w: jnp.asarray(np.concatenate([w, w], axis=0))
    seq = lambda rows, w: pl.BlockSpec((1, rows, w), lambda bi: (bi, 0, 0))
    full = lambda shp: pl.BlockSpec(shp, lambda bi: (0, 0))
    return pl.pallas_call(
        _attn_decode_kernel,
        grid=(b,),
        in_specs=[seq(nq, ATT_WIDTH), seq(win, KV_WIDTH), seq(win, KV_WIDTH), seq(nq, KV_WIDTH), seq(nq, KV_WIDTH),
                  full((2 * nq, win)), full((2 * nq, KEY_TILE))],
        out_specs=seq(nq, ATT_WIDTH),
        out_shape=jax.ShapeDtypeStruct((b, nq, ATT_WIDTH), BF16),
        scratch_shapes=[pltpu.VMEM((KEY_TILE, KV_WIDTH), F32), pltpu.VMEM((KEY_TILE, KV_WIDTH), F32)],
        compiler_params=_cparams(("arbitrary",)),
        name="decode_attn",
    )(q, k_cache, v_cache, k_new, v_new, stack(wc), stack(wn))


ATT_SUPER = 2048
ATT_UNIT = 128
ATT_SPAN = max(w // d for w, d in DILATED_BRANCHES)


ATT_R4 = 4


def _attn_window_kernel(q0_ref, q1_ref, k_ref, v_ref, o_ref, k4_s, v4_s, q4_s, acc_s, m_s, l_s, tmp_s, nat_s):
    sb = q0_ref.shape[1]
    sbi = pl.program_id(2)
    u_rows, span, r4 = ATT_UNIT, ATT_SPAN, ATT_R4
    nkeys = u_rows + span
    cls = sb // r4
    look = MAX_WINDOW // r4
    units = sb // u_rows
    assert [d for _, d in DILATED_BRANCHES] == [1, r4, r4 * r4] and sb // (r4 * r4) == u_rows
    lane = lax.broadcasted_iota(I32, (u_rows, LANES), 1)
    lo = lane < HEAD_DIM
    row_i = lax.broadcasted_iota(I32, (2 * u_rows, nkeys), 0) & (u_rows - 1)
    rc = row_i - lax.broadcasted_iota(I32, (2 * u_rows, nkeys), 1)
    span_all = jnp.full((2 * u_rows, nkeys), span, I32)
    piece = 2 * u_rows

    def regroup(src_ref, dst, nat0, loc0, n):
        for c in range(r4):
            for j0 in range(0, n, piece):
                dst[c, loc0 + j0:loc0 + j0 + piece, :] = src_ref[0, pl.ds(nat0 + r4 * j0 + c, piece, stride=r4), :]

    @pl.when(sbi > 0)
    def _():
        regroup(k_ref, k4_s, sbi * sb - look * r4, 0, look)
        regroup(v_ref, v4_s, sbi * sb - look * r4, 0, look)

    @pl.when(sbi == 0)
    def _():
        k4_s[:, 0:look, :] = jnp.zeros((r4, look, LANES), F32)
        v4_s[:, 0:look, :] = jnp.zeros((r4, look, LANES), F32)

    regroup(k_ref, k4_s, sbi * sb, look, cls)
    regroup(v_ref, v4_s, sbi * sb, look, cls)
    base4 = sbi * cls - look

    for half, qh_ref in enumerate((q0_ref, q1_ref)):
        valid = lo if half == 0 else jnp.logical_not(lo)
        regroup(qh_ref, q4_s, 0, 0, cls)

        def softmax_unit(q, kt, vt, delta, limit):
            qs = pltpu.roll(q, HEAD_DIM, 1)
            qa_src, qb_src = (q, qs) if half == 0 else (qs, q)
            qq = jnp.concatenate([jnp.where(valid, qa_src, 0.0), jnp.where(valid, qb_src, 0.0)],
                                 axis=0).astype(BF16)
            s = lax.dot_general(qq, kt.astype(BF16), (((1,), (1,)), ((), ())), preferred_element_type=F32)
            dist = rc + delta
            s = jnp.where(pltpu.bitcast(dist, jnp.uint32) <= pltpu.bitcast(limit, jnp.uint32), s, NEG_BIG)
            m = jnp.max(s, axis=1, keepdims=True)
            p = jnp.exp(s - m)
            l = jnp.sum(p, axis=1, keepdims=True)
            acc = jnp.dot(p.astype(BF16), vt.astype(BF16), preferred_element_type=F32)
            return jnp.broadcast_to(m, (2 * u_rows, LANES)), jnp.broadcast_to(l, (2 * u_rows, LANES)), acc

        def merge_store(rows, m, l, acc):
            ld = lambda ref: jnp.concatenate([ref[rows[0], :], ref[rows[1], :]], axis=0)
            m_old, l_old, acc_old = ld(m_s), ld(l_s), ld(acc_s)
            m_new = jnp.maximum(m_old, m)
            a_old, a_new = jnp.exp(m_old - m_new), jnp.exp(m - m_new)
            l = a_old * l_old + a_new * l
            acc = a_old * acc_old + a_new * acc
            for h2 in (0, 1):
                sl = slice(h2 * u_rows, (h2 + 1) * u_rows)
                m_s[rows[h2], :] = m_new[sl]
                l_s[rows[h2], :] = l[sl]
                acc_s[rows[h2], :] = acc[sl]

        def unit_d1(u, c):
            q0 = pl.multiple_of(u * u_rows, u_rows)
            qpos = sbi * sb + u * u_rows
            kpos = pl.multiple_of(jnp.maximum(qpos - span, 0), u_rows)
            m, l, acc = softmax_unit(qh_ref[0, pl.ds(q0, u_rows), :], k_ref[0, pl.ds(kpos, nkeys), :],
                                     v_ref[0, pl.ds(kpos, nkeys), :], qpos - kpos, span_all)
            part = u_rows // r4
            for a, (val, dst) in enumerate(((m, m_s), (l, l_s), (acc, acc_s))):
                tmp_s[a] = val
                for h2 in (0, 1):
                    for cl in range(r4):
                        d0 = pl.multiple_of(h2 * sb + cl * cls + u * part, part)
                        dst[pl.ds(d0, part), :] = tmp_s[a, pl.ds(h2 * u_rows + cl, part, stride=r4), :]
            return c

        def unit_d4(idx, c):
            cl, u = idx & (r4 - 1), idx >> 2
            qpos = sbi * cls + u * u_rows
            kpos = jnp.maximum(qpos - span, 0)
            kloc = pl.multiple_of(kpos - base4, u_rows)
            q0 = pl.multiple_of(u * u_rows, u_rows)
            m, l, acc = softmax_unit(q4_s[cl, pl.ds(q0, u_rows), :], k4_s[cl, pl.ds(kloc, nkeys), :],
                                     v4_s[cl, pl.ds(kloc, nkeys), :], qpos - kpos, span_all)
            r0 = pl.multiple_of(cl * cls + u * u_rows, u_rows)
            merge_store((pl.ds(r0, u_rows), pl.ds(sb + r0, u_rows)), m, l, acc)
            return c

        def unit_d16(idx, c):
            cl, sg = idx & (r4 - 1), idx >> 2
            qpos = sbi * u_rows
            kloc = sg + r4 * (qpos - span) - base4
            m, l, acc = softmax_unit(q4_s[cl, pl.ds(sg, u_rows, stride=r4), :],
                                     k4_s[cl, pl.ds(kloc, nkeys, stride=r4), :],
                                     v4_s[cl, pl.ds(kloc, nkeys, stride=r4), :], span,
                                     jnp.minimum(row_i + qpos, span))
            r0 = cl * cls + sg
            merge_store((pl.ds(r0, u_rows, stride=r4), pl.ds(sb + r0, u_rows, stride=r4)), m, l, acc)
            return c

        for body in (unit_d1, unit_d4, unit_d16):
            lax.fori_loop(0, units, body, 0, unroll=8)

        for h2 in (0, 1):
            for cl in range(r4):
                for j0 in range(0, cls, piece):
                    rows = slice(h2 * sb + cl * cls + j0, h2 * sb + cl * cls + j0 + piece)
                    nat_s[h2, pl.ds(r4 * j0 + cl, piece, stride=r4), :] = acc_s[rows, :] / l_s[rows, :]
        lo2 = jnp.concatenate([lo, lo], axis=0)
        for c in range(sb // piece):
            ra = slice(c * piece, (c + 1) * piece)
            oa, ob = nat_s[0, ra, :], nat_s[1, ra, :]
            if half == 0:
                out = jnp.where(lo2, oa, pltpu.roll(ob, HEAD_DIM, 1))
            else:
                out = jnp.where(lo2, pltpu.roll(oa, HEAD_DIM, 1), ob)
            o_ref[0, ra, half * LANES:(half + 1) * LANES] = out.astype(BF16)


def _attention_window(q, k, v):
    b, l, _ = q.shape
    sb = ATT_SUPER
    max_dil = max(d for _, d in DILATED_BRANCHES)
    assert l % sb == 0 and l // max_dil >= ATT_UNIT + ATT_SPAN
    qspec = lambda h: pl.BlockSpec((1, sb, LANES), lambda bi, pi, si: (bi, si, 2 * pi + h))
    kspec = pl.BlockSpec((1, l, LANES), lambda bi, pi, si: (bi, 0, pi))
    return pl.pallas_call(
        _attn_window_kernel,
        grid=(b, KV_WIDTH // LANES, l // sb),
        in_specs=[qspec(0), qspec(1), kspec, kspec],
        out_specs=pl.BlockSpec((1, sb, 2 * LANES), lambda bi, pi, si: (bi, si, pi)),
        out_shape=jax.ShapeDtypeStruct((b, l, ATT_WIDTH), BF16),
        scratch_shapes=[pltpu.VMEM((ATT_R4, (MAX_WINDOW + sb) // ATT_R4, LANES), F32)] * 2
        + [pltpu.VMEM((ATT_R4, sb // ATT_R4, LANES), F32)]
        + [pltpu.VMEM((2 * sb, LANES), F32)] * 3
        + [pltpu.VMEM((3, 2 * ATT_UNIT, LANES), F32), pltpu.VMEM((2, sb, LANES), F32)],
        compiler_params=_cparams(("parallel", "parallel", "arbitrary")),
        name="window_attn",
    )(q, q, k, v)


def _ssd_kernel(xbc_ref, dt_ref, z_ref, ssm0_ref, conv0_ref, cw_ref, cb_ref, dtb_ref, alog_ref, dsk_ref,
                nw_ref, ex_ref, y_ref, ssm_ref, conv_ref, xpad, dtpad, s_scr, *, n_valid):
    q = SSD_CHUNK
    lb = xbc_ref.shape[1]
    ci = pl.program_id(1)
    nc = pl.num_programs(1)
    gw = SSD_WIDTH // SSD_GROUPS
    hpg = SSD_HEADS // SSD_GROUPS

    @pl.when(ci == 0)
    def _():
        xpad[0:SUBLANES, :] = jnp.zeros((SUBLANES, CONV_DIM), F32)
        xpad[SUBLANES - (CONV_W - 1):SUBLANES, :] = conv0_ref[0]
        for g in range(SSD_GROUPS):
            s_scr[g] = jnp.transpose(ssm0_ref[0, g * hpg:(g + 1) * hpg].reshape(gw, SSD_STATE))

    xpad[SUBLANES:SUBLANES + lb, :] = xbc_ref[0]
    dtpad[0:lb, :] = dt_ref[0]
    if lb < q:
        xpad[SUBLANES + lb:SUBLANES + q, :] = jnp.zeros((q - lb, CONV_DIM), F32)
        dtpad[lb:q, :] = jnp.zeros((q - lb, LANES), F32)

    conv = cb_ref[...]
    for k in range(CONV_W):
        sh = CONV_W - 1 - k
        conv = conv + xpad[SUBLANES - sh:SUBLANES - sh + q, :] * cw_ref[k:k + 1, :]
    act = _silu(conv)

    @pl.when(ci == nc - 1)
    def _():
        conv_ref[0] = xpad[SUBLANES + n_valid - (CONV_W - 1):SUBLANES + n_valid, :]

    xpad[0:SUBLANES, :] = xpad[q:q + SUBLANES, :]

    xs = act[:, :SSD_WIDTH]
    row = lax.broadcasted_iota(I32, (q, LANES), 0)
    dtr = dtpad[...] + dtb_ref[...]
    dt = jnp.maximum(dtr, 0.0) + jnp.log(1.0 + jnp.exp(-jnp.abs(dtr)))
    dt = jnp.where(row < n_valid, dt, 0.0)
    a = -jnp.exp(alog_ref[...])
    da = dt * a
    r2 = lax.broadcasted_iota(I32, (q, q), 0)
    c2 = lax.broadcasted_iota(I32, (q, q), 1)
    causal = r2 >= c2
    a_cs = jnp.dot(causal.astype(F32), da, precision=HIGHEST, preferred_element_type=F32)
    a_cst = jnp.transpose(a_cs)
    a_last = a_cs[q - 1:q, :]
    ex = ex_ref[...]
    per_head = jnp.concatenate([dt, jnp.exp(a_cs), jnp.exp(a_last - a_cs),
                                jnp.broadcast_to(jnp.exp(a_last), (SUBLANES, LANES))], axis=0)
    t_hi = per_head.astype(BF16)
    rem = per_head - t_hi.astype(F32)
    t_mid = rem.astype(BF16)
    t_lo = (rem - t_mid.astype(F32)).astype(BF16)
    spread = (jnp.dot(t_hi, ex, preferred_element_type=F32) + jnp.dot(t_mid, ex, preferred_element_type=F32)
              + jnp.dot(t_lo, ex, preferred_element_type=F32))
    dt_x, ea_x, te_x, cd_x = spread[0:q], spread[q:2 * q], spread[2 * q:3 * q], spread[3 * q:3 * q + 1]
    xdt = xs * dt_x
    xdt_b = xdt.astype(BF16)
    xw_b = (xdt * te_x).astype(BF16)
    lo = lax.broadcasted_iota(I32, (q, LANES), 1) < SSD_HEAD_DIM

    ys = []
    for g in range(SSD_GROUPS):
        bm = act[:, SSD_WIDTH + g * SSD_STATE:SSD_WIDTH + (g + 1) * SSD_STATE]
        cm = act[:, SSD_WIDTH + (SSD_GROUPS + g) * SSD_STATE:SSD_WIDTH + (SSD_GROUPS + g + 1) * SSD_STATE]
        bm_b, cm_b = bm.astype(BF16), cm.astype(BF16)
        cb = lax.dot_general(cm_b, bm_b, (((1,), (1,)), ((), ())), preferred_element_type=F32)
        s_old = s_scr[g]
        y_off = jnp.dot(cm_b, s_old.astype(BF16), preferred_element_type=F32)
        for jp in range(hpg // 2):
            pair = g * (hpg // 2) + jp
            yp = []
            for hh in (0, 1):
                h = 2 * pair + hh
                seg = a_cs[:, h:h + 1] - a_cst[h:h + 1, :]
                dec = jnp.exp(jnp.where(causal, seg, NEG_BIG))
                mm = (cb * dec).astype(BF16)
                yp.append(jnp.dot(mm, xdt_b[:, pair * LANES:(pair + 1) * LANES], preferred_element_type=F32))
            ys.append(jnp.where(lo, yp[0], yp[1]) + y_off[:, jp * LANES:(jp + 1) * LANES]
                      * ea_x[:, pair * LANES:(pair + 1) * LANES])
        bmt_b = jnp.transpose(bm).astype(BF16)
        s_new = s_old * cd_x[:, g * gw:(g + 1) * gw] + jnp.dot(bmt_b, xw_b[:, g * gw:(g + 1) * gw],
                                                              preferred_element_type=F32)
        s_scr[g] = s_new

    y = jnp.concatenate(ys, axis=1) + dsk_ref[...] * xs

    @pl.when(ci == nc - 1)
    def _():
        for g in range(SSD_GROUPS):
            ssm_ref[0, g * hpg:(g + 1) * hpg] = jnp.transpose(s_scr[g]).reshape(hpg, SSD_HEAD_DIM, SSD_STATE)

    hg = y[:lb] * _silu(z_ref[0])
    outs = []
    for g in range(SSD_GROUPS):
        part = hg[:, g * gw:(g + 1) * gw]
        ms = jnp.mean(part * part, axis=1, keepdims=True)
        outs.append(part * lax.rsqrt(ms + NORM_EPS))
    y_ref[0] = (jnp.concatenate(outs, axis=1) * nw_ref[...]).astype(BF16)


def _ssd(xbc, dt_raw, z, ssm0, conv0, prm, lb, n_valid):
    b, l, _ = xbc.shape
    nc = l // lb
    cw, cbias, dtb, alog, dsk, nw, ex = prm
    full = lambda a: pl.BlockSpec(a.shape, lambda bi, ci: (0,) * a.ndim)
    seq = lambda w: pl.BlockSpec((1, lb, w), lambda bi, ci: (bi, ci, 0))
    return pl.pallas_call(
        functools.partial(_ssd_kernel, n_valid=n_valid),
        grid=(b, nc),
        in_specs=[seq(CONV_DIM), seq(LANES), seq(SSD_WIDTH),
                  pl.BlockSpec((1, SSD_HEADS, SSD_HEAD_DIM, SSD_STATE), lambda bi, ci: (bi, 0, 0, 0)),
                  pl.BlockSpec((1, CONV_W - 1, CONV_DIM), lambda bi, ci: (bi, 0, 0)),
                  full(cw), full(cbias), full(dtb), full(alog), full(dsk), full(nw), full(ex)],
        out_specs=(seq(SSD_WIDTH),
                   pl.BlockSpec((1, SSD_HEADS, SSD_HEAD_DIM, SSD_STATE), lambda bi, ci: (bi, 0, 0, 0)),
                   pl.BlockSpec((1, CONV_W - 1, CONV_DIM), lambda bi, ci: (bi, 0, 0))),
        out_shape=(jax.ShapeDtypeStruct((b, l, SSD_WIDTH), BF16),
                   jax.ShapeDtypeStruct((b, SSD_HEADS, SSD_HEAD_DIM, SSD_STATE), F32),
                   jax.ShapeDtypeStruct((b, CONV_W - 1, CONV_DIM), F32)),
        scratch_shapes=[pltpu.VMEM((SSD_CHUNK + 2 * SUBLANES, CONV_DIM), F32),
                        pltpu.VMEM((SSD_CHUNK, LANES), F32),
                        pltpu.VMEM((SSD_GROUPS, SSD_STATE, SSD_WIDTH // SSD_GROUPS), F32)],
        compiler_params=_cparams(("parallel", "arbitrary")),
        name="conv_ssd",
    )(xbc, dt_raw, z, ssm0, conv0, cw, cbias, dtb, alog, dsk, nw, ex)


def _layer_norm(r, g, b):
    mu = jnp.mean(r, axis=1, keepdims=True)
    d = r - mu
    var = jnp.mean(d * d, axis=1, keepdims=True)
    return d * lax.rsqrt(var + NORM_EPS) * g + b


def _mix_kernel(att_ref, ssd_ref, x_ref, wo_ref, g_ref, b_ref, *rest):
    h_ref, hs_ref = rest[-2:]
    tm = x_ref.shape[0]
    mix = jnp.dot(att_ref[...], wo_ref[0:ATT_WIDTH, :], preferred_element_type=F32)
    mix = mix + jnp.dot(ssd_ref[...], wo_ref[ATT_WIDTH:MIX_WIDTH, :], preferred_element_type=F32)
    h = _layer_norm(DEEPNORM_ALPHA * x_ref[...] + mix, g_ref[...], b_ref[...])
    h_ref[...] = h
    for j in range(ROW_SLAB):
        hs_ref[pl.ds(j, tm, stride=ROW_SLAB), :] = h[:, j * LANES:(j + 1) * LANES]


def _mix(att, ssd, x2d, wo_bf, g, b, tm, t_total, row0=0, into=None):
    t = x2d.shape[0]
    b0 = row0 // tm
    row = lambda w: pl.BlockSpec((tm, w), lambda i: (i, 0))
    vec = pl.BlockSpec((1, D_MODEL), lambda i: (0, 0))
    hbm = pl.BlockSpec(memory_space=pl.ANY)
    extra = () if into is None else tuple(into)
    return pl.pallas_call(
        _mix_kernel,
        grid=(t // tm,),
        in_specs=[row(ATT_WIDTH), row(SSD_WIDTH), row(D_MODEL),
                  pl.BlockSpec((MIX_WIDTH, D_MODEL), lambda i: (0, 0)), vec, vec] + [hbm] * len(extra),
        out_specs=(pl.BlockSpec((tm, D_MODEL), lambda i: (b0 + i, 0)),
                   pl.BlockSpec((tm * ROW_SLAB, LANES), lambda i: (b0 + i, 0))),
        out_shape=(jax.ShapeDtypeStruct((t_total, D_MODEL), F32),
                   jax.ShapeDtypeStruct((t_total * ROW_SLAB, LANES), F32)),
        input_output_aliases={} if into is None else {6: 0, 7: 1},
        compiler_params=_cparams(("parallel",)),
        name="outproj_ln1",
    )(att, ssd, x2d, wo_bf, g, b, *extra)


def _router_kernel(h_ref, wr_ref, bias_ref, tope_ref, rank_ref, gt_ref, cnt_ref, gscr):
    tm = h_ref.shape[0]
    i = pl.program_id(0)
    per_group = N_EXPERTS // N_EXPERT_GROUPS

    @pl.when(i == 0)
    def _():
        cnt_ref[...] = jnp.zeros_like(cnt_ref)

    logits = lax.dot_general(wr_ref[...], h_ref[...], (((1,), (1,)), ((), ())),
                             precision=HIGHEST, preferred_element_type=F32)
    s = 1.0 / (1.0 + jnp.exp(-logits))
    sel = s + bias_ref[...]
    neg_inf = -jnp.inf
    e_iota = lax.broadcasted_iota(I32, (N_EXPERTS, tm), 0).astype(F32)
    g_iota = lax.broadcasted_iota(I32, (per_group, tm), 0).astype(F32)

    gscore = []
    for g in range(N_EXPERT_GROUPS):
        blk = sel[g * per_group:(g + 1) * per_group]
        m1 = jnp.max(blk, axis=0, keepdims=True)
        a1 = jnp.min(jnp.where(blk == m1, g_iota, float(per_group)), axis=0, keepdims=True)
        m2 = jnp.max(jnp.where(g_iota == a1, neg_inf, blk), axis=0, keepdims=True)
        gscore.append(m1 + m2)
    blocks = []
    for g in range(N_EXPERT_GROUPS):
        beaten = jnp.zeros((1, tm), F32)
        for o in range(N_EXPERT_GROUPS):
            if o < g:
                beaten = beaten + jnp.where(gscore[o] >= gscore[g], 1.0, 0.0)
            elif o > g:
                beaten = beaten + jnp.where(gscore[o] > gscore[g], 1.0, 0.0)
        blocks.append(jnp.where(beaten < float(TOPK_GROUPS), sel[g * per_group:(g + 1) * per_group], neg_inf))
    cand = jnp.concatenate(blocks, axis=0)

    tops, gsel = [], []
    onehot = jnp.zeros((N_EXPERTS, tm), F32)
    for _ in range(TOP_K):
        mx = jnp.max(cand, axis=0, keepdims=True)
        ix = jnp.min(jnp.where(cand == mx, e_iota, float(N_EXPERTS)), axis=0, keepdims=True)
        hit = e_iota == ix
        tops.append(ix)
        gsel.append(jnp.sum(jnp.where(hit, s, 0.0), axis=0, keepdims=True))
        onehot = jnp.where(hit, 1.0, onehot)
        cand = jnp.where(hit, neg_inf, cand)
    den = gsel[0]
    for k in range(1, TOP_K):
        den = den + gsel[k]

    t_r = lax.broadcasted_iota(I32, (tm, tm), 0)
    t_c = lax.broadcasted_iota(I32, (tm, tm), 1)
    before = (t_r < t_c).astype(BF16)
    oh_b = onehot.astype(BF16)
    base = cnt_ref[...] + jnp.dot(oh_b, before, preferred_element_type=F32)
    cnt_ref[...] = cnt_ref[...] + jnp.dot(oh_b, jnp.ones((tm, LANES), BF16), preferred_element_type=F32)

    gscr[...] = jnp.zeros_like(gscr)
    for k in range(TOP_K):
        tope_ref[k:k + 1, :] = tops[k].astype(I32)
        rank_ref[k:k + 1, :] = jnp.sum(jnp.where(e_iota == tops[k], base, 0.0), axis=0, keepdims=True).astype(I32)
        gscr[k:k + 1, :] = gsel[k] / den * ROUTED_SCALE
    gt_ref[...] = jnp.transpose(gscr[...])


def _router(h_all, wr_t, bias_b):
    t = h_all.shape[0]
    tm = ROUTER_TILE
    return pl.pallas_call(
        _router_kernel,
        grid=(t // tm,),
        in_specs=[pl.BlockSpec((tm, D_MODEL), lambda i: (i, 0)),
                  pl.BlockSpec((N_EXPERTS, D_MODEL), lambda i: (0, 0)),
                  pl.BlockSpec((N_EXPERTS, tm), lambda i: (0, 0))],
        out_specs=(pl.BlockSpec((TOP_K, tm), lambda i: (0, i)),
                   pl.BlockSpec((TOP_K, tm), lambda i: (0, i)),
                   pl.BlockSpec((tm, LANES), lambda i: (i, 0)),
                   pl.BlockSpec((N_EXPERTS, LANES), lambda i: (0, 0))),
        out_shape=(jax.ShapeDtypeStruct((TOP_K, t), I32),
                   jax.ShapeDtypeStruct((TOP_K, t), I32),
                   jax.ShapeDtypeStruct((t, LANES), F32),
                   jax.ShapeDtypeStruct((N_EXPERTS, LANES), F32)),
        scratch_shapes=[pltpu.VMEM((tm, LANES), F32)],
        compiler_params=_cparams(("arbitrary",)),
        name="router",
    )(h_all, wr_t, bias_b)


_PAD_SIZES = tuple(2 ** p for p in range(int(math.log2(EXPERT_BLOCK)) - 1, -1, -1))


def _slab(ref, row, n=1):
    return ref.at[pl.ds(pl.multiple_of(row * ROW_SLAB, ROW_SLAB), n * ROW_SLAB)]


def _padfill_kernel(pstart_ref, pcnt_ref, tail_ref, xs_ref, zero_scr, sem):
    zero_scr[...] = jnp.zeros_like(zero_scr)
    half = EXPERT_BLOCK // 2
    tail_copy = lambda j: pltpu.make_async_copy(_slab(zero_scr, 0, half), _slab(xs_ref, tail_ref[0] + j * half, half),
                                                sem)

    def tail_start(j, c):
        tail_copy(j).start()
        return c

    def tail_wait(j, c):
        tail_copy(j).wait()
        return c

    lax.fori_loop(0, tail_ref[1], tail_start, 0)

    def copies(e):
        base = pstart_ref[e]
        cnt = pcnt_ref[e]
        out = []
        for sz in _PAD_SIZES:
            out.append(((cnt & sz) != 0, pltpu.make_async_copy(_slab(zero_scr, 0, sz), _slab(xs_ref, base, sz), sem)))
            base = base + (cnt & sz)
        return out

    def start(e, c):
        for pred, cp in copies(e):
            @pl.when(pred)
            def _():
                cp.start()
        return c

    def wait(e, c):
        for pred, cp in copies(e):
            @pl.when(pred)
            def _():
                cp.wait()
        return c

    lax.fori_loop(0, N_EXPERTS, start, 0)
    lax.fori_loop(0, N_EXPERTS, wait, 0)
    lax.fori_loop(0, tail_ref[1], tail_wait, 0)


def _padfill(pad_start, pad_cnt, tail, n_rows):
    return pl.pallas_call(
        _padfill_kernel,
        grid_spec=pltpu.PrefetchScalarGridSpec(
            num_scalar_prefetch=3, grid=(1,), in_specs=[],
            out_specs=pl.BlockSpec(memory_space=pl.ANY),
            scratch_shapes=[pltpu.VMEM((EXPERT_BLOCK // 2 * ROW_SLAB, LANES), F32), pltpu.SemaphoreType.DMA]),
        out_shape=jax.ShapeDtypeStruct((n_rows * ROW_SLAB, LANES), F32),
        compiler_params=_cparams(("arbitrary",)),
        name="dispatch_padfill",
    )(pad_start, pad_cnt, tail)


def _dest_kernel(tope_ref, rank_ref, pstart_ref, dest_ref):
    tm = tope_ref.shape[1]
    e_iota = lax.broadcasted_iota(I32, (N_EXPERTS, tm), 0)
    ps = pstart_ref[...]
    for k in range(TOP_K):
        base = jnp.sum(jnp.where(e_iota == tope_ref[k:k + 1, :], ps, 0.0), axis=0, keepdims=True)
        dest_ref[k:k + 1, :] = base.astype(I32) + rank_ref[k:k + 1, :]


def _dest(top_e, rank, pstarts):
    t = top_e.shape[1]
    tm = DISPATCH_TILE
    blk = pl.BlockSpec((TOP_K, tm), lambda i: (0, i))
    return pl.pallas_call(
        _dest_kernel,
        grid=(t // tm,),
        in_specs=[blk, blk, pl.BlockSpec((N_EXPERTS, tm), lambda i: (0, 0))],
        out_specs=blk,
        out_shape=jax.ShapeDtypeStruct((TOP_K, t), I32),
        compiler_params=_cparams(("parallel",)),
        name="dispatch_dest",
    )(top_e, rank, jnp.broadcast_to(pstarts.astype(F32)[:, None], (N_EXPERTS, tm)))


def _dest_tiles(dest, tm):
    t = dest.shape[1]
    return jnp.transpose(dest.reshape(TOP_K, t // tm, tm), (1, 0, 2)).reshape(t // tm, TOP_K * tm)


def _dispatch_kernel(dest_ref, h_ref, xs_in_ref, xs_ref, idx_smem, isem, sem):
    del xs_in_ref
    i = pl.program_id(0)
    tm = DISPATCH_TILE
    cp = pltpu.make_async_copy(dest_ref.at[i], idx_smem, isem)
    cp.start()
    cp.wait()

    def row_copy(t, k):
        return pltpu.make_async_copy(_slab(h_ref, t), _slab(xs_ref, idx_smem[k * tm + t]), sem)

    def start(t, c):
        for k in range(TOP_K):
            row_copy(t, k).start(priority=k % 2)
        return c

    def wait(t, c):
        for k in range(TOP_K):
            row_copy(t, k).wait()
        return c

    lax.fori_loop(0, tm, start, 0)
    lax.fori_loop(0, tm, wait, 0)


def _dispatch(dest_tiles, h_slab, xs):
    tm = DISPATCH_TILE
    return pl.pallas_call(
        _dispatch_kernel,
        grid=(dest_tiles.shape[0],),
        in_specs=[pl.BlockSpec(dest_tiles.shape, lambda i: (0, 0)),
                  pl.BlockSpec((tm * ROW_SLAB, LANES), lambda i: (i, 0)),
                  pl.BlockSpec(memory_space=pl.ANY)],
        out_specs=pl.BlockSpec(memory_space=pl.ANY),
        out_shape=jax.ShapeDtypeStruct(xs.shape, xs.dtype),
        scratch_shapes=[pltpu.SMEM((tm * TOP_K,), I32), pltpu.SemaphoreType.DMA, pltpu.SemaphoreType.DMA],
        input_output_aliases={2: 0},
        compiler_params=_cparams(("arbitrary",)),
        name="dispatch_rows",
    )(dest_tiles, h_slab, xs)


def _expert_kernel(bi_ref, be_ref, nu_ref, x_ref, wg_ref, wu_ref, wd_ref, y_ref, wg_b, wu_b, wd_b):
    del bi_ref
    i = pl.program_id(0)

    @pl.when(i < nu_ref[0])
    def _():
        prev = be_ref[jnp.maximum(i - 1, 0)]

        @pl.when(jnp.logical_or(i == 0, be_ref[i] != prev))
        def _():
            wg_b[...] = wg_ref[0].astype(BF16)
            wu_b[...] = wu_ref[0].astype(BF16)
            wd_b[...] = wd_ref[0].astype(BF16)

        rows = EXPERT_BLOCK // EXPERT_CHUNKS
        for c in range(EXPERT_CHUNKS):
            r0 = c * rows * ROW_SLAB
            xb = jnp.concatenate([x_ref[pl.ds(r0 + j, rows, stride=ROW_SLAB), :] for j in range(ROW_SLAB)],
                                 axis=1).astype(BF16)
            g = jnp.dot(xb, wg_b[...], preferred_element_type=F32)
            u = jnp.dot(xb, wu_b[...], preferred_element_type=F32)
            a = (_silu(g) * u).astype(BF16)
            y = jnp.dot(a, wd_b[...], preferred_element_type=F32)
            for j in range(ROW_SLAB):
                y_ref[pl.ds(r0 + j, rows, stride=ROW_SLAB), :] = y[:, j * LANES:(j + 1) * LANES]

    @pl.when(i >= nu_ref[0])
    def _():
        y_ref[...] = jnp.zeros_like(y_ref)


def _experts(blk_idx, blk_exp, n_used, xs, wg, wu, wd):
    n_rows = xs.shape[0] // ROW_SLAB
    nb = n_rows // EXPERT_BLOCK
    rows = pl.BlockSpec((EXPERT_BLOCK * ROW_SLAB, LANES), lambda i, bi, be, nu: (bi[i], 0))
    out_rows = pl.BlockSpec((EXPERT_BLOCK * ROW_SLAB, LANES), lambda i, bi, be, nu: (i, 0))
    return pl.pallas_call(
        _expert_kernel,
        grid_spec=pltpu.PrefetchScalarGridSpec(
            num_scalar_prefetch=3, grid=(nb,),
            in_specs=[rows,
                      pl.BlockSpec((1, D_MODEL, EXPERT_FF), lambda i, bi, be, nu: (be[i], 0, 0)),
                      pl.BlockSpec((1, D_MODEL, EXPERT_FF), lambda i, bi, be, nu: (be[i], 0, 0)),
                      pl.BlockSpec((1, EXPERT_FF, D_MODEL), lambda i, bi, be, nu: (be[i], 0, 0))],
            out_specs=out_rows,
            scratch_shapes=[pltpu.VMEM((D_MODEL, EXPERT_FF), BF16), pltpu.VMEM((D_MODEL, EXPERT_FF), BF16),
                            pltpu.VMEM((EXPERT_FF, D_MODEL), BF16)]),
        out_shape=jax.ShapeDtypeStruct(xs.shape, F32),
        compiler_params=_cparams(("arbitrary",)),
        name="expert_ffn",
    )(blk_idx, blk_exp, n_used, xs, wg, wu, wd)


def _combine_kernel(dest_ref, h_ref, gt_ref, ys_ref, sg_ref, su_ref, sd_ref, g_ref, b_ref, o_ref, o2_ref,
                    idx0, idx1, gbuf, isem, sem, *, n_first):
    idx_smem = (idx0, idx1)
    i = pl.program_id(0)
    n = pl.num_programs(0)
    tm = h_ref.shape[0]

    def idx_copy(step, s):
        return pltpu.make_async_copy(dest_ref.at[step], idx_smem[s], isem.at[s])

    def row_copy(s, t, k, src_row):
        return pltpu.make_async_copy(_slab(ys_ref, src_row), _slab(gbuf.at[s, k], t), sem.at[s])

    def issue(s):
        def body(t, c):
            for k in range(TOP_K):
                row_copy(s, t, k, idx_smem[s][k * tm + t]).start(priority=k % 2)
            return c
        lax.fori_loop(0, tm, body, 0)

    def drain(s):
        def body(t, c):
            for k in range(TOP_K):
                row_copy(s, t, k, 0).wait()
            return c
        lax.fori_loop(0, tm, body, 0)

    @pl.when(i == 0)
    def _():
        first = idx_copy(0, 0)
        first.start()
        first.wait()
        issue(0)

        @pl.when(n > 1)
        def _():
            idx_copy(1, 1).start()

    def step(slot):
        nslot = 1 - slot

        @pl.when(i + 1 < n)
        def _():
            idx_copy(i + 1, nslot).wait()
            issue(nslot)

            @pl.when(i + 2 < n)
            def _():
                idx_copy(i + 2, slot).start()

        h = h_ref[...]
        hb = h.astype(BF16)
        g = jnp.dot(hb, sg_ref[...], preferred_element_type=F32)
        u = jnp.dot(hb, su_ref[...], preferred_element_type=F32)
        f = jnp.dot((_silu(g) * u).astype(BF16), sd_ref[...], preferred_element_type=F32)
        drain(slot)
        gt = gt_ref[...]
        cols = []
        for j in range(ROW_SLAB):
            fj = f[:, j * LANES:(j + 1) * LANES]
            for k in range(TOP_K):
                fj = fj + gbuf[slot, k, pl.ds(j, tm, stride=ROW_SLAB), :] * gt[:, k:k + 1]
            cols.append(fj)
        f = jnp.concatenate(cols, axis=1)
        out = _layer_norm(DEEPNORM_ALPHA * h + f, g_ref[...], b_ref[...])

        @pl.when(i < n_first)
        def _():
            o_ref[...] = out

        @pl.when(i == n_first)
        def _():
            o2_ref[...] = out

    for parity in (0, 1):
        pl.when((i & 1) == parity)(functools.partial(step, parity))


def _combine(dest_tiles, h_all, gates_t, ys, sg_bf, su_bf, sd_bf, g, b, n_first):
    t = h_all.shape[0]
    tm = COMBINE_TILE
    vec = pl.BlockSpec((1, D_MODEL), lambda i: (0, 0))
    return pl.pallas_call(
        functools.partial(_combine_kernel, n_first=n_first),
        grid=(t // tm,),
        in_specs=[pl.BlockSpec(dest_tiles.shape, lambda i: (0, 0)),
                  pl.BlockSpec((tm, D_MODEL), lambda i: (i, 0)),
                  pl.BlockSpec((tm, LANES), lambda i: (i, 0)),
                  pl.BlockSpec(memory_space=pl.ANY),
                  pl.BlockSpec((D_MODEL, SHARED_FF), lambda i: (0, 0)),
                  pl.BlockSpec((D_MODEL, SHARED_FF), lambda i: (0, 0)),
                  pl.BlockSpec((SHARED_FF, D_MODEL), lambda i: (0, 0)), vec, vec],
        out_specs=(pl.BlockSpec((tm, D_MODEL), lambda i: (jnp.minimum(i, n_first - 1), 0)),
                   pl.BlockSpec((tm, D_MODEL), lambda i: (0, 0))),
        out_shape=(jax.ShapeDtypeStruct((n_first * tm, D_MODEL), F32), jax.ShapeDtypeStruct((tm, D_MODEL), F32)),
        scratch_shapes=[pltpu.SMEM((tm * TOP_K,), I32), pltpu.SMEM((tm * TOP_K,), I32),
                        pltpu.VMEM((2, TOP_K, tm * ROW_SLAB, LANES), F32),
                        pltpu.SemaphoreType.DMA((2,)), pltpu.SemaphoreType.DMA((2,))],
        compiler_params=_cparams(("arbitrary",)),
        name="combine_ln2",
    )(dest_tiles, h_all, gates_t, ys, sg_bf, su_bf, sd_bf, g, b)


def _moe(h_all, h_slab, n_first, wr, rbias, wg, wu, wd, sg, su, sd, g2, b2):
    t = h_all.shape[0]
    top_e, rank, gates_t, cnt = _router(h_all, jnp.transpose(wr),
                                        jnp.broadcast_to(rbias[:, None], (N_EXPERTS, ROUTER_TILE)))
    counts = cnt[:, 0].astype(I32)
    pcounts = (counts + EXPERT_BLOCK - 1) // EXPERT_BLOCK * EXPERT_BLOCK
    pends = jnp.cumsum(pcounts)
    pstarts = pends - pcounts
    n_rows = (t * TOP_K // EXPERT_BLOCK + N_EXPERTS - 1) * EXPERT_BLOCK
    nb = n_rows // EXPERT_BLOCK
    n_used = (pends[-1] // EXPERT_BLOCK).astype(I32)
    blk_idx = jnp.minimum(jnp.arange(nb, dtype=I32), n_used - 1)
    blk_exp = jnp.minimum(jnp.sum((pends[None, :] <= (blk_idx * EXPERT_BLOCK)[:, None]).astype(I32), axis=1),
                          N_EXPERTS - 1)
    dest = _dest(top_e, rank, pstarts)
    tail = jnp.stack([pends[-1], (n_rows - pends[-1]) // (EXPERT_BLOCK // 2)]).astype(I32)
    xs = _padfill((pstarts + counts).astype(I32), (pcounts - counts).astype(I32), tail, n_rows)
    xs = _dispatch(_dest_tiles(dest, DISPATCH_TILE), h_slab, xs)
    ys = _experts(blk_idx, blk_exp, n_used.reshape(1), xs, wg, wu, wd)
    return _combine(_dest_tiles(dest, COMBINE_TILE), h_all, gates_t, ys,
                    sg.astype(BF16), su.astype(BF16), sd.astype(BF16), g2[None, :], b2[None, :], n_first)


def _expand_matrix():
    h = np.arange(LANES)[:, None]
    c = np.arange(SSD_WIDTH)[None, :]
    return jnp.asarray((c // SSD_HEAD_DIM == h).astype(np.float32), dtype=BF16)


def kernel(x_prompt, x_sample, cache_k, cache_v, state_ssm, state_conv, w_in, conv_w, conv_b, dt_bias, a_log, d_skip, ssd_norm_w, w_out, ln1_g, ln1_b, w_router, router_bias, w_exp_gate, w_exp_up, w_exp_down, w_sh_gate, w_sh_up, w_sh_down, ln2_g, ln2_b):
    bp, lp, _ = x_prompt.shape
    bs, ls, _ = x_sample.shape
    win = cache_k.shape[2]
    keep = min(MAX_WINDOW, lp)
    assert lp % SSD_CHUNK == 0 and ls <= SUBLANES and win % KEY_TILE == 0 and win >= MAX_WINDOW

    w_bf = jnp.pad(w_in[0], ((0, 0), (0, IN_COLS_PAD - IN_COLS))).astype(BF16)
    wo_bf = w_out[0].astype(BF16)
    pad_l = lambda v: jnp.pad(v, (0, LANES - v.shape[0]))[None, :]
    ssd_prm = (conv_w[0], conv_b[0][None, :], pad_l(dt_bias[0]), pad_l(a_log[0]),
               jnp.repeat(d_skip[0], SSD_HEAD_DIM)[None, :], ssd_norm_w[0][None, :], _expand_matrix())

    tp = bp * lp
    tm_p = 256 if lp % 256 == 0 else SSD_CHUNK
    tm_mix = 512 if tp % 512 == 0 else SSD_CHUNK
    cos_p, sin_p = _rope_tables(jnp.arange(lp, dtype=F32))
    xp2 = x_prompt.reshape(tp, D_MODEL)
    q, kf, vf, z, xbc, dtr = _inproj(xp2, w_bf, cos_p, sin_p, tm_p, lp // tm_p)
    att = _attention_window(q.reshape(bp, lp, ATT_WIDTH), kf.reshape(bp, lp, KV_WIDTH), vf.reshape(bp, lp, KV_WIDTH))
    ssd_y, ssm_p, conv_p = _ssd(xbc.reshape(bp, lp, CONV_DIM), dtr.reshape(bp, lp, LANES),
                                z.reshape(bp, lp, SSD_WIDTH),
                                jnp.zeros((bp, SSD_HEADS, SSD_HEAD_DIM, SSD_STATE), F32),
                                jnp.zeros((bp, CONV_W - 1, CONV_DIM), F32), ssd_prm, SSD_CHUNK, SSD_CHUNK)
    ts = bs * ls
    t_pad = -(-(tp + ts) // DISPATCH_TILE) * DISPATCH_TILE
    tail = t_pad - tp
    assert tp % tail == 0 and ts <= COMBINE_TILE and tp % COMBINE_TILE == 0
    h_bufs = _mix(att.reshape(tp, ATT_WIDTH), ssd_y.reshape(tp, SSD_WIDTH), xp2, wo_bf, ln1_g, ln1_b, tm_mix, t_pad)

    pos_s = (PAST_LEN + jnp.arange(ls, dtype=jnp.int32)).astype(F32)
    cos_s, sin_s = _rope_tables(jnp.tile(pos_s, bs))
    xs2 = x_sample.reshape(ts, D_MODEL)
    q_s, kf_s, vf_s, z_s, xbc_s, dtr_s = _inproj(xs2, w_bf, cos_s, sin_s, ts, 1)
    rows8 = lambda a, w: jnp.pad(a.reshape(bs, ls, w), ((0, 0), (0, SUBLANES - ls), (0, 0)))
    att_s = _attention_decode(rows8(q_s, ATT_WIDTH), cache_k[0].reshape(bs, win, KV_WIDTH),
                              cache_v[0].reshape(bs, win, KV_WIDTH), rows8(kf_s, KV_WIDTH),
                              rows8(vf_s, KV_WIDTH))[:, :ls]
    ssd_s, ssm_s, conv_s = _ssd(rows8(xbc_s, CONV_DIM), rows8(dtr_s, LANES), rows8(z_s, SSD_WIDTH),
                                state_ssm[0], state_conv[0], ssd_prm, SUBLANES, ls)
    tail_rows = lambda a: jnp.pad(a, ((0, tail - ts), (0, 0)))
    h_all, h_slab = _mix(tail_rows(att_s.reshape(ts, ATT_WIDTH)), tail_rows(ssd_s[:, :ls].reshape(ts, SSD_WIDTH)),
                         tail_rows(xs2), wo_bf, ln1_g, ln1_b, tail, t_pad, row0=tp, into=h_bufs)

    y_p, y_s = _moe(h_all, h_slab, tp // COMBINE_TILE, w_router[0], router_bias[0], w_exp_gate[0], w_exp_up[0],
                    w_exp_down[0], w_sh_gate[0], w_sh_up[0], w_sh_down[0], ln2_g[0], ln2_b[0])

    kv5 = lambda a, b, l: a.reshape(1, b, l, N_KV_HEADS, HEAD_DIM)
    return (y_p.reshape(bp, lp, D_MODEL), y_s[:ts].reshape(bs, ls, D_MODEL),
            kv5(kf.reshape(bp, lp, KV_WIDTH)[:, lp - keep:], bp, keep),
            kv5(vf.reshape(bp, lp, KV_WIDTH)[:, lp - keep:], bp, keep),
            ssm_p[None], conv_p[None],
            kv5(kf_s, bs, ls), kv5(vf_s, bs, ls), ssm_s[None], conv_s[None])
```

```python
import functools
import math

import jax
import jax.numpy as jnp
import numpy as np
from jax import lax
from jax.experimental import pallas as pl
from jax.experimental.pallas import tpu as pltpu

F32 = jnp.float32
BF16 = jnp.bfloat16
I32 = jnp.int32

D_MODEL = 1024
PAST_LEN = 16384
HEAD_DIM = 64
N_ATT_HEADS = 16
N_KV_HEADS = 8
ATT_WIDTH = N_ATT_HEADS * HEAD_DIM
KV_WIDTH = N_KV_HEADS * HEAD_DIM
DILATED_BRANCHES = ((128, 1), (512, 4), (2048, 16))
MAX_WINDOW = 2048
ROPE_THETA = 10000.0
SSD_HEADS = 16
SSD_HEAD_DIM = 64
SSD_WIDTH = SSD_HEADS * SSD_HEAD_DIM
SSD_GROUPS = 2
SSD_STATE = 128
SSD_CHUNK = 128
CONV_W = 4
CONV_DIM = SSD_WIDTH + 2 * SSD_GROUPS * SSD_STATE
MIX_WIDTH = ATT_WIDTH + SSD_WIDTH
IN_COLS = ATT_WIDTH + 2 * KV_WIDTH + SSD_WIDTH + CONV_DIM + SSD_HEADS
N_EXPERTS = 256
TOP_K = 8
N_EXPERT_GROUPS = 8
TOPK_GROUPS = 4
EXPERT_FF = 256
SHARED_FF = 256
ROUTED_SCALE = 2.5
DEPTH = 1
DEEPNORM_ALPHA = (2.0 * DEPTH) ** 0.25
NORM_EPS = 1e-5

LANES = 128
SUBLANES = 8
VMEM_LIMIT = 56 * 1024 * 1024

IN_COLS_PAD = ATT_WIDTH + 2 * KV_WIDTH + SSD_WIDTH + CONV_DIM + LANES
KEY_TILE = 128
EXPERT_BLOCK = 512
EXPERT_CHUNKS = 1
ROUTER_TILE = 128
DISPATCH_TILE = 256
COMBINE_TILE = 128
ROW_SLAB = D_MODEL // LANES
NEG_BIG = -1e30
HIGHEST = lax.Precision.HIGHEST


def _cparams(sem):
    return pltpu.CompilerParams(dimension_semantics=sem, vmem_limit_bytes=VMEM_LIMIT)


def _silu(x):
    return x * (1.0 / (1.0 + jnp.exp(-x)))


def _inproj_kernel(x_ref, w_ref, cq_ref, sq_ref, ck_ref, sk_ref,
                   q_ref, kf_ref, vf_ref, z_ref, xbc_ref, dt_ref):
    tm = x_ref.shape[0]
    xb = x_ref[...].astype(BF16)
    lane = lax.broadcasted_iota(I32, (tm, LANES), 1)
    first_half = (lane % HEAD_DIM) < (HEAD_DIM // 2)

    def rope(a, c, s):
        partner = jnp.where(first_half, pltpu.roll(a, LANES - HEAD_DIM // 2, 1), pltpu.roll(a, HEAD_DIM // 2, 1))
        return a * c + partner * s

    c0 = 0
    acc = jnp.dot(xb, w_ref[:, c0:c0 + ATT_WIDTH], preferred_element_type=F32)
    cq, sq = cq_ref[...], sq_ref[...]
    for j in range(ATT_WIDTH // LANES):
        q_ref[:, j * LANES:(j + 1) * LANES] = rope(acc[:, j * LANES:(j + 1) * LANES], cq, sq)
    c0 += ATT_WIDTH
    acc = jnp.dot(xb, w_ref[:, c0:c0 + KV_WIDTH], preferred_element_type=F32)
    ck, sk = ck_ref[...], sk_ref[...]
    for j in range(KV_WIDTH // LANES):
        kf_ref[:, j * LANES:(j + 1) * LANES] = rope(acc[:, j * LANES:(j + 1) * LANES], ck, sk)
    c0 += KV_WIDTH
    vf_ref[...] = jnp.dot(xb, w_ref[:, c0:c0 + KV_WIDTH], preferred_element_type=F32)
    c0 += KV_WIDTH
    z_ref[...] = jnp.dot(xb, w_ref[:, c0:c0 + SSD_WIDTH], preferred_element_type=F32)
    c0 += SSD_WIDTH
    xbc_ref[...] = jnp.dot(xb, w_ref[:, c0:c0 + CONV_DIM], preferred_element_type=F32)
    c0 += CONV_DIM
    dt_ref[...] = jnp.dot(xb, w_ref[:, c0:c0 + LANES], preferred_element_type=F32)


def _rope_tables(pos):
    half = HEAD_DIM // 2
    inv = ROPE_THETA ** (-jnp.arange(half, dtype=F32) / half)
    ang = pos[:, None] * inv[None, :]
    cos, sin = jnp.cos(ang), jnp.sin(ang)
    c = jnp.concatenate([cos, cos, cos, cos], axis=1)
    s = jnp.concatenate([-sin, sin, -sin, sin], axis=1)
    return c, s


def _inproj(x2d, w_bf, cos_t, sin_t, tm, tiles_per_seq):
    t = x2d.shape[0]
    scale = HEAD_DIM ** -0.5
    tab = pl.BlockSpec((tm, LANES), lambda i: (i % tiles_per_seq, 0))
    row = lambda w: pl.BlockSpec((tm, w), lambda i: (i, 0))
    outs = (
        jax.ShapeDtypeStruct((t, ATT_WIDTH), F32),
        jax.ShapeDtypeStruct((t, KV_WIDTH), F32),
        jax.ShapeDtypeStruct((t, KV_WIDTH), F32),
        jax.ShapeDtypeStruct((t, SSD_WIDTH), F32),
        jax.ShapeDtypeStruct((t, CONV_DIM), F32),
        jax.ShapeDtypeStruct((t, LANES), F32),
    )
    return pl.pallas_call(
        _inproj_kernel,
        grid=(t // tm,),
        in_specs=[row(D_MODEL), pl.BlockSpec((D_MODEL, IN_COLS_PAD), lambda i: (0, 0)), tab, tab, tab, tab],
        out_specs=(row(ATT_WIDTH), row(KV_WIDTH), row(KV_WIDTH), row(SSD_WIDTH), row(CONV_DIM), row(LANES)),
        out_shape=outs,
        compiler_params=_cparams(("parallel",)),
        name="inproj",
    )(x2d, w_bf, cos_t * scale, sin_t * scale, cos_t, sin_t)


def _branch_weight(d):
    w = np.zeros(d.shape, np.float32)
    for window, dil in DILATED_BRANCHES:
        w += ((d >= 0) & (d <= window) & (d % dil == 0)).astype(np.float32)
    return w


def _attn_decode_kernel(q_ref, kc_ref, vc_ref, kn_ref, vn_ref, wc_ref, wn_ref, o_ref, ktail, vtail):
    nq = q_ref.shape[1]
    lane = lax.broadcasted_iota(I32, (nq, LANES), 1)
    lo = lane < HEAD_DIM
    ktail[...] = jnp.zeros_like(ktail)
    vtail[...] = jnp.zeros_like(vtail)
    ktail[0:nq, :] = kn_ref[0]
    vtail[0:nq, :] = vn_ref[0]
    wc, wn = wc_ref[...], wn_ref[...]
    nt = (((1,), (1,)), ((), ()))
    for g in range(N_KV_HEADS):
        pair, half = divmod(g, 2)
        cols = slice(pair * LANES, (pair + 1) * LANES)
        q = q_ref[0, :, g * LANES:(g + 1) * LANES]
        qs = pltpu.roll(q, HEAD_DIM, 1)
        valid = lo if half == 0 else jnp.logical_not(lo)
        qa_src, qb_src = (q, qs) if half == 0 else (qs, q)
        qq = jnp.concatenate([jnp.where(valid, qa_src, 0.0), jnp.where(valid, qb_src, 0.0)], axis=0).astype(BF16)
        s_c = lax.dot_general(qq, kc_ref[0, :, cols].astype(BF16), nt, preferred_element_type=F32)
        s_n = lax.dot_general(qq, ktail[:, cols].astype(BF16), nt, preferred_element_type=F32)
        s_c = jnp.where(wc > 0.0, s_c, NEG_BIG)
        s_n = jnp.where(wn > 0.0, s_n, NEG_BIG)
        m = jnp.maximum(jnp.max(s_c, axis=1, keepdims=True), jnp.max(s_n, axis=1, keepdims=True))
        p_c = jnp.exp(s_c - m) * wc
        p_n = jnp.exp(s_n - m) * wn
        l = jnp.sum(p_c, axis=1, keepdims=True) + jnp.sum(p_n, axis=1, keepdims=True)
        acc = jnp.dot(p_c.astype(BF16), vc_ref[0, :, cols].astype(BF16), preferred_element_type=F32)
        acc = acc + jnp.dot(p_n.astype(BF16), vtail[:, cols].astype(BF16), preferred_element_type=F32)
        o = acc / l
        oa, ob = o[:nq], o[nq:]
        if half == 0:
            out = jnp.where(lo, oa, pltpu.roll(ob, HEAD_DIM, 1))
        else:
            out = jnp.where(lo, pltpu.roll(oa, HEAD_DIM, 1), ob)
        o_ref[0, :, g * LANES:(g + 1) * LANES] = out.astype(BF16)


def _attention_decode(q, k_cache, v_cache, k_new, v_new):
    b, nq, _ = q.shape
    win = k_cache.shape[1]
    i = np.arange(nq)[:, None]
    wc = _branch_weight(win + i - np.arange(win)[None, :])
    wn = _branch_weight(i - np.arange(KEY_TILE)[None, :])
    wn[:, nq:] = 0.0
    stack = lambda w: jnp.asarray(np.concatenate([w, w], axis=0))
    seq = lambda rows, w: pl.BlockSpec((1, rows, w), lambda bi: (bi, 0, 0))
    full = lambda shp: pl.BlockSpec(shp, lambda bi: (0, 0))
    return pl.pallas_call(
        _attn_decode_kernel,
        grid=(b,),
        in_specs=[seq(nq, ATT_WIDTH), seq(win, KV_WIDTH), seq(win, KV_WIDTH), seq(nq, KV_WIDTH), seq(nq, KV_WIDTH),
                  full((2 * nq, win)), full((2 * nq, KEY_TILE))],
        out_specs=seq(nq, ATT_WIDTH),
        out_shape=jax.ShapeDtypeStruct((b, nq, ATT_WIDTH), BF16),
        scratch_shapes=[pltpu.VMEM((KEY_TILE, KV_WIDTH), F32), pltpu.VMEM((KEY_TILE, KV_WIDTH), F32)],
        compiler_params=_cparams(("arbitrary",)),
        name="decode_attn",
    )(q, k_cache, v_cache, k_new, v_new, stack(wc), stack(wn))


ATT_SUPER = 2048
ATT_UNIT = 128
ATT_SPAN = max(w // d for w, d in DILATED_BRANCHES)


ATT_R4 = 4


MASK_FULL, MASK_CLAMPED, MASK_BEFORE_START = 0, 1, 2


def _window_masks():
    i = (np.arange(2 * ATT_UNIT) % ATT_UNIT)[:, None]
    c = np.arange(ATT_UNIT + ATT_SPAN)[None, :]
    ok = lambda d: (d >= 0) & (d <= ATT_SPAN)
    keep = np.stack([ok(ATT_SPAN + i - c), ok(i - c), ok(ATT_SPAN + i - c) & (c >= ATT_SPAN)])
    return jnp.asarray(np.where(keep, 0.0, NEG_BIG).astype(np.float32))


def _attn_window_kernel(q0_ref, q1_ref, k_ref, v_ref, mask_ref, o_ref,
                        k4_s, v4_s, q4_s, acc_s, m_s, l_s, tmp_s, nat_s):
    sb = q0_ref.shape[1]
    sbi = pl.program_id(2)
    u_rows, span, r4 = ATT_UNIT, ATT_SPAN, ATT_R4
    nkeys = u_rows + span
    cls = sb // r4
    look = MAX_WINDOW // r4
    units = sb // u_rows
    assert [d for _, d in DILATED_BRANCHES] == [1, r4, r4 * r4] and sb // (r4 * r4) == u_rows
    lane = lax.broadcasted_iota(I32, (u_rows, LANES), 1)
    lo = lane < HEAD_DIM
    piece = 2 * u_rows

    def regroup(src_ref, dst, nat0, loc0, n):
        for c in range(r4):
            for j0 in range(0, n, piece):
                dst[c, loc0 + j0:loc0 + j0 + piece, :] = src_ref[0, pl.ds(nat0 + r4 * j0 + c, piece, stride=r4), :]

    @pl.when(sbi > 0)
    def _():
        regroup(k_ref, k4_s, sbi * sb - look * r4, 0, look)
        regroup(v_ref, v4_s, sbi * sb - look * r4, 0, look)

    @pl.when(sbi == 0)
    def _():
        k4_s[:, 0:look, :] = jnp.zeros((r4, look, LANES), F32)
        v4_s[:, 0:look, :] = jnp.zeros((r4, look, LANES), F32)

    regroup(k_ref, k4_s, sbi * sb, look, cls)
    regroup(v_ref, v4_s, sbi * sb, look, cls)
    base4 = sbi * cls - look

    for half, qh_ref in enumerate((q0_ref, q1_ref)):
        valid = lo if half == 0 else jnp.logical_not(lo)
        regroup(qh_ref, q4_s, 0, 0, cls)

        def softmax_unit(q, kt, vt, pattern):
            qs = pltpu.roll(q, HEAD_DIM, 1)
            qa_src, qb_src = (q, qs) if half == 0 else (qs, q)
            qq = jnp.concatenate([jnp.where(valid, qa_src, 0.0), jnp.where(valid, qb_src, 0.0)],
                                 axis=0).astype(BF16)
            s = lax.dot_general(qq, kt.astype(BF16), (((1,), (1,)), ((), ())), preferred_element_type=F32)
            s = s + mask_ref[pattern]
            m = jnp.max(s, axis=1, keepdims=True)
            p = jnp.exp(s - m)
            l = jnp.sum(p, axis=1, keepdims=True)
            acc = jnp.dot(p.astype(BF16), vt.astype(BF16), preferred_element_type=F32)
            return jnp.broadcast_to(m, (2 * u_rows, LANES)), jnp.broadcast_to(l, (2 * u_rows, LANES)), acc

        def merge_store(rows, m, l, acc):
            ld = lambda ref: jnp.concatenate([ref[rows[0], :], ref[rows[1], :]], axis=0)
            m_old, l_old, acc_old = ld(m_s), ld(l_s), ld(acc_s)
            m_new = jnp.maximum(m_old, m)
            a_old, a_new = jnp.exp(m_old - m_new), jnp.exp(m - m_new)
            l = a_old * l_old + a_new * l
            acc = a_old * acc_old + a_new * acc
            for h2 in (0, 1):
                sl = slice(h2 * u_rows, (h2 + 1) * u_rows)
                m_s[rows[h2], :] = m_new[sl]
                l_s[rows[h2], :] = l[sl]
                acc_s[rows[h2], :] = acc[sl]

        def unit_d1(u, c):
            q0 = pl.multiple_of(u * u_rows, u_rows)
            qpos = sbi * sb + u * u_rows
            kpos = pl.multiple_of(jnp.maximum(qpos - span, 0), u_rows)
            m, l, acc = softmax_unit(qh_ref[0, pl.ds(q0, u_rows), :], k_ref[0, pl.ds(kpos, nkeys), :],
                                     v_ref[0, pl.ds(kpos, nkeys), :], jnp.where(qpos == 0, MASK_CLAMPED, MASK_FULL))
            part = u_rows // r4
            for a, (val, dst) in enumerate(((m, m_s), (l, l_s), (acc, acc_s))):
                tmp_s[a] = val
                for h2 in (0, 1):
                    for cl in range(r4):
                        d0 = pl.multiple_of(h2 * sb + cl * cls + u * part, part)
                        dst[pl.ds(d0, part), :] = tmp_s[a, pl.ds(h2 * u_rows + cl, part, stride=r4), :]
            return c

        def unit_d4(idx, c):
            cl, u = idx & (r4 - 1), idx >> 2
            qpos = sbi * cls + u * u_rows
            kpos = jnp.maximum(qpos - span, 0)
            kloc = pl.multiple_of(kpos - base4, u_rows)
            q0 = pl.multiple_of(u * u_rows, u_rows)
            m, l, acc = softmax_unit(q4_s[cl, pl.ds(q0, u_rows), :], k4_s[cl, pl.ds(kloc, nkeys), :],
                                     v4_s[cl, pl.ds(kloc, nkeys), :], jnp.where(qpos == 0, MASK_CLAMPED, MASK_FULL))
            r0 = pl.multiple_of(cl * cls + u * u_rows, u_rows)
            merge_store((pl.ds(r0, u_rows), pl.ds(sb + r0, u_rows)), m, l, acc)
            return c

        def unit_d16(idx, c):
            cl, sg = idx & (r4 - 1), idx >> 2
            qpos = sbi * u_rows
            kloc = sg + r4 * (qpos - span) - base4
            m, l, acc = softmax_unit(q4_s[cl, pl.ds(sg, u_rows, stride=r4), :],
                                     k4_s[cl, pl.ds(kloc, nkeys, stride=r4), :],
                                     v4_s[cl, pl.ds(kloc, nkeys, stride=r4), :],
                                     jnp.where(qpos == 0, MASK_BEFORE_START, MASK_FULL))
            r0 = cl * cls + sg
            merge_store((pl.ds(r0, u_rows, stride=r4), pl.ds(sb + r0, u_rows, stride=r4)), m, l, acc)
            return c

        for body in (unit_d1, unit_d4, unit_d16):
            lax.fori_loop(0, units, body, 0, unroll=8)

        for h2 in (0, 1):
            for cl in range(r4):
                for j0 in range(0, cls, piece):
                    rows = slice(h2 * sb + cl * cls + j0, h2 * sb + cl * cls + j0 + piece)
                    nat_s[h2, pl.ds(r4 * j0 + cl, piece, stride=r4), :] = acc_s[rows, :] / l_s[rows, :]
        lo2 = jnp.concatenate([lo, lo], axis=0)
        for c in range(sb // piece):
            ra = slice(c * piece, (c + 1) * piece)
            oa, ob = nat_s[0, ra, :], nat_s[1, ra, :]
            if half == 0:
                out = jnp.where(lo2, oa, pltpu.roll(ob, HEAD_DIM, 1))
            else:
                out = jnp.where(lo2, pltpu.roll(oa, HEAD_DIM, 1), ob)
            o_ref[0, ra, half * LANES:(half + 1) * LANES] = out.astype(BF16)


def _attention_window(q, k, v):
    b, l, _ = q.shape
    sb = ATT_SUPER
    max_dil = max(d for _, d in DILATED_BRANCHES)
    assert l % sb == 0 and l // max_dil >= ATT_UNIT + ATT_SPAN
    qspec = lambda h: pl.BlockSpec((1, sb, LANES), lambda bi, pi, si: (bi, si, 2 * pi + h))
    kspec = pl.BlockSpec((1, l, LANES), lambda bi, pi, si: (bi, 0, pi))
    return pl.pallas_call(
        _attn_window_kernel,
        grid=(b, KV_WIDTH // LANES, l // sb),
        in_specs=[qspec(0), qspec(1), kspec, kspec,
                  pl.BlockSpec((3, 2 * ATT_UNIT, ATT_UNIT + ATT_SPAN), lambda bi, pi, si: (0, 0, 0))],
        out_specs=pl.BlockSpec((1, sb, 2 * LANES), lambda bi, pi, si: (bi, si, pi)),
        out_shape=jax.ShapeDtypeStruct((b, l, ATT_WIDTH), BF16),
        scratch_shapes=[pltpu.VMEM((ATT_R4, (MAX_WINDOW + sb) // ATT_R4, LANES), F32)] * 2
        + [pltpu.VMEM((ATT_R4, sb // ATT_R4, LANES), F32)]
        + [pltpu.VMEM((2 * sb, LANES), F32)] * 3
        + [pltpu.VMEM((3, 2 * ATT_UNIT, LANES), F32), pltpu.VMEM((2, sb, LANES), F32)],
        compiler_params=_cparams(("parallel", "parallel", "arbitrary")),
        name="window_attn",
    )(q, q, k, v, _window_masks())


def _ssd_kernel(xbc_ref, dt_ref, z_ref, ssm0_ref, conv0_ref, cw_ref, cb_ref, dtb_ref, alog_ref, dsk_ref,
                nw_ref, ex_ref, y_ref, ssm_ref, conv_ref, xpad, dtpad, s_scr, *, n_valid):
    q = SSD_CHUNK
    lb = xbc_ref.shape[1]
    ci = pl.program_id(1)
    nc = pl.num_programs(1)
    gw = SSD_WIDTH // SSD_GROUPS
    hpg = SSD_HEADS // SSD_GROUPS

    @pl.when(ci == 0)
    def _():
        xpad[0:SUBLANES, :] = jnp.zeros((SUBLANES, CONV_DIM), F32)
        xpad[SUBLANES - (CONV_W - 1):SUBLANES, :] = conv0_ref[0]
        for g in range(SSD_GROUPS):
            s_scr[g] = jnp.transpose(ssm0_ref[0, g * hpg:(g + 1) * hpg].reshape(gw, SSD_STATE))

    xpad[SUBLANES:SUBLANES + lb, :] = xbc_ref[0]
    dtpad[0:lb, :] = dt_ref[0]
    if lb < q:
        xpad[SUBLANES + lb:SUBLANES + q, :] = jnp.zeros((q - lb, CONV_DIM), F32)
        dtpad[lb:q, :] = jnp.zeros((q - lb, LANES), F32)

    conv = cb_ref[...]
    for k in range(CONV_W):
        sh = CONV_W - 1 - k
        conv = conv + xpad[SUBLANES - sh:SUBLANES - sh + q, :] * cw_ref[k:k + 1, :]
    act = _silu(conv)

    @pl.when(ci == nc - 1)
    def _():
        conv_ref[0] = xpad[SUBLANES + n_valid - (CONV_W - 1):SUBLANES + n_valid, :]

    xpad[0:SUBLANES, :] = xpad[q:q + SUBLANES, :]

    xs = act[:, :SSD_WIDTH]
    row = lax.broadcasted_iota(I32, (q, LANES), 0)
    dtr = dtpad[...] + dtb_ref[...]
    dt = jnp.maximum(dtr, 0.0) + jnp.log(1.0 + jnp.exp(-jnp.abs(dtr)))
    dt = jnp.where(row < n_valid, dt, 0.0)
    a = -jnp.exp(alog_ref[...])
    da = dt * a
    r2 = lax.broadcasted_iota(I32, (q, q), 0)
    c2 = lax.broadcasted_iota(I32, (q, q), 1)
    causal = r2 >= c2
    a_cs = jnp.dot(causal.astype(F32), da, precision=HIGHEST, preferred_element_type=F32)
    a_cst = jnp.transpose(a_cs)
    a_last = a_cs[q - 1:q, :]
    ex = ex_ref[...]
    per_head = jnp.concatenate([dt, jnp.exp(a_cs), jnp.exp(a_last - a_cs),
                                jnp.broadcast_to(jnp.exp(a_last), (SUBLANES, LANES))], axis=0)
    t_hi = per_head.astype(BF16)
    rem = per_head - t_hi.astype(F32)
    t_mid = rem.astype(BF16)
    t_lo = (rem - t_mid.astype(F32)).astype(BF16)
    spread = (jnp.dot(t_hi, ex, preferred_element_type=F32) + jnp.dot(t_mid, ex, preferred_element_type=F32)
              + jnp.dot(t_lo, ex, preferred_element_type=F32))
    dt_x, ea_x, te_x, cd_x = spread[0:q], spread[q:2 * q], spread[2 * q:3 * q], spread[3 * q:3 * q + 1]
    xdt = xs * dt_x
    xdt_b = xdt.astype(BF16)
    xw_b = (xdt * te_x).astype(BF16)
    lo = lax.broadcasted_iota(I32, (q, LANES), 1) < SSD_HEAD_DIM

    ys = []
    for g in range(SSD_GROUPS):
        bm = act[:, SSD_WIDTH + g * SSD_STATE:SSD_WIDTH + (g + 1) * SSD_STATE]
        cm = act[:, SSD_WIDTH + (SSD_GROUPS + g) * SSD_STATE:SSD_WIDTH + (SSD_GROUPS + g + 1) * SSD_STATE]
        bm_b, cm_b = bm.astype(BF16), cm.astype(BF16)
        cb = lax.dot_general(cm_b, bm_b, (((1,), (1,)), ((), ())), preferred_element_type=F32)
        s_old = s_scr[g]
        y_off = jnp.dot(cm_b, s_old.astype(BF16), preferred_element_type=F32)
        for jp in range(hpg // 2):
            pair = g * (hpg // 2) + jp
            yp = []
            for hh in (0, 1):
                h = 2 * pair + hh
                seg = a_cs[:, h:h + 1] - a_cst[h:h + 1, :]
                dec = jnp.exp(jnp.where(causal, seg, NEG_BIG))
                mm = (cb * dec).astype(BF16)
                yp.append(jnp.dot(mm, xdt_b[:, pair * LANES:(pair + 1) * LANES], preferred_element_type=F32))
            ys.append(jnp.where(lo, yp[0], yp[1]) + y_off[:, jp * LANES:(jp + 1) * LANES]
                      * ea_x[:, pair * LANES:(pair + 1) * LANES])
        bmt_b = jnp.transpose(bm).astype(BF16)
        s_new = s_old * cd_x[:, g * gw:(g + 1) * gw] + jnp.dot(bmt_b, xw_b[:, g * gw:(g + 1) * gw],
                                                              preferred_element_type=F32)
        s_scr[g] = s_new

    y = jnp.concatenate(ys, axis=1) + dsk_ref[...] * xs

    @pl.when(ci == nc - 1)
    def _():
        for g in range(SSD_GROUPS):
            ssm_ref[0, g * hpg:(g + 1) * hpg] = jnp.transpose(s_scr[g]).reshape(hpg, SSD_HEAD_DIM, SSD_STATE)

    hg = y[:lb] * _silu(z_ref[0])
    outs = []
    for g in range(SSD_GROUPS):
        part = hg[:, g * gw:(g + 1) * gw]
        ms = jnp.mean(part * part, axis=1, keepdims=True)
        outs.append(part * lax.rsqrt(ms + NORM_EPS))
    y_ref[0] = (jnp.concatenate(outs, axis=1) * nw_ref[...]).astype(BF16)


def _ssd(xbc, dt_raw, z, ssm0, conv0, prm, lb, n_valid):
    b, l, _ = xbc.shape
    nc = l // lb
    cw, cbias, dtb, alog, dsk, nw, ex = prm
    full = lambda a: pl.BlockSpec(a.shape, lambda bi, ci: (0,) * a.ndim)
    seq = lambda w: pl.BlockSpec((1, lb, w), lambda bi, ci: (bi, ci, 0))
    return pl.pallas_call(
        functools.partial(_ssd_kernel, n_valid=n_valid),
        grid=(b, nc),
        in_specs=[seq(CONV_DIM), seq(LANES), seq(SSD_WIDTH),
                  pl.BlockSpec((1, SSD_HEADS, SSD_HEAD_DIM, SSD_STATE), lambda bi, ci: (bi, 0, 0, 0)),
                  pl.BlockSpec((1, CONV_W - 1, CONV_DIM), lambda bi, ci: (bi, 0, 0)),
                  full(cw), full(cbias), full(dtb), full(alog), full(dsk), full(nw), full(ex)],
        out_specs=(seq(SSD_WIDTH),
                   pl.BlockSpec((1, SSD_HEADS, SSD_HEAD_DIM, SSD_STATE), lambda bi, ci: (bi, 0, 0, 0)),
                   pl.BlockSpec((1, CONV_W - 1, CONV_DIM), lambda bi, ci: (bi, 0, 0))),
        out_shape=(jax.ShapeDtypeStruct((b, l, SSD_WIDTH), BF16),
                   jax.ShapeDtypeStruct((b, SSD_HEADS, SSD_HEAD_DIM, SSD_STATE), F32),
                   jax.ShapeDtypeStruct((b, CONV_W - 1, CONV_DIM), F32)),
        scratch_shapes=[pltpu.VMEM((SSD_CHUNK + 2 * SUBLANES, CONV_DIM), F32),
                        pltpu.VMEM((SSD_CHUNK, LANES), F32),
                        pltpu.VMEM((SSD_GROUPS, SSD_STATE, SSD_WIDTH // SSD_GROUPS), F32)],
        compiler_params=_cparams(("parallel", "arbitrary")),
        name="conv_ssd",
    )(xbc, dt_raw, z, ssm0, conv0, cw, cbias, dtb, alog, dsk, nw, ex)


def _layer_norm(r, g, b):
    mu = jnp.mean(r, axis=1, keepdims=True)
    d = r - mu
    var = jnp.mean(d * d, axis=1, keepdims=True)
    return d * lax.rsqrt(var + NORM_EPS) * g + b


def _mix_kernel(att_ref, ssd_ref, x_ref, wo_ref, g_ref, b_ref, *rest):
    h_ref, hs_ref = rest[-2:]
    tm = x_ref.shape[0]
    mix = jnp.dot(att_ref[...], wo_ref[0:ATT_WIDTH, :], preferred_element_type=F32)
    mix = mix + jnp.dot(ssd_ref[...], wo_ref[ATT_WIDTH:MIX_WIDTH, :], preferred_element_type=F32)
    h = _layer_norm(DEEPNORM_ALPHA * x_ref[...] + mix, g_ref[...], b_ref[...])
    h_ref[...] = h
    for j in range(ROW_SLAB):
        hs_ref[pl.ds(j, tm, stride=ROW_SLAB), :] = h[:, j * LANES:(j + 1) * LANES]


def _mix(att, ssd, x2d, wo_bf, g, b, tm, t_total, row0=0, into=None):
    t = x2d.shape[0]
    b0 = row0 // tm
    row = lambda w: pl.BlockSpec((tm, w), lambda i: (i, 0))
    vec = pl.BlockSpec((1, D_MODEL), lambda i: (0, 0))
    hbm = pl.BlockSpec(memory_space=pl.ANY)
    extra = () if into is None else tuple(into)
    return pl.pallas_call(
        _mix_kernel,
        grid=(t // tm,),
        in_specs=[row(ATT_WIDTH), row(SSD_WIDTH), row(D_MODEL),
                  pl.BlockSpec((MIX_WIDTH, D_MODEL), lambda i: (0, 0)), vec, vec] + [hbm] * len(extra),
        out_specs=(pl.BlockSpec((tm, D_MODEL), lambda i: (b0 + i, 0)),
                   pl.BlockSpec((tm * ROW_SLAB, LANES), lambda i: (b0 + i, 0))),
        out_shape=(jax.ShapeDtypeStruct((t_total, D_MODEL), F32),
                   jax.ShapeDtypeStruct((t_total * ROW_SLAB, LANES), F32)),
        input_output_aliases={} if into is None else {6: 0, 7: 1},
        compiler_params=_cparams(("parallel",)),
        name="outproj_ln1",
    )(att, ssd, x2d, wo_bf, g, b, *extra)


def _router_kernel(h_ref, wrh_ref, wrl_ref, bias_ref, tope_ref, rank_ref, gt_ref, cnt_ref, gscr):
    tm = h_ref.shape[0]
    i = pl.program_id(0)
    per_group = N_EXPERTS // N_EXPERT_GROUPS

    @pl.when(i == 0)
    def _():
        cnt_ref[...] = jnp.zeros_like(cnt_ref)

    h = h_ref[...]
    h_hi = h.astype(BF16)
    h_lo = (h - h_hi.astype(F32)).astype(BF16)
    nt = lambda a, b: lax.dot_general(a, b, (((1,), (1,)), ((), ())), preferred_element_type=F32)
    logits = nt(wrh_ref[...], h_hi) + nt(wrh_ref[...], h_lo) + nt(wrl_ref[...], h_hi)
    s = 1.0 / (1.0 + jnp.exp(-logits))
    sel = s + bias_ref[...]
    neg_inf = -jnp.inf
    e_iota = lax.broadcasted_iota(I32, (N_EXPERTS, tm), 0).astype(F32)
    g_iota = lax.broadcasted_iota(I32, (per_group, tm), 0).astype(F32)

    gscore = []
    for g in range(N_EXPERT_GROUPS):
        blk = sel[g * per_group:(g + 1) * per_group]
        m1 = jnp.max(blk, axis=0, keepdims=True)
        a1 = jnp.min(jnp.where(blk == m1, g_iota, float(per_group)), axis=0, keepdims=True)
        m2 = jnp.max(jnp.where(g_iota == a1, neg_inf, blk), axis=0, keepdims=True)
        gscore.append(m1 + m2)
    blocks = []
    for g in range(N_EXPERT_GROUPS):
        beaten = jnp.zeros((1, tm), F32)
        for o in range(N_EXPERT_GROUPS):
            if o < g:
                beaten = beaten + jnp.where(gscore[o] >= gscore[g], 1.0, 0.0)
            elif o > g:
                beaten = beaten + jnp.where(gscore[o] > gscore[g], 1.0, 0.0)
        blocks.append(jnp.where(beaten < float(TOPK_GROUPS), sel[g * per_group:(g + 1) * per_group], neg_inf))
    cand = jnp.concatenate(blocks, axis=0)

    tops, gsel = [], []
    onehot = jnp.zeros((N_EXPERTS, tm), F32)
    for _ in range(TOP_K):
        mx = jnp.max(cand, axis=0, keepdims=True)
        ix = jnp.min(jnp.where(cand == mx, e_iota, float(N_EXPERTS)), axis=0, keepdims=True)
        hit = e_iota == ix
        tops.append(ix)
        gsel.append(jnp.sum(jnp.where(hit, s, 0.0), axis=0, keepdims=True))
        onehot = jnp.where(hit, 1.0, onehot)
        cand = jnp.where(hit, neg_inf, cand)
    den = gsel[0]
    for k in range(1, TOP_K):
        den = den + gsel[k]

    t_r = lax.broadcasted_iota(I32, (tm, tm), 0)
    t_c = lax.broadcasted_iota(I32, (tm, tm), 1)
    before = (t_r < t_c).astype(BF16)
    oh_b = onehot.astype(BF16)
    base = cnt_ref[...] + jnp.dot(oh_b, before, preferred_element_type=F32)
    cnt_ref[...] = cnt_ref[...] + jnp.dot(oh_b, jnp.ones((tm, LANES), BF16), preferred_element_type=F32)

    gscr[...] = jnp.zeros_like(gscr)
    for k in range(TOP_K):
        tope_ref[k:k + 1, :] = tops[k].astype(I32)
        rank_ref[k:k + 1, :] = jnp.sum(jnp.where(e_iota == tops[k], base, 0.0), axis=0, keepdims=True).astype(I32)
        gscr[k:k + 1, :] = gsel[k] / den * ROUTED_SCALE
    gt_ref[...] = jnp.transpose(gscr[...])


def _router(h_all, wr_t, bias_b):
    t = h_all.shape[0]
    tm = ROUTER_TILE
    wr_hi = wr_t.astype(BF16)
    wr_lo = (wr_t - wr_hi.astype(F32)).astype(BF16)
    return pl.pallas_call(
        _router_kernel,
        grid=(t // tm,),
        in_specs=[pl.BlockSpec((tm, D_MODEL), lambda i: (i, 0)),
                  pl.BlockSpec((N_EXPERTS, D_MODEL), lambda i: (0, 0)),
                  pl.BlockSpec((N_EXPERTS, D_MODEL), lambda i: (0, 0)),
                  pl.BlockSpec((N_EXPERTS, tm), lambda i: (0, 0))],
        out_specs=(pl.BlockSpec((TOP_K, tm), lambda i: (0, i)),
                   pl.BlockSpec((TOP_K, tm), lambda i: (0, i)),
                   pl.BlockSpec((tm, LANES), lambda i: (i, 0)),
                   pl.BlockSpec((N_EXPERTS, LANES), lambda i: (0, 0))),
        out_shape=(jax.ShapeDtypeStruct((TOP_K, t), I32),
                   jax.ShapeDtypeStruct((TOP_K, t), I32),
                   jax.ShapeDtypeStruct((t, LANES), F32),
                   jax.ShapeDtypeStruct((N_EXPERTS, LANES), F32)),
        scratch_shapes=[pltpu.VMEM((tm, LANES), F32)],
        compiler_params=_cparams(("arbitrary",)),
        name="router",
    )(h_all, wr_hi, wr_lo, bias_b)


_PAD_SIZES = tuple(2 ** p for p in range(int(math.log2(EXPERT_BLOCK)) - 1, -1, -1))


def _slab(ref, row, n=1):
    return ref.at[pl.ds(pl.multiple_of(row * ROW_SLAB, ROW_SLAB), n * ROW_SLAB)]


def _padfill_kernel(pstart_ref, pcnt_ref, tail_ref, xs_ref, zero_scr, sem):
    zero_scr[...] = jnp.zeros_like(zero_scr)
    half = EXPERT_BLOCK // 2
    tail_copy = lambda j: pltpu.make_async_copy(_slab(zero_scr, 0, half), _slab(xs_ref, tail_ref[0] + j * half, half),
                                                sem)

    def tail_start(j, c):
        tail_copy(j).start()
        return c

    def tail_wait(j, c):
        tail_copy(j).wait()
        return c

    lax.fori_loop(0, tail_ref[1], tail_start, 0)

    def copies(e):
        base = pstart_ref[e]
        cnt = pcnt_ref[e]
        out = []
        for sz in _PAD_SIZES:
            out.append(((cnt & sz) != 0, pltpu.make_async_copy(_slab(zero_scr, 0, sz), _slab(xs_ref, base, sz), sem)))
            base = base + (cnt & sz)
        return out

    def start(e, c):
        for pred, cp in copies(e):
            @pl.when(pred)
            def _():
                cp.start()
        return c

    def wait(e, c):
        for pred, cp in copies(e):
            @pl.when(pred)
            def _():
                cp.wait()
        return c

    lax.fori_loop(0, N_EXPERTS, start, 0)
    lax.fori_loop(0, N_EXPERTS, wait, 0)
    lax.fori_loop(0, tail_ref[1], tail_wait, 0)


def _padfill(pad_start, pad_cnt, tail, n_rows):
    return pl.pallas_call(
        _padfill_kernel,
        grid_spec=pltpu.PrefetchScalarGridSpec(
            num_scalar_prefetch=3, grid=(1,), in_specs=[],
            out_specs=pl.BlockSpec(memory_space=pl.ANY),
            scratch_shapes=[pltpu.VMEM((EXPERT_BLOCK // 2 * ROW_SLAB, LANES), F32), pltpu.SemaphoreType.DMA]),
        out_shape=jax.ShapeDtypeStruct((n_rows * ROW_SLAB, LANES), F32),
        compiler_params=_cparams(("arbitrary",)),
        name="dispatch_padfill",
    )(pad_start, pad_cnt, tail)


def _dest_kernel(tope_ref, rank_ref, pstart_ref, dest_ref):
    tm = tope_ref.shape[1]
    e_iota = lax.broadcasted_iota(I32, (N_EXPERTS, tm), 0)
    ps = pstart_ref[...]
    for k in range(TOP_K):
        base = jnp.sum(jnp.where(e_iota == tope_ref[k:k + 1, :], ps, 0.0), axis=0, keepdims=True)
        dest_ref[k:k + 1, :] = base.astype(I32) + rank_ref[k:k + 1, :]


def _dest(top_e, rank, pstarts):
    t = top_e.shape[1]
    tm = DISPATCH_TILE
    blk = pl.BlockSpec((TOP_K, tm), lambda i: (0, i))
    return pl.pallas_call(
        _dest_kernel,
        grid=(t // tm,),
        in_specs=[blk, blk, pl.BlockSpec((N_EXPERTS, tm), lambda i: (0, 0))],
        out_specs=blk,
        out_shape=jax.ShapeDtypeStruct((TOP_K, t), I32),
        compiler_params=_cparams(("parallel",)),
        name="dispatch_dest",
    )(top_e, rank, jnp.broadcast_to(pstarts.astype(F32)[:, None], (N_EXPERTS, tm)))


def _dest_tiles(dest, tm):
    t = dest.shape[1]
    return jnp.transpose(dest.reshape(TOP_K, t // tm, tm), (1, 0, 2)).reshape(t // tm, TOP_K * tm)


def _dispatch_kernel(dest_ref, h_ref, xs_in_ref, xs_ref, idx_smem, isem, sem):
    del xs_in_ref
    i = pl.program_id(0)
    tm = DISPATCH_TILE
    cp = pltpu.make_async_copy(dest_ref.at[i], idx_smem, isem)
    cp.start()
    cp.wait()

    def row_copy(t, k):
        return pltpu.make_async_copy(_slab(h_ref, t), _slab(xs_ref, idx_smem[k * tm + t]), sem)

    def start(t, c):
        for k in range(TOP_K):
            row_copy(t, k).start(priority=k % 2)
        return c

    def wait(t, c):
        for k in range(TOP_K):
            row_copy(t, k).wait()
        return c

    lax.fori_loop(0, tm, start, 0)
    lax.fori_loop(0, tm, wait, 0)


def _dispatch(dest_tiles, h_slab, xs):
    tm = DISPATCH_TILE
    return pl.pallas_call(
        _dispatch_kernel,
        grid=(dest_tiles.shape[0],),
        in_specs=[pl.BlockSpec(dest_tiles.shape, lambda i: (0, 0)),
                  pl.BlockSpec((tm * ROW_SLAB, LANES), lambda i: (i, 0)),
                  pl.BlockSpec(memory_space=pl.ANY)],
        out_specs=pl.BlockSpec(memory_space=pl.ANY),
        out_shape=jax.ShapeDtypeStruct(xs.shape, xs.dtype),
        scratch_shapes=[pltpu.SMEM((tm * TOP_K,), I32), pltpu.SemaphoreType.DMA, pltpu.SemaphoreType.DMA],
        input_output_aliases={2: 0},
        compiler_params=_cparams(("arbitrary",)),
        name="dispatch_rows",
    )(dest_tiles, h_slab, xs)


def _expert_kernel(bi_ref, be_ref, nu_ref, x_ref, wg_ref, wu_ref, wd_ref, y_ref, wg_b, wu_b, wd_b):
    del bi_ref
    i = pl.program_id(0)

    @pl.when(i < nu_ref[0])
    def _():
        prev = be_ref[jnp.maximum(i - 1, 0)]

        @pl.when(jnp.logical_or(i == 0, be_ref[i] != prev))
        def _():
            wg_b[...] = wg_ref[0].astype(BF16)
            wu_b[...] = wu_ref[0].astype(BF16)
            wd_b[...] = wd_ref[0].astype(BF16)

        rows = EXPERT_BLOCK // EXPERT_CHUNKS
        for c in range(EXPERT_CHUNKS):
            r0 = c * rows * ROW_SLAB
            xb = jnp.concatenate([x_ref[pl.ds(r0 + j, rows, stride=ROW_SLAB), :] for j in range(ROW_SLAB)],
                                 axis=1).astype(BF16)
            g = jnp.dot(xb, wg_b[...], preferred_element_type=F32)
            u = jnp.dot(xb, wu_b[...], preferred_element_type=F32)
            a = (_silu(g) * u).astype(BF16)
            y = jnp.dot(a, wd_b[...], preferred_element_type=F32)
            for j in range(ROW_SLAB):
                y_ref[pl.ds(r0 + j, rows, stride=ROW_SLAB), :] = y[:, j * LANES:(j + 1) * LANES]

    @pl.when(i >= nu_ref[0])
    def _():
        y_ref[...] = jnp.zeros_like(y_ref)


def _experts(blk_idx, blk_exp, n_used, xs, wg, wu, wd):
    n_rows = xs.shape[0] // ROW_SLAB
    nb = n_rows // EXPERT_BLOCK
    rows = pl.BlockSpec((EXPERT_BLOCK * ROW_SLAB, LANES), lambda i, bi, be, nu: (bi[i], 0))
    out_rows = pl.BlockSpec((EXPERT_BLOCK * ROW_SLAB, LANES), lambda i, bi, be, nu: (i, 0))
    return pl.pallas_call(
        _expert_kernel,
        grid_spec=pltpu.PrefetchScalarGridSpec(
            num_scalar_prefetch=3, grid=(nb,),
            in_specs=[rows,
                      pl.BlockSpec((1, D_MODEL, EXPERT_FF), lambda i, bi, be, nu: (be[i], 0, 0)),
                      pl.BlockSpec((1, D_MODEL, EXPERT_FF), lambda i, bi, be, nu: (be[i], 0, 0)),
                      pl.BlockSpec((1, EXPERT_FF, D_MODEL), lambda i, bi, be, nu: (be[i], 0, 0))],
            out_specs=out_rows,
            scratch_shapes=[pltpu.VMEM((D_MODEL, EXPERT_FF), BF16), pltpu.VMEM((D_MODEL, EXPERT_FF), BF16),
                            pltpu.VMEM((EXPERT_FF, D_MODEL), BF16)]),
        out_shape=jax.ShapeDtypeStruct(xs.shape, F32),
        compiler_params=_cparams(("arbitrary",)),
        name="expert_ffn",
    )(blk_idx, blk_exp, n_used, xs, wg, wu, wd)


def _combine_kernel(dest_ref, h_ref, gt_ref, ys_ref, sg_ref, su_ref, sd_ref, g_ref, b_ref, o_ref, o2_ref,
                    idx0, idx1, gbuf, isem, sem, *, n_first):
    idx_smem = (idx0, idx1)
    i = pl.program_id(0)
    n = pl.num_programs(0)
    tm = h_ref.shape[0]

    def idx_copy(step, s):
        return pltpu.make_async_copy(dest_ref.at[step], idx_smem[s], isem.at[s])

    def row_copy(s, t, k, src_row):
        return pltpu.make_async_copy(_slab(ys_ref, src_row), _slab(gbuf.at[s, k], t), sem.at[s])

    def issue(s):
        def body(t, c):
            for k in range(TOP_K):
                row_copy(s, t, k, idx_smem[s][k * tm + t]).start(priority=k % 2)
            return c
        lax.fori_loop(0, tm, body, 0)

    def drain(s):
        def body(t, c):
            for k in range(TOP_K):
                row_copy(s, t, k, 0).wait()
            return c
        lax.fori_loop(0, tm, body, 0)

    @pl.when(i == 0)
    def _():
        first = idx_copy(0, 0)
        first.start()
        first.wait()
        issue(0)

        @pl.when(n > 1)
        def _():
            idx_copy(1, 1).start()

    def step(slot):
        nslot = 1 - slot

        @pl.when(i + 1 < n)
        def _():
            idx_copy(i + 1, nslot).wait()
            issue(nslot)

            @pl.when(i + 2 < n)
            def _():
                idx_copy(i + 2, slot).start()

        h = h_ref[...]
        hb = h.astype(BF16)
        g = jnp.dot(hb, sg_ref[...], preferred_element_type=F32)
        u = jnp.dot(hb, su_ref[...], preferred_element_type=F32)
        f = jnp.dot((_silu(g) * u).astype(BF16), sd_ref[...], preferred_element_type=F32)
        drain(slot)
        gt = gt_ref[...]
        cols = []
        for j in range(ROW_SLAB):
            fj = f[:, j * LANES:(j + 1) * LANES]
            for k in range(TOP_K):
                fj = fj + gbuf[slot, k, pl.ds(j, tm, stride=ROW_SLAB), :] * gt[:, k:k + 1]
            cols.append(fj)
        f = jnp.concatenate(cols, axis=1)
        out = _layer_norm(DEEPNORM_ALPHA * h + f, g_ref[...], b_ref[...])

        @pl.when(i < n_first)
        def _():
            o_ref[...] = out

        @pl.when(i == n_first)
        def _():
            o2_ref[...] = out

    for parity in (0, 1):
        pl.when((i & 1) == parity)(functools.partial(step, parity))


def _combine(dest_tiles, h_all, gates_t, ys, sg_bf, su_bf, sd_bf, g, b, n_first):
    t = h_all.shape[0]
    tm = COMBINE_TILE
    vec = pl.BlockSpec((1, D_MODEL), lambda i: (0, 0))
    return pl.pallas_call(
        functools.partial(_combine_kernel, n_first=n_first),
        grid=(t // tm,),
        in_specs=[pl.BlockSpec(dest_tiles.shape, lambda i: (0, 0)),
                  pl.BlockSpec((tm, D_MODEL), lambda i: (i, 0)),
                  pl.BlockSpec((tm, LANES), lambda i: (i, 0)),
                  pl.BlockSpec(memory_space=pl.ANY),
                  pl.BlockSpec((D_MODEL, SHARED_FF), lambda i: (0, 0)),
                  pl.BlockSpec((D_MODEL, SHARED_FF), lambda i: (0, 0)),
                  pl.BlockSpec((SHARED_FF, D_MODEL), lambda i: (0, 0)), vec, vec],
        out_specs=(pl.BlockSpec((tm, D_MODEL), lambda i: (jnp.minimum(i, n_first - 1), 0)),
                   pl.BlockSpec((tm, D_MODEL), lambda i: (0, 0))),
        out_shape=(jax.ShapeDtypeStruct((n_first * tm, D_MODEL), F32), jax.ShapeDtypeStruct((tm, D_MODEL), F32)),
        scratch_shapes=[pltpu.SMEM((tm * TOP_K,), I32), pltpu.SMEM((tm * TOP_K,), I32),
                        pltpu.VMEM((2, TOP_K, tm * ROW_SLAB, LANES), F32),
                        pltpu.SemaphoreType.DMA((2,)), pltpu.SemaphoreType.DMA((2,))],
        compiler_params=_cparams(("arbitrary",)),
        name="combine_ln2",
    )(dest_tiles, h_all, gates_t, ys, sg_bf, su_bf, sd_bf, g, b)


def _moe(h_all, h_slab, n_first, wr, rbias, wg, wu, wd, sg, su, sd, g2, b2):
    t = h_all.shape[0]
    top_e, rank, gates_t, cnt = _router(h_all, jnp.transpose(wr),
                                        jnp.broadcast_to(rbias[:, None], (N_EXPERTS, ROUTER_TILE)))
    counts = cnt[:, 0].astype(I32)
    pcounts = (counts + EXPERT_BLOCK - 1) // EXPERT_BLOCK * EXPERT_BLOCK
    pends = jnp.cumsum(pcounts)
    pstarts = pends - pcounts
    n_rows = (t * TOP_K // EXPERT_BLOCK + N_EXPERTS - 1) * EXPERT_BLOCK
    nb = n_rows // EXPERT_BLOCK
    n_used = (pends[-1] // EXPERT_BLOCK).astype(I32)
    blk_idx = jnp.minimum(jnp.arange(nb, dtype=I32), n_used - 1)
    blk_exp = jnp.minimum(jnp.sum((pends[None, :] <= (blk_idx * EXPERT_BLOCK)[:, None]).astype(I32), axis=1),
                          N_EXPERTS - 1)
    dest = _dest(top_e, rank, pstarts)
    tail = jnp.stack([pends[-1], (n_rows - pends[-1]) // (EXPERT_BLOCK // 2)]).astype(I32)
    xs = _padfill((pstarts + counts).astype(I32), (pcounts - counts).astype(I32), tail, n_rows)
    xs = _dispatch(_dest_tiles(dest, DISPATCH_TILE), h_slab, xs)
    ys = _experts(blk_idx, blk_exp, n_used.reshape(1), xs, wg, wu, wd)
    return _combine(_dest_tiles(dest, COMBINE_TILE), h_all, gates_t, ys,
                    sg.astype(BF16), su.astype(BF16), sd.astype(BF16), g2[None, :], b2[None, :], n_first)


def _expand_matrix():
    h = np.arange(LANES)[:, None]
    c = np.arange(SSD_WIDTH)[None, :]
    return jnp.asarray((c // SSD_HEAD_DIM == h).astype(np.float32), dtype=BF16)


def kernel(x_prompt, x_sample, cache_k, cache_v, state_ssm, state_conv, w_in, conv_w, conv_b, dt_bias, a_log, d_skip, ssd_norm_w, w_out, ln1_g, ln1_b, w_router, router_bias, w_exp_gate, w_exp_up, w_exp_down, w_sh_gate, w_sh_up, w_sh_down, ln2_g, ln2_b):
    bp, lp, _ = x_prompt.shape
    bs, ls, _ = x_sample.shape
    win = cache_k.shape[2]
    keep = min(MAX_WINDOW, lp)
    assert lp % SSD_CHUNK == 0 and ls <= SUBLANES and win % KEY_TILE == 0 and win >= MAX_WINDOW

    w_bf = jnp.pad(w_in[0], ((0, 0), (0, IN_COLS_PAD - IN_COLS))).astype(BF16)
    wo_bf = w_out[0].astype(BF16)
    pad_l = lambda v: jnp.pad(v, (0, LANES - v.shape[0]))[None, :]
    ssd_prm = (conv_w[0], conv_b[0][None, :], pad_l(dt_bias[0]), pad_l(a_log[0]),
               jnp.repeat(d_skip[0], SSD_HEAD_DIM)[None, :], ssd_norm_w[0][None, :], _expand_matrix())

    tp = bp * lp
    tm_p = 256 if lp % 256 == 0 else SSD_CHUNK
    tm_mix = 512 if tp % 512 == 0 else SSD_CHUNK
    cos_p, sin_p = _rope_tables(jnp.arange(lp, dtype=F32))
    xp2 = x_prompt.reshape(tp, D_MODEL)
    q, kf, vf, z, xbc, dtr = _inproj(xp2, w_bf, cos_p, sin_p, tm_p, lp // tm_p)
    att = _attention_window(q.reshape(bp, lp, ATT_WIDTH), kf.reshape(bp, lp, KV_WIDTH), vf.reshape(bp, lp, KV_WIDTH))
    ssd_y, ssm_p, conv_p = _ssd(xbc.reshape(bp, lp, CONV_DIM), dtr.reshape(bp, lp, LANES),
                                z.reshape(bp, lp, SSD_WIDTH),
                                jnp.zeros((bp, SSD_HEADS, SSD_HEAD_DIM, SSD_STATE), F32),
                                jnp.zeros((bp, CONV_W - 1, CONV_DIM), F32), ssd_prm, SSD_CHUNK, SSD_CHUNK)
    ts = bs * ls
    t_pad = -(-(tp + ts) // DISPATCH_TILE) * DISPATCH_TILE
    tail = t_pad - tp
    assert tp % tail == 0 and ts <= COMBINE_TILE and tp % COMBINE_TILE == 0
    h_bufs = _mix(att.reshape(tp, ATT_WIDTH), ssd_y.reshape(tp, SSD_WIDTH), xp2, wo_bf, ln1_g, ln1_b, tm_mix, t_pad)

    pos_s = (PAST_LEN + jnp.arange(ls, dtype=jnp.int32)).astype(F32)
    cos_s, sin_s = _rope_tables(jnp.tile(pos_s, bs))
    xs2 = x_sample.reshape(ts, D_MODEL)
    q_s, kf_s, vf_s, z_s, xbc_s, dtr_s = _inproj(xs2, w_bf, cos_s, sin_s, ts, 1)
    rows8 = lambda a, w: jnp.pad(a.reshape(bs, ls, w), ((0, 0), (0, SUBLANES - ls), (0, 0)))
    att_s = _attention_decode(rows8(q_s, ATT_WIDTH), cache_k[0].reshape(bs, win, KV_WIDTH),
                              cache_v[0].reshape(bs, win, KV_WIDTH), rows8(kf_s, KV_WIDTH),
                              rows8(vf_s, KV_WIDTH))[:, :ls]
    ssd_s, ssm_s, conv_s = _ssd(rows8(xbc_s, CONV_DIM), rows8(dtr_s, LANES), rows8(z_s, SSD_WIDTH),
                                state_ssm[0], state_conv[0], ssd_prm, SUBLANES, ls)
    tail_rows = lambda a: jnp.pad(a, ((0, tail - ts), (0, 0)))
    h_all, h_slab = _mix(tail_rows(att_s.reshape(ts, ATT_WIDTH)), tail_rows(ssd_s[:, :ls].reshape(ts, SSD_WIDTH)),
                         tail_rows(xs2), wo_bf, ln1_g, ln1_b, tail, t_pad, row0=tp, into=h_bufs)

    y_p, y_s = _moe(h_all, h_slab, tp // COMBINE_TILE, w_router[0], router_bias[0], w_exp_gate[0], w_exp_up[0],
                    w_exp_down[0], w_sh_gate[0], w_sh_up[0], w_sh_down[0], ln2_g[0], ln2_b[0])

    kv5 = lambda a, b, l: a.reshape(1, b, l, N_KV_HEADS, HEAD_DIM)
    return (y_p.reshape(bp, lp, D_MODEL), y_s[:ts].reshape(bs, ls, D_MODEL),
            kv5(kf.reshape(bp, lp, KV_WIDTH)[:, lp - keep:], bp, keep),
            kv5(vf.reshape(bp, lp, KV_WIDTH)[:, lp - keep:], bp, keep),
            ssm_p[None], conv_p[None],
            kv5(kf_s, bs, ls), kv5(vf_s, bs, ls), ssm_s[None], conv_s[None])
```

```python
import functools
import math

import jax
import jax.numpy as jnp
import numpy as np
from jax import lax
from jax.experimental import pallas as pl
from jax.experimental.pallas import tpu as pltpu

F32 = jnp.float32
BF16 = jnp.bfloat16
I32 = jnp.int32

D_MODEL = 1024
PAST_LEN = 16384
HEAD_DIM = 64
N_ATT_HEADS = 16
N_KV_HEADS = 8
ATT_WIDTH = N_ATT_HEADS * HEAD_DIM
KV_WIDTH = N_KV_HEADS * HEAD_DIM
DILATED_BRANCHES = ((128, 1), (512, 4), (2048, 16))
MAX_WINDOW = 2048
ROPE_THETA = 10000.0
SSD_HEADS = 16
SSD_HEAD_DIM = 64
SSD_WIDTH = SSD_HEADS * SSD_HEAD_DIM
SSD_GROUPS = 2
SSD_STATE = 128
SSD_CHUNK = 128
CONV_W = 4
CONV_DIM = SSD_WIDTH + 2 * SSD_GROUPS * SSD_STATE
MIX_WIDTH = ATT_WIDTH + SSD_WIDTH
IN_COLS = ATT_WIDTH + 2 * KV_WIDTH + SSD_WIDTH + CONV_DIM + SSD_HEADS
N_EXPERTS = 256
TOP_K = 8
N_EXPERT_GROUPS = 8
TOPK_GROUPS = 4
EXPERT_FF = 256
SHARED_FF = 256
ROUTED_SCALE = 2.5
DEPTH = 1
DEEPNORM_ALPHA = (2.0 * DEPTH) ** 0.25
NORM_EPS = 1e-5

LANES = 128
SUBLANES = 8
VMEM_LIMIT = 56 * 1024 * 1024

IN_COLS_PAD = ATT_WIDTH + 2 * KV_WIDTH + SSD_WIDTH + CONV_DIM + LANES
KEY_TILE = 128
EXPERT_BLOCK = 512
ROUTER_TILE = 128
DISPATCH_TILE = 256
COMBINE_TILE = 128
ROW_SLAB = D_MODEL // LANES
NEG_BIG = -1e30
HIGHEST = lax.Precision.HIGHEST


def _cparams(sem):
    return pltpu.CompilerParams(dimension_semantics=sem, vmem_limit_bytes=VMEM_LIMIT)


def _silu(x):
    return x * (1.0 / (1.0 + jnp.exp(-x)))


def _inproj_kernel(x_ref, w_ref, cq_ref, sq_ref, ck_ref, sk_ref,
                   q_ref, kf_ref, vf_ref, z_ref, xbc_ref, dt_ref):
    tm = x_ref.shape[0]
    xb = x_ref[...].astype(BF16)
    lane = lax.broadcasted_iota(I32, (tm, LANES), 1)
    first_half = (lane % HEAD_DIM) < (HEAD_DIM // 2)

    def rope(a, c, s):
        partner = jnp.where(first_half, pltpu.roll(a, LANES - HEAD_DIM // 2, 1), pltpu.roll(a, HEAD_DIM // 2, 1))
        return a * c + partner * s

    c0 = 0
    acc = jnp.dot(xb, w_ref[:, c0:c0 + ATT_WIDTH], preferred_element_type=F32)
    cq, sq = cq_ref[...], sq_ref[...]
    for j in range(ATT_WIDTH // LANES):
        q_ref[:, j * LANES:(j + 1) * LANES] = rope(acc[:, j * LANES:(j + 1) * LANES], cq, sq)
    c0 += ATT_WIDTH
    acc = jnp.dot(xb, w_ref[:, c0:c0 + KV_WIDTH], preferred_element_type=F32)
    ck, sk = ck_ref[...], sk_ref[...]
    for j in range(KV_WIDTH // LANES):
        kf_ref[:, j * LANES:(j + 1) * LANES] = rope(acc[:, j * LANES:(j + 1) * LANES], ck, sk)
    c0 += KV_WIDTH
    vf_ref[...] = jnp.dot(xb, w_ref[:, c0:c0 + KV_WIDTH], preferred_element_type=F32)
    c0 += KV_WIDTH
    z_ref[...] = jnp.dot(xb, w_ref[:, c0:c0 + SSD_WIDTH], preferred_element_type=F32)
    c0 += SSD_WIDTH
    xbc_ref[...] = jnp.dot(xb, w_ref[:, c0:c0 + CONV_DIM], preferred_element_type=F32)
    c0 += CONV_DIM
    dt_ref[...] = jnp.dot(xb, w_ref[:, c0:c0 + LANES], preferred_element_type=F32)


def _rope_tables(pos):
    half = HEAD_DIM // 2
    inv = ROPE_THETA ** (-jnp.arange(half, dtype=F32) / half)
    ang = pos[:, None] * inv[None, :]
    cos, sin = jnp.cos(ang), jnp.sin(ang)
    c = jnp.concatenate([cos, cos, cos, cos], axis=1)
    s = jnp.concatenate([-sin, sin, -sin, sin], axis=1)
    return c, s


def _inproj(x2d, w_bf, cos_t, sin_t, tm, tiles_per_seq):
    t = x2d.shape[0]
    scale = HEAD_DIM ** -0.5
    tab = pl.BlockSpec((tm, LANES), lambda i: (i % tiles_per_seq, 0))
    row = lambda w: pl.BlockSpec((tm, w), lambda i: (i, 0))
    outs = (
        jax.ShapeDtypeStruct((t, ATT_WIDTH), F32),
        jax.ShapeDtypeStruct((t, KV_WIDTH), F32),
        jax.ShapeDtypeStruct((t, KV_WIDTH), F32),
        jax.ShapeDtypeStruct((t, SSD_WIDTH), F32),
        jax.ShapeDtypeStruct((t, CONV_DIM), F32),
        jax.ShapeDtypeStruct((t, LANES), F32),
    )
    return pl.pallas_call(
        _inproj_kernel,
        grid=(t // tm,),
        in_specs=[row(D_MODEL), pl.BlockSpec((D_MODEL, IN_COLS_PAD), lambda i: (0, 0)), tab, tab, tab, tab],
        out_specs=(row(ATT_WIDTH), row(KV_WIDTH), row(KV_WIDTH), row(SSD_WIDTH), row(CONV_DIM), row(LANES)),
        out_shape=outs,
        compiler_params=_cparams(("parallel",)),
        name="inproj",
    )(x2d, w_bf, cos_t * scale, sin_t * scale, cos_t, sin_t)


def _branch_weight(d):
    w = np.zeros(d.shape, np.float32)
    for window, dil in DILATED_BRANCHES:
        w += ((d >= 0) & (d <= window) & (d % dil == 0)).astype(np.float32)
    return w


def _attn_decode_kernel(q_ref, kc_ref, vc_ref, kn_ref, vn_ref, wc_ref, wn_ref, o_ref, ktail, vtail):
    nq = q_ref.shape[1]
    lane = lax.broadcasted_iota(I32, (nq, LANES), 1)
    lo = lane < HEAD_DIM
    ktail[...] = jnp.zeros_like(ktail)
    vtail[...] = jnp.zeros_like(vtail)
    ktail[0:nq, :] = kn_ref[0]
    vtail[0:nq, :] = vn_ref[0]
    wc, wn = wc_ref[...], wn_ref[...]
    nt = (((1,), (1,)), ((), ()))
    for g in range(N_KV_HEADS):
        pair, half = divmod(g, 2)
        cols = slice(pair * LANES, (pair + 1) * LANES)
        q = q_ref[0, :, g * LANES:(g + 1) * LANES]
        qs = pltpu.roll(q, HEAD_DIM, 1)
        valid = lo if half == 0 else jnp.logical_not(lo)
        qa_src, qb_src = (q, qs) if half == 0 else (qs, q)
        qq = jnp.concatenate([jnp.where(valid, qa_src, 0.0), jnp.where(valid, qb_src, 0.0)], axis=0).astype(BF16)
        s_c = lax.dot_general(qq, kc_ref[0, :, cols], nt, preferred_element_type=F32)
        s_n = lax.dot_general(qq, ktail[:, cols].astype(BF16), nt, preferred_element_type=F32)
        s_c = jnp.where(wc > 0.0, s_c, NEG_BIG)
        s_n = jnp.where(wn > 0.0, s_n, NEG_BIG)
        m = jnp.maximum(jnp.max(s_c, axis=1, keepdims=True), jnp.max(s_n, axis=1, keepdims=True))
        p_c = jnp.exp(s_c - m) * wc
        p_n = jnp.exp(s_n - m) * wn
        l = jnp.sum(p_c, axis=1, keepdims=True) + jnp.sum(p_n, axis=1, keepdims=True)
        acc = jnp.dot(p_c.astype(BF16), vc_ref[0, :, cols], preferred_element_type=F32)
        acc = acc + jnp.dot(p_n.astype(BF16), vtail[:, cols].astype(BF16), preferred_element_type=F32)
        o = acc / l
        oa, ob = o[:nq], o[nq:]
        if half == 0:
            out = jnp.where(lo, oa, pltpu.roll(ob, HEAD_DIM, 1))
        else:
            out = jnp.where(lo, pltpu.roll(oa, HEAD_DIM, 1), ob)
        o_ref[0, :, g * LANES:(g + 1) * LANES] = out.astype(BF16)


def _attention_decode(q, k_cache, v_cache, k_new, v_new):
    b, nq, _ = q.shape
    win = k_cache.shape[1]
    i = np.arange(nq)[:, None]
    wc = _branch_weight(win + i - np.arange(win)[None, :])
    wn = _branch_weight(i - np.arange(KEY_TILE)[None, :])
    wn[:, nq:] = 0.0
    stack = lambda w: jnp.asarray(np.concatenate([w, w], axis=0))
    seq = lambda rows, w: pl.BlockSpec((1, rows, w), lambda bi: (bi, 0, 0))
    full = lambda shp: pl.BlockSpec(shp, lambda bi: (0, 0))
    return pl.pallas_call(
        _attn_decode_kernel,
        grid=(b,),
        in_specs=[seq(nq, ATT_WIDTH), seq(win, KV_WIDTH), seq(win, KV_WIDTH), seq(nq, KV_WIDTH), seq(nq, KV_WIDTH),
                  full((2 * nq, win)), full((2 * nq, KEY_TILE))],
        out_specs=seq(nq, ATT_WIDTH),
        out_shape=jax.ShapeDtypeStruct((b, nq, ATT_WIDTH), BF16),
        scratch_shapes=[pltpu.VMEM((KEY_TILE, KV_WIDTH), F32), pltpu.VMEM((KEY_TILE, KV_WIDTH), F32)],
        compiler_params=_cparams(("arbitrary",)),
        name="decode_attn",
    )(q, k_cache, v_cache, k_new, v_new, stack(wc), stack(wn))


ATT_SUPER = 2048
ATT_UNIT = 128
ATT_SPAN = max(w // d for w, d in DILATED_BRANCHES)


ATT_R4 = 4


MASK_FULL, MASK_CLAMPED, MASK_BEFORE_START = 0, 1, 2


def _window_masks():
    i = (np.arange(2 * ATT_UNIT) % ATT_UNIT)[:, None]
    c = np.arange(ATT_UNIT + ATT_SPAN)[None, :]
    ok = lambda d: (d >= 0) & (d <= ATT_SPAN)
    keep = np.stack([ok(ATT_SPAN + i - c), ok(i - c), ok(ATT_SPAN + i - c) & (c >= ATT_SPAN)])
    return jnp.asarray(np.where(keep, 0.0, NEG_BIG).astype(np.float32))


def _attn_window_kernel(q0_ref, q1_ref, k_ref, v_ref, mask_ref, o_ref,
                        k4_s, v4_s, q4_s, acc_s, m_s, l_s, tmp_s, nat_s):
    sb = q0_ref.shape[1]
    sbi = pl.program_id(2)
    u_rows, span, r4 = ATT_UNIT, ATT_SPAN, ATT_R4
    nkeys = u_rows + span
    cls = sb // r4
    look = MAX_WINDOW // r4
    units = sb // u_rows
    assert [d for _, d in DILATED_BRANCHES] == [1, r4, r4 * r4] and sb // (r4 * r4) == u_rows
    lane = lax.broadcasted_iota(I32, (u_rows, LANES), 1)
    lo = lane < HEAD_DIM
    piece = 2 * u_rows

    def regroup(src_ref, dst, nat0, loc0, n):
        for c in range(r4):
            for j0 in range(0, n, piece):
                dst[c, loc0 + j0:loc0 + j0 + piece, :] = src_ref[0, pl.ds(nat0 + r4 * j0 + c, piece, stride=r4), :]

    @pl.when(sbi > 0)
    def _():
        regroup(k_ref, k4_s, sbi * sb - look * r4, 0, look)
        regroup(v_ref, v4_s, sbi * sb - look * r4, 0, look)

    @pl.when(sbi == 0)
    def _():
        k4_s[:, 0:look, :] = jnp.zeros((r4, look, LANES), F32)
        v4_s[:, 0:look, :] = jnp.zeros((r4, look, LANES), F32)

    regroup(k_ref, k4_s, sbi * sb, look, cls)
    regroup(v_ref, v4_s, sbi * sb, look, cls)
    base4 = sbi * cls - look

    for half, qh_ref in enumerate((q0_ref, q1_ref)):
        valid = lo if half == 0 else jnp.logical_not(lo)
        regroup(qh_ref, q4_s, 0, 0, cls)

        def softmax_unit(q, kt, vt, pattern):
            qs = pltpu.roll(q, HEAD_DIM, 1)
            qa_src, qb_src = (q, qs) if half == 0 else (qs, q)
            qq = jnp.concatenate([jnp.where(valid, qa_src, 0.0), jnp.where(valid, qb_src, 0.0)],
                                 axis=0).astype(BF16)
            s = lax.dot_general(qq, kt.astype(BF16), (((1,), (1,)), ((), ())), preferred_element_type=F32)
            s = s + mask_ref[pattern]
            m = jnp.max(s, axis=1, keepdims=True)
            p = jnp.exp(s - m)
            l = jnp.sum(p, axis=1, keepdims=True)
            acc = jnp.dot(p.astype(BF16), vt.astype(BF16), preferred_element_type=F32)
            return jnp.broadcast_to(m, (2 * u_rows, LANES)), jnp.broadcast_to(l, (2 * u_rows, LANES)), acc

        def merge_store(rows, m, l, acc):
            ld = lambda ref: jnp.concatenate([ref[rows[0], :], ref[rows[1], :]], axis=0)
            m_old, l_old, acc_old = ld(m_s), ld(l_s), ld(acc_s)
            m_new = jnp.maximum(m_old, m)
            a_old, a_new = jnp.exp(m_old - m_new), jnp.exp(m - m_new)
            l = a_old * l_old + a_new * l
            acc = a_old * acc_old + a_new * acc
            for h2 in (0, 1):
                sl = slice(h2 * u_rows, (h2 + 1) * u_rows)
                m_s[rows[h2], :] = m_new[sl]
                l_s[rows[h2], :] = l[sl]
                acc_s[rows[h2], :] = acc[sl]

        def unit_d1(u, c):
            q0 = pl.multiple_of(u * u_rows, u_rows)
            qpos = sbi * sb + u * u_rows
            kpos = pl.multiple_of(jnp.maximum(qpos - span, 0), u_rows)
            m, l, acc = softmax_unit(qh_ref[0, pl.ds(q0, u_rows), :], k_ref[0, pl.ds(kpos, nkeys), :],
                                     v_ref[0, pl.ds(kpos, nkeys), :], jnp.where(qpos == 0, MASK_CLAMPED, MASK_FULL))
            part = u_rows // r4
            for a, (val, dst) in enumerate(((m, m_s), (l, l_s), (acc, acc_s))):
                tmp_s[a] = val
                for h2 in (0, 1):
                    for cl in range(r4):
                        d0 = pl.multiple_of(h2 * sb + cl * cls + u * part, part)
                        dst[pl.ds(d0, part), :] = tmp_s[a, pl.ds(h2 * u_rows + cl, part, stride=r4), :]
            return c

        def unit_d4(idx, c):
            cl, u = idx & (r4 - 1), idx >> 2
            qpos = sbi * cls + u * u_rows
            kpos = jnp.maximum(qpos - span, 0)
            kloc = pl.multiple_of(kpos - base4, u_rows)
            q0 = pl.multiple_of(u * u_rows, u_rows)
            m, l, acc = softmax_unit(q4_s[cl, pl.ds(q0, u_rows), :], k4_s[cl, pl.ds(kloc, nkeys), :],
                                     v4_s[cl, pl.ds(kloc, nkeys), :], jnp.where(qpos == 0, MASK_CLAMPED, MASK_FULL))
            r0 = pl.multiple_of(cl * cls + u * u_rows, u_rows)
            merge_store((pl.ds(r0, u_rows), pl.ds(sb + r0, u_rows)), m, l, acc)
            return c

        def unit_d16(idx, c):
            cl, sg = idx & (r4 - 1), idx >> 2
            qpos = sbi * u_rows
            kloc = sg + r4 * (qpos - span) - base4
            m, l, acc = softmax_unit(q4_s[cl, pl.ds(sg, u_rows, stride=r4), :],
                                     k4_s[cl, pl.ds(kloc, nkeys, stride=r4), :],
                                     v4_s[cl, pl.ds(kloc, nkeys, stride=r4), :],
                                     jnp.where(qpos == 0, MASK_BEFORE_START, MASK_FULL))
            r0 = cl * cls + sg
            merge_store((pl.ds(r0, u_rows, stride=r4), pl.ds(sb + r0, u_rows, stride=r4)), m, l, acc)
            return c

        for body in (unit_d1, unit_d4, unit_d16):
            lax.fori_loop(0, units, body, 0, unroll=8)

        for h2 in (0, 1):
            for cl in range(r4):
                for j0 in range(0, cls, piece):
                    rows = slice(h2 * sb + cl * cls + j0, h2 * sb + cl * cls + j0 + piece)
                    nat_s[h2, pl.ds(r4 * j0 + cl, piece, stride=r4), :] = acc_s[rows, :] / l_s[rows, :]
        lo2 = jnp.concatenate([lo, lo], axis=0)
        for c in range(sb // piece):
            ra = slice(c * piece, (c + 1) * piece)
            oa, ob = nat_s[0, ra, :], nat_s[1, ra, :]
            if half == 0:
                out = jnp.where(lo2, oa, pltpu.roll(ob, HEAD_DIM, 1))
            else:
                out = jnp.where(lo2, pltpu.roll(oa, HEAD_DIM, 1), ob)
            o_ref[0, ra, half * LANES:(half + 1) * LANES] = out.astype(BF16)


def _attention_window(q, k, v):
    b, l, _ = q.shape
    sb = ATT_SUPER
    max_dil = max(d for _, d in DILATED_BRANCHES)
    assert l % sb == 0 and l // max_dil >= ATT_UNIT + ATT_SPAN
    qspec = lambda h: pl.BlockSpec((1, sb, LANES), lambda bi, pi, si: (bi, si, 2 * pi + h))
    kspec = pl.BlockSpec((1, l, LANES), lambda bi, pi, si: (bi, 0, pi))
    return pl.pallas_call(
        _attn_window_kernel,
        grid=(b, KV_WIDTH // LANES, l // sb),
        in_specs=[qspec(0), qspec(1), kspec, kspec,
                  pl.BlockSpec((3, 2 * ATT_UNIT, ATT_UNIT + ATT_SPAN), lambda bi, pi, si: (0, 0, 0))],
        out_specs=pl.BlockSpec((1, sb, 2 * LANES), lambda bi, pi, si: (bi, si, pi)),
        out_shape=jax.ShapeDtypeStruct((b, l, ATT_WIDTH), BF16),
        scratch_shapes=[pltpu.VMEM((ATT_R4, (MAX_WINDOW + sb) // ATT_R4, LANES), F32)] * 2
        + [pltpu.VMEM((ATT_R4, sb // ATT_R4, LANES), F32)]
        + [pltpu.VMEM((2 * sb, LANES), F32)] * 3
        + [pltpu.VMEM((3, 2 * ATT_UNIT, LANES), F32), pltpu.VMEM((2, sb, LANES), F32)],
        compiler_params=_cparams(("parallel", "parallel", "arbitrary")),
        name="window_attn",
    )(q, q, k, v, _window_masks())


def _ssd_kernel(xbc_ref, dt_ref, z_ref, ssm0_ref, conv0_ref, cw_ref, cb_ref, dtb_ref, alog_ref, dsk_ref,
                nw_ref, ex_ref, y_ref, ssm_ref, conv_ref, xpad, dtpad, s_scr, *, n_valid):
    q = SSD_CHUNK
    lb = xbc_ref.shape[1]
    ci = pl.program_id(1)
    nc = pl.num_programs(1)
    gw = SSD_WIDTH // SSD_GROUPS
    hpg = SSD_HEADS // SSD_GROUPS

    @pl.when(ci == 0)
    def _():
        xpad[0:SUBLANES, :] = jnp.zeros((SUBLANES, CONV_DIM), F32)
        xpad[SUBLANES - (CONV_W - 1):SUBLANES, :] = conv0_ref[0]
        for g in range(SSD_GROUPS):
            s_scr[g] = jnp.transpose(ssm0_ref[0, g * hpg:(g + 1) * hpg].reshape(gw, SSD_STATE))

    xpad[SUBLANES:SUBLANES + lb, :] = xbc_ref[0]
    dtpad[0:lb, :] = dt_ref[0]
    if lb < q:
        xpad[SUBLANES + lb:SUBLANES + q, :] = jnp.zeros((q - lb, CONV_DIM), F32)
        dtpad[lb:q, :] = jnp.zeros((q - lb, LANES), F32)

    conv = cb_ref[...]
    for k in range(CONV_W):
        sh = CONV_W - 1 - k
        conv = conv + xpad[SUBLANES - sh:SUBLANES - sh + q, :] * cw_ref[k:k + 1, :]
    act = _silu(conv)

    @pl.when(ci == nc - 1)
    def _():
        conv_ref[0] = xpad[SUBLANES + n_valid - (CONV_W - 1):SUBLANES + n_valid, :]

    xpad[0:SUBLANES, :] = xpad[q:q + SUBLANES, :]

    xs = act[:, :SSD_WIDTH]
    row = lax.broadcasted_iota(I32, (q, LANES), 0)
    dtr = dtpad[...] + dtb_ref[...]
    dt = jnp.maximum(dtr, 0.0) + jnp.log(1.0 + jnp.exp(-jnp.abs(dtr)))
    dt = jnp.where(row < n_valid, dt, 0.0)
    a = -jnp.exp(alog_ref[...])
    da = dt * a
    r2 = lax.broadcasted_iota(I32, (q, q), 0)
    c2 = lax.broadcasted_iota(I32, (q, q), 1)
    causal = r2 >= c2
    a_cs = jnp.dot(causal.astype(F32), da, precision=HIGHEST, preferred_element_type=F32)
    a_cst = jnp.transpose(a_cs)
    a_last = a_cs[q - 1:q, :]
    ex = ex_ref[...]
    per_head = jnp.concatenate([dt, jnp.exp(a_cs), jnp.exp(a_last - a_cs),
                                jnp.broadcast_to(jnp.exp(a_last), (SUBLANES, LANES))], axis=0)
    t_hi = per_head.astype(BF16)
    rem = per_head - t_hi.astype(F32)
    t_mid = rem.astype(BF16)
    t_lo = (rem - t_mid.astype(F32)).astype(BF16)
    spread = (jnp.dot(t_hi, ex, preferred_element_type=F32) + jnp.dot(t_mid, ex, preferred_element_type=F32)
              + jnp.dot(t_lo, ex, preferred_element_type=F32))
    dt_x, ea_x, te_x, cd_x = spread[0:q], spread[q:2 * q], spread[2 * q:3 * q], spread[3 * q:3 * q + 1]
    xdt = xs * dt_x
    xdt_b = xdt.astype(BF16)
    xw_b = (xdt * te_x).astype(BF16)
    lo = lax.broadcasted_iota(I32, (q, LANES), 1) < SSD_HEAD_DIM

    ys = []
    for g in range(SSD_GROUPS):
        bm = act[:, SSD_WIDTH + g * SSD_STATE:SSD_WIDTH + (g + 1) * SSD_STATE]
        cm = act[:, SSD_WIDTH + (SSD_GROUPS + g) * SSD_STATE:SSD_WIDTH + (SSD_GROUPS + g + 1) * SSD_STATE]
        bm_b, cm_b = bm.astype(BF16), cm.astype(BF16)
        cb = lax.dot_general(cm_b, bm_b, (((1,), (1,)), ((), ())), preferred_element_type=F32)
        s_old = s_scr[g]
        y_off = jnp.dot(cm_b, s_old.astype(BF16), preferred_element_type=F32)
        for jp in range(hpg // 2):
            pair = g * (hpg // 2) + jp
            yp = []
            for hh in (0, 1):
                h = 2 * pair + hh
                seg = a_cs[:, h:h + 1] - a_cst[h:h + 1, :]
                dec = jnp.exp(jnp.where(causal, seg, NEG_BIG))
                mm = (cb * dec).astype(BF16)
                yp.append(jnp.dot(mm, xdt_b[:, pair * LANES:(pair + 1) * LANES], preferred_element_type=F32))
            ys.append(jnp.where(lo, yp[0], yp[1]) + y_off[:, jp * LANES:(jp + 1) * LANES]
                      * ea_x[:, pair * LANES:(pair + 1) * LANES])
        bmt_b = jnp.transpose(bm).astype(BF16)
        s_new = s_old * cd_x[:, g * gw:(g + 1) * gw] + jnp.dot(bmt_b, xw_b[:, g * gw:(g + 1) * gw],
                                                              preferred_element_type=F32)
        s_scr[g] = s_new

    y = jnp.concatenate(ys, axis=1) + dsk_ref[...] * xs

    @pl.when(ci == nc - 1)
    def _():
        for g in range(SSD_GROUPS):
            ssm_ref[0, g * hpg:(g + 1) * hpg] = jnp.transpose(s_scr[g]).reshape(hpg, SSD_HEAD_DIM, SSD_STATE)

    hg = y[:lb] * _silu(z_ref[0])
    outs = []
    for g in range(SSD_GROUPS):
        part = hg[:, g * gw:(g + 1) * gw]
        ms = jnp.mean(part * part, axis=1, keepdims=True)
        outs.append(part * lax.rsqrt(ms + NORM_EPS))
    y_ref[0] = (jnp.concatenate(outs, axis=1) * nw_ref[...]).astype(BF16)


def _ssd(xbc, dt_raw, z, ssm0, conv0, prm, lb, n_valid):
    b, l, _ = xbc.shape
    nc = l // lb
    cw, cbias, dtb, alog, dsk, nw, ex = prm
    full = lambda a: pl.BlockSpec(a.shape, lambda bi, ci: (0,) * a.ndim)
    seq = lambda w: pl.BlockSpec((1, lb, w), lambda bi, ci: (bi, ci, 0))
    return pl.pallas_call(
        functools.partial(_ssd_kernel, n_valid=n_valid),
        grid=(b, nc),
        in_specs=[seq(CONV_DIM), seq(LANES), seq(SSD_WIDTH),
                  pl.BlockSpec((1, SSD_HEADS, SSD_HEAD_DIM, SSD_STATE), lambda bi, ci: (bi, 0, 0, 0)),
                  pl.BlockSpec((1, CONV_W - 1, CONV_DIM), lambda bi, ci: (bi, 0, 0)),
                  full(cw), full(cbias), full(dtb), full(alog), full(dsk), full(nw), full(ex)],
        out_specs=(seq(SSD_WIDTH),
                   pl.BlockSpec((1, SSD_HEADS, SSD_HEAD_DIM, SSD_STATE), lambda bi, ci: (bi, 0, 0, 0)),
                   pl.BlockSpec((1, CONV_W - 1, CONV_DIM), lambda bi, ci: (bi, 0, 0))),
        out_shape=(jax.ShapeDtypeStruct((b, l, SSD_WIDTH), BF16),
                   jax.ShapeDtypeStruct((b, SSD_HEADS, SSD_HEAD_DIM, SSD_STATE), F32),
                   jax.ShapeDtypeStruct((b, CONV_W - 1, CONV_DIM), F32)),
        scratch_shapes=[pltpu.VMEM((SSD_CHUNK + 2 * SUBLANES, CONV_DIM), F32),
                        pltpu.VMEM((SSD_CHUNK, LANES), F32),
                        pltpu.VMEM((SSD_GROUPS, SSD_STATE, SSD_WIDTH // SSD_GROUPS), F32)],
        compiler_params=_cparams(("parallel", "arbitrary")),
        name="conv_ssd",
    )(xbc, dt_raw, z, ssm0, conv0, cw, cbias, dtb, alog, dsk, nw, ex)


def _layer_norm(r, g, b):
    mu = jnp.mean(r, axis=1, keepdims=True)
    d = r - mu
    var = jnp.mean(d * d, axis=1, keepdims=True)
    return d * lax.rsqrt(var + NORM_EPS) * g + b


def _mix_kernel(att_ref, ssd_ref, x_ref, wo_ref, g_ref, b_ref, *rest):
    h_ref, hs_ref = rest[-2:]
    tm = x_ref.shape[0]
    mix = jnp.dot(att_ref[...], wo_ref[0:ATT_WIDTH, :], preferred_element_type=F32)
    mix = mix + jnp.dot(ssd_ref[...], wo_ref[ATT_WIDTH:MIX_WIDTH, :], preferred_element_type=F32)
    h = _layer_norm(DEEPNORM_ALPHA * x_ref[...] + mix, g_ref[...], b_ref[...])
    h_ref[...] = h
    for j in range(ROW_SLAB):
        hs_ref[pl.ds(j, tm, stride=ROW_SLAB), :] = h[:, j * LANES:(j + 1) * LANES]


def _mix(att, ssd, x2d, wo_bf, g, b, tm, t_total, row0=0, into=None):
    t = x2d.shape[0]
    b0 = row0 // tm
    row = lambda w: pl.BlockSpec((tm, w), lambda i: (i, 0))
    vec = pl.BlockSpec((1, D_MODEL), lambda i: (0, 0))
    hbm = pl.BlockSpec(memory_space=pl.ANY)
    extra = () if into is None else tuple(into)
    return pl.pallas_call(
        _mix_kernel,
        grid=(t // tm,),
        in_specs=[row(ATT_WIDTH), row(SSD_WIDTH), row(D_MODEL),
                  pl.BlockSpec((MIX_WIDTH, D_MODEL), lambda i: (0, 0)), vec, vec] + [hbm] * len(extra),
        out_specs=(pl.BlockSpec((tm, D_MODEL), lambda i: (b0 + i, 0)),
                   pl.BlockSpec((tm * ROW_SLAB, LANES), lambda i: (b0 + i, 0))),
        out_shape=(jax.ShapeDtypeStruct((t_total, D_MODEL), F32),
                   jax.ShapeDtypeStruct((t_total * ROW_SLAB, LANES), F32)),
        input_output_aliases={} if into is None else {6: 0, 7: 1},
        compiler_params=_cparams(("parallel",)),
        name="outproj_ln1",
    )(att, ssd, x2d, wo_bf, g, b, *extra)


def _router_kernel(h_ref, wrh_ref, wrl_ref, bias_ref, tope_ref, rank_ref, gt_ref, cnt_ref, gscr):
    tm = h_ref.shape[0]
    i = pl.program_id(0)
    per_group = N_EXPERTS // N_EXPERT_GROUPS

    @pl.when(i == 0)
    def _():
        cnt_ref[...] = jnp.zeros_like(cnt_ref)

    h = h_ref[...]
    h_hi = h.astype(BF16)
    h_lo = (h - h_hi.astype(F32)).astype(BF16)
    nt = lambda a, b: lax.dot_general(a, b, (((1,), (1,)), ((), ())), preferred_element_type=F32)
    logits = nt(wrh_ref[...], h_hi) + nt(wrh_ref[...], h_lo) + nt(wrl_ref[...], h_hi)
    s = 1.0 / (1.0 + jnp.exp(-logits))
    sel = s + bias_ref[...]
    neg_inf = -jnp.inf
    e_iota = lax.broadcasted_iota(I32, (N_EXPERTS, tm), 0).astype(F32)
    g_iota = lax.broadcasted_iota(I32, (per_group, tm), 0).astype(F32)

    gscore = []
    for g in range(N_EXPERT_GROUPS):
        blk = sel[g * per_group:(g + 1) * per_group]
        m1 = jnp.max(blk, axis=0, keepdims=True)
        a1 = jnp.min(jnp.where(blk == m1, g_iota, float(per_group)), axis=0, keepdims=True)
        m2 = jnp.max(jnp.where(g_iota == a1, neg_inf, blk), axis=0, keepdims=True)
        gscore.append(m1 + m2)
    blocks = []
    for g in range(N_EXPERT_GROUPS):
        beaten = jnp.zeros((1, tm), F32)
        for o in range(N_EXPERT_GROUPS):
            if o < g:
                beaten = beaten + jnp.where(gscore[o] >= gscore[g], 1.0, 0.0)
            elif o > g:
                beaten = beaten + jnp.where(gscore[o] > gscore[g], 1.0, 0.0)
        blocks.append(jnp.where(beaten < float(TOPK_GROUPS), sel[g * per_group:(g + 1) * per_group], neg_inf))
    cand = jnp.concatenate(blocks, axis=0)

    tops, gsel = [], []
    onehot = jnp.zeros((N_EXPERTS, tm), F32)
    for _ in range(TOP_K):
        mx = jnp.max(cand, axis=0, keepdims=True)
        ix = jnp.min(jnp.where(cand == mx, e_iota, float(N_EXPERTS)), axis=0, keepdims=True)
        hit = e_iota == ix
        tops.append(ix)
        gsel.append(jnp.sum(jnp.where(hit, s, 0.0), axis=0, keepdims=True))
        onehot = jnp.where(hit, 1.0, onehot)
        cand = jnp.where(hit, neg_inf, cand)
    den = gsel[0]
    for k in range(1, TOP_K):
        den = den + gsel[k]

    t_r = lax.broadcasted_iota(I32, (tm, tm), 0)
    t_c = lax.broadcasted_iota(I32, (tm, tm), 1)
    before = (t_r < t_c).astype(BF16)
    oh_b = onehot.astype(BF16)
    base = cnt_ref[...] + jnp.dot(oh_b, before, preferred_element_type=F32)
    cnt_ref[...] = cnt_ref[...] + jnp.dot(oh_b, jnp.ones((tm, LANES), BF16), preferred_element_type=F32)

    gscr[...] = jnp.zeros_like(gscr)
    for k in range(TOP_K):
        tope_ref[k:k + 1, :] = tops[k].astype(I32)
        rank_ref[k:k + 1, :] = jnp.sum(jnp.where(e_iota == tops[k], base, 0.0), axis=0, keepdims=True).astype(I32)
        gscr[k:k + 1, :] = gsel[k] / den * ROUTED_SCALE
    gt_ref[...] = jnp.transpose(gscr[...])


def _router(h_all, wr_t, bias_b):
    t = h_all.shape[0]
    tm = ROUTER_TILE
    wr_hi = wr_t.astype(BF16)
    wr_lo = (wr_t - wr_hi.astype(F32)).astype(BF16)
    return pl.pallas_call(
        _router_kernel,
        grid=(t // tm,),
        in_specs=[pl.BlockSpec((tm, D_MODEL), lambda i: (i, 0)),
                  pl.BlockSpec((N_EXPERTS, D_MODEL), lambda i: (0, 0)),
                  pl.BlockSpec((N_EXPERTS, D_MODEL), lambda i: (0, 0)),
                  pl.BlockSpec((N_EXPERTS, tm), lambda i: (0, 0))],
        out_specs=(pl.BlockSpec((TOP_K, tm), lambda i: (0, i)),
                   pl.BlockSpec((TOP_K, tm), lambda i: (0, i)),
                   pl.BlockSpec((tm, LANES), lambda i: (i, 0)),
                   pl.BlockSpec((N_EXPERTS, LANES), lambda i: (0, 0))),
        out_shape=(jax.ShapeDtypeStruct((TOP_K, t), I32),
                   jax.ShapeDtypeStruct((TOP_K, t), I32),
                   jax.ShapeDtypeStruct((t, LANES), F32),
                   jax.ShapeDtypeStruct((N_EXPERTS, LANES), F32)),
        scratch_shapes=[pltpu.VMEM((tm, LANES), F32)],
        compiler_params=_cparams(("arbitrary",)),
        name="router",
    )(h_all, wr_hi, wr_lo, bias_b)


_PAD_SIZES = tuple(2 ** p for p in range(int(math.log2(EXPERT_BLOCK)) - 1, -1, -1))


def _slab(ref, row, n=1):
    return ref.at[pl.ds(pl.multiple_of(row * ROW_SLAB, ROW_SLAB), n * ROW_SLAB)]


def _padfill_kernel(pstart_ref, pcnt_ref, tail_ref, xs_ref, zero_scr, sem):
    zero_scr[...] = jnp.zeros_like(zero_scr)
    half = EXPERT_BLOCK // 2
    tail_copy = lambda j: pltpu.make_async_copy(_slab(zero_scr, 0, half), _slab(xs_ref, tail_ref[0] + j * half, half),
                                                sem)

    def tail_start(j, c):
        tail_copy(j).start()
        return c

    def tail_wait(j, c):
        tail_copy(j).wait()
        return c

    lax.fori_loop(0, tail_ref[1], tail_start, 0)

    def copies(e):
        base = pstart_ref[e]
        cnt = pcnt_ref[e]
        out = []
        for sz in _PAD_SIZES:
            out.append(((cnt & sz) != 0, pltpu.make_async_copy(_slab(zero_scr, 0, sz), _slab(xs_ref, base, sz), sem)))
            base = base + (cnt & sz)
        return out

    def start(e, c):
        for pred, cp in copies(e):
            @pl.when(pred)
            def _():
                cp.start()
        return c

    def wait(e, c):
        for pred, cp in copies(e):
            @pl.when(pred)
            def _():
                cp.wait()
        return c

    lax.fori_loop(0, N_EXPERTS, start, 0)
    lax.fori_loop(0, N_EXPERTS, wait, 0)
    lax.fori_loop(0, tail_ref[1], tail_wait, 0)


def _padfill(pad_start, pad_cnt, tail, n_rows):
    return pl.pallas_call(
        _padfill_kernel,
        grid_spec=pltpu.PrefetchScalarGridSpec(
            num_scalar_prefetch=3, grid=(1,), in_specs=[],
            out_specs=pl.BlockSpec(memory_space=pl.ANY),
            scratch_shapes=[pltpu.VMEM((EXPERT_BLOCK // 2 * ROW_SLAB, LANES), F32), pltpu.SemaphoreType.DMA]),
        out_shape=jax.ShapeDtypeStruct((n_rows * ROW_SLAB, LANES), F32),
        compiler_params=_cparams(("arbitrary",)),
        name="dispatch_padfill",
    )(pad_start, pad_cnt, tail)


def _dest_kernel(tope_ref, rank_ref, pstart_ref, dest_ref):
    tm = tope_ref.shape[1]
    e_iota = lax.broadcasted_iota(I32, (N_EXPERTS, tm), 0)
    ps = pstart_ref[...]
    for k in range(TOP_K):
        base = jnp.sum(jnp.where(e_iota == tope_ref[k:k + 1, :], ps, 0.0), axis=0, keepdims=True)
        dest_ref[k:k + 1, :] = base.astype(I32) + rank_ref[k:k + 1, :]


def _dest(top_e, rank, pstarts):
    t = top_e.shape[1]
    tm = DISPATCH_TILE
    blk = pl.BlockSpec((TOP_K, tm), lambda i: (0, i))
    return pl.pallas_call(
        _dest_kernel,
        grid=(t // tm,),
        in_specs=[blk, blk, pl.BlockSpec((N_EXPERTS, tm), lambda i: (0, 0))],
        out_specs=blk,
        out_shape=jax.ShapeDtypeStruct((TOP_K, t), I32),
        compiler_params=_cparams(("parallel",)),
        name="dispatch_dest",
    )(top_e, rank, jnp.broadcast_to(pstarts.astype(F32)[:, None], (N_EXPERTS, tm)))


def _dest_tiles(dest, tm):
    t = dest.shape[1]
    return jnp.transpose(dest.reshape(TOP_K, t // tm, tm), (1, 0, 2)).reshape(t // tm, TOP_K * tm)


def _dispatch_kernel(dest_ref, h_ref, xs_in_ref, xs_ref, idx_smem, isem, sem):
    del xs_in_ref
    i = pl.program_id(0)
    tm = DISPATCH_TILE
    cp = pltpu.make_async_copy(dest_ref.at[i], idx_smem, isem)
    cp.start()
    cp.wait()

    def row_copy(t, k):
        return pltpu.make_async_copy(_slab(h_ref, t), _slab(xs_ref, idx_smem[k * tm + t]), sem)

    def start(t, c):
        for k in range(TOP_K):
            row_copy(t, k).start(priority=k % 2)
        return c

    def wait(t, c):
        for k in range(TOP_K):
            row_copy(t, k).wait()
        return c

    lax.fori_loop(0, tm, start, 0)
    lax.fori_loop(0, tm, wait, 0)


def _dispatch(dest_tiles, h_slab, xs):
    tm = DISPATCH_TILE
    return pl.pallas_call(
        _dispatch_kernel,
        grid=(dest_tiles.shape[0],),
        in_specs=[pl.BlockSpec(dest_tiles.shape, lambda i: (0, 0)),
                  pl.BlockSpec((tm * ROW_SLAB, LANES), lambda i: (i, 0)),
                  pl.BlockSpec(memory_space=pl.ANY)],
        out_specs=pl.BlockSpec(memory_space=pl.ANY),
        out_shape=jax.ShapeDtypeStruct(xs.shape, xs.dtype),
        scratch_shapes=[pltpu.SMEM((tm * TOP_K,), I32), pltpu.SemaphoreType.DMA, pltpu.SemaphoreType.DMA],
        input_output_aliases={2: 0},
        compiler_params=_cparams(("arbitrary",)),
        name="dispatch_rows",
    )(dest_tiles, h_slab, xs)


def _expert_kernel(bi_ref, be_ref, nu_ref, x_ref, wg_ref, wu_ref, wd_ref, y_ref, wg_b, wu_b, wd_b):
    del bi_ref
    i = pl.program_id(0)

    @pl.when(i < nu_ref[0])
    def _():
        prev = be_ref[jnp.maximum(i - 1, 0)]

        @pl.when(jnp.logical_or(i == 0, be_ref[i] != prev))
        def _():
            wg_b[...] = wg_ref[0].astype(BF16)
            wu_b[...] = wu_ref[0].astype(BF16)
            wd_b[...] = wd_ref[0].astype(BF16)

        xb = jnp.concatenate([x_ref[pl.ds(c, EXPERT_BLOCK, stride=ROW_SLAB), :] for c in range(ROW_SLAB)],
                             axis=1).astype(BF16)
        g = jnp.dot(xb, wg_b[...], preferred_element_type=F32)
        u = jnp.dot(xb, wu_b[...], preferred_element_type=F32)
        a = (_silu(g) * u).astype(BF16)
        y = jnp.dot(a, wd_b[...], preferred_element_type=F32)
        for c in range(ROW_SLAB):
            y_ref[pl.ds(c, EXPERT_BLOCK, stride=ROW_SLAB), :] = y[:, c * LANES:(c + 1) * LANES]

    @pl.when(i >= nu_ref[0])
    def _():
        y_ref[...] = jnp.zeros_like(y_ref)


def _experts(blk_idx, blk_exp, n_used, xs, wg, wu, wd):
    n_rows = xs.shape[0] // ROW_SLAB
    nb = n_rows // EXPERT_BLOCK
    rows = pl.BlockSpec((EXPERT_BLOCK * ROW_SLAB, LANES), lambda i, bi, be, nu: (bi[i], 0))
    out_rows = pl.BlockSpec((EXPERT_BLOCK * ROW_SLAB, LANES), lambda i, bi, be, nu: (i, 0))
    return pl.pallas_call(
        _expert_kernel,
        grid_spec=pltpu.PrefetchScalarGridSpec(
            num_scalar_prefetch=3, grid=(nb,),
            in_specs=[rows,
                      pl.BlockSpec((1, D_MODEL, EXPERT_FF), lambda i, bi, be, nu: (be[i], 0, 0)),
                      pl.BlockSpec((1, D_MODEL, EXPERT_FF), lambda i, bi, be, nu: (be[i], 0, 0)),
                      pl.BlockSpec((1, EXPERT_FF, D_MODEL), lambda i, bi, be, nu: (be[i], 0, 0))],
            out_specs=out_rows,
            scratch_shapes=[pltpu.VMEM((D_MODEL, EXPERT_FF), BF16), pltpu.VMEM((D_MODEL, EXPERT_FF), BF16),
                            pltpu.VMEM((EXPERT_FF, D_MODEL), BF16)]),
        out_shape=jax.ShapeDtypeStruct(xs.shape, F32),
        compiler_params=_cparams(("arbitrary",)),
        name="expert_ffn",
    )(blk_idx, blk_exp, n_used, xs, wg, wu, wd)


def _combine_kernel(dest_ref, h_ref, gt_ref, ys_ref, sg_ref, su_ref, sd_ref, g_ref, b_ref, o_ref, o2_ref,
                    idx0, idx1, gbuf, isem, sem, *, n_first):
    idx_smem = (idx0, idx1)
    i = pl.program_id(0)
    n = pl.num_programs(0)
    tm = h_ref.shape[0]

    def idx_copy(step, s):
        return pltpu.make_async_copy(dest_ref.at[step], idx_smem[s], isem.at[s])

    def row_copy(s, t, k, src_row):
        return pltpu.make_async_copy(_slab(ys_ref, src_row), _slab(gbuf.at[s, k], t), sem.at[s])

    def issue(s):
        def body(t, c):
            for k in range(TOP_K):
                row_copy(s, t, k, idx_smem[s][k * tm + t]).start(priority=k % 2)
            return c
        lax.fori_loop(0, tm, body, 0)

    def drain(s):
        def body(t, c):
            for k in range(TOP_K):
                row_copy(s, t, k, 0).wait()
            return c
        lax.fori_loop(0, tm, body, 0)

    @pl.when(i == 0)
    def _():
        first = idx_copy(0, 0)
        first.start()
        first.wait()
        issue(0)

        @pl.when(n > 1)
        def _():
            idx_copy(1, 1).start()

    def step(slot):
        nslot = 1 - slot

        @pl.when(i + 1 < n)
        def _():
            idx_copy(i + 1, nslot).wait()
            issue(nslot)

            @pl.when(i + 2 < n)
            def _():
                idx_copy(i + 2, slot).start()

        h = h_ref[...]
        hb = h.astype(BF16)
        g = jnp.dot(hb, sg_ref[...], preferred_element_type=F32)
        u = jnp.dot(hb, su_ref[...], preferred_element_type=F32)
        f = jnp.dot((_silu(g) * u).astype(BF16), sd_ref[...], preferred_element_type=F32)
        drain(slot)
        gt = gt_ref[...]
        cols = []
        for j in range(ROW_SLAB):
            fj = f[:, j * LANES:(j + 1) * LANES]
            for k in range(TOP_K):
                fj = fj + gbuf[slot, k, pl.ds(j, tm, stride=ROW_SLAB), :] * gt[:, k:k + 1]
            cols.append(fj)
        f = jnp.concatenate(cols, axis=1)
        out = _layer_norm(DEEPNORM_ALPHA * h + f, g_ref[...], b_ref[...])

        @pl.when(i < n_first)
        def _():
            o_ref[...] = out

        @pl.when(i == n_first)
        def _():
            o2_ref[...] = out

    for parity in (0, 1):
        pl.when((i & 1) == parity)(functools.partial(step, parity))


def _combine(dest_tiles, h_all, gates_t, ys, sg_bf, su_bf, sd_bf, g, b, n_first):
    t = h_all.shape[0]
    tm = COMBINE_TILE
    vec = pl.BlockSpec((1, D_MODEL), lambda i: (0, 0))
    return pl.pallas_call(
        functools.partial(_combine_kernel, n_first=n_first),
        grid=(t // tm,),
        in_specs=[pl.BlockSpec(dest_tiles.shape, lambda i: (0, 0)),
                  pl.BlockSpec((tm, D_MODEL), lambda i: (i, 0)),
                  pl.BlockSpec((tm, LANES), lambda i: (i, 0)),
                  pl.BlockSpec(memory_space=pl.ANY),
                  pl.BlockSpec((D_MODEL, SHARED_FF), lambda i: (0, 0)),
                  pl.BlockSpec((D_MODEL, SHARED_FF), lambda i: (0, 0)),
                  pl.BlockSpec((SHARED_FF, D_MODEL), lambda i: (0, 0)), vec, vec],
        out_specs=(pl.BlockSpec((tm, D_MODEL), lambda i: (jnp.minimum(i, n_first - 1), 0)),
                   pl.BlockSpec((tm, D_MODEL), lambda i: (0, 0))),
        out_shape=(jax.ShapeDtypeStruct((n_first * tm, D_MODEL), F32), jax.ShapeDtypeStruct((tm, D_MODEL), F32)),
        scratch_shapes=[pltpu.SMEM((tm * TOP_K,), I32), pltpu.SMEM((tm * TOP_K,), I32),
                        pltpu.VMEM((2, TOP_K, tm * ROW_SLAB, LANES), F32),
                        pltpu.SemaphoreType.DMA((2,)), pltpu.SemaphoreType.DMA((2,))],
        compiler_params=_cparams(("arbitrary",)),
        name="combine_ln2",
    )(dest_tiles, h_all, gates_t, ys, sg_bf, su_bf, sd_bf, g, b)


def _moe(h_all, h_slab, n_first, wr, rbias, wg, wu, wd, sg, su, sd, g2, b2):
    t = h_all.shape[0]
    top_e, rank, gates_t, cnt = _router(h_all, jnp.transpose(wr),
                                        jnp.broadcast_to(rbias[:, None], (N_EXPERTS, ROUTER_TILE)))
    counts = cnt[:, 0].astype(I32)
    pcounts = (counts + EXPERT_BLOCK - 1) // EXPERT_BLOCK * EXPERT_BLOCK
    pends = jnp.cumsum(pcounts)
    pstarts = pends - pcounts
    n_rows = (t * TOP_K // EXPERT_BLOCK + N_EXPERTS - 1) * EXPERT_BLOCK
    nb = n_rows // EXPERT_BLOCK
    n_used = (pends[-1] // EXPERT_BLOCK).astype(I32)
    blk_idx = jnp.minimum(jnp.arange(nb, dtype=I32), n_used - 1)
    blk_exp = jnp.minimum(jnp.sum((pends[None, :] <= (blk_idx * EXPERT_BLOCK)[:, None]).astype(I32), axis=1),
                          N_EXPERTS - 1)
    dest = _dest(top_e, rank, pstarts)
    tail = jnp.stack([pends[-1], (n_rows - pends[-1]) // (EXPERT_BLOCK // 2)]).astype(I32)
    xs = _padfill((pstarts + counts).astype(I32), (pcounts - counts).astype(I32), tail, n_rows)
    xs = _dispatch(_dest_tiles(dest, DISPATCH_TILE), h_slab, xs)
    ys = _experts(blk_idx, blk_exp, n_used.reshape(1), xs, wg, wu, wd)
    return _combine(_dest_tiles(dest, COMBINE_TILE), h_all, gates_t, ys,
                    sg.astype(BF16), su.astype(BF16), sd.astype(BF16), g2[None, :], b2[None, :], n_first)


def _expand_matrix():
    h = np.arange(LANES)[:, None]
    c = np.arange(SSD_WIDTH)[None, :]
    return jnp.asarray((c // SSD_HEAD_DIM == h).astype(np.float32), dtype=BF16)


def kernel(x_prompt, x_sample, cache_k, cache_v, state_ssm, state_conv, w_in, conv_w, conv_b, dt_bias, a_log, d_skip, ssd_norm_w, w_out, ln1_g, ln1_b, w_router, router_bias, w_exp_gate, w_exp_up, w_exp_down, w_sh_gate, w_sh_up, w_sh_down, ln2_g, ln2_b):
    bp, lp, _ = x_prompt.shape
    bs, ls, _ = x_sample.shape
    win = cache_k.shape[2]
    keep = min(MAX_WINDOW, lp)
    assert lp % SSD_CHUNK == 0 and ls <= SUBLANES and win % KEY_TILE == 0 and win >= MAX_WINDOW

    w_bf = jnp.pad(w_in[0], ((0, 0), (0, IN_COLS_PAD - IN_COLS))).astype(BF16)
    wo_bf = w_out[0].astype(BF16)
    pad_l = lambda v: jnp.pad(v, (0, LANES - v.shape[0]))[None, :]
    ssd_prm = (conv_w[0], conv_b[0][None, :], pad_l(dt_bias[0]), pad_l(a_log[0]),
               jnp.repeat(d_skip[0], SSD_HEAD_DIM)[None, :], ssd_norm_w[0][None, :], _expand_matrix())

    tp = bp * lp
    tm_p = 256 if lp % 256 == 0 else SSD_CHUNK
    tm_mix = 512 if tp % 512 == 0 else SSD_CHUNK
    cos_p, sin_p = _rope_tables(jnp.arange(lp, dtype=F32))
    xp2 = x_prompt.reshape(tp, D_MODEL)
    q, kf, vf, z, xbc, dtr = _inproj(xp2, w_bf, cos_p, sin_p, tm_p, lp // tm_p)
    att = _attention_window(q.reshape(bp, lp, ATT_WIDTH), kf.reshape(bp, lp, KV_WIDTH), vf.reshape(bp, lp, KV_WIDTH))
    ssd_y, ssm_p, conv_p = _ssd(xbc.reshape(bp, lp, CONV_DIM), dtr.reshape(bp, lp, LANES),
                                z.reshape(bp, lp, SSD_WIDTH),
                                jnp.zeros((bp, SSD_HEADS, SSD_HEAD_DIM, SSD_STATE), F32),
                                jnp.zeros((bp, CONV_W - 1, CONV_DIM), F32), ssd_prm, SSD_CHUNK, SSD_CHUNK)
    ts = bs * ls
    t_pad = -(-(tp + ts) // DISPATCH_TILE) * DISPATCH_TILE
    tail = t_pad - tp
    assert tp % tail == 0 and ts <= COMBINE_TILE and tp % COMBINE_TILE == 0
    h_bufs = _mix(att.reshape(tp, ATT_WIDTH), ssd_y.reshape(tp, SSD_WIDTH), xp2, wo_bf, ln1_g, ln1_b, tm_mix, t_pad)

    pos_s = (PAST_LEN + jnp.arange(ls, dtype=jnp.int32)).astype(F32)
    cos_s, sin_s = _rope_tables(jnp.tile(pos_s, bs))
    xs2 = x_sample.reshape(ts, D_MODEL)
    q_s, kf_s, vf_s, z_s, xbc_s, dtr_s = _inproj(xs2, w_bf, cos_s, sin_s, ts, 1)
    rows8 = lambda a, w: jnp.pad(a.reshape(bs, ls, w), ((0, 0), (0, SUBLANES - ls), (0, 0)))
    att_s = _attention_decode(rows8(q_s, ATT_WIDTH), cache_k[0].reshape(bs, win, KV_WIDTH).astype(BF16),
                              cache_v[0].reshape(bs, win, KV_WIDTH).astype(BF16), rows8(kf_s, KV_WIDTH),
                              rows8(vf_s, KV_WIDTH))[:, :ls]
    ssd_s, ssm_s, conv_s = _ssd(rows8(xbc_s, CONV_DIM), rows8(dtr_s, LANES), rows8(z_s, SSD_WIDTH),
                                state_ssm[0], state_conv[0], ssd_prm, SUBLANES, ls)
    tail_rows = lambda a: jnp.pad(a, ((0, tail - ts), (0, 0)))
    h_all, h_slab = _mix(tail_rows(att_s.reshape(ts, ATT_WIDTH)), tail_rows(ssd_s[:, :ls].reshape(ts, SSD_WIDTH)),
                         tail_rows(xs2), wo_bf, ln1_g, ln1_b, tail, t_pad, row0=tp, into=h_bufs)

    y_p, y_s = _moe(h_all, h_slab, tp // COMBINE_TILE, w_router[0], router_bias[0], w_exp_gate[0], w_exp_up[0],
                    w_exp_down[0], w_sh_gate[0], w_sh_up[0], w_sh_down[0], ln2_g[0], ln2_b[0])

    kv5 = lambda a, b, l: a.reshape(1, b, l, N_KV_HEADS, HEAD_DIM)
    return (y_p.reshape(bp, lp, D_MODEL), y_s[:ts].reshape(bs, ls, D_MODEL),
            kv5(kf.reshape(bp, lp, KV_WIDTH)[:, lp - keep:], bp, keep),
            kv5(vf.reshape(bp, lp, KV_WIDTH)[:, lp - keep:], bp, keep),
            ssm_p[None], conv_p[None],
            kv5(kf_s, bs, ls), kv5(vf_s, bs, ls), ssm_s[None], conv_s[None])
```

```python
import functools
import math

import jax
import jax.numpy as jnp
import numpy as np
from jax import lax
from jax.experimental import pallas as pl
from jax.experimental.pallas import tpu as pltpu

F32 = jnp.float32
BF16 = jnp.bfloat16
I32 = jnp.int32

D_MODEL = 1024
PAST_LEN = 16384
HEAD_DIM = 64
N_ATT_HEADS = 16
N_KV_HEADS = 8
ATT_WIDTH = N_ATT_HEADS * HEAD_DIM
KV_WIDTH = N_KV_HEADS * HEAD_DIM
DILATED_BRANCHES = ((128, 1), (512, 4), (2048, 16))
MAX_WINDOW = 2048
ROPE_THETA = 10000.0
SSD_HEADS = 16
SSD_HEAD_DIM = 64
SSD_WIDTH = SSD_HEADS * SSD_HEAD_DIM
SSD_GROUPS = 2
SSD_STATE = 128
SSD_CHUNK = 128
CONV_W = 4
CONV_DIM = SSD_WIDTH + 2 * SSD_GROUPS * SSD_STATE
MIX_WIDTH = ATT_WIDTH + SSD_WIDTH
IN_COLS = ATT_WIDTH + 2 * KV_WIDTH + SSD_WIDTH + CONV_DIM + SSD_HEADS
N_EXPERTS = 256
TOP_K = 8
N_EXPERT_GROUPS = 8
TOPK_GROUPS = 4
EXPERT_FF = 256
SHARED_FF = 256
ROUTED_SCALE = 2.5
DEPTH = 1
DEEPNORM_ALPHA = (2.0 * DEPTH) ** 0.25
NORM_EPS = 1e-5

LANES = 128
SUBLANES = 8
VMEM_LIMIT = 56 * 1024 * 1024

IN_COLS_PAD = ATT_WIDTH + 2 * KV_WIDTH + SSD_WIDTH + CONV_DIM + LANES
KEY_TILE = 128
EXPERT_BLOCK = 512
ROUTER_TILE = 128
DISPATCH_TILE = 256
COMBINE_TILE = 128
ROW_SLAB = D_MODEL // LANES
NEG_BIG = -1e30
HIGHEST = lax.Precision.HIGHEST


def _cparams(sem):
    return pltpu.CompilerParams(dimension_semantics=sem, vmem_limit_bytes=VMEM_LIMIT)


def _silu(x):
    return x * (1.0 / (1.0 + jnp.exp(-x)))


def _inproj_kernel(x_ref, w_ref, cq_ref, sq_ref, ck_ref, sk_ref,
                   q_ref, kf_ref, vf_ref, z_ref, xbc_ref, dt_ref):
    tm = x_ref.shape[0]
    xb = x_ref[...].astype(BF16)
    lane = lax.broadcasted_iota(I32, (tm, LANES), 1)
    first_half = (lane % HEAD_DIM) < (HEAD_DIM // 2)

    def rope(a, c, s):
        partner = jnp.where(first_half, pltpu.roll(a, LANES - HEAD_DIM // 2, 1), pltpu.roll(a, HEAD_DIM // 2, 1))
        return a * c + partner * s

    c0 = 0
    acc = jnp.dot(xb, w_ref[:, c0:c0 + ATT_WIDTH], preferred_element_type=F32)
    cq, sq = cq_ref[...], sq_ref[...]
    for j in range(ATT_WIDTH // LANES):
        q_ref[:, j * LANES:(j + 1) * LANES] = rope(acc[:, j * LANES:(j + 1) * LANES], cq, sq)
    c0 += ATT_WIDTH
    acc = jnp.dot(xb, w_ref[:, c0:c0 + KV_WIDTH], preferred_element_type=F32)
    ck, sk = ck_ref[...], sk_ref[...]
    for j in range(KV_WIDTH // LANES):
        kf_ref[:, j * LANES:(j + 1) * LANES] = rope(acc[:, j * LANES:(j + 1) * LANES], ck, sk)
    c0 += KV_WIDTH
    vf_ref[...] = jnp.dot(xb, w_ref[:, c0:c0 + KV_WIDTH], preferred_element_type=F32)
    c0 += KV_WIDTH
    z_ref[...] = jnp.dot(xb, w_ref[:, c0:c0 + SSD_WIDTH], preferred_element_type=F32)
    c0 += SSD_WIDTH
    xbc_ref[...] = jnp.dot(xb, w_ref[:, c0:c0 + CONV_DIM], preferred_element_type=F32)
    c0 += CONV_DIM
    dt_ref[...] = jnp.dot(xb, w_ref[:, c0:c0 + LANES], preferred_element_type=F32)


def _rope_tables(pos):
    half = HEAD_DIM // 2
    inv = ROPE_THETA ** (-jnp.arange(half, dtype=F32) / half)
    ang = pos[:, None] * inv[None, :]
    cos, sin = jnp.cos(ang), jnp.sin(ang)
    c = jnp.concatenate([cos, cos, cos, cos], axis=1)
    s = jnp.concatenate([-sin, sin, -sin, sin], axis=1)
    return c, s


def _inproj(x2d, w_bf, cos_t, sin_t, tm, tiles_per_seq):
    t = x2d.shape[0]
    scale = HEAD_DIM ** -0.5
    tab = pl.BlockSpec((tm, LANES), lambda i: (i % tiles_per_seq, 0))
    row = lambda w: pl.BlockSpec((tm, w), lambda i: (i, 0))
    outs = (
        jax.ShapeDtypeStruct((t, ATT_WIDTH), F32),
        jax.ShapeDtypeStruct((t, KV_WIDTH), F32),
        jax.ShapeDtypeStruct((t, KV_WIDTH), F32),
        jax.ShapeDtypeStruct((t, SSD_WIDTH), F32),
        jax.ShapeDtypeStruct((t, CONV_DIM), F32),
        jax.ShapeDtypeStruct((t, LANES), F32),
    )
    return pl.pallas_call(
        _inproj_kernel,
        grid=(t // tm,),
        in_specs=[row(D_MODEL), pl.BlockSpec((D_MODEL, IN_COLS_PAD), lambda i: (0, 0)), tab, tab, tab, tab],
        out_specs=(row(ATT_WIDTH), row(KV_WIDTH), row(KV_WIDTH), row(SSD_WIDTH), row(CONV_DIM), row(LANES)),
        out_shape=outs,
        compiler_params=_cparams(("parallel",)),
        name="inproj",
    )(x2d, w_bf, cos_t * scale, sin_t * scale, cos_t, sin_t)


def _branch_weight(d):
    w = np.zeros(d.shape, np.float32)
    for window, dil in DILATED_BRANCHES:
        w += ((d >= 0) & (d <= window) & (d % dil == 0)).astype(np.float32)
    return w


def _attn_decode_kernel(q_ref, kc_ref, vc_ref, kn_ref, vn_ref, wc_ref, wn_ref, o_ref, ktail, vtail):
    nq = q_ref.shape[1]
    lane = lax.broadcasted_iota(I32, (nq, LANES), 1)
    lo = lane < HEAD_DIM
    ktail[...] = jnp.zeros_like(ktail)
    vtail[...] = jnp.zeros_like(vtail)
    ktail[0:nq, :] = kn_ref[0]
    vtail[0:nq, :] = vn_ref[0]
    wc, wn = wc_ref[...], wn_ref[...]
    nt = (((1,), (1,)), ((), ()))
    for g in range(N_KV_HEADS):
        pair, half = divmod(g, 2)
        cols = slice(pair * LANES, (pair + 1) * LANES)
        q = q_ref[0, :, g * LANES:(g + 1) * LANES]
        qs = pltpu.roll(q, HEAD_DIM, 1)
        valid = lo if half == 0 else jnp.logical_not(lo)
        qa_src, qb_src = (q, qs) if half == 0 else (qs, q)
        qq = jnp.concatenate([jnp.where(valid, qa_src, 0.0), jnp.where(valid, qb_src, 0.0)], axis=0).astype(BF16)
        s_c = lax.dot_general(qq, kc_ref[0, :, cols].astype(BF16), nt, preferred_element_type=F32)
        s_n = lax.dot_general(qq, ktail[:, cols].astype(BF16), nt, preferred_element_type=F32)
        s_c = jnp.where(wc > 0.0, s_c, NEG_BIG)
        s_n = jnp.where(wn > 0.0, s_n, NEG_BIG)
        m = jnp.maximum(jnp.max(s_c, axis=1, keepdims=True), jnp.max(s_n, axis=1, keepdims=True))
        p_c = jnp.exp(s_c - m) * wc
        p_n = jnp.exp(s_n - m) * wn
        l = jnp.sum(p_c, axis=1, keepdims=True) + jnp.sum(p_n, axis=1, keepdims=True)
        acc = jnp.dot(p_c.astype(BF16), vc_ref[0, :, cols].astype(BF16), preferred_element_type=F32)
        acc = acc + jnp.dot(p_n.astype(BF16), vtail[:, cols].astype(BF16), preferred_element_type=F32)
        o = acc / l
        oa, ob = o[:nq], o[nq:]
        if half == 0:
            out = jnp.where(lo, oa, pltpu.roll(ob, HEAD_DIM, 1))
        else:
            out = jnp.where(lo, pltpu.roll(oa, HEAD_DIM, 1), ob)
        o_ref[0, :, g * LANES:(g + 1) * LANES] = out.astype(BF16)


def _attention_decode(q, k_cache, v_cache, k_new, v_new):
    b, nq, _ = q.shape
    win = k_cache.shape[1]
    i = np.arange(nq)[:, None]
    wc = _branch_weight(win + i - np.arange(win)[None, :])
    wn = _branch_weight(i - np.arange(KEY_TILE)[None, :])
    wn[:, nq:] = 0.0
    stack = lambda w: jnp.asarray(np.concatenate([w, w], axis=0))
    seq = lambda rows, w: pl.BlockSpec((1, rows, w), lambda bi: (bi, 0, 0))
    full = lambda shp: pl.BlockSpec(shp, lambda bi: (0, 0))
    return pl.pallas_call(
        _attn_decode_kernel,
        grid=(b,),
        in_specs=[seq(nq, ATT_WIDTH), seq(win, KV_WIDTH), seq(win, KV_WIDTH), seq(nq, KV_WIDTH), seq(nq, KV_WIDTH),
                  full((2 * nq, win)), full((2 * nq, KEY_TILE))],
        out_specs=seq(nq, ATT_WIDTH),
        out_shape=jax.ShapeDtypeStruct((b, nq, ATT_WIDTH), BF16),
        scratch_shapes=[pltpu.VMEM((KEY_TILE, KV_WIDTH), F32), pltpu.VMEM((KEY_TILE, KV_WIDTH), F32)],
        compiler_params=_cparams(("arbitrary",)),
        name="decode_attn",
    )(q, k_cache, v_cache, k_new, v_new, stack(wc), stack(wn))


ATT_SUPER = 2048
ATT_UNIT = 128
ATT_SPAN = max(w // d for w, d in DILATED_BRANCHES)


ATT_R4 = 4


MASK_FULL, MASK_CLAMPED, MASK_BEFORE_START = 0, 1, 2


def _window_masks():
    i = (np.arange(2 * ATT_UNIT) % ATT_UNIT)[:, None]
    c = np.arange(ATT_UNIT + ATT_SPAN)[None, :]
    ok = lambda d: (d >= 0) & (d <= ATT_SPAN)
    keep = np.stack([ok(ATT_SPAN + i - c), ok(i - c), ok(ATT_SPAN + i - c) & (c >= ATT_SPAN)])
    return jnp.asarray(np.where(keep, 0.0, NEG_BIG).astype(np.float32))


def _attn_window_kernel(q0_ref, q1_ref, k_ref, v_ref, mask_ref, o_ref,
                        k4_s, v4_s, q4_s, acc_s, m_s, l_s, tmp_s, nat_s):
    sb = q0_ref.shape[1]
    sbi = pl.program_id(2)
    u_rows, span, r4 = ATT_UNIT, ATT_SPAN, ATT_R4
    nkeys = u_rows + span
    cls = sb // r4
    look = MAX_WINDOW // r4
    units = sb // u_rows
    assert [d for _, d in DILATED_BRANCHES] == [1, r4, r4 * r4] and sb // (r4 * r4) == u_rows
    lane = lax.broadcasted_iota(I32, (u_rows, LANES), 1)
    lo = lane < HEAD_DIM
    piece = 2 * u_rows

    def regroup(src_ref, dst, nat0, loc0, n):
        for c in range(r4):
            for j0 in range(0, n, piece):
                dst[c, loc0 + j0:loc0 + j0 + piece, :] = src_ref[0, pl.ds(nat0 + r4 * j0 + c, piece, stride=r4), :]

    @pl.when(sbi > 0)
    def _():
        regroup(k_ref, k4_s, sbi * sb - look * r4, 0, look)
        regroup(v_ref, v4_s, sbi * sb - look * r4, 0, look)

    @pl.when(sbi == 0)
    def _():
        k4_s[:, 0:look, :] = jnp.zeros((r4, look, LANES), F32)
        v4_s[:, 0:look, :] = jnp.zeros((r4, look, LANES), F32)

    regroup(k_ref, k4_s, sbi * sb, look, cls)
    regroup(v_ref, v4_s, sbi * sb, look, cls)
    base4 = sbi * cls - look

    for half, qh_ref in enumerate((q0_ref, q1_ref)):
        valid = lo if half == 0 else jnp.logical_not(lo)
        regroup(qh_ref, q4_s, 0, 0, cls)

        def softmax_unit(q, kt, vt, pattern):
            qs = pltpu.roll(q, HEAD_DIM, 1)
            qa_src, qb_src = (q, qs) if half == 0 else (qs, q)
            qq = jnp.concatenate([jnp.where(valid, qa_src, 0.0), jnp.where(valid, qb_src, 0.0)],
                                 axis=0).astype(BF16)
            s = lax.dot_general(qq, kt.astype(BF16), (((1,), (1,)), ((), ())), preferred_element_type=F32)
            s = s + mask_ref[pattern]
            m = jnp.max(s, axis=1, keepdims=True)
            p = jnp.exp(s - m)
            l = jnp.sum(p, axis=1, keepdims=True)
            acc = jnp.dot(p.astype(BF16), vt.astype(BF16), preferred_element_type=F32)
            return jnp.broadcast_to(m, (2 * u_rows, LANES)), jnp.broadcast_to(l, (2 * u_rows, LANES)), acc

        def merge_store(rows, m, l, acc):
            ld = lambda ref: jnp.concatenate([ref[rows[0], :], ref[rows[1], :]], axis=0)
            m_old, l_old, acc_old = ld(m_s), ld(l_s), ld(acc_s)
            m_new = jnp.maximum(m_old, m)
            a_old, a_new = jnp.exp(m_old - m_new), jnp.exp(m - m_new)
            l = a_old * l_old + a_new * l
            acc = a_old * acc_old + a_new * acc
            for h2 in (0, 1):
                sl = slice(h2 * u_rows, (h2 + 1) * u_rows)
                m_s[rows[h2], :] = m_new[sl]
                l_s[rows[h2], :] = l[sl]
                acc_s[rows[h2], :] = acc[sl]

        def unit_d1(u, c):
            q0 = pl.multiple_of(u * u_rows, u_rows)
            qpos = sbi * sb + u * u_rows
            kpos = pl.multiple_of(jnp.maximum(qpos - span, 0), u_rows)
            m, l, acc = softmax_unit(qh_ref[0, pl.ds(q0, u_rows), :], k_ref[0, pl.ds(kpos, nkeys), :],
                                     v_ref[0, pl.ds(kpos, nkeys), :], jnp.where(qpos == 0, MASK_CLAMPED, MASK_FULL))
            part = u_rows // r4
            for a, (val, dst) in enumerate(((m, m_s), (l, l_s), (acc, acc_s))):
                tmp_s[a] = val
                for h2 in (0, 1):
                    for cl in range(r4):
                        d0 = pl.multiple_of(h2 * sb + cl * cls + u * part, part)
                        dst[pl.ds(d0, part), :] = tmp_s[a, pl.ds(h2 * u_rows + cl, part, stride=r4), :]
            return c

        def unit_d4(idx, c):
            cl, u = idx & (r4 - 1), idx >> 2
            qpos = sbi * cls + u * u_rows
            kpos = jnp.maximum(qpos - span, 0)
            kloc = pl.multiple_of(kpos - base4, u_rows)
            q0 = pl.multiple_of(u * u_rows, u_rows)
            m, l, acc = softmax_unit(q4_s[cl, pl.ds(q0, u_rows), :], k4_s[cl, pl.ds(kloc, nkeys), :],
                                     v4_s[cl, pl.ds(kloc, nkeys), :], jnp.where(qpos == 0, MASK_CLAMPED, MASK_FULL))
            r0 = pl.multiple_of(cl * cls + u * u_rows, u_rows)
            merge_store((pl.ds(r0, u_rows), pl.ds(sb + r0, u_rows)), m, l, acc)
            return c

        def unit_d16(idx, c):
            cl, sg = idx & (r4 - 1), idx >> 2
            qpos = sbi * u_rows
            kloc = sg + r4 * (qpos - span) - base4
            m, l, acc = softmax_unit(q4_s[cl, pl.ds(sg, u_rows, stride=r4), :],
                                     k4_s[cl, pl.ds(kloc, nkeys, stride=r4), :],
                                     v4_s[cl, pl.ds(kloc, nkeys, stride=r4), :],
                                     jnp.where(qpos == 0, MASK_BEFORE_START, MASK_FULL))
            r0 = cl * cls + sg
            merge_store((pl.ds(r0, u_rows, stride=r4), pl.ds(sb + r0, u_rows, stride=r4)), m, l, acc)
            return c

        for body in (unit_d1, unit_d4, unit_d16):
            lax.fori_loop(0, units, body, 0, unroll=8)

        for h2 in (0, 1):
            for cl in range(r4):
                for j0 in range(0, cls, piece):
                    rows = slice(h2 * sb + cl * cls + j0, h2 * sb + cl * cls + j0 + piece)
                    nat_s[h2, pl.ds(r4 * j0 + cl, piece, stride=r4), :] = acc_s[rows, :] / l_s[rows, :]
        lo2 = jnp.concatenate([lo, lo], axis=0)
        for c in range(sb // piece):
            ra = slice(c * piece, (c + 1) * piece)
            oa, ob = nat_s[0, ra, :], nat_s[1, ra, :]
            if half == 0:
                out = jnp.where(lo2, oa, pltpu.roll(ob, HEAD_DIM, 1))
            else:
                out = jnp.where(lo2, pltpu.roll(oa, HEAD_DIM, 1), ob)
            o_ref[0, ra, half * LANES:(half + 1) * LANES] = out.astype(BF16)


def _attention_window(q, k, v):
    b, l, _ = q.shape
    sb = ATT_SUPER
    max_dil = max(d for _, d in DILATED_BRANCHES)
    assert l % sb == 0 and l // max_dil >= ATT_UNIT + ATT_SPAN
    qspec = lambda h: pl.BlockSpec((1, sb, LANES), lambda bi, pi, si: (bi, si, 2 * pi + h))
    kspec = pl.BlockSpec((1, l, LANES), lambda bi, pi, si: (bi, 0, pi))
    return pl.pallas_call(
        _attn_window_kernel,
        grid=(b, KV_WIDTH // LANES, l // sb),
        in_specs=[qspec(0), qspec(1), kspec, kspec,
                  pl.BlockSpec((3, 2 * ATT_UNIT, ATT_UNIT + ATT_SPAN), lambda bi, pi, si: (0, 0, 0))],
        out_specs=pl.BlockSpec((1, sb, 2 * LANES), lambda bi, pi, si: (bi, si, pi)),
        out_shape=jax.ShapeDtypeStruct((b, l, ATT_WIDTH), BF16),
        scratch_shapes=[pltpu.VMEM((ATT_R4, (MAX_WINDOW + sb) // ATT_R4, LANES), F32)] * 2
        + [pltpu.VMEM((ATT_R4, sb // ATT_R4, LANES), F32)]
        + [pltpu.VMEM((2 * sb, LANES), F32)] * 3
        + [pltpu.VMEM((3, 2 * ATT_UNIT, LANES), F32), pltpu.VMEM((2, sb, LANES), F32)],
        compiler_params=_cparams(("parallel", "parallel", "arbitrary")),
        name="window_attn",
    )(q, q, k, v, _window_masks())


def _ssd_kernel(xbc_ref, dt_ref, z_ref, ssm0_ref, conv0_ref, cw_ref, cb_ref, dtb_ref, alog_ref, dsk_ref,
                nw_ref, ex_ref, y_ref, ssm_ref, conv_ref, xpad, dtpad, s_scr, *, n_valid):
    q = SSD_CHUNK
    lb = xbc_ref.shape[1]
    ci = pl.program_id(1)
    nc = pl.num_programs(1)
    gw = SSD_WIDTH // SSD_GROUPS
    hpg = SSD_HEADS // SSD_GROUPS

    @pl.when(ci == 0)
    def _():
        xpad[0:SUBLANES, :] = jnp.zeros((SUBLANES, CONV_DIM), F32)
        xpad[SUBLANES - (CONV_W - 1):SUBLANES, :] = conv0_ref[0]
        for g in range(SSD_GROUPS):
            s_scr[g] = jnp.transpose(ssm0_ref[0, g * hpg:(g + 1) * hpg].reshape(gw, SSD_STATE))

    xpad[SUBLANES:SUBLANES + lb, :] = xbc_ref[0]
    dtpad[0:lb, :] = dt_ref[0]
    if lb < q:
        xpad[SUBLANES + lb:SUBLANES + q, :] = jnp.zeros((q - lb, CONV_DIM), F32)
        dtpad[lb:q, :] = jnp.zeros((q - lb, LANES), F32)

    conv = cb_ref[...]
    for k in range(CONV_W):
        sh = CONV_W - 1 - k
        conv = conv + xpad[SUBLANES - sh:SUBLANES - sh + q, :] * cw_ref[k:k + 1, :]
    act = _silu(conv)

    @pl.when(ci == nc - 1)
    def _():
        conv_ref[0] = xpad[SUBLANES + n_valid - (CONV_W - 1):SUBLANES + n_valid, :]

    xpad[0:SUBLANES, :] = xpad[q:q + SUBLANES, :]

    xs = act[:, :SSD_WIDTH]
    row = lax.broadcasted_iota(I32, (q, LANES), 0)
    dtr = dtpad[...] + dtb_ref[...]
    dt = jnp.maximum(dtr, 0.0) + jnp.log(1.0 + jnp.exp(-jnp.abs(dtr)))
    dt = jnp.where(row < n_valid, dt, 0.0)
    a = -jnp.exp(alog_ref[...])
    da = dt * a
    r2 = lax.broadcasted_iota(I32, (q, q), 0)
    c2 = lax.broadcasted_iota(I32, (q, q), 1)
    causal = r2 >= c2
    a_cs = jnp.dot(causal.astype(F32), da, precision=HIGHEST, preferred_element_type=F32)
    a_cst = jnp.transpose(a_cs)
    a_last = a_cs[q - 1:q, :]
    ex = ex_ref[...]
    per_head = jnp.concatenate([dt, jnp.exp(a_cs), jnp.exp(a_last - a_cs),
                                jnp.broadcast_to(jnp.exp(a_last), (SUBLANES, LANES))], axis=0)
    t_hi = per_head.astype(BF16)
    rem = per_head - t_hi.astype(F32)
    t_mid = rem.astype(BF16)
    t_lo = (rem - t_mid.astype(F32)).astype(BF16)
    spread = (jnp.dot(t_hi, ex, preferred_element_type=F32) + jnp.dot(t_mid, ex, preferred_element_type=F32)
              + jnp.dot(t_lo, ex, preferred_element_type=F32))
    dt_x, ea_x, te_x, cd_x = spread[0:q], spread[q:2 * q], spread[2 * q:3 * q], spread[3 * q:3 * q + 1]
    xdt = xs * dt_x
    xdt_b = xdt.astype(BF16)
    xw_b = (xdt * te_x).astype(BF16)
    lo = lax.broadcasted_iota(I32, (q, LANES), 1) < SSD_HEAD_DIM

    ys = []
    for g in range(SSD_GROUPS):
        bm = act[:, SSD_WIDTH + g * SSD_STATE:SSD_WIDTH + (g + 1) * SSD_STATE]
        cm = act[:, SSD_WIDTH + (SSD_GROUPS + g) * SSD_STATE:SSD_WIDTH + (SSD_GROUPS + g + 1) * SSD_STATE]
        bm_b, cm_b = bm.astype(BF16), cm.astype(BF16)
        cb = lax.dot_general(cm_b, bm_b, (((1,), (1,)), ((), ())), preferred_element_type=F32)
        s_old = s_scr[g]
        y_off = jnp.dot(cm_b, s_old.astype(BF16), preferred_element_type=F32)
        for jp in range(hpg // 2):
            pair = g * (hpg // 2) + jp
            yp = []
            for hh in (0, 1):
                h = 2 * pair + hh
                seg = a_cs[:, h:h + 1] - a_cst[h:h + 1, :]
                dec = jnp.exp(jnp.where(causal, seg, NEG_BIG))
                mm = (cb * dec).astype(BF16)
                yp.append(jnp.dot(mm, xdt_b[:, pair * LANES:(pair + 1) * LANES], preferred_element_type=F32))
            ys.append(jnp.where(lo, yp[0], yp[1]) + y_off[:, jp * LANES:(jp + 1) * LANES]
                      * ea_x[:, pair * LANES:(pair + 1) * LANES])
        bmt_b = jnp.transpose(bm).astype(BF16)
        s_new = s_old * cd_x[:, g * gw:(g + 1) * gw] + jnp.dot(bmt_b, xw_b[:, g * gw:(g + 1) * gw],
                                                              preferred_element_type=F32)
        s_scr[g] = s_new

    y = jnp.concatenate(ys, axis=1) + dsk_ref[...] * xs

    @pl.when(ci == nc - 1)
    def _():
        for g in range(SSD_GROUPS):
            ssm_ref[0, g * hpg:(g + 1) * hpg] = jnp.transpose(s_scr[g]).reshape(hpg, SSD_HEAD_DIM, SSD_STATE)

    hg = y[:lb] * _silu(z_ref[0])
    outs = []
    for g in range(SSD_GROUPS):
        part = hg[:, g * gw:(g + 1) * gw]
        ms = jnp.mean(part * part, axis=1, keepdims=True)
        outs.append(part * lax.rsqrt(ms + NORM_EPS))
    y_ref[0] = (jnp.concatenate(outs, axis=1) * nw_ref[...]).astype(BF16)


def _ssd(xbc, dt_raw, z, ssm0, conv0, prm, lb, n_valid):
    b, l, _ = xbc.shape
    nc = l // lb
    cw, cbias, dtb, alog, dsk, nw, ex = prm
    full = lambda a: pl.BlockSpec(a.shape, lambda bi, ci: (0,) * a.ndim)
    seq = lambda w: pl.BlockSpec((1, lb, w), lambda bi, ci: (bi, ci, 0))
    return pl.pallas_call(
        functools.partial(_ssd_kernel, n_valid=n_valid),
        grid=(b, nc),
        in_specs=[seq(CONV_DIM), seq(LANES), seq(SSD_WIDTH),
                  pl.BlockSpec((1, SSD_HEADS, SSD_HEAD_DIM, SSD_STATE), lambda bi, ci: (bi, 0, 0, 0)),
                  pl.BlockSpec((1, CONV_W - 1, CONV_DIM), lambda bi, ci: (bi, 0, 0)),
                  full(cw), full(cbias), full(dtb), full(alog), full(dsk), full(nw), full(ex)],
        out_specs=(seq(SSD_WIDTH),
                   pl.BlockSpec((1, SSD_HEADS, SSD_HEAD_DIM, SSD_STATE), lambda bi, ci: (bi, 0, 0, 0)),
                   pl.BlockSpec((1, CONV_W - 1, CONV_DIM), lambda bi, ci: (bi, 0, 0))),
        out_shape=(jax.ShapeDtypeStruct((b, l, SSD_WIDTH), BF16),
                   jax.ShapeDtypeStruct((b, SSD_HEADS, SSD_HEAD_DIM, SSD_STATE), F32),
                   jax.ShapeDtypeStruct((b, CONV_W - 1, CONV_DIM), F32)),
        scratch_shapes=[pltpu.VMEM((SSD_CHUNK + 2 * SUBLANES, CONV_DIM), F32),
                        pltpu.VMEM((SSD_CHUNK, LANES), F32),
                        pltpu.VMEM((SSD_GROUPS, SSD_STATE, SSD_WIDTH // SSD_GROUPS), F32)],
        compiler_params=_cparams(("parallel", "arbitrary")),
        name="conv_ssd",
    )(xbc, dt_raw, z, ssm0, conv0, cw, cbias, dtb, alog, dsk, nw, ex)


def _layer_norm(r, g, b):
    mu = jnp.mean(r, axis=1, keepdims=True)
    d = r - mu
    var = jnp.mean(d * d, axis=1, keepdims=True)
    return d * lax.rsqrt(var + NORM_EPS) * g + b


def _mix_kernel(att_ref, ssd_ref, x_ref, wo_ref, g_ref, b_ref, *rest):
    h_ref, hs_ref = rest[-2:]
    tm = x_ref.shape[0]
    mix = jnp.dot(att_ref[...], wo_ref[0:ATT_WIDTH, :], preferred_element_type=F32)
    mix = mix + jnp.dot(ssd_ref[...], wo_ref[ATT_WIDTH:MIX_WIDTH, :], preferred_element_type=F32)
    h = _layer_norm(DEEPNORM_ALPHA * x_ref[...] + mix, g_ref[...], b_ref[...])
    h_ref[...] = h
    for j in range(ROW_SLAB):
        hs_ref[pl.ds(j, tm, stride=ROW_SLAB), :] = h[:, j * LANES:(j + 1) * LANES]


def _mix(att, ssd, x2d, wo_bf, g, b, tm, t_total, row0=0, into=None):
    t = x2d.shape[0]
    b0 = row0 // tm
    row = lambda w: pl.BlockSpec((tm, w), lambda i: (i, 0))
    vec = pl.BlockSpec((1, D_MODEL), lambda i: (0, 0))
    hbm = pl.BlockSpec(memory_space=pl.ANY)
    extra = () if into is None else tuple(into)
    return pl.pallas_call(
        _mix_kernel,
        grid=(t // tm,),
        in_specs=[row(ATT_WIDTH), row(SSD_WIDTH), row(D_MODEL),
                  pl.BlockSpec((MIX_WIDTH, D_MODEL), lambda i: (0, 0)), vec, vec] + [hbm] * len(extra),
        out_specs=(pl.BlockSpec((tm, D_MODEL), lambda i: (b0 + i, 0)),
                   pl.BlockSpec((tm * ROW_SLAB, LANES), lambda i: (b0 + i, 0))),
        out_shape=(jax.ShapeDtypeStruct((t_total, D_MODEL), F32),
                   jax.ShapeDtypeStruct((t_total * ROW_SLAB, LANES), F32)),
        input_output_aliases={} if into is None else {6: 0, 7: 1},
        compiler_params=_cparams(("parallel",)),
        name="outproj_ln1",
    )(att, ssd, x2d, wo_bf, g, b, *extra)


def _router_kernel(h_ref, wrh_ref, wrl_ref, bias_ref, tope_ref, rank_ref, gt_ref, cnt_ref, gscr):
    tm = h_ref.shape[0]
    i = pl.program_id(0)
    per_group = N_EXPERTS // N_EXPERT_GROUPS

    @pl.when(i == 0)
    def _():
        cnt_ref[...] = jnp.zeros_like(cnt_ref)

    h = h_ref[...]
    h_hi = h.astype(BF16)
    h_lo = (h - h_hi.astype(F32)).astype(BF16)
    nt = lambda a, b: lax.dot_general(a, b, (((1,), (1,)), ((), ())), preferred_element_type=F32)
    logits = nt(wrh_ref[...], h_hi) + nt(wrh_ref[...], h_lo) + nt(wrl_ref[...], h_hi)
    s = 1.0 / (1.0 + jnp.exp(-logits))
    sel = s + bias_ref[...]
    neg_inf = -jnp.inf
    e_iota = lax.broadcasted_iota(I32, (N_EXPERTS, tm), 0).astype(F32)
    g_iota = lax.broadcasted_iota(I32, (per_group, tm), 0).astype(F32)

    gscore = []
    for g in range(N_EXPERT_GROUPS):
        blk = sel[g * per_group:(g + 1) * per_group]
        m1 = jnp.max(blk, axis=0, keepdims=True)
        a1 = jnp.min(jnp.where(blk == m1, g_iota, float(per_group)), axis=0, keepdims=True)
        m2 = jnp.max(jnp.where(g_iota == a1, neg_inf, blk), axis=0, keepdims=True)
        gscore.append(m1 + m2)
    blocks = []
    for g in range(N_EXPERT_GROUPS):
        beaten = jnp.zeros((1, tm), F32)
        for o in range(N_EXPERT_GROUPS):
            if o < g:
                beaten = beaten + jnp.where(gscore[o] >= gscore[g], 1.0, 0.0)
            elif o > g:
                beaten = beaten + jnp.where(gscore[o] > gscore[g], 1.0, 0.0)
        blocks.append(jnp.where(beaten < float(TOPK_GROUPS), sel[g * per_group:(g + 1) * per_group], neg_inf))
    cand = jnp.concatenate(blocks, axis=0)

    tops, gsel = [], []
    onehot = jnp.zeros((N_EXPERTS, tm), F32)
    for _ in range(TOP_K):
        mx = jnp.max(cand, axis=0, keepdims=True)
        ix = jnp.min(jnp.where(cand == mx, e_iota, float(N_EXPERTS)), axis=0, keepdims=True)
        hit = e_iota == ix
        tops.append(ix)
        gsel.append(jnp.sum(jnp.where(hit, s, 0.0), axis=0, keepdims=True))
        onehot = jnp.where(hit, 1.0, onehot)
        cand = jnp.where(hit, neg_inf, cand)
    den = gsel[0]
    for k in range(1, TOP_K):
        den = den + gsel[k]

    t_r = lax.broadcasted_iota(I32, (tm, tm), 0)
    t_c = lax.broadcasted_iota(I32, (tm, tm), 1)
    before = (t_r < t_c).astype(BF16)
    oh_b = onehot.astype(BF16)
    base = cnt_ref[...] + jnp.dot(oh_b, before, preferred_element_type=F32)
    cnt_ref[...] = cnt_ref[...] + jnp.dot(oh_b, jnp.ones((tm, LANES), BF16), preferred_element_type=F32)

    gscr[...] = jnp.zeros_like(gscr)
    for k in range(TOP_K):
        tope_ref[k:k + 1, :] = tops[k].astype(I32)
        rank_ref[k:k + 1, :] = jnp.sum(jnp.where(e_iota == tops[k], base, 0.0), axis=0, keepdims=True).astype(I32)
        gscr[k:k + 1, :] = gsel[k] / den * ROUTED_SCALE
    gt_ref[...] = jnp.transpose(gscr[...])


def _router(h_all, wr_t, bias_b):
    t = h_all.shape[0]
    tm = ROUTER_TILE
    wr_hi = wr_t.astype(BF16)
    wr_lo = (wr_t - wr_hi.astype(F32)).astype(BF16)
    return pl.pallas_call(
        _router_kernel,
        grid=(t // tm,),
        in_specs=[pl.BlockSpec((tm, D_MODEL), lambda i: (i, 0)),
                  pl.BlockSpec((N_EXPERTS, D_MODEL), lambda i: (0, 0)),
                  pl.BlockSpec((N_EXPERTS, D_MODEL), lambda i: (0, 0)),
                  pl.BlockSpec((N_EXPERTS, tm), lambda i: (0, 0))],
        out_specs=(pl.BlockSpec((TOP_K, tm), lambda i: (0, i)),
                   pl.BlockSpec((TOP_K, tm), lambda i: (0, i)),
                   pl.BlockSpec((tm, LANES), lambda i: (i, 0)),
                   pl.BlockSpec((N_EXPERTS, LANES), lambda i: (0, 0))),
        out_shape=(jax.ShapeDtypeStruct((TOP_K, t), I32),
                   jax.ShapeDtypeStruct((TOP_K, t), I32),
                   jax.ShapeDtypeStruct((t, LANES), F32),
                   jax.ShapeDtypeStruct((N_EXPERTS, LANES), F32)),
        scratch_shapes=[pltpu.VMEM((tm, LANES), F32)],
        compiler_params=_cparams(("arbitrary",)),
        name="router",
    )(h_all, wr_hi, wr_lo, bias_b)


_PAD_SIZES = tuple(2 ** p for p in range(int(math.log2(EXPERT_BLOCK)) - 1, -1, -1))


def _slab(ref, row, n=1):
    return ref.at[pl.ds(pl.multiple_of(row * ROW_SLAB, ROW_SLAB), n * ROW_SLAB)]


def _padfill_kernel(pstart_ref, pcnt_ref, tail_ref, xs_ref, zero_scr, sem):
    zero_scr[...] = jnp.zeros_like(zero_scr)
    half = EXPERT_BLOCK // 2
    tail_copy = lambda j: pltpu.make_async_copy(_slab(zero_scr, 0, half), _slab(xs_ref, tail_ref[0] + j * half, half),
                                                sem)

    def tail_start(j, c):
        tail_copy(j).start()
        return c

    def tail_wait(j, c):
        tail_copy(j).wait()
        return c

    lax.fori_loop(0, tail_ref[1], tail_start, 0)

    def copies(e):
        base = pstart_ref[e]
        cnt = pcnt_ref[e]
        out = []
        for sz in _PAD_SIZES:
            out.append(((cnt & sz) != 0, pltpu.make_async_copy(_slab(zero_scr, 0, sz), _slab(xs_ref, base, sz), sem)))
            base = base + (cnt & sz)
        return out

    def start(e, c):
        for pred, cp in copies(e):
            @pl.when(pred)
            def _():
                cp.start()
        return c

    def wait(e, c):
        for pred, cp in copies(e):
            @pl.when(pred)
            def _():
                cp.wait()
        return c

    lax.fori_loop(0, N_EXPERTS, start, 0)
    lax.fori_loop(0, N_EXPERTS, wait, 0)
    lax.fori_loop(0, tail_ref[1], tail_wait, 0)


def _padfill(pad_start, pad_cnt, tail, n_rows):
    return pl.pallas_call(
        _padfill_kernel,
        grid_spec=pltpu.PrefetchScalarGridSpec(
            num_scalar_prefetch=3, grid=(1,), in_specs=[],
            out_specs=pl.BlockSpec(memory_space=pl.ANY),
            scratch_shapes=[pltpu.VMEM((EXPERT_BLOCK // 2 * ROW_SLAB, LANES), F32), pltpu.SemaphoreType.DMA]),
        out_shape=jax.ShapeDtypeStruct((n_rows * ROW_SLAB, LANES), F32),
        compiler_params=_cparams(("arbitrary",)),
        name="dispatch_padfill",
    )(pad_start, pad_cnt, tail)


def _dest_kernel(tope_ref, rank_ref, pstart_ref, dest_ref):
    tm = tope_ref.shape[1]
    e_iota = lax.broadcasted_iota(I32, (N_EXPERTS, tm), 0)
    ps = pstart_ref[...]
    for k in range(TOP_K):
        base = jnp.sum(jnp.where(e_iota == tope_ref[k:k + 1, :], ps, 0.0), axis=0, keepdims=True)
        dest_ref[k:k + 1, :] = base.astype(I32) + rank_ref[k:k + 1, :]


def _dest(top_e, rank, pstarts):
    t = top_e.shape[1]
    tm = DISPATCH_TILE
    blk = pl.BlockSpec((TOP_K, tm), lambda i: (0, i))
    return pl.pallas_call(
        _dest_kernel,
        grid=(t // tm,),
        in_specs=[blk, blk, pl.BlockSpec((N_EXPERTS, tm), lambda i: (0, 0))],
        out_specs=blk,
        out_shape=jax.ShapeDtypeStruct((TOP_K, t), I32),
        compiler_params=_cparams(("parallel",)),
        name="dispatch_dest",
    )(top_e, rank, jnp.broadcast_to(pstarts.astype(F32)[:, None], (N_EXPERTS, tm)))


def _dest_tiles(dest, tm):
    t = dest.shape[1]
    return jnp.transpose(dest.reshape(TOP_K, t // tm, tm), (1, 0, 2)).reshape(t // tm, TOP_K * tm)


def _dispatch_kernel(dest_ref, h_ref, xs_in_ref, xs_ref, idx_smem, isem, sem):
    del xs_in_ref
    i = pl.program_id(0)
    tm = DISPATCH_TILE
    cp = pltpu.make_async_copy(dest_ref.at[i], idx_smem, isem)
    cp.start()
    cp.wait()

    def row_copy(t, k):
        return pltpu.make_async_copy(_slab(h_ref, t), _slab(xs_ref, idx_smem[k * tm + t]), sem)

    def start(t, c):
        for k in range(TOP_K):
            row_copy(t, k).start(priority=k % 2)
        return c

    def wait(t, c):
        for k in range(TOP_K):
            row_copy(t, k).wait()
        return c

    lax.fori_loop(0, tm, start, 0)
    lax.fori_loop(0, tm, wait, 0)


def _dispatch(dest_tiles, h_slab, xs):
    tm = DISPATCH_TILE
    return pl.pallas_call(
        _dispatch_kernel,
        grid=(dest_tiles.shape[0],),
        in_specs=[pl.BlockSpec(dest_tiles.shape, lambda i: (0, 0)),
                  pl.BlockSpec((tm * ROW_SLAB, LANES), lambda i: (i, 0)),
                  pl.BlockSpec(memory_space=pl.ANY)],
        out_specs=pl.BlockSpec(memory_space=pl.ANY),
        out_shape=jax.ShapeDtypeStruct(xs.shape, xs.dtype),
        scratch_shapes=[pltpu.SMEM((tm * TOP_K,), I32), pltpu.SemaphoreType.DMA, pltpu.SemaphoreType.DMA],
        input_output_aliases={2: 0},
        compiler_params=_cparams(("arbitrary",)),
        name="dispatch_rows",
    )(dest_tiles, h_slab, xs)


def _expert_kernel(bi_ref, be_ref, nu_ref, x_ref, wg_ref, wu_ref, wd_ref, y_ref, wg_b, wu_b, wd_b):
    del bi_ref
    i = pl.program_id(0)

    @pl.when(i < nu_ref[0])
    def _():
        prev = be_ref[jnp.maximum(i - 1, 0)]

        @pl.when(jnp.logical_or(i == 0, be_ref[i] != prev))
        def _():
            wg_b[...] = wg_ref[0].astype(BF16)
            wu_b[...] = wu_ref[0].astype(BF16)
            wd_b[...] = wd_ref[0].astype(BF16)

        xb = jnp.concatenate([x_ref[pl.ds(c, EXPERT_BLOCK, stride=ROW_SLAB), :] for c in range(ROW_SLAB)],
                             axis=1).astype(BF16)
        g = jnp.dot(xb, wg_b[...], preferred_element_type=F32)
        u = jnp.dot(xb, wu_b[...], preferred_element_type=F32)
        a = (_silu(g) * u).astype(BF16)
        y = jnp.dot(a, wd_b[...], preferred_element_type=F32)
        for c in range(ROW_SLAB):
            y_ref[pl.ds(c, EXPERT_BLOCK, stride=ROW_SLAB), :] = y[:, c * LANES:(c + 1) * LANES]

    @pl.when(i >= nu_ref[0])
    def _():
        y_ref[...] = jnp.zeros_like(y_ref)


def _experts(blk_idx, blk_exp, n_used, xs, wg, wu, wd):
    n_rows = xs.shape[0] // ROW_SLAB
    nb = n_rows // EXPERT_BLOCK
    rows = pl.BlockSpec((EXPERT_BLOCK * ROW_SLAB, LANES), lambda i, bi, be, nu: (bi[i], 0))
    out_rows = pl.BlockSpec((EXPERT_BLOCK * ROW_SLAB, LANES), lambda i, bi, be, nu: (i, 0))
    return pl.pallas_call(
        _expert_kernel,
        grid_spec=pltpu.PrefetchScalarGridSpec(
            num_scalar_prefetch=3, grid=(nb,),
            in_specs=[rows,
                      pl.BlockSpec((1, D_MODEL, EXPERT_FF), lambda i, bi, be, nu: (be[i], 0, 0)),
                      pl.BlockSpec((1, D_MODEL, EXPERT_FF), lambda i, bi, be, nu: (be[i], 0, 0)),
                      pl.BlockSpec((1, EXPERT_FF, D_MODEL), lambda i, bi, be, nu: (be[i], 0, 0))],
            out_specs=out_rows,
            scratch_shapes=[pltpu.VMEM((D_MODEL, EXPERT_FF), BF16), pltpu.VMEM((D_MODEL, EXPERT_FF), BF16),
                            pltpu.VMEM((EXPERT_FF, D_MODEL), BF16)]),
        out_shape=jax.ShapeDtypeStruct(xs.shape, F32),
        compiler_params=_cparams(("arbitrary",)),
        name="expert_ffn",
    )(blk_idx, blk_exp, n_used, xs, wg, wu, wd)


def _combine_kernel(dest_ref, h_ref, gt_ref, ys_ref, sg_ref, su_ref, sd_ref, g_ref, b_ref, o_ref, o2_ref,
                    idx0, idx1, gbuf, isem, sem, *, n_first):
    idx_smem = (idx0, idx1)
    i = pl.program_id(0)
    n = pl.num_programs(0)
    tm = h_ref.shape[0]

    def idx_copy(step, s):
        return pltpu.make_async_copy(dest_ref.at[step], idx_smem[s], isem.at[s])

    def row_copy(s, t, k, src_row):
        return pltpu.make_async_copy(_slab(ys_ref, src_row), _slab(gbuf.at[s, k], t), sem.at[s])

    def issue(s):
        def body(t, c):
            for k in range(TOP_K):
                row_copy(s, t, k, idx_smem[s][k * tm + t]).start(priority=k % 2)
            return c
        lax.fori_loop(0, tm, body, 0)

    def drain(s):
        def body(t, c):
            for k in range(TOP_K):
                row_copy(s, t, k, 0).wait()
            return c
        lax.fori_loop(0, tm, body, 0)

    @pl.when(i == 0)
    def _():
        first = idx_copy(0, 0)
        first.start()
        first.wait()
        issue(0)

        @pl.when(n > 1)
        def _():
            idx_copy(1, 1).start()

    def step(slot):
        nslot = 1 - slot

        @pl.when(i + 1 < n)
        def _():
            idx_copy(i + 1, nslot).wait()
            issue(nslot)

            @pl.when(i + 2 < n)
            def _():
                idx_copy(i + 2, slot).start()

        h = h_ref[...]
        hb = h.astype(BF16)
        g = jnp.dot(hb, sg_ref[...], preferred_element_type=F32)
        u = jnp.dot(hb, su_ref[...], preferred_element_type=F32)
        f = jnp.dot((_silu(g) * u).astype(BF16), sd_ref[...], preferred_element_type=F32)
        drain(slot)
        gt = gt_ref[...]
        cols = []
        for j in range(ROW_SLAB):
            fj = f[:, j * LANES:(j + 1) * LANES]
            for k in range(TOP_K):
                fj = fj + gbuf[slot, k, pl.ds(j, tm, stride=ROW_SLAB), :] * gt[:, k:k + 1]
            cols.append(fj)
        f = jnp.concatenate(cols, axis=1)
        out = _layer_norm(DEEPNORM_ALPHA * h + f, g_ref[...], b_ref[...])

        @pl.when(i < n_first)
        def _():
            o_ref[...] = out

        @pl.when(i == n_first)
        def _():
            o2_ref[...] = out

    for parity in (0, 1):
        pl.when((i & 1) == parity)(functools.partial(step, parity))


def _combine(dest_tiles, h_all, gates_t, ys, sg_bf, su_bf, sd_bf, g, b, n_first):
    t = h_all.shape[0]
    tm = COMBINE_TILE
    vec = pl.BlockSpec((1, D_MODEL), lambda i: (0, 0))
    return pl.pallas_call(
        functools.partial(_combine_kernel, n_first=n_first),
        grid=(t // tm,),
        in_specs=[pl.BlockSpec(dest_tiles.shape, lambda i: (0, 0)),
                  pl.BlockSpec((tm, D_MODEL), lambda i: (i, 0)),
                  pl.BlockSpec((tm, LANES), lambda i: (i, 0)),
                  pl.BlockSpec(memory_space=pl.ANY),
                  pl.BlockSpec((D_MODEL, SHARED_FF), lambda i: (0, 0)),
                  pl.BlockSpec((D_MODEL, SHARED_FF), lambda i: (0, 0)),
                  pl.BlockSpec((SHARED_FF, D_MODEL), lambda i: (0, 0)), vec, vec],
        out_specs=(pl.BlockSpec((tm, D_MODEL), lambda i: (jnp.minimum(i, n_first - 1), 0)),
                   pl.BlockSpec((tm, D_MODEL), lambda i: (0, 0))),
        out_shape=(jax.ShapeDtypeStruct((n_first * tm, D_MODEL), F32), jax.ShapeDtypeStruct((tm, D_MODEL), F32)),
        scratch_shapes=[pltpu.SMEM((tm * TOP_K,), I32), pltpu.SMEM((tm * TOP_K,), I32),
                        pltpu.VMEM((2, TOP_K, tm * ROW_SLAB, LANES), F32),
                        pltpu.SemaphoreType.DMA((2,)), pltpu.SemaphoreType.DMA((2,))],
        compiler_params=_cparams(("arbitrary",)),
        name="combine_ln2",
    )(dest_tiles, h_all, gates_t, ys, sg_bf, su_bf, sd_bf, g, b)


def _moe(h_all, h_slab, n_first, wr, rbias, wg, wu, wd, sg, su, sd, g2, b2):
    t = h_all.shape[0]
    top_e, rank, gates_t, cnt = _router(h_all, jnp.transpose(wr),
                                        jnp.broadcast_to(rbias[:, None], (N_EXPERTS, ROUTER_TILE)))
    counts = cnt[:, 0].astype(I32)
    pcounts = (counts + EXPERT_BLOCK - 1) // EXPERT_BLOCK * EXPERT_BLOCK
    pends = jnp.cumsum(pcounts)
    pstarts = pends - pcounts
    n_rows = (t * TOP_K // EXPERT_BLOCK + N_EXPERTS - 1) * EXPERT_BLOCK
    nb = n_rows // EXPERT_BLOCK
    n_used = (pends[-1] // EXPERT_BLOCK).astype(I32)
    blk_idx = jnp.minimum(jnp.arange(nb, dtype=I32), n_used - 1)
    blk_exp = jnp.minimum(jnp.sum((pends[None, :] <= (blk_idx * EXPERT_BLOCK)[:, None]).astype(I32), axis=1),
                          N_EXPERTS - 1)
    dest = _dest(top_e, rank, pstarts)
    tail = jnp.stack([pends[-1], (n_rows - pends[-1]) // (EXPERT_BLOCK // 2)]).astype(I32)
    xs = _padfill((pstarts + counts).astype(I32), (pcounts - counts).astype(I32), tail, n_rows)
    xs = _dispatch(_dest_tiles(dest, DISPATCH_TILE), h_slab, xs)
    ys = _experts(blk_idx, blk_exp, n_used.reshape(1), xs, wg, wu, wd)
    return _combine(_dest_tiles(dest, COMBINE_TILE), h_all, gates_t, ys,
                    sg.astype(BF16), su.astype(BF16), sd.astype(BF16), g2[None, :], b2[None, :], n_first)


def _expand_matrix():
    h = np.arange(LANES)[:, None]
    c = np.arange(SSD_WIDTH)[None, :]
    return jnp.asarray((c // SSD_HEAD_DIM == h).astype(np.float32), dtype=BF16)


def kernel(x_prompt, x_sample, cache_k, cache_v, state_ssm, state_conv, w_in, conv_w, conv_b, dt_bias, a_log, d_skip, ssd_norm_w, w_out, ln1_g, ln1_b, w_router, router_bias, w_exp_gate, w_exp_up, w_exp_down, w_sh_gate, w_sh_up, w_sh_down, ln2_g, ln2_b):
    bp, lp, _ = x_prompt.shape
    bs, ls, _ = x_sample.shape
    win = cache_k.shape[2]
    keep = min(MAX_WINDOW, lp)
    assert lp % SSD_CHUNK == 0 and ls <= SUBLANES and win % KEY_TILE == 0 and win >= MAX_WINDOW

    w_bf = jnp.pad(w_in[0], ((0, 0), (0, IN_COLS_PAD - IN_COLS))).astype(BF16)
    wo_bf = w_out[0].astype(BF16)
    pad_l = lambda v: jnp.pad(v, (0, LANES - v.shape[0]))[None, :]
    ssd_prm = (conv_w[0], conv_b[0][None, :], pad_l(dt_bias[0]), pad_l(a_log[0]),
               jnp.repeat(d_skip[0], SSD_HEAD_DIM)[None, :], ssd_norm_w[0][None, :], _expand_matrix())

    tp = bp * lp
    tm_p = 512 if lp % 512 == 0 else SSD_CHUNK
    tm_mix = 512 if tp % 512 == 0 else SSD_CHUNK
    cos_p, sin_p = _rope_tables(jnp.arange(lp, dtype=F32))
    xp2 = x_prompt.reshape(tp, D_MODEL)
    q, kf, vf, z, xbc, dtr = _inproj(xp2, w_bf, cos_p, sin_p, tm_p, lp // tm_p)
    att = _attention_window(q.reshape(bp, lp, ATT_WIDTH), kf.reshape(bp, lp, KV_WIDTH), vf.reshape(bp, lp, KV_WIDTH))
    ssd_y, ssm_p, conv_p = _ssd(xbc.reshape(bp, lp, CONV_DIM), dtr.reshape(bp, lp, LANES),
                                z.reshape(bp, lp, SSD_WIDTH),
                                jnp.zeros((bp, SSD_HEADS, SSD_HEAD_DIM, SSD_STATE), F32),
                                jnp.zeros((bp, CONV_W - 1, CONV_DIM), F32), ssd_prm, SSD_CHUNK, SSD_CHUNK)
    ts = bs * ls
    t_pad = -(-(tp + ts) // DISPATCH_TILE) * DISPATCH_TILE
    tail = t_pad - tp
    assert tp % tail == 0 and ts <= COMBINE_TILE and tp % COMBINE_TILE == 0
    h_bufs = _mix(att.reshape(tp, ATT_WIDTH), ssd_y.reshape(tp, SSD_WIDTH), xp2, wo_bf, ln1_g, ln1_b, tm_mix, t_pad)

    pos_s = (PAST_LEN + jnp.arange(ls, dtype=jnp.int32)).astype(F32)
    cos_s, sin_s = _rope_tables(jnp.tile(pos_s, bs))
    xs2 = x_sample.reshape(ts, D_MODEL)
    q_s, kf_s, vf_s, z_s, xbc_s, dtr_s = _inproj(xs2, w_bf, cos_s, sin_s, ts, 1)
    rows8 = lambda a, w: jnp.pad(a.reshape(bs, ls, w), ((0, 0), (0, SUBLANES - ls), (0, 0)))
    att_s = _attention_decode(rows8(q_s, ATT_WIDTH), cache_k[0].reshape(bs, win, KV_WIDTH),
                              cache_v[0].reshape(bs, win, KV_WIDTH), rows8(kf_s, KV_WIDTH),
                              rows8(vf_s, KV_WIDTH))[:, :ls]
    ssd_s, ssm_s, conv_s = _ssd(rows8(xbc_s, CONV_DIM), rows8(dtr_s, LANES), rows8(z_s, SSD_WIDTH),
                                state_ssm[0], state_conv[0], ssd_prm, SUBLANES, ls)
    tail_rows = lambda a: jnp.pad(a, ((0, tail - ts), (0, 0)))
    h_all, h_slab = _mix(tail_rows(att_s.reshape(ts, ATT_WIDTH)), tail_rows(ssd_s[:, :ls].reshape(ts, SSD_WIDTH)),
                         tail_rows(xs2), wo_bf, ln1_g, ln1_b, tail, t_pad, row0=tp, into=h_bufs)

    y_p, y_s = _moe(h_all, h_slab, tp // COMBINE_TILE, w_router[0], router_bias[0], w_exp_gate[0], w_exp_up[0],
                    w_exp_down[0], w_sh_gate[0], w_sh_up[0], w_sh_down[0], ln2_g[0], ln2_b[0])

    kv5 = lambda a, b, l: a.reshape(1, b, l, N_KV_HEADS, HEAD_DIM)
    return (y_p.reshape(bp, lp, D_MODEL), y_s[:ts].reshape(bs, ls, D_MODEL),
            kv5(kf.reshape(bp, lp, KV_WIDTH)[:, lp - keep:], bp, keep),
            kv5(vf.reshape(bp, lp, KV_WIDTH)[:, lp - keep:], bp, keep),
            ssm_p[None], conv_p[None],
            kv5(kf_s, bs, ls), kv5(vf_s, bs, ls), ssm_s[None], conv_s[None])
```

```python
import functools
import math

import jax
import jax.numpy as jnp
import numpy as np
from jax import lax
from jax.experimental import pallas as pl
from jax.experimental.pallas import tpu as pltpu

F32 = jnp.float32
BF16 = jnp.bfloat16
I32 = jnp.int32

D_MODEL = 1024
PAST_LEN = 16384
HEAD_DIM = 64
N_ATT_HEADS = 16
N_KV_HEADS = 8
ATT_WIDTH = N_ATT_HEADS * HEAD_DIM
KV_WIDTH = N_KV_HEADS * HEAD_DIM
DILATED_BRANCHES = ((128, 1), (512, 4), (2048, 16))
MAX_WINDOW = 2048
ROPE_THETA = 10000.0
SSD_HEADS = 16
SSD_HEAD_DIM = 64
SSD_WIDTH = SSD_HEADS * SSD_HEAD_DIM
SSD_GROUPS = 2
SSD_STATE = 128
SSD_CHUNK = 128
CONV_W = 4
CONV_DIM = SSD_WIDTH + 2 * SSD_GROUPS * SSD_STATE
MIX_WIDTH = ATT_WIDTH + SSD_WIDTH
IN_COLS = ATT_WIDTH + 2 * KV_WIDTH + SSD_WIDTH + CONV_DIM + SSD_HEADS
N_EXPERTS = 256
TOP_K = 8
N_EXPERT_GROUPS = 8
TOPK_GROUPS = 4
EXPERT_FF = 256
SHARED_FF = 256
ROUTED_SCALE = 2.5
DEPTH = 1
DEEPNORM_ALPHA = (2.0 * DEPTH) ** 0.25
NORM_EPS = 1e-5

LANES = 128
SUBLANES = 8
VMEM_LIMIT = 56 * 1024 * 1024

IN_COLS_PAD = ATT_WIDTH + 2 * KV_WIDTH + SSD_WIDTH + CONV_DIM + LANES
KEY_TILE = 128
EXPERT_BLOCK = 512
ROUTER_TILE = 128
DISPATCH_TILE = 256
COMBINE_TILE = 128
ROW_SLAB = D_MODEL // LANES
NEG_BIG = -1e30
HIGHEST = lax.Precision.HIGHEST


def _cparams(sem):
    return pltpu.CompilerParams(dimension_semantics=sem, vmem_limit_bytes=VMEM_LIMIT)


def _silu(x):
    return x * (1.0 / (1.0 + jnp.exp(-x)))


def _inproj_kernel(x_ref, w_ref, cq_ref, sq_ref, ck_ref, sk_ref,
                   q_ref, kf_ref, vf_ref, z_ref, xbc_ref, dt_ref):
    tm = x_ref.shape[0]
    xb = x_ref[...].astype(BF16)
    lane = lax.broadcasted_iota(I32, (tm, LANES), 1)
    first_half = (lane % HEAD_DIM) < (HEAD_DIM // 2)

    def rope(a, c, s):
        partner = jnp.where(first_half, pltpu.roll(a, LANES - HEAD_DIM // 2, 1), pltpu.roll(a, HEAD_DIM // 2, 1))
        return a * c + partner * s

    c0 = 0
    acc = jnp.dot(xb, w_ref[:, c0:c0 + ATT_WIDTH], preferred_element_type=F32)
    cq, sq = cq_ref[...], sq_ref[...]
    for j in range(ATT_WIDTH // LANES):
        q_ref[:, j * LANES:(j + 1) * LANES] = rope(acc[:, j * LANES:(j + 1) * LANES], cq, sq)
    c0 += ATT_WIDTH
    acc = jnp.dot(xb, w_ref[:, c0:c0 + KV_WIDTH], preferred_element_type=F32)
    ck, sk = ck_ref[...], sk_ref[...]
    for j in range(KV_WIDTH // LANES):
        kf_ref[:, j * LANES:(j + 1) * LANES] = rope(acc[:, j * LANES:(j + 1) * LANES], ck, sk)
    c0 += KV_WIDTH
    vf_ref[...] = jnp.dot(xb, w_ref[:, c0:c0 + KV_WIDTH], preferred_element_type=F32)
    c0 += KV_WIDTH
    z_ref[...] = jnp.dot(xb, w_ref[:, c0:c0 + SSD_WIDTH], preferred_element_type=F32)
    c0 += SSD_WIDTH
    xbc_ref[...] = jnp.dot(xb, w_ref[:, c0:c0 + CONV_DIM], preferred_element_type=F32)
    c0 += CONV_DIM
    dt_ref[...] = jnp.dot(xb, w_ref[:, c0:c0 + LANES], preferred_element_type=F32)


def _rope_tables(pos):
    half = HEAD_DIM // 2
    inv = ROPE_THETA ** (-jnp.arange(half, dtype=F32) / half)
    ang = pos[:, None] * inv[None, :]
    cos, sin = jnp.cos(ang), jnp.sin(ang)
    c = jnp.concatenate([cos, cos, cos, cos], axis=1)
    s = jnp.concatenate([-sin, sin, -sin, sin], axis=1)
    return c, s


def _inproj(x2d, w_bf, cos_t, sin_t, tm, tiles_per_seq):
    t = x2d.shape[0]
    scale = HEAD_DIM ** -0.5
    tab = pl.BlockSpec((tm, LANES), lambda i: (i % tiles_per_seq, 0))
    row = lambda w: pl.BlockSpec((tm, w), lambda i: (i, 0))
    outs = (
        jax.ShapeDtypeStruct((t, ATT_WIDTH), F32),
        jax.ShapeDtypeStruct((t, KV_WIDTH), F32),
        jax.ShapeDtypeStruct((t, KV_WIDTH), F32),
        jax.ShapeDtypeStruct((t, SSD_WIDTH), F32),
        jax.ShapeDtypeStruct((t, CONV_DIM), F32),
        jax.ShapeDtypeStruct((t, LANES), F32),
    )
    return pl.pallas_call(
        _inproj_kernel,
        grid=(t // tm,),
        in_specs=[row(D_MODEL), pl.BlockSpec((D_MODEL, IN_COLS_PAD), lambda i: (0, 0)), tab, tab, tab, tab],
        out_specs=(row(ATT_WIDTH), row(KV_WIDTH), row(KV_WIDTH), row(SSD_WIDTH), row(CONV_DIM), row(LANES)),
        out_shape=outs,
        compiler_params=_cparams(("parallel",)),
        name="inproj",
    )(x2d, w_bf, cos_t * scale, sin_t * scale, cos_t, sin_t)


def _branch_weight(d):
    w = np.zeros(d.shape, np.float32)
    for window, dil in DILATED_BRANCHES:
        w += ((d >= 0) & (d <= window) & (d % dil == 0)).astype(np.float32)
    return w


def _attn_decode_kernel(q_ref, kc_ref, vc_ref, kn_ref, vn_ref, wc_ref, wn_ref, o_ref, ktail, vtail):
    nq = q_ref.shape[1]
    lane = lax.broadcasted_iota(I32, (nq, LANES), 1)
    lo = lane < HEAD_DIM
    ktail[...] = jnp.zeros_like(ktail)
    vtail[...] = jnp.zeros_like(vtail)
    ktail[0:nq, :] = kn_ref[0]
    vtail[0:nq, :] = vn_ref[0]
    wc, wn = wc_ref[...], wn_ref[...]
    nt = (((1,), (1,)), ((), ()))
    for g in range(N_KV_HEADS):
        pair, half = divmod(g, 2)
        cols = slice(pair * LANES, (pair + 1) * LANES)
        q = q_ref[0, :, g * LANES:(g + 1) * LANES]
        qs = pltpu.roll(q, HEAD_DIM, 1)
        valid = lo if half == 0 else jnp.logical_not(lo)
        qa_src, qb_src = (q, qs) if half == 0 else (qs, q)
        qq = jnp.concatenate([jnp.where(valid, qa_src, 0.0), jnp.where(valid, qb_src, 0.0)], axis=0).astype(BF16)
        s_c = lax.dot_general(qq, kc_ref[0, :, cols].astype(BF16), nt, preferred_element_type=F32)
        s_n = lax.dot_general(qq, ktail[:, cols].astype(BF16), nt, preferred_element_type=F32)
        s_c = jnp.where(wc > 0.0, s_c, NEG_BIG)
        s_n = jnp.where(wn > 0.0, s_n, NEG_BIG)
        m = jnp.maximum(jnp.max(s_c, axis=1, keepdims=True), jnp.max(s_n, axis=1, keepdims=True))
        p_c = jnp.exp(s_c - m) * wc
        p_n = jnp.exp(s_n - m) * wn
        l = jnp.sum(p_c, axis=1, keepdims=True) + jnp.sum(p_n, axis=1, keepdims=True)
        acc = jnp.dot(p_c.astype(BF16), vc_ref[0, :, cols].astype(BF16), preferred_element_type=F32)
        acc = acc + jnp.dot(p_n.astype(BF16), vtail[:, cols].astype(BF16), preferred_element_type=F32)
        o = acc / l
        oa, ob = o[:nq], o[nq:]
        if half == 0:
            out = jnp.where(lo, oa, pltpu.roll(ob, HEAD_DIM, 1))
        else:
            out = jnp.where(lo, pltpu.roll(oa, HEAD_DIM, 1), ob)
        o_ref[0, :, g * LANES:(g + 1) * LANES] = out.astype(BF16)


def _attention_decode(q, k_cache, v_cache, k_new, v_new):
    b, nq, _ = q.shape
    win = k_cache.shape[1]
    i = np.arange(nq)[:, None]
    wc = _branch_weight(win + i - np.arange(win)[None, :])
    wn = _branch_weight(i - np.arange(KEY_TILE)[None, :])
    wn[:, nq:] = 0.0
    stack = lambda w: jnp.asarray(np.concatenate([w, w], axis=0))
    seq = lambda rows, w: pl.BlockSpec((1, rows, w), lambda bi: (bi, 0, 0))
    full = lambda shp: pl.BlockSpec(shp, lambda bi: (0, 0))
    return pl.pallas_call(
        _attn_decode_kernel,
        grid=(b,),
        in_specs=[seq(nq, ATT_WIDTH), seq(win, KV_WIDTH), seq(win, KV_WIDTH), seq(nq, KV_WIDTH), seq(nq, KV_WIDTH),
                  full((2 * nq, win)), full((2 * nq, KEY_TILE))],
        out_specs=seq(nq, ATT_WIDTH),
        out_shape=jax.ShapeDtypeStruct((b, nq, ATT_WIDTH), BF16),
        scratch_shapes=[pltpu.VMEM((KEY_TILE, KV_WIDTH), F32), pltpu.VMEM((KEY_TILE, KV_WIDTH), F32)],
        compiler_params=_cparams(("arbitrary",)),
        name="decode_attn",
    )(q, k_cache, v_cache, k_new, v_new, stack(wc), stack(wn))


ATT_SUPER = 2048
ATT_UNIT = 128
ATT_SPAN = max(w // d for w, d in DILATED_BRANCHES)


ATT_R4 = 4


MASK_FULL, MASK_CLAMPED, MASK_BEFORE_START = 0, 1, 2


def _window_masks():
    i = (np.arange(2 * ATT_UNIT) % ATT_UNIT)[:, None]
    c = np.arange(ATT_UNIT + ATT_SPAN)[None, :]
    ok = lambda d: (d >= 0) & (d <= ATT_SPAN)
    keep = np.stack([ok(ATT_SPAN + i - c), ok(i - c), ok(ATT_SPAN + i - c) & (c >= ATT_SPAN)])
    return jnp.asarray(np.where(keep, 0.0, NEG_BIG).astype(np.float32))


def _attn_window_kernel(q0_ref, q1_ref, k_ref, v_ref, mask_ref, o_ref,
                        k4_s, v4_s, q4_s, acc_s, m_s, l_s, tmp_s, nat_s):
    sb = q0_ref.shape[1]
    sbi = pl.program_id(2)
    u_rows, span, r4 = ATT_UNIT, ATT_SPAN, ATT_R4
    nkeys = u_rows + span
    cls = sb // r4
    look = MAX_WINDOW // r4
    units = sb // u_rows
    assert [d for _, d in DILATED_BRANCHES] == [1, r4, r4 * r4] and sb // (r4 * r4) == u_rows
    lane = lax.broadcasted_iota(I32, (u_rows, LANES), 1)
    lo = lane < HEAD_DIM
    piece = 2 * u_rows

    def regroup(src_ref, dst, nat0, loc0, n):
        for c in range(r4):
            for j0 in range(0, n, piece):
                dst[c, loc0 + j0:loc0 + j0 + piece, :] = src_ref[0, pl.ds(nat0 + r4 * j0 + c, piece, stride=r4), :]

    @pl.when(sbi > 0)
    def _():
        regroup(k_ref, k4_s, sbi * sb - look * r4, 0, look)
        regroup(v_ref, v4_s, sbi * sb - look * r4, 0, look)

    @pl.when(sbi == 0)
    def _():
        k4_s[:, 0:look, :] = jnp.zeros((r4, look, LANES), F32)
        v4_s[:, 0:look, :] = jnp.zeros((r4, look, LANES), F32)

    regroup(k_ref, k4_s, sbi * sb, look, cls)
    regroup(v_ref, v4_s, sbi * sb, look, cls)
    base4 = sbi * cls - look

    for half, qh_ref in enumerate((q0_ref, q1_ref)):
        valid = lo if half == 0 else jnp.logical_not(lo)
        regroup(qh_ref, q4_s, 0, 0, cls)

        def softmax_unit(q, kt, vt, pattern):
            qs = pltpu.roll(q, HEAD_DIM, 1)
            qa_src, qb_src = (q, qs) if half == 0 else (qs, q)
            qq = jnp.concatenate([jnp.where(valid, qa_src, 0.0), jnp.where(valid, qb_src, 0.0)],
                                 axis=0).astype(BF16)
            s = lax.dot_general(qq, kt.astype(BF16), (((1,), (1,)), ((), ())), preferred_element_type=F32)
            s = s + mask_ref[pattern]
            m = jnp.max(s, axis=1, keepdims=True)
            p = jnp.exp(s - m)
            l = jnp.sum(p, axis=1, keepdims=True)
            acc = jnp.dot(p.astype(BF16), vt.astype(BF16), preferred_element_type=F32)
            return jnp.broadcast_to(m, (2 * u_rows, LANES)), jnp.broadcast_to(l, (2 * u_rows, LANES)), acc

        def merge_store(rows, m, l, acc):
            ld = lambda ref: jnp.concatenate([ref[rows[0], :], ref[rows[1], :]], axis=0)
            m_old, l_old, acc_old = ld(m_s), ld(l_s), ld(acc_s)
            m_new = jnp.maximum(m_old, m)
            a_old, a_new = jnp.exp(m_old - m_new), jnp.exp(m - m_new)
            l = a_old * l_old + a_new * l
            acc = a_old * acc_old + a_new * acc
            for h2 in (0, 1):
                sl = slice(h2 * u_rows, (h2 + 1) * u_rows)
                m_s[rows[h2], :] = m_new[sl]
                l_s[rows[h2], :] = l[sl]
                acc_s[rows[h2], :] = acc[sl]

        def unit_d1(u, c):
            q0 = pl.multiple_of(u * u_rows, u_rows)
            qpos = sbi * sb + u * u_rows
            kpos = pl.multiple_of(jnp.maximum(qpos - span, 0), u_rows)
            m, l, acc = softmax_unit(qh_ref[0, pl.ds(q0, u_rows), :], k_ref[0, pl.ds(kpos, nkeys), :],
                                     v_ref[0, pl.ds(kpos, nkeys), :], jnp.where(qpos == 0, MASK_CLAMPED, MASK_FULL))
            part = u_rows // r4
            for a, (val, dst) in enumerate(((m, m_s), (l, l_s), (acc, acc_s))):
                tmp_s[a] = val
                for h2 in (0, 1):
                    for cl in range(r4):
                        d0 = pl.multiple_of(h2 * sb + cl * cls + u * part, part)
                        dst[pl.ds(d0, part), :] = tmp_s[a, pl.ds(h2 * u_rows + cl, part, stride=r4), :]
            return c

        def unit_d4(idx, c):
            cl, u = idx & (r4 - 1), idx >> 2
            qpos = sbi * cls + u * u_rows
            kpos = jnp.maximum(qpos - span, 0)
            kloc = pl.multiple_of(kpos - base4, u_rows)
            q0 = pl.multiple_of(u * u_rows, u_rows)
            m, l, acc = softmax_unit(q4_s[cl, pl.ds(q0, u_rows), :], k4_s[cl, pl.ds(kloc, nkeys), :],
                                     v4_s[cl, pl.ds(kloc, nkeys), :], jnp.where(qpos == 0, MASK_CLAMPED, MASK_FULL))
            r0 = pl.multiple_of(cl * cls + u * u_rows, u_rows)
            merge_store((pl.ds(r0, u_rows), pl.ds(sb + r0, u_rows)), m, l, acc)
            return c

        def unit_d16(idx, c):
            cl, sg = idx & (r4 - 1), idx >> 2
            qpos = sbi * u_rows
            kloc = sg + r4 * (qpos - span) - base4
            m, l, acc = softmax_unit(q4_s[cl, pl.ds(sg, u_rows, stride=r4), :],
                                     k4_s[cl, pl.ds(kloc, nkeys, stride=r4), :],
                                     v4_s[cl, pl.ds(kloc, nkeys, stride=r4), :],
                                     jnp.where(qpos == 0, MASK_BEFORE_START, MASK_FULL))
            r0 = cl * cls + sg
            merge_store((pl.ds(r0, u_rows, stride=r4), pl.ds(sb + r0, u_rows, stride=r4)), m, l, acc)
            return c

        for body in (unit_d1, unit_d4, unit_d16):
            lax.fori_loop(0, units, body, 0, unroll=8)

        for h2 in (0, 1):
            for cl in range(r4):
                for j0 in range(0, cls, piece):
                    rows = slice(h2 * sb + cl * cls + j0, h2 * sb + cl * cls + j0 + piece)
                    nat_s[h2, pl.ds(r4 * j0 + cl, piece, stride=r4), :] = acc_s[rows, :] / l_s[rows, :]
        lo2 = jnp.concatenate([lo, lo], axis=0)
        for c in range(sb // piece):
            ra = slice(c * piece, (c + 1) * piece)
            oa, ob = nat_s[0, ra, :], nat_s[1, ra, :]
            if half == 0:
                out = jnp.where(lo2, oa, pltpu.roll(ob, HEAD_DIM, 1))
            else:
                out = jnp.where(lo2, pltpu.roll(oa, HEAD_DIM, 1), ob)
            o_ref[0, ra, half * LANES:(half + 1) * LANES] = out.astype(BF16)


def _attention_window(q, k, v):
    b, l, _ = q.shape
    sb = ATT_SUPER
    max_dil = max(d for _, d in DILATED_BRANCHES)
    assert l % sb == 0 and l // max_dil >= ATT_UNIT + ATT_SPAN
    qspec = lambda h: pl.BlockSpec((1, sb, LANES), lambda bi, pi, si: (bi, si, 2 * pi + h))
    kspec = pl.BlockSpec((1, l, LANES), lambda bi, pi, si: (bi, 0, pi))
    return pl.pallas_call(
        _attn_window_kernel,
        grid=(b, KV_WIDTH // LANES, l // sb),
        in_specs=[qspec(0), qspec(1), kspec, kspec,
                  pl.BlockSpec((3, 2 * ATT_UNIT, ATT_UNIT + ATT_SPAN), lambda bi, pi, si: (0, 0, 0))],
        out_specs=pl.BlockSpec((1, sb, 2 * LANES), lambda bi, pi, si: (bi, si, pi)),
        out_shape=jax.ShapeDtypeStruct((b, l, ATT_WIDTH), BF16),
        scratch_shapes=[pltpu.VMEM((ATT_R4, (MAX_WINDOW + sb) // ATT_R4, LANES), F32)] * 2
        + [pltpu.VMEM((ATT_R4, sb // ATT_R4, LANES), F32)]
        + [pltpu.VMEM((2 * sb, LANES), F32)] * 3
        + [pltpu.VMEM((3, 2 * ATT_UNIT, LANES), F32), pltpu.VMEM((2, sb, LANES), F32)],
        compiler_params=_cparams(("parallel", "parallel", "arbitrary")),
        name="window_attn",
    )(q, q, k, v, _window_masks())


def _ssd_kernel(xbc_ref, dt_ref, z_ref, ssm0_ref, conv0_ref, cw_ref, cb_ref, dtb_ref, alog_ref, dsk_ref,
                nw_ref, ex_ref, y_ref, ssm_ref, conv_ref, xpad, dtpad, s_scr, *, n_valid):
    q = SSD_CHUNK
    lb = xbc_ref.shape[1]
    ci = pl.program_id(1)
    nc = pl.num_programs(1)
    gw = SSD_WIDTH // SSD_GROUPS
    hpg = SSD_HEADS // SSD_GROUPS

    @pl.when(ci == 0)
    def _():
        xpad[0:SUBLANES, :] = jnp.zeros((SUBLANES, CONV_DIM), F32)
        xpad[SUBLANES - (CONV_W - 1):SUBLANES, :] = conv0_ref[0]
        for g in range(SSD_GROUPS):
            s_scr[g] = jnp.transpose(ssm0_ref[0, g * hpg:(g + 1) * hpg].reshape(gw, SSD_STATE))

    xpad[SUBLANES:SUBLANES + lb, :] = xbc_ref[0]
    dtpad[0:lb, :] = dt_ref[0]
    if lb < q:
        xpad[SUBLANES + lb:SUBLANES + q, :] = jnp.zeros((q - lb, CONV_DIM), F32)
        dtpad[lb:q, :] = jnp.zeros((q - lb, LANES), F32)

    conv = cb_ref[...]
    for k in range(CONV_W):
        sh = CONV_W - 1 - k
        conv = conv + xpad[SUBLANES - sh:SUBLANES - sh + q, :] * cw_ref[k:k + 1, :]
    act = _silu(conv)

    @pl.when(ci == nc - 1)
    def _():
        conv_ref[0] = xpad[SUBLANES + n_valid - (CONV_W - 1):SUBLANES + n_valid, :]

    xpad[0:SUBLANES, :] = xpad[q:q + SUBLANES, :]

    xs = act[:, :SSD_WIDTH]
    row = lax.broadcasted_iota(I32, (q, LANES), 0)
    dtr = dtpad[...] + dtb_ref[...]
    dt = jnp.maximum(dtr, 0.0) + jnp.log(1.0 + jnp.exp(-jnp.abs(dtr)))
    dt = jnp.where(row < n_valid, dt, 0.0)
    a = -jnp.exp(alog_ref[...])
    da = dt * a
    r2 = lax.broadcasted_iota(I32, (q, q), 0)
    c2 = lax.broadcasted_iota(I32, (q, q), 1)
    causal = r2 >= c2
    a_cs = jnp.dot(causal.astype(F32), da, precision=HIGHEST, preferred_element_type=F32)
    a_cst = jnp.transpose(a_cs)
    a_last = a_cs[q - 1:q, :]
    ex = ex_ref[...]
    per_head = jnp.concatenate([dt, jnp.exp(a_cs), jnp.exp(a_last - a_cs),
                                jnp.broadcast_to(jnp.exp(a_last), (SUBLANES, LANES))], axis=0)
    t_hi = per_head.astype(BF16)
    rem = per_head - t_hi.astype(F32)
    t_mid = rem.astype(BF16)
    t_lo = (rem - t_mid.astype(F32)).astype(BF16)
    spread = (jnp.dot(t_hi, ex, preferred_element_type=F32) + jnp.dot(t_mid, ex, preferred_element_type=F32)
              + jnp.dot(t_lo, ex, preferred_element_type=F32))
    dt_x, ea_x, te_x, cd_x = spread[0:q], spread[q:2 * q], spread[2 * q:3 * q], spread[3 * q:3 * q + 1]
    xdt = xs * dt_x
    xdt_b = xdt.astype(BF16)
    xw_b = (xdt * te_x).astype(BF16)
    lo = lax.broadcasted_iota(I32, (q, LANES), 1) < SSD_HEAD_DIM

    ys = []
    for g in range(SSD_GROUPS):
        bm = act[:, SSD_WIDTH + g * SSD_STATE:SSD_WIDTH + (g + 1) * SSD_STATE]
        cm = act[:, SSD_WIDTH + (SSD_GROUPS + g) * SSD_STATE:SSD_WIDTH + (SSD_GROUPS + g + 1) * SSD_STATE]
        bm_b, cm_b = bm.astype(BF16), cm.astype(BF16)
        cb = lax.dot_general(cm_b, bm_b, (((1,), (1,)), ((), ())), preferred_element_type=F32)
        s_old = s_scr[g]
        y_off = jnp.dot(cm_b, s_old.astype(BF16), preferred_element_type=F32)
        for jp in range(hpg // 2):
            pair = g * (hpg // 2) + jp
            yp = []
            for hh in (0, 1):
                h = 2 * pair + hh
                seg = a_cs[:, h:h + 1] - a_cst[h:h + 1, :]
                dec = jnp.exp(jnp.where(causal, seg, NEG_BIG))
                mm = (cb * dec).astype(BF16)
                yp.append(jnp.dot(mm, xdt_b[:, pair * LANES:(pair + 1) * LANES], preferred_element_type=F32))
            ys.append(jnp.where(lo, yp[0], yp[1]) + y_off[:, jp * LANES:(jp + 1) * LANES]
                      * ea_x[:, pair * LANES:(pair + 1) * LANES])
        bmt_b = jnp.transpose(bm).astype(BF16)
        s_new = s_old * cd_x[:, g * gw:(g + 1) * gw] + jnp.dot(bmt_b, xw_b[:, g * gw:(g + 1) * gw],
                                                              preferred_element_type=F32)
        s_scr[g] = s_new

    y = jnp.concatenate(ys, axis=1) + dsk_ref[...] * xs

    @pl.when(ci == nc - 1)
    def _():
        for g in range(SSD_GROUPS):
            ssm_ref[0, g * hpg:(g + 1) * hpg] = jnp.transpose(s_scr[g]).reshape(hpg, SSD_HEAD_DIM, SSD_STATE)

    hg = y[:lb] * _silu(z_ref[0])
    outs = []
    for g in range(SSD_GROUPS):
        part = hg[:, g * gw:(g + 1) * gw]
        ms = jnp.mean(part * part, axis=1, keepdims=True)
        outs.append(part * lax.rsqrt(ms + NORM_EPS))
    y_ref[0] = (jnp.concatenate(outs, axis=1) * nw_ref[...]).astype(BF16)


def _ssd(xbc, dt_raw, z, ssm0, conv0, prm, lb, n_valid):
    b, l, _ = xbc.shape
    nc = l // lb
    cw, cbias, dtb, alog, dsk, nw, ex = prm
    full = lambda a: pl.BlockSpec(a.shape, lambda bi, ci: (0,) * a.ndim)
    seq = lambda w: pl.BlockSpec((1, lb, w), lambda bi, ci: (bi, ci, 0))
    return pl.pallas_call(
        functools.partial(_ssd_kernel, n_valid=n_valid),
        grid=(b, nc),
        in_specs=[seq(CONV_DIM), seq(LANES), seq(SSD_WIDTH),
                  pl.BlockSpec((1, SSD_HEADS, SSD_HEAD_DIM, SSD_STATE), lambda bi, ci: (bi, 0, 0, 0)),
                  pl.BlockSpec((1, CONV_W - 1, CONV_DIM), lambda bi, ci: (bi, 0, 0)),
                  full(cw), full(cbias), full(dtb), full(alog), full(dsk), full(nw), full(ex)],
        out_specs=(seq(SSD_WIDTH),
                   pl.BlockSpec((1, SSD_HEADS, SSD_HEAD_DIM, SSD_STATE), lambda bi, ci: (bi, 0, 0, 0)),
                   pl.BlockSpec((1, CONV_W - 1, CONV_DIM), lambda bi, ci: (bi, 0, 0))),
        out_shape=(jax.ShapeDtypeStruct((b, l, SSD_WIDTH), BF16),
                   jax.ShapeDtypeStruct((b, SSD_HEADS, SSD_HEAD_DIM, SSD_STATE), F32),
                   jax.ShapeDtypeStruct((b, CONV_W - 1, CONV_DIM), F32)),
        scratch_shapes=[pltpu.VMEM((SSD_CHUNK + 2 * SUBLANES, CONV_DIM), F32),
                        pltpu.VMEM((SSD_CHUNK, LANES), F32),
                        pltpu.VMEM((SSD_GROUPS, SSD_STATE, SSD_WIDTH // SSD_GROUPS), F32)],
        compiler_params=_cparams(("parallel", "arbitrary")),
        name="conv_ssd",
    )(xbc, dt_raw, z, ssm0, conv0, cw, cbias, dtb, alog, dsk, nw, ex)


def _layer_norm(r, g, b):
    mu = jnp.mean(r, axis=1, keepdims=True)
    d = r - mu
    var = jnp.mean(d * d, axis=1, keepdims=True)
    return d * lax.rsqrt(var + NORM_EPS) * g + b


def _mix_kernel(att_ref, ssd_ref, x_ref, wo_ref, g_ref, b_ref, *rest):
    h_ref, hs_ref = rest[-2:]
    tm = x_ref.shape[0]
    mix = jnp.dot(att_ref[...], wo_ref[0:ATT_WIDTH, :], preferred_element_type=F32)
    mix = mix + jnp.dot(ssd_ref[...], wo_ref[ATT_WIDTH:MIX_WIDTH, :], preferred_element_type=F32)
    h = _layer_norm(DEEPNORM_ALPHA * x_ref[...] + mix, g_ref[...], b_ref[...])
    h_ref[...] = h
    for j in range(ROW_SLAB):
        hs_ref[pl.ds(j, tm, stride=ROW_SLAB), :] = h[:, j * LANES:(j + 1) * LANES]


def _mix(att, ssd, x2d, wo_bf, g, b, tm, t_total, row0=0, into=None):
    t = x2d.shape[0]
    b0 = row0 // tm
    row = lambda w: pl.BlockSpec((tm, w), lambda i: (i, 0))
    vec = pl.BlockSpec((1, D_MODEL), lambda i: (0, 0))
    hbm = pl.BlockSpec(memory_space=pl.ANY)
    extra = () if into is None else tuple(into)
    return pl.pallas_call(
        _mix_kernel,
        grid=(t // tm,),
        in_specs=[row(ATT_WIDTH), row(SSD_WIDTH), row(D_MODEL),
                  pl.BlockSpec((MIX_WIDTH, D_MODEL), lambda i: (0, 0)), vec, vec] + [hbm] * len(extra),
        out_specs=(pl.BlockSpec((tm, D_MODEL), lambda i: (b0 + i, 0)),
                   pl.BlockSpec((tm * ROW_SLAB, LANES), lambda i: (b0 + i, 0))),
        out_shape=(jax.ShapeDtypeStruct((t_total, D_MODEL), F32),
                   jax.ShapeDtypeStruct((t_total * ROW_SLAB, LANES), F32)),
        input_output_aliases={} if into is None else {6: 0, 7: 1},
        compiler_params=_cparams(("parallel",)),
        name="outproj_ln1",
    )(att, ssd, x2d, wo_bf, g, b, *extra)


def _router_kernel(h_ref, wrh_ref, wrl_ref, bias_ref, tope_ref, rank_ref, gt_ref, cnt_ref, gscr):
    tm = h_ref.shape[0]
    i = pl.program_id(0)
    per_group = N_EXPERTS // N_EXPERT_GROUPS

    @pl.when(i == 0)
    def _():
        cnt_ref[...] = jnp.zeros_like(cnt_ref)

    h = h_ref[...]
    h_hi = h.astype(BF16)
    h_lo = (h - h_hi.astype(F32)).astype(BF16)
    nt = lambda a, b: lax.dot_general(a, b, (((1,), (1,)), ((), ())), preferred_element_type=F32)
    logits = nt(wrh_ref[...], h_hi) + nt(wrh_ref[...], h_lo) + nt(wrl_ref[...], h_hi)
    s = 1.0 / (1.0 + jnp.exp(-logits))
    sel = s + bias_ref[...]
    neg_inf = -jnp.inf
    e_iota = lax.broadcasted_iota(I32, (N_EXPERTS, tm), 0).astype(F32)
    g_iota = lax.broadcasted_iota(I32, (per_group, tm), 0).astype(F32)

    gscore = []
    for g in range(N_EXPERT_GROUPS):
        blk = sel[g * per_group:(g + 1) * per_group]
        m1 = jnp.max(blk, axis=0, keepdims=True)
        a1 = jnp.min(jnp.where(blk == m1, g_iota, float(per_group)), axis=0, keepdims=True)
        m2 = jnp.max(jnp.where(g_iota == a1, neg_inf, blk), axis=0, keepdims=True)
        gscore.append(m1 + m2)
    blocks = []
    for g in range(N_EXPERT_GROUPS):
        beaten = jnp.zeros((1, tm), F32)
        for o in range(N_EXPERT_GROUPS):
            if o < g:
                beaten = beaten + jnp.where(gscore[o] >= gscore[g], 1.0, 0.0)
            elif o > g:
                beaten = beaten + jnp.where(gscore[o] > gscore[g], 1.0, 0.0)
        blocks.append(jnp.where(beaten < float(TOPK_GROUPS), sel[g * per_group:(g + 1) * per_group], neg_inf))
    cand = jnp.concatenate(blocks, axis=0)

    tops, gsel = [], []
    onehot = jnp.zeros((N_EXPERTS, tm), F32)
    for _ in range(TOP_K):
        mx = jnp.max(cand, axis=0, keepdims=True)
        ix = jnp.min(jnp.where(cand == mx, e_iota, float(N_EXPERTS)), axis=0, keepdims=True)
        hit = e_iota == ix
        tops.append(ix)
        gsel.append(jnp.sum(jnp.where(hit, s, 0.0), axis=0, keepdims=True))
        onehot = jnp.where(hit, 1.0, onehot)
        cand = jnp.where(hit, neg_inf, cand)
    den = gsel[0]
    for k in range(1, TOP_K):
        den = den + gsel[k]

    t_r = lax.broadcasted_iota(I32, (tm, tm), 0)
    t_c = lax.broadcasted_iota(I32, (tm, tm), 1)
    before = (t_r < t_c).astype(BF16)
    oh_b = onehot.astype(BF16)
    base = cnt_ref[...] + jnp.dot(oh_b, before, preferred_element_type=F32)
    cnt_ref[...] = cnt_ref[...] + jnp.dot(oh_b, jnp.ones((tm, LANES), BF16), preferred_element_type=F32)

    gscr[...] = jnp.zeros_like(gscr)
    for k in range(TOP_K):
        tope_ref[k:k + 1, :] = tops[k].astype(I32)
        rank_ref[k:k + 1, :] = jnp.sum(jnp.where(e_iota == tops[k], base, 0.0), axis=0, keepdims=True).astype(I32)
        gscr[k:k + 1, :] = gsel[k] / den * ROUTED_SCALE
    gt_ref[...] = jnp.transpose(gscr[...])


def _router(h_all, wr_t, bias_b):
    t = h_all.shape[0]
    tm = ROUTER_TILE
    wr_hi = wr_t.astype(BF16)
    wr_lo = (wr_t - wr_hi.astype(F32)).astype(BF16)
    return pl.pallas_call(
        _router_kernel,
        grid=(t // tm,),
        in_specs=[pl.BlockSpec((tm, D_MODEL), lambda i: (i, 0)),
                  pl.BlockSpec((N_EXPERTS, D_MODEL), lambda i: (0, 0)),
                  pl.BlockSpec((N_EXPERTS, D_MODEL), lambda i: (0, 0)),
                  pl.BlockSpec((N_EXPERTS, tm), lambda i: (0, 0))],
        out_specs=(pl.BlockSpec((TOP_K, tm), lambda i: (0, i)),
                   pl.BlockSpec((TOP_K, tm), lambda i: (0, i)),
                   pl.BlockSpec((tm, LANES), lambda i: (i, 0)),
                   pl.BlockSpec((N_EXPERTS, LANES), lambda i: (0, 0))),
        out_shape=(jax.ShapeDtypeStruct((TOP_K, t), I32),
                   jax.ShapeDtypeStruct((TOP_K, t), I32),
                   jax.ShapeDtypeStruct((t, LANES), F32),
                   jax.ShapeDtypeStruct((N_EXPERTS, LANES), F32)),
        scratch_shapes=[pltpu.VMEM((tm, LANES), F32)],
        compiler_params=_cparams(("arbitrary",)),
        name="router",
    )(h_all, wr_hi, wr_lo, bias_b)


_PAD_SIZES = tuple(2 ** p for p in range(int(math.log2(EXPERT_BLOCK)) - 1, -1, -1))


def _slab(ref, row, n=1):
    return ref.at[pl.ds(pl.multiple_of(row * ROW_SLAB, ROW_SLAB), n * ROW_SLAB)]


def _padfill_kernel(pstart_ref, pcnt_ref, tail_ref, xs_ref, zero_scr, sem):
    zero_scr[...] = jnp.zeros_like(zero_scr)
    half = EXPERT_BLOCK // 2
    tail_copy = lambda j: pltpu.make_async_copy(_slab(zero_scr, 0, half), _slab(xs_ref, tail_ref[0] + j * half, half),
                                                sem)

    def tail_start(j, c):
        tail_copy(j).start()
        return c

    def tail_wait(j, c):
        tail_copy(j).wait()
        return c

    lax.fori_loop(0, tail_ref[1], tail_start, 0)

    def copies(e):
        base = pstart_ref[e]
        cnt = pcnt_ref[e]
        out = []
        for sz in _PAD_SIZES:
            out.append(((cnt & sz) != 0, pltpu.make_async_copy(_slab(zero_scr, 0, sz), _slab(xs_ref, base, sz), sem)))
            base = base + (cnt & sz)
        return out

    def start(e, c):
        for pred, cp in copies(e):
            @pl.when(pred)
            def _():
                cp.start()
        return c

    def wait(e, c):
        for pred, cp in copies(e):
            @pl.when(pred)
            def _():
                cp.wait()
        return c

    lax.fori_loop(0, N_EXPERTS, start, 0)
    lax.fori_loop(0, N_EXPERTS, wait, 0)
    lax.fori_loop(0, tail_ref[1], tail_wait, 0)


def _padfill(pad_start, pad_cnt, tail, n_rows):
    return pl.pallas_call(
        _padfill_kernel,
        grid_spec=pltpu.PrefetchScalarGridSpec(
            num_scalar_prefetch=3, grid=(1,), in_specs=[],
            out_specs=pl.BlockSpec(memory_space=pl.ANY),
            scratch_shapes=[pltpu.VMEM((EXPERT_BLOCK // 2 * ROW_SLAB, LANES), F32), pltpu.SemaphoreType.DMA]),
        out_shape=jax.ShapeDtypeStruct((n_rows * ROW_SLAB, LANES), F32),
        compiler_params=_cparams(("arbitrary",)),
        name="dispatch_padfill",
    )(pad_start, pad_cnt, tail)


def _dest_kernel(tope_ref, rank_ref, pstart_ref, dest_ref):
    tm = tope_ref.shape[1]
    e_iota = lax.broadcasted_iota(I32, (N_EXPERTS, tm), 0)
    ps = pstart_ref[...]
    for k in range(TOP_K):
        base = jnp.sum(jnp.where(e_iota == tope_ref[k:k + 1, :], ps, 0.0), axis=0, keepdims=True)
        dest_ref[k:k + 1, :] = base.astype(I32) + rank_ref[k:k + 1, :]


def _dest(top_e, rank, pstarts):
    t = top_e.shape[1]
    tm = DISPATCH_TILE
    blk = pl.BlockSpec((TOP_K, tm), lambda i: (0, i))
    return pl.pallas_call(
        _dest_kernel,
        grid=(t // tm,),
        in_specs=[blk, blk, pl.BlockSpec((N_EXPERTS, tm), lambda i: (0, 0))],
        out_specs=blk,
        out_shape=jax.ShapeDtypeStruct((TOP_K, t), I32),
        compiler_params=_cparams(("parallel",)),
        name="dispatch_dest",
    )(top_e, rank, jnp.broadcast_to(pstarts.astype(F32)[:, None], (N_EXPERTS, tm)))


def _dest_tiles(dest, tm):
    t = dest.shape[1]
    return jnp.transpose(dest.reshape(TOP_K, t // tm, tm), (1, 0, 2)).reshape(t // tm, TOP_K * tm)


def _dispatch_kernel(dest_ref, h_ref, xs_in_ref, xs_ref, idx_smem, isem, sem):
    del xs_in_ref
    i = pl.program_id(0)
    tm = DISPATCH_TILE
    cp = pltpu.make_async_copy(dest_ref.at[i], idx_smem, isem)
    cp.start()
    cp.wait()

    def row_copy(t, k):
        return pltpu.make_async_copy(_slab(h_ref, t), _slab(xs_ref, idx_smem[k * tm + t]), sem)

    def start(t, c):
        for k in range(TOP_K):
            row_copy(t, k).start(priority=k % 2)
        return c

    def wait(t, c):
        for k in range(TOP_K):
            row_copy(t, k).wait()
        return c

    lax.fori_loop(0, tm, start, 0)
    lax.fori_loop(0, tm, wait, 0)


def _dispatch(dest_tiles, h_slab, xs):
    tm = DISPATCH_TILE
    return pl.pallas_call(
        _dispatch_kernel,
        grid=(dest_tiles.shape[0],),
        in_specs=[pl.BlockSpec(dest_tiles.shape, lambda i: (0, 0)),
                  pl.BlockSpec((tm * ROW_SLAB, LANES), lambda i: (i, 0)),
                  pl.BlockSpec(memory_space=pl.ANY)],
        out_specs=pl.BlockSpec(memory_space=pl.ANY),
        out_shape=jax.ShapeDtypeStruct(xs.shape, xs.dtype),
        scratch_shapes=[pltpu.SMEM((tm * TOP_K,), I32), pltpu.SemaphoreType.DMA, pltpu.SemaphoreType.DMA],
        input_output_aliases={2: 0},
        compiler_params=_cparams(("arbitrary",)),
        name="dispatch_rows",
    )(dest_tiles, h_slab, xs)


def _expert_kernel(bi_ref, be_ref, nu_ref, x_ref, wg_ref, wu_ref, wd_ref, y_ref, wg_b, wu_b, wd_b):
    del bi_ref
    i = pl.program_id(0)

    @pl.when(i < nu_ref[0])
    def _():
        prev = be_ref[jnp.maximum(i - 1, 0)]

        @pl.when(jnp.logical_or(i == 0, be_ref[i] != prev))
        def _():
            wg_b[...] = wg_ref[0].astype(BF16)
            wu_b[...] = wu_ref[0].astype(BF16)
            wd_b[...] = wd_ref[0].astype(BF16)

        xb = jnp.concatenate([x_ref[pl.ds(c, EXPERT_BLOCK, stride=ROW_SLAB), :] for c in range(ROW_SLAB)],
                             axis=1).astype(BF16)
        g = jnp.dot(xb, wg_b[...], preferred_element_type=F32)
        u = jnp.dot(xb, wu_b[...], preferred_element_type=F32)
        a = (_silu(g) * u).astype(BF16)
        y = jnp.dot(a, wd_b[...], preferred_element_type=F32)
        for c in range(ROW_SLAB):
            y_ref[pl.ds(c, EXPERT_BLOCK, stride=ROW_SLAB), :] = y[:, c * LANES:(c + 1) * LANES]


def _experts(blk_idx, blk_exp, n_used, xs, wg, wu, wd):
    n_rows = xs.shape[0] // ROW_SLAB
    nb = n_rows // EXPERT_BLOCK
    rows = pl.BlockSpec((EXPERT_BLOCK * ROW_SLAB, LANES), lambda i, bi, be, nu: (bi[i], 0))
    return pl.pallas_call(
        _expert_kernel,
        grid_spec=pltpu.PrefetchScalarGridSpec(
            num_scalar_prefetch=3, grid=(nb,),
            in_specs=[rows,
                      pl.BlockSpec((1, D_MODEL, EXPERT_FF), lambda i, bi, be, nu: (be[i], 0, 0)),
                      pl.BlockSpec((1, D_MODEL, EXPERT_FF), lambda i, bi, be, nu: (be[i], 0, 0)),
                      pl.BlockSpec((1, EXPERT_FF, D_MODEL), lambda i, bi, be, nu: (be[i], 0, 0))],
            out_specs=rows,
            scratch_shapes=[pltpu.VMEM((D_MODEL, EXPERT_FF), BF16), pltpu.VMEM((D_MODEL, EXPERT_FF), BF16),
                            pltpu.VMEM((EXPERT_FF, D_MODEL), BF16)]),
        out_shape=jax.ShapeDtypeStruct(xs.shape, F32),
        input_output_aliases={3: 0},
        compiler_params=_cparams(("arbitrary",)),
        name="expert_ffn",
    )(blk_idx, blk_exp, n_used, xs, wg, wu, wd)


def _combine_kernel(dest_ref, h_ref, gt_ref, ys_ref, sg_ref, su_ref, sd_ref, g_ref, b_ref, o_ref, o2_ref,
                    idx0, idx1, gbuf, isem, sem, *, n_first):
    idx_smem = (idx0, idx1)
    i = pl.program_id(0)
    n = pl.num_programs(0)
    tm = h_ref.shape[0]

    def idx_copy(step, s):
        return pltpu.make_async_copy(dest_ref.at[step], idx_smem[s], isem.at[s])

    def row_copy(s, t, k, src_row):
        return pltpu.make_async_copy(_slab(ys_ref, src_row), _slab(gbuf.at[s, k], t), sem.at[s])

    def issue(s):
        def body(t, c):
            for k in range(TOP_K):
                row_copy(s, t, k, idx_smem[s][k * tm + t]).start(priority=k % 2)
            return c
        lax.fori_loop(0, tm, body, 0)

    def drain(s):
        def body(t, c):
            for k in range(TOP_K):
                row_copy(s, t, k, 0).wait()
            return c
        lax.fori_loop(0, tm, body, 0)

    @pl.when(i == 0)
    def _():
        first = idx_copy(0, 0)
        first.start()
        first.wait()
        issue(0)

        @pl.when(n > 1)
        def _():
            idx_copy(1, 1).start()

    def step(slot):
        nslot = 1 - slot

        @pl.when(i + 1 < n)
        def _():
            idx_copy(i + 1, nslot).wait()
            issue(nslot)

            @pl.when(i + 2 < n)
            def _():
                idx_copy(i + 2, slot).start()

        h = h_ref[...]
        hb = h.astype(BF16)
        g = jnp.dot(hb, sg_ref[...], preferred_element_type=F32)
        u = jnp.dot(hb, su_ref[...], preferred_element_type=F32)
        f = jnp.dot((_silu(g) * u).astype(BF16), sd_ref[...], preferred_element_type=F32)
        drain(slot)
        gt = gt_ref[...]
        cols = []
        for j in range(ROW_SLAB):
            fj = f[:, j * LANES:(j + 1) * LANES]
            for k in range(TOP_K):
                fj = fj + gbuf[slot, k, pl.ds(j, tm, stride=ROW_SLAB), :] * gt[:, k:k + 1]
            cols.append(fj)
        f = jnp.concatenate(cols, axis=1)
        out = _layer_norm(DEEPNORM_ALPHA * h + f, g_ref[...], b_ref[...])

        @pl.when(i < n_first)
        def _():
            o_ref[...] = out

        @pl.when(i == n_first)
        def _():
            o2_ref[...] = out

    for parity in (0, 1):
        pl.when((i & 1) == parity)(functools.partial(step, parity))


def _combine(dest_tiles, h_all, gates_t, ys, sg_bf, su_bf, sd_bf, g, b, n_first):
    t = h_all.shape[0]
    tm = COMBINE_TILE
    vec = pl.BlockSpec((1, D_MODEL), lambda i: (0, 0))
    return pl.pallas_call(
        functools.partial(_combine_kernel, n_first=n_first),
        grid=(t // tm,),
        in_specs=[pl.BlockSpec(dest_tiles.shape, lambda i: (0, 0)),
                  pl.BlockSpec((tm, D_MODEL), lambda i: (i, 0)),
                  pl.BlockSpec((tm, LANES), lambda i: (i, 0)),
                  pl.BlockSpec(memory_space=pl.ANY),
                  pl.BlockSpec((D_MODEL, SHARED_FF), lambda i: (0, 0)),
                  pl.BlockSpec((D_MODEL, SHARED_FF), lambda i: (0, 0)),
                  pl.BlockSpec((SHARED_FF, D_MODEL), lambda i: (0, 0)), vec, vec],
        out_specs=(pl.BlockSpec((tm, D_MODEL), lambda i: (jnp.minimum(i, n_first - 1), 0)),
                   pl.BlockSpec((tm, D_MODEL), lambda i: (0, 0))),
        out_shape=(jax.ShapeDtypeStruct((n_first * tm, D_MODEL), F32), jax.ShapeDtypeStruct((tm, D_MODEL), F32)),
        scratch_shapes=[pltpu.SMEM((tm * TOP_K,), I32), pltpu.SMEM((tm * TOP_K,), I32),
                        pltpu.VMEM((2, TOP_K, tm * ROW_SLAB, LANES), F32),
                        pltpu.SemaphoreType.DMA((2,)), pltpu.SemaphoreType.DMA((2,))],
        compiler_params=_cparams(("arbitrary",)),
        name="combine_ln2",
    )(dest_tiles, h_all, gates_t, ys, sg_bf, su_bf, sd_bf, g, b)


def _moe(h_all, h_slab, n_first, wr, rbias, wg, wu, wd, sg, su, sd, g2, b2):
    t = h_all.shape[0]
    top_e, rank, gates_t, cnt = _router(h_all, jnp.transpose(wr),
                                        jnp.broadcast_to(rbias[:, None], (N_EXPERTS, ROUTER_TILE)))
    counts = cnt[:, 0].astype(I32)
    pcounts = (counts + EXPERT_BLOCK - 1) // EXPERT_BLOCK * EXPERT_BLOCK
    pends = jnp.cumsum(pcounts)
    pstarts = pends - pcounts
    n_rows = (t * TOP_K // EXPERT_BLOCK + N_EXPERTS - 1) * EXPERT_BLOCK
    nb = n_rows // EXPERT_BLOCK
    n_used = (pends[-1] // EXPERT_BLOCK).astype(I32)
    blk_idx = jnp.minimum(jnp.arange(nb, dtype=I32), n_used - 1)
    blk_exp = jnp.minimum(jnp.sum((pends[None, :] <= (blk_idx * EXPERT_BLOCK)[:, None]).astype(I32), axis=1),
                          N_EXPERTS - 1)
    dest = _dest(top_e, rank, pstarts)
    tail = jnp.stack([pends[-1], (n_rows - pends[-1]) // (EXPERT_BLOCK // 2)]).astype(I32)
    xs = _padfill((pstarts + counts).astype(I32), (pcounts - counts).astype(I32), tail, n_rows)
    xs = _dispatch(_dest_tiles(dest, DISPATCH_TILE), h_slab, xs)
    ys = _experts(blk_idx, blk_exp, n_used.reshape(1), xs, wg, wu, wd)
    return _combine(_dest_tiles(dest, COMBINE_TILE), h_all, gates_t, ys,
                    sg.astype(BF16), su.astype(BF16), sd.astype(BF16), g2[None, :], b2[None, :], n_first)


def _expand_matrix():
    h = np.arange(LANES)[:, None]
    c = np.arange(SSD_WIDTH)[None, :]
    return jnp.asarray((c // SSD_HEAD_DIM == h).astype(np.float32), dtype=BF16)


def kernel(x_prompt, x_sample, cache_k, cache_v, state_ssm, state_conv, w_in, conv_w, conv_b, dt_bias, a_log, d_skip, ssd_norm_w, w_out, ln1_g, ln1_b, w_router, router_bias, w_exp_gate, w_exp_up, w_exp_down, w_sh_gate, w_sh_up, w_sh_down, ln2_g, ln2_b):
    bp, lp, _ = x_prompt.shape
    bs, ls, _ = x_sample.shape
    win = cache_k.shape[2]
    keep = min(MAX_WINDOW, lp)
    assert lp % SSD_CHUNK == 0 and ls <= SUBLANES and win % KEY_TILE == 0 and win >= MAX_WINDOW

    w_bf = jnp.pad(w_in[0], ((0, 0), (0, IN_COLS_PAD - IN_COLS))).astype(BF16)
    wo_bf = w_out[0].astype(BF16)
    pad_l = lambda v: jnp.pad(v, (0, LANES - v.shape[0]))[None, :]
    ssd_prm = (conv_w[0], conv_b[0][None, :], pad_l(dt_bias[0]), pad_l(a_log[0]),
               jnp.repeat(d_skip[0], SSD_HEAD_DIM)[None, :], ssd_norm_w[0][None, :], _expand_matrix())

    tp = bp * lp
    tm_p = 512 if lp % 512 == 0 else SSD_CHUNK
    tm_mix = 512 if tp % 512 == 0 else SSD_CHUNK
    cos_p, sin_p = _rope_tables(jnp.arange(lp, dtype=F32))
    xp2 = x_prompt.reshape(tp, D_MODEL)
    q, kf, vf, z, xbc, dtr = _inproj(xp2, w_bf, cos_p, sin_p, tm_p, lp // tm_p)
    att = _attention_window(q.reshape(bp, lp, ATT_WIDTH), kf.reshape(bp, lp, KV_WIDTH), vf.reshape(bp, lp, KV_WIDTH))
    ssd_y, ssm_p, conv_p = _ssd(xbc.reshape(bp, lp, CONV_DIM), dtr.reshape(bp, lp, LANES),
                                z.reshape(bp, lp, SSD_WIDTH),
                                jnp.zeros((bp, SSD_HEADS, SSD_HEAD_DIM, SSD_STATE), F32),
                                jnp.zeros((bp, CONV_W - 1, CONV_DIM), F32), ssd_prm, SSD_CHUNK, SSD_CHUNK)
    ts = bs * ls
    t_pad = -(-(tp + ts) // DISPATCH_TILE) * DISPATCH_TILE
    tail = t_pad - tp
    assert tp % tail == 0 and ts <= COMBINE_TILE and tp % COMBINE_TILE == 0
    h_bufs = _mix(att.reshape(tp, ATT_WIDTH), ssd_y.reshape(tp, SSD_WIDTH), xp2, wo_bf, ln1_g, ln1_b, tm_mix, t_pad)

    pos_s = (PAST_LEN + jnp.arange(ls, dtype=jnp.int32)).astype(F32)
    cos_s, sin_s = _rope_tables(jnp.tile(pos_s, bs))
    xs2 = x_sample.reshape(ts, D_MODEL)
    q_s, kf_s, vf_s, z_s, xbc_s, dtr_s = _inproj(xs2, w_bf, cos_s, sin_s, ts, 1)
    rows8 = lambda a, w: jnp.pad(a.reshape(bs, ls, w), ((0, 0), (0, SUBLANES - ls), (0, 0)))
    att_s = _attention_decode(rows8(q_s, ATT_WIDTH), cache_k[0].reshape(bs, win, KV_WIDTH),
                              cache_v[0].reshape(bs, win, KV_WIDTH), rows8(kf_s, KV_WIDTH),
                              rows8(vf_s, KV_WIDTH))[:, :ls]
    ssd_s, ssm_s, conv_s = _ssd(rows8(xbc_s, CONV_DIM), rows8(dtr_s, LANES), rows8(z_s, SSD_WIDTH),
                                state_ssm[0], state_conv[0], ssd_prm, SUBLANES, ls)
    tail_rows = lambda a: jnp.pad(a, ((0, tail - ts), (0, 0)))
    h_all, h_slab = _mix(tail_rows(att_s.reshape(ts, ATT_WIDTH)), tail_rows(ssd_s[:, :ls].reshape(ts, SSD_WIDTH)),
                         tail_rows(xs2), wo_bf, ln1_g, ln1_b, tail, t_pad, row0=tp, into=h_bufs)

    y_p, y_s = _moe(h_all, h_slab, tp // COMBINE_TILE, w_router[0], router_bias[0], w_exp_gate[0], w_exp_up[0],
                    w_exp_down[0], w_sh_gate[0], w_sh_up[0], w_sh_down[0], ln2_g[0], ln2_b[0])

    kv5 = lambda a, b, l: a.reshape(1, b, l, N_KV_HEADS, HEAD_DIM)
    return (y_p.reshape(bp, lp, D_MODEL), y_s[:ts].reshape(bs, ls, D_MODEL),
            kv5(kf.reshape(bp, lp, KV_WIDTH)[:, lp - keep:], bp, keep),
            kv5(vf.reshape(bp, lp, KV_WIDTH)[:, lp - keep:], bp, keep),
            ssm_p[None], conv_p[None],
            kv5(kf_s, bs, ls), kv5(vf_s, bs, ls), ssm_s[None], conv_s[None])
```

```python
import functools
import math

import jax
import jax.numpy as jnp
import numpy as np
from jax import lax
from jax.experimental import pallas as pl
from jax.experimental.pallas import tpu as pltpu

F32 = jnp.float32
BF16 = jnp.bfloat16
I32 = jnp.int32

D_MODEL = 1024
PAST_LEN = 16384
HEAD_DIM = 64
N_ATT_HEADS = 16
N_KV_HEADS = 8
ATT_WIDTH = N_ATT_HEADS * HEAD_DIM
KV_WIDTH = N_KV_HEADS * HEAD_DIM
DILATED_BRANCHES = ((128, 1), (512, 4), (2048, 16))
MAX_WINDOW = 2048
ROPE_THETA = 10000.0
SSD_HEADS = 16
SSD_HEAD_DIM = 64
SSD_WIDTH = SSD_HEADS * SSD_HEAD_DIM
SSD_GROUPS = 2
SSD_STATE = 128
SSD_CHUNK = 128
CONV_W = 4
CONV_DIM = SSD_WIDTH + 2 * SSD_GROUPS * SSD_STATE
MIX_WIDTH = ATT_WIDTH + SSD_WIDTH
IN_COLS = ATT_WIDTH + 2 * KV_WIDTH + SSD_WIDTH + CONV_DIM + SSD_HEADS
N_EXPERTS = 256
TOP_K = 8
N_EXPERT_GROUPS = 8
TOPK_GROUPS = 4
EXPERT_FF = 256
SHARED_FF = 256
ROUTED_SCALE = 2.5
DEPTH = 1
DEEPNORM_ALPHA = (2.0 * DEPTH) ** 0.25
NORM_EPS = 1e-5

LANES = 128
SUBLANES = 8
VMEM_LIMIT = 56 * 1024 * 1024

IN_COLS_PAD = ATT_WIDTH + 2 * KV_WIDTH + SSD_WIDTH + CONV_DIM + LANES
KEY_TILE = 128
EXPERT_BLOCK = 512
ROUTER_TILE = 128
DISPATCH_TILE = 256
COMBINE_TILE = 128
ROW_SLAB = D_MODEL // LANES
NEG_BIG = -1e30
HIGHEST = lax.Precision.HIGHEST


def _cparams(sem):
    return pltpu.CompilerParams(dimension_semantics=sem, vmem_limit_bytes=VMEM_LIMIT)


def _silu(x):
    return x * (1.0 / (1.0 + jnp.exp(-x)))


def _inproj_kernel(x_ref, w_ref, cq_ref, sq_ref, ck_ref, sk_ref,
                   q_ref, kf_ref, vf_ref, z_ref, xbc_ref, dt_ref):
    tm = x_ref.shape[0]
    xb = x_ref[...].astype(BF16)
    lane = lax.broadcasted_iota(I32, (tm, LANES), 1)
    first_half = (lane % HEAD_DIM) < (HEAD_DIM // 2)

    def rope(a, c, s):
        partner = jnp.where(first_half, pltpu.roll(a, LANES - HEAD_DIM // 2, 1), pltpu.roll(a, HEAD_DIM // 2, 1))
        return a * c + partner * s

    c0 = 0
    acc = jnp.dot(xb, w_ref[:, c0:c0 + ATT_WIDTH], preferred_element_type=F32)
    cq, sq = cq_ref[...], sq_ref[...]
    for j in range(ATT_WIDTH // LANES):
        q_ref[:, j * LANES:(j + 1) * LANES] = rope(acc[:, j * LANES:(j + 1) * LANES], cq, sq)
    c0 += ATT_WIDTH
    acc = jnp.dot(xb, w_ref[:, c0:c0 + KV_WIDTH], preferred_element_type=F32)
    ck, sk = ck_ref[...], sk_ref[...]
    for j in range(KV_WIDTH // LANES):
        kf_ref[:, j * LANES:(j + 1) * LANES] = rope(acc[:, j * LANES:(j + 1) * LANES], ck, sk)
    c0 += KV_WIDTH
    vf_ref[...] = jnp.dot(xb, w_ref[:, c0:c0 + KV_WIDTH], preferred_element_type=F32)
    c0 += KV_WIDTH
    z_ref[...] = jnp.dot(xb, w_ref[:, c0:c0 + SSD_WIDTH], preferred_element_type=F32).astype(BF16)
    c0 += SSD_WIDTH
    xbc_ref[...] = jnp.dot(xb, w_ref[:, c0:c0 + CONV_DIM], preferred_element_type=F32)
    c0 += CONV_DIM
    dt_ref[...] = jnp.dot(xb, w_ref[:, c0:c0 + LANES], preferred_element_type=F32)


def _rope_tables(pos):
    half = HEAD_DIM // 2
    inv = ROPE_THETA ** (-jnp.arange(half, dtype=F32) / half)
    ang = pos[:, None] * inv[None, :]
    cos, sin = jnp.cos(ang), jnp.sin(ang)
    c = jnp.concatenate([cos, cos, cos, cos], axis=1)
    s = jnp.concatenate([-sin, sin, -sin, sin], axis=1)
    return c, s


def _inproj(x2d, w_bf, cos_t, sin_t, tm, tiles_per_seq):
    t = x2d.shape[0]
    scale = HEAD_DIM ** -0.5
    tab = pl.BlockSpec((tm, LANES), lambda i: (i % tiles_per_seq, 0))
    row = lambda w: pl.BlockSpec((tm, w), lambda i: (i, 0))
    outs = (
        jax.ShapeDtypeStruct((t, ATT_WIDTH), F32),
        jax.ShapeDtypeStruct((t, KV_WIDTH), F32),
        jax.ShapeDtypeStruct((t, KV_WIDTH), F32),
        jax.ShapeDtypeStruct((t, SSD_WIDTH), BF16),
        jax.ShapeDtypeStruct((t, CONV_DIM), F32),
        jax.ShapeDtypeStruct((t, LANES), F32),
    )
    return pl.pallas_call(
        _inproj_kernel,
        grid=(t // tm,),
        in_specs=[row(D_MODEL), pl.BlockSpec((D_MODEL, IN_COLS_PAD), lambda i: (0, 0)), tab, tab, tab, tab],
        out_specs=(row(ATT_WIDTH), row(KV_WIDTH), row(KV_WIDTH), row(SSD_WIDTH), row(CONV_DIM), row(LANES)),
        out_shape=outs,
        compiler_params=_cparams(("parallel",)),
        name="inproj",
    )(x2d, w_bf, cos_t * scale, sin_t * scale, cos_t, sin_t)


def _branch_weight(d):
    w = np.zeros(d.shape, np.float32)
    for window, dil in DILATED_BRANCHES:
        w += ((d >= 0) & (d <= window) & (d % dil == 0)).astype(np.float32)
    return w


def _attn_decode_kernel(q_ref, kc_ref, vc_ref, kn_ref, vn_ref, wc_ref, wn_ref, o_ref, ktail, vtail):
    nq = q_ref.shape[1]
    lane = lax.broadcasted_iota(I32, (nq, LANES), 1)
    lo = lane < HEAD_DIM
    ktail[...] = jnp.zeros_like(ktail)
    vtail[...] = jnp.zeros_like(vtail)
    ktail[0:nq, :] = kn_ref[0]
    vtail[0:nq, :] = vn_ref[0]
    wc, wn = wc_ref[...], wn_ref[...]
    nt = (((1,), (1,)), ((), ()))
    for g in range(N_KV_HEADS):
        pair, half = divmod(g, 2)
        cols = slice(pair * LANES, (pair + 1) * LANES)
        q = q_ref[0, :, g * LANES:(g + 1) * LANES]
        qs = pltpu.roll(q, HEAD_DIM, 1)
        valid = lo if half == 0 else jnp.logical_not(lo)
        qa_src, qb_src = (q, qs) if half == 0 else (qs, q)
        qq = jnp.concatenate([jnp.where(valid, qa_src, 0.0), jnp.where(valid, qb_src, 0.0)], axis=0).astype(BF16)
        s_c = lax.dot_general(qq, kc_ref[0, :, cols].astype(BF16), nt, preferred_element_type=F32)
        s_n = lax.dot_general(qq, ktail[:, cols].astype(BF16), nt, preferred_element_type=F32)
        s_c = jnp.where(wc > 0.0, s_c, NEG_BIG)
        s_n = jnp.where(wn > 0.0, s_n, NEG_BIG)
        m = jnp.maximum(jnp.max(s_c, axis=1, keepdims=True), jnp.max(s_n, axis=1, keepdims=True))
        p_c = jnp.exp(s_c - m) * wc
        p_n = jnp.exp(s_n - m) * wn
        l = jnp.sum(p_c, axis=1, keepdims=True) + jnp.sum(p_n, axis=1, keepdims=True)
        acc = jnp.dot(p_c.astype(BF16), vc_ref[0, :, cols].astype(BF16), preferred_element_type=F32)
        acc = acc + jnp.dot(p_n.astype(BF16), vtail[:, cols].astype(BF16), preferred_element_type=F32)
        o = acc / l
        oa, ob = o[:nq], o[nq:]
        if half == 0:
            out = jnp.where(lo, oa, pltpu.roll(ob, HEAD_DIM, 1))
        else:
            out = jnp.where(lo, pltpu.roll(oa, HEAD_DIM, 1), ob)
        o_ref[0, :, g * LANES:(g + 1) * LANES] = out.astype(BF16)


def _attention_decode(q, k_cache, v_cache, k_new, v_new):
    b, nq, _ = q.shape
    win = k_cache.shape[1]
    i = np.arange(nq)[:, None]
    wc = _branch_weight(win + i - np.arange(win)[None, :])
    wn = _branch_weight(i - np.arange(KEY_TILE)[None, :])
    wn[:, nq:] = 0.0
    stack = lambda w: jnp.asarray(np.concatenate([w, w], axis=0))
    seq = lambda rows, w: pl.BlockSpec((1, rows, w), lambda bi: (bi, 0, 0))
    full = lambda shp: pl.BlockSpec(shp, lambda bi: (0, 0))
    return pl.pallas_call(
        _attn_decode_kernel,
        grid=(b,),
        in_specs=[seq(nq, ATT_WIDTH), seq(win, KV_WIDTH), seq(win, KV_WIDTH), seq(nq, KV_WIDTH), seq(nq, KV_WIDTH),
                  full((2 * nq, win)), full((2 * nq, KEY_TILE))],
        out_specs=seq(nq, ATT_WIDTH),
        out_shape=jax.ShapeDtypeStruct((b, nq, ATT_WIDTH), BF16),
        scratch_shapes=[pltpu.VMEM((KEY_TILE, KV_WIDTH), F32), pltpu.VMEM((KEY_TILE, KV_WIDTH), F32)],
        compiler_params=_cparams(("arbitrary",)),
        name="decode_attn",
    )(q, k_cache, v_cache, k_new, v_new, stack(wc), stack(wn))


ATT_SUPER = 2048
ATT_UNIT = 128
ATT_SPAN = max(w // d for w, d in DILATED_BRANCHES)


ATT_R4 = 4


MASK_FULL, MASK_CLAMPED, MASK_BEFORE_START = 0, 1, 2


def _window_masks():
    i = (np.arange(2 * ATT_UNIT) % ATT_UNIT)[:, None]
    c = np.arange(ATT_UNIT + ATT_SPAN)[None, :]
    ok = lambda d: (d >= 0) & (d <= ATT_SPAN)
    keep = np.stack([ok(ATT_SPAN + i - c), ok(i - c), ok(ATT_SPAN + i - c) & (c >= ATT_SPAN)])
    return jnp.asarray(np.where(keep, 0.0, NEG_BIG).astype(np.float32))


def _attn_window_kernel(q0_ref, q1_ref, k_ref, v_ref, mask_ref, o_ref,
                        k4_s, v4_s, q4_s, acc_s, m_s, l_s, tmp_s, nat_s):
    sb = q0_ref.shape[1]
    sbi = pl.program_id(2)
    u_rows, span, r4 = ATT_UNIT, ATT_SPAN, ATT_R4
    nkeys = u_rows + span
    cls = sb // r4
    look = MAX_WINDOW // r4
    units = sb // u_rows
    assert [d for _, d in DILATED_BRANCHES] == [1, r4, r4 * r4] and sb // (r4 * r4) == u_rows
    lane = lax.broadcasted_iota(I32, (u_rows, LANES), 1)
    lo = lane < HEAD_DIM
    piece = 2 * u_rows

    def regroup(src_ref, dst, nat0, loc0, n):
        for c in range(r4):
            for j0 in range(0, n, piece):
                dst[c, loc0 + j0:loc0 + j0 + piece, :] = src_ref[0, pl.ds(nat0 + r4 * j0 + c, piece, stride=r4), :]

    @pl.when(sbi > 0)
    def _():
        regroup(k_ref, k4_s, sbi * sb - look * r4, 0, look)
        regroup(v_ref, v4_s, sbi * sb - look * r4, 0, look)

    @pl.when(sbi == 0)
    def _():
        k4_s[:, 0:look, :] = jnp.zeros((r4, look, LANES), F32)
        v4_s[:, 0:look, :] = jnp.zeros((r4, look, LANES), F32)

    regroup(k_ref, k4_s, sbi * sb, look, cls)
    regroup(v_ref, v4_s, sbi * sb, look, cls)
    base4 = sbi * cls - look

    for half, qh_ref in enumerate((q0_ref, q1_ref)):
        valid = lo if half == 0 else jnp.logical_not(lo)
        regroup(qh_ref, q4_s, 0, 0, cls)

        def softmax_unit(q, kt, vt, pattern):
            qs = pltpu.roll(q, HEAD_DIM, 1)
            qa_src, qb_src = (q, qs) if half == 0 else (qs, q)
            qq = jnp.concatenate([jnp.where(valid, qa_src, 0.0), jnp.where(valid, qb_src, 0.0)],
                                 axis=0).astype(BF16)
            s = lax.dot_general(qq, kt.astype(BF16), (((1,), (1,)), ((), ())), preferred_element_type=F32)
            s = s + mask_ref[pattern]
            m = jnp.max(s, axis=1, keepdims=True)
            p = jnp.exp(s - m)
            l = jnp.sum(p, axis=1, keepdims=True)
            acc = jnp.dot(p.astype(BF16), vt.astype(BF16), preferred_element_type=F32)
            return jnp.broadcast_to(m, (2 * u_rows, LANES)), jnp.broadcast_to(l, (2 * u_rows, LANES)), acc

        def merge_store(rows, m, l, acc):
            ld = lambda ref: jnp.concatenate([ref[rows[0], :], ref[rows[1], :]], axis=0)
            m_old, l_old, acc_old = ld(m_s), ld(l_s), ld(acc_s)
            m_new = jnp.maximum(m_old, m)
            a_old, a_new = jnp.exp(m_old - m_new), jnp.exp(m - m_new)
            l = a_old * l_old + a_new * l
            acc = a_old * acc_old + a_new * acc
            for h2 in (0, 1):
                sl = slice(h2 * u_rows, (h2 + 1) * u_rows)
                m_s[rows[h2], :] = m_new[sl]
                l_s[rows[h2], :] = l[sl]
                acc_s[rows[h2], :] = acc[sl]

        def unit_d1(u, c):
            q0 = pl.multiple_of(u * u_rows, u_rows)
            qpos = sbi * sb + u * u_rows
            kpos = pl.multiple_of(jnp.maximum(qpos - span, 0), u_rows)
            m, l, acc = softmax_unit(qh_ref[0, pl.ds(q0, u_rows), :], k_ref[0, pl.ds(kpos, nkeys), :],
                                     v_ref[0, pl.ds(kpos, nkeys), :], jnp.where(qpos == 0, MASK_CLAMPED, MASK_FULL))
            part = u_rows // r4
            for a, (val, dst) in enumerate(((m, m_s), (l, l_s), (acc, acc_s))):
                tmp_s[a] = val
                for h2 in (0, 1):
                    for cl in range(r4):
                        d0 = pl.multiple_of(h2 * sb + cl * cls + u * part, part)
                        dst[pl.ds(d0, part), :] = tmp_s[a, pl.ds(h2 * u_rows + cl, part, stride=r4), :]
            return c

        def unit_d4(idx, c):
            cl, u = idx & (r4 - 1), idx >> 2
            qpos = sbi * cls + u * u_rows
            kpos = jnp.maximum(qpos - span, 0)
            kloc = pl.multiple_of(kpos - base4, u_rows)
            q0 = pl.multiple_of(u * u_rows, u_rows)
            m, l, acc = softmax_unit(q4_s[cl, pl.ds(q0, u_rows), :], k4_s[cl, pl.ds(kloc, nkeys), :],
                                     v4_s[cl, pl.ds(kloc, nkeys), :], jnp.where(qpos == 0, MASK_CLAMPED, MASK_FULL))
            r0 = pl.multiple_of(cl * cls + u * u_rows, u_rows)
            merge_store((pl.ds(r0, u_rows), pl.ds(sb + r0, u_rows)), m, l, acc)
            return c

        def unit_d16(idx, c):
            cl, sg = idx & (r4 - 1), idx >> 2
            qpos = sbi * u_rows
            kloc = sg + r4 * (qpos - span) - base4
            m, l, acc = softmax_unit(q4_s[cl, pl.ds(sg, u_rows, stride=r4), :],
                                     k4_s[cl, pl.ds(kloc, nkeys, stride=r4), :],
                                     v4_s[cl, pl.ds(kloc, nkeys, stride=r4), :],
                                     jnp.where(qpos == 0, MASK_BEFORE_START, MASK_FULL))
            r0 = cl * cls + sg
            merge_store((pl.ds(r0, u_rows, stride=r4), pl.ds(sb + r0, u_rows, stride=r4)), m, l, acc)
            return c

        for body in (unit_d1, unit_d4, unit_d16):
            lax.fori_loop(0, units, body, 0, unroll=8)

        for h2 in (0, 1):
            for cl in range(r4):
                for j0 in range(0, cls, piece):
                    rows = slice(h2 * sb + cl * cls + j0, h2 * sb + cl * cls + j0 + piece)
                    nat_s[h2, pl.ds(r4 * j0 + cl, piece, stride=r4), :] = acc_s[rows, :] / l_s[rows, :]
        lo2 = jnp.concatenate([lo, lo], axis=0)
        for c in range(sb // piece):
            ra = slice(c * piece, (c + 1) * piece)
            oa, ob = nat_s[0, ra, :], nat_s[1, ra, :]
            if half == 0:
                out = jnp.where(lo2, oa, pltpu.roll(ob, HEAD_DIM, 1))
            else:
                out = jnp.where(lo2, pltpu.roll(oa, HEAD_DIM, 1), ob)
            o_ref[0, ra, half * LANES:(half + 1) * LANES] = out.astype(BF16)


def _attention_window(q, k, v):
    b, l, _ = q.shape
    sb = ATT_SUPER
    max_dil = max(d for _, d in DILATED_BRANCHES)
    assert l % sb == 0 and l // max_dil >= ATT_UNIT + ATT_SPAN
    qspec = lambda h: pl.BlockSpec((1, sb, LANES), lambda bi, pi, si: (bi, si, 2 * pi + h))
    kspec = pl.BlockSpec((1, l, LANES), lambda bi, pi, si: (bi, 0, pi))
    return pl.pallas_call(
        _attn_window_kernel,
        grid=(b, KV_WIDTH // LANES, l // sb),
        in_specs=[qspec(0), qspec(1), kspec, kspec,
                  pl.BlockSpec((3, 2 * ATT_UNIT, ATT_UNIT + ATT_SPAN), lambda bi, pi, si: (0, 0, 0))],
        out_specs=pl.BlockSpec((1, sb, 2 * LANES), lambda bi, pi, si: (bi, si, pi)),
        out_shape=jax.ShapeDtypeStruct((b, l, ATT_WIDTH), BF16),
        scratch_shapes=[pltpu.VMEM((ATT_R4, (MAX_WINDOW + sb) // ATT_R4, LANES), F32)] * 2
        + [pltpu.VMEM((ATT_R4, sb // ATT_R4, LANES), F32)]
        + [pltpu.VMEM((2 * sb, LANES), F32)] * 3
        + [pltpu.VMEM((3, 2 * ATT_UNIT, LANES), F32), pltpu.VMEM((2, sb, LANES), F32)],
        compiler_params=_cparams(("parallel", "parallel", "arbitrary")),
        name="window_attn",
    )(q, q, k, v, _window_masks())


def _ssd_kernel(xbc_ref, dt_ref, z_ref, ssm0_ref, conv0_ref, cw_ref, cb_ref, dtb_ref, alog_ref, dsk_ref,
                nw_ref, ex_ref, y_ref, ssm_ref, conv_ref, xpad, dtpad, s_scr, *, n_valid):
    q = SSD_CHUNK
    lb = xbc_ref.shape[1]
    ci = pl.program_id(1)
    nc = pl.num_programs(1)
    gw = SSD_WIDTH // SSD_GROUPS
    hpg = SSD_HEADS // SSD_GROUPS

    @pl.when(ci == 0)
    def _():
        xpad[0:SUBLANES, :] = jnp.zeros((SUBLANES, CONV_DIM), F32)
        xpad[SUBLANES - (CONV_W - 1):SUBLANES, :] = conv0_ref[0]
        for g in range(SSD_GROUPS):
            s_scr[g] = jnp.transpose(ssm0_ref[0, g * hpg:(g + 1) * hpg].reshape(gw, SSD_STATE))

    xpad[SUBLANES:SUBLANES + lb, :] = xbc_ref[0]
    dtpad[0:lb, :] = dt_ref[0]
    if lb < q:
        xpad[SUBLANES + lb:SUBLANES + q, :] = jnp.zeros((q - lb, CONV_DIM), F32)
        dtpad[lb:q, :] = jnp.zeros((q - lb, LANES), F32)

    conv = cb_ref[...]
    for k in range(CONV_W):
        sh = CONV_W - 1 - k
        conv = conv + xpad[SUBLANES - sh:SUBLANES - sh + q, :] * cw_ref[k:k + 1, :]
    act = _silu(conv)

    @pl.when(ci == nc - 1)
    def _():
        conv_ref[0] = xpad[SUBLANES + n_valid - (CONV_W - 1):SUBLANES + n_valid, :]

    xpad[0:SUBLANES, :] = xpad[q:q + SUBLANES, :]

    xs = act[:, :SSD_WIDTH]
    row = lax.broadcasted_iota(I32, (q, LANES), 0)
    dtr = dtpad[...] + dtb_ref[...]
    dt = jnp.maximum(dtr, 0.0) + jnp.log(1.0 + jnp.exp(-jnp.abs(dtr)))
    dt = jnp.where(row < n_valid, dt, 0.0)
    a = -jnp.exp(alog_ref[...])
    da = dt * a
    r2 = lax.broadcasted_iota(I32, (q, q), 0)
    c2 = lax.broadcasted_iota(I32, (q, q), 1)
    causal = r2 >= c2
    a_cs = jnp.dot(causal.astype(F32), da, precision=HIGHEST, preferred_element_type=F32)
    a_cst = jnp.transpose(a_cs)
    a_last = a_cs[q - 1:q, :]
    ex = ex_ref[...]
    per_head = jnp.concatenate([dt, jnp.exp(a_cs), jnp.exp(a_last - a_cs),
                                jnp.broadcast_to(jnp.exp(a_last), (SUBLANES, LANES))], axis=0)
    t_hi = per_head.astype(BF16)
    rem = per_head - t_hi.astype(F32)
    t_mid = rem.astype(BF16)
    t_lo = (rem - t_mid.astype(F32)).astype(BF16)
    spread = (jnp.dot(t_hi, ex, preferred_element_type=F32) + jnp.dot(t_mid, ex, preferred_element_type=F32)
              + jnp.dot(t_lo, ex, preferred_element_type=F32))
    dt_x, ea_x, te_x, cd_x = spread[0:q], spread[q:2 * q], spread[2 * q:3 * q], spread[3 * q:3 * q + 1]
    xdt = xs * dt_x
    xdt_b = xdt.astype(BF16)
    xw_b = (xdt * te_x).astype(BF16)
    lo = lax.broadcasted_iota(I32, (q, LANES), 1) < SSD_HEAD_DIM

    ys = []
    for g in range(SSD_GROUPS):
        bm = act[:, SSD_WIDTH + g * SSD_STATE:SSD_WIDTH + (g + 1) * SSD_STATE]
        cm = act[:, SSD_WIDTH + (SSD_GROUPS + g) * SSD_STATE:SSD_WIDTH + (SSD_GROUPS + g + 1) * SSD_STATE]
        bm_b, cm_b = bm.astype(BF16), cm.astype(BF16)
        cb = lax.dot_general(cm_b, bm_b, (((1,), (1,)), ((), ())), preferred_element_type=F32)
        s_old = s_scr[g]
        y_off = jnp.dot(cm_b, s_old.astype(BF16), preferred_element_type=F32)
        for jp in range(hpg // 2):
            pair = g * (hpg // 2) + jp
            yp = []
            for hh in (0, 1):
                h = 2 * pair + hh
                seg = a_cs[:, h:h + 1] - a_cst[h:h + 1, :]
                dec = jnp.exp(jnp.where(causal, seg, NEG_BIG))
                mm = (cb * dec).astype(BF16)
                yp.append(jnp.dot(mm, xdt_b[:, pair * LANES:(pair + 1) * LANES], preferred_element_type=F32))
            ys.append(jnp.where(lo, yp[0], yp[1]) + y_off[:, jp * LANES:(jp + 1) * LANES]
                      * ea_x[:, pair * LANES:(pair + 1) * LANES])
        bmt_b = jnp.transpose(bm).astype(BF16)
        s_new = s_old * cd_x[:, g * gw:(g + 1) * gw] + jnp.dot(bmt_b, xw_b[:, g * gw:(g + 1) * gw],
                                                              preferred_element_type=F32)
        s_scr[g] = s_new

    y = jnp.concatenate(ys, axis=1) + dsk_ref[...] * xs

    @pl.when(ci == nc - 1)
    def _():
        for g in range(SSD_GROUPS):
            ssm_ref[0, g * hpg:(g + 1) * hpg] = jnp.transpose(s_scr[g]).reshape(hpg, SSD_HEAD_DIM, SSD_STATE)

    hg = y[:lb] * _silu(z_ref[0].astype(F32))
    outs = []
    for g in range(SSD_GROUPS):
        part = hg[:, g * gw:(g + 1) * gw]
        ms = jnp.mean(part * part, axis=1, keepdims=True)
        outs.append(part * lax.rsqrt(ms + NORM_EPS))
    y_ref[0] = (jnp.concatenate(outs, axis=1) * nw_ref[...]).astype(BF16)


def _ssd(xbc, dt_raw, z, ssm0, conv0, prm, lb, n_valid):
    b, l, _ = xbc.shape
    nc = l // lb
    cw, cbias, dtb, alog, dsk, nw, ex = prm
    full = lambda a: pl.BlockSpec(a.shape, lambda bi, ci: (0,) * a.ndim)
    seq = lambda w: pl.BlockSpec((1, lb, w), lambda bi, ci: (bi, ci, 0))
    return pl.pallas_call(
        functools.partial(_ssd_kernel, n_valid=n_valid),
        grid=(b, nc),
        in_specs=[seq(CONV_DIM), seq(LANES), seq(SSD_WIDTH),
                  pl.BlockSpec((1, SSD_HEADS, SSD_HEAD_DIM, SSD_STATE), lambda bi, ci: (bi, 0, 0, 0)),
                  pl.BlockSpec((1, CONV_W - 1, CONV_DIM), lambda bi, ci: (bi, 0, 0)),
                  full(cw), full(cbias), full(dtb), full(alog), full(dsk), full(nw), full(ex)],
        out_specs=(seq(SSD_WIDTH),
                   pl.BlockSpec((1, SSD_HEADS, SSD_HEAD_DIM, SSD_STATE), lambda bi, ci: (bi, 0, 0, 0)),
                   pl.BlockSpec((1, CONV_W - 1, CONV_DIM), lambda bi, ci: (bi, 0, 0))),
        out_shape=(jax.ShapeDtypeStruct((b, l, SSD_WIDTH), BF16),
                   jax.ShapeDtypeStruct((b, SSD_HEADS, SSD_HEAD_DIM, SSD_STATE), F32),
                   jax.ShapeDtypeStruct((b, CONV_W - 1, CONV_DIM), F32)),
        scratch_shapes=[pltpu.VMEM((SSD_CHUNK + 2 * SUBLANES, CONV_DIM), F32),
                        pltpu.VMEM((SSD_CHUNK, LANES), F32),
                        pltpu.VMEM((SSD_GROUPS, SSD_STATE, SSD_WIDTH // SSD_GROUPS), F32)],
        compiler_params=_cparams(("parallel", "arbitrary")),
        name="conv_ssd",
    )(xbc, dt_raw, z, ssm0, conv0, cw, cbias, dtb, alog, dsk, nw, ex)


def _layer_norm(r, g, b):
    mu = jnp.mean(r, axis=1, keepdims=True)
    d = r - mu
    var = jnp.mean(d * d, axis=1, keepdims=True)
    return d * lax.rsqrt(var + NORM_EPS) * g + b


def _mix_kernel(att_ref, ssd_ref, x_ref, wo_ref, g_ref, b_ref, *rest):
    h_ref, hs_ref = rest[-2:]
    tm = x_ref.shape[0]
    mix = jnp.dot(att_ref[...], wo_ref[0:ATT_WIDTH, :], preferred_element_type=F32)
    mix = mix + jnp.dot(ssd_ref[...], wo_ref[ATT_WIDTH:MIX_WIDTH, :], preferred_element_type=F32)
    h = _layer_norm(DEEPNORM_ALPHA * x_ref[...] + mix, g_ref[...], b_ref[...])
    h_ref[...] = h
    for j in range(ROW_SLAB):
        hs_ref[pl.ds(j, tm, stride=ROW_SLAB), :] = h[:, j * LANES:(j + 1) * LANES]


def _mix(att, ssd, x2d, wo_bf, g, b, tm, t_total, row0=0, into=None):
    t = x2d.shape[0]
    b0 = row0 // tm
    row = lambda w: pl.BlockSpec((tm, w), lambda i: (i, 0))
    vec = pl.BlockSpec((1, D_MODEL), lambda i: (0, 0))
    hbm = pl.BlockSpec(memory_space=pl.ANY)
    extra = () if into is None else tuple(into)
    return pl.pallas_call(
        _mix_kernel,
        grid=(t // tm,),
        in_specs=[row(ATT_WIDTH), row(SSD_WIDTH), row(D_MODEL),
                  pl.BlockSpec((MIX_WIDTH, D_MODEL), lambda i: (0, 0)), vec, vec] + [hbm] * len(extra),
        out_specs=(pl.BlockSpec((tm, D_MODEL), lambda i: (b0 + i, 0)),
                   pl.BlockSpec((tm * ROW_SLAB, LANES), lambda i: (b0 + i, 0))),
        out_shape=(jax.ShapeDtypeStruct((t_total, D_MODEL), F32),
                   jax.ShapeDtypeStruct((t_total * ROW_SLAB, LANES), F32)),
        input_output_aliases={} if into is None else {6: 0, 7: 1},
        compiler_params=_cparams(("parallel",)),
        name="outproj_ln1",
    )(att, ssd, x2d, wo_bf, g, b, *extra)


def _router_kernel(h_ref, wrh_ref, wrl_ref, bias_ref, tope_ref, rank_ref, gt_ref, cnt_ref, gscr):
    tm = h_ref.shape[0]
    i = pl.program_id(0)
    per_group = N_EXPERTS // N_EXPERT_GROUPS

    @pl.when(i == 0)
    def _():
        cnt_ref[...] = jnp.zeros_like(cnt_ref)

    h = h_ref[...]
    h_hi = h.astype(BF16)
    h_lo = (h - h_hi.astype(F32)).astype(BF16)
    nt = lambda a, b: lax.dot_general(a, b, (((1,), (1,)), ((), ())), preferred_element_type=F32)
    logits = nt(wrh_ref[...], h_hi) + nt(wrh_ref[...], h_lo) + nt(wrl_ref[...], h_hi)
    s = 1.0 / (1.0 + jnp.exp(-logits))
    sel = s + bias_ref[...]
    neg_inf = -jnp.inf
    e_iota = lax.broadcasted_iota(I32, (N_EXPERTS, tm), 0).astype(F32)
    g_iota = lax.broadcasted_iota(I32, (per_group, tm), 0).astype(F32)

    gscore = []
    for g in range(N_EXPERT_GROUPS):
        blk = sel[g * per_group:(g + 1) * per_group]
        m1 = jnp.max(blk, axis=0, keepdims=True)
        a1 = jnp.min(jnp.where(blk == m1, g_iota, float(per_group)), axis=0, keepdims=True)
        m2 = jnp.max(jnp.where(g_iota == a1, neg_inf, blk), axis=0, keepdims=True)
        gscore.append(m1 + m2)
    blocks = []
    for g in range(N_EXPERT_GROUPS):
        beaten = jnp.zeros((1, tm), F32)
        for o in range(N_EXPERT_GROUPS):
            if o < g:
                beaten = beaten + jnp.where(gscore[o] >= gscore[g], 1.0, 0.0)
            elif o > g:
                beaten = beaten + jnp.where(gscore[o] > gscore[g], 1.0, 0.0)
        blocks.append(jnp.where(beaten < float(TOPK_GROUPS), sel[g * per_group:(g + 1) * per_group], neg_inf))
    cand = jnp.concatenate(blocks, axis=0)

    tops, gsel = [], []
    onehot = jnp.zeros((N_EXPERTS, tm), F32)
    for _ in range(TOP_K):
        mx = jnp.max(cand, axis=0, keepdims=True)
        ix = jnp.min(jnp.where(cand == mx, e_iota, float(N_EXPERTS)), axis=0, keepdims=True)
        hit = e_iota == ix
        tops.append(ix)
        gsel.append(jnp.sum(jnp.where(hit, s, 0.0), axis=0, keepdims=True))
        onehot = jnp.where(hit, 1.0, onehot)
        cand = jnp.where(hit, neg_inf, cand)
    den = gsel[0]
    for k in range(1, TOP_K):
        den = den + gsel[k]

    t_r = lax.broadcasted_iota(I32, (tm, tm), 0)
    t_c = lax.broadcasted_iota(I32, (tm, tm), 1)
    before = (t_r < t_c).astype(BF16)
    oh_b = onehot.astype(BF16)
    base = cnt_ref[...] + jnp.dot(oh_b, before, preferred_element_type=F32)
    cnt_ref[...] = cnt_ref[...] + jnp.dot(oh_b, jnp.ones((tm, LANES), BF16), preferred_element_type=F32)

    gscr[...] = jnp.zeros_like(gscr)
    for k in range(TOP_K):
        tope_ref[k:k + 1, :] = tops[k].astype(I32)
        rank_ref[k:k + 1, :] = jnp.sum(jnp.where(e_iota == tops[k], base, 0.0), axis=0, keepdims=True).astype(I32)
        gscr[k:k + 1, :] = gsel[k] / den * ROUTED_SCALE
    gt_ref[...] = jnp.transpose(gscr[...])


def _router(h_all, wr_t, bias_b):
    t = h_all.shape[0]
    tm = ROUTER_TILE
    wr_hi = wr_t.astype(BF16)
    wr_lo = (wr_t - wr_hi.astype(F32)).astype(BF16)
    return pl.pallas_call(
        _router_kernel,
        grid=(t // tm,),
        in_specs=[pl.BlockSpec((tm, D_MODEL), lambda i: (i, 0)),
                  pl.BlockSpec((N_EXPERTS, D_MODEL), lambda i: (0, 0)),
                  pl.BlockSpec((N_EXPERTS, D_MODEL), lambda i: (0, 0)),
                  pl.BlockSpec((N_EXPERTS, tm), lambda i: (0, 0))],
        out_specs=(pl.BlockSpec((TOP_K, tm), lambda i: (0, i)),
                   pl.BlockSpec((TOP_K, tm), lambda i: (0, i)),
                   pl.BlockSpec((tm, LANES), lambda i: (i, 0)),
                   pl.BlockSpec((N_EXPERTS, LANES), lambda i: (0, 0))),
        out_shape=(jax.ShapeDtypeStruct((TOP_K, t), I32),
                   jax.ShapeDtypeStruct((TOP_K, t), I32),
                   jax.ShapeDtypeStruct((t, LANES), F32),
                   jax.ShapeDtypeStruct((N_EXPERTS, LANES), F32)),
        scratch_shapes=[pltpu.VMEM((tm, LANES), F32)],
        compiler_params=_cparams(("arbitrary",)),
        name="router",
    )(h_all, wr_hi, wr_lo, bias_b)


_PAD_SIZES = tuple(2 ** p for p in range(int(math.log2(EXPERT_BLOCK)) - 1, -1, -1))


def _slab(ref, row, n=1):
    return ref.at[pl.ds(pl.multiple_of(row * ROW_SLAB, ROW_SLAB), n * ROW_SLAB)]


def _padfill_kernel(pstart_ref, pcnt_ref, tail_ref, xs_ref, zero_scr, sem):
    zero_scr[...] = jnp.zeros_like(zero_scr)
    half = EXPERT_BLOCK // 2
    tail_copy = lambda j: pltpu.make_async_copy(_slab(zero_scr, 0, half), _slab(xs_ref, tail_ref[0] + j * half, half),
                                                sem)

    def tail_start(j, c):
        tail_copy(j).start()
        return c

    def tail_wait(j, c):
        tail_copy(j).wait()
        return c

    lax.fori_loop(0, tail_ref[1], tail_start, 0)

    def copies(e):
        base = pstart_ref[e]
        cnt = pcnt_ref[e]
        out = []
        for sz in _PAD_SIZES:
            out.append(((cnt & sz) != 0, pltpu.make_async_copy(_slab(zero_scr, 0, sz), _slab(xs_ref, base, sz), sem)))
            base = base + (cnt & sz)
        return out

    def start(e, c):
        for pred, cp in copies(e):
            @pl.when(pred)
            def _():
                cp.start()
        return c

    def wait(e, c):
        for pred, cp in copies(e):
            @pl.when(pred)
            def _():
                cp.wait()
        return c

    lax.fori_loop(0, N_EXPERTS, start, 0)
    lax.fori_loop(0, N_EXPERTS, wait, 0)
    lax.fori_loop(0, tail_ref[1], tail_wait, 0)


def _padfill(pad_start, pad_cnt, tail, n_rows):
    return pl.pallas_call(
        _padfill_kernel,
        grid_spec=pltpu.PrefetchScalarGridSpec(
            num_scalar_prefetch=3, grid=(1,), in_specs=[],
            out_specs=pl.BlockSpec(memory_space=pl.ANY),
            scratch_shapes=[pltpu.VMEM((EXPERT_BLOCK // 2 * ROW_SLAB, LANES), F32), pltpu.SemaphoreType.DMA]),
        out_shape=jax.ShapeDtypeStruct((n_rows * ROW_SLAB, LANES), F32),
        compiler_params=_cparams(("arbitrary",)),
        name="dispatch_padfill",
    )(pad_start, pad_cnt, tail)


def _dest_kernel(tope_ref, rank_ref, pstart_ref, dest_ref):
    tm = tope_ref.shape[1]
    e_iota = lax.broadcasted_iota(I32, (N_EXPERTS, tm), 0)
    ps = pstart_ref[...]
    for k in range(TOP_K):
        base = jnp.sum(jnp.where(e_iota == tope_ref[k:k + 1, :], ps, 0.0), axis=0, keepdims=True)
        dest_ref[k:k + 1, :] = base.astype(I32) + rank_ref[k:k + 1, :]


def _dest(top_e, rank, pstarts):
    t = top_e.shape[1]
    tm = max(m * LANES for m in range(1, 9) if t % (m * LANES) == 0)
    blk = pl.BlockSpec((TOP_K, tm), lambda i: (0, i))
    return pl.pallas_call(
        _dest_kernel,
        grid=(t // tm,),
        in_specs=[blk, blk, pl.BlockSpec((N_EXPERTS, tm), lambda i: (0, 0))],
        out_specs=blk,
        out_shape=jax.ShapeDtypeStruct((TOP_K, t), I32),
        compiler_params=_cparams(("parallel",)),
        name="dispatch_dest",
    )(top_e, rank, jnp.broadcast_to(pstarts.astype(F32)[:, None], (N_EXPERTS, tm)))


def _dest_tiles(dest, tm):
    t = dest.shape[1]
    return jnp.transpose(dest.reshape(TOP_K, t // tm, tm), (1, 0, 2)).reshape(t // tm, TOP_K * tm)


def _dispatch_kernel(dest_ref, h_ref, xs_in_ref, xs_ref, idx_smem, isem, sem):
    del xs_in_ref
    i = pl.program_id(0)
    tm = DISPATCH_TILE
    cp = pltpu.make_async_copy(dest_ref.at[i], idx_smem, isem)
    cp.start()
    cp.wait()

    def row_copy(t, k):
        return pltpu.make_async_copy(_slab(h_ref, t), _slab(xs_ref, idx_smem[k * tm + t]), sem)

    def start(t, c):
        for k in range(TOP_K):
            row_copy(t, k).start(priority=k % 2)
        return c

    def wait(t, c):
        for k in range(TOP_K):
            row_copy(t, k).wait()
        return c

    lax.fori_loop(0, tm, start, 0)
    lax.fori_loop(0, tm, wait, 0)


def _dispatch(dest_tiles, h_slab, xs):
    tm = DISPATCH_TILE
    return pl.pallas_call(
        _dispatch_kernel,
        grid=(dest_tiles.shape[0],),
        in_specs=[pl.BlockSpec(dest_tiles.shape, lambda i: (0, 0)),
                  pl.BlockSpec((tm * ROW_SLAB, LANES), lambda i: (i, 0)),
                  pl.BlockSpec(memory_space=pl.ANY)],
        out_specs=pl.BlockSpec(memory_space=pl.ANY),
        out_shape=jax.ShapeDtypeStruct(xs.shape, xs.dtype),
        scratch_shapes=[pltpu.SMEM((tm * TOP_K,), I32), pltpu.SemaphoreType.DMA, pltpu.SemaphoreType.DMA],
        input_output_aliases={2: 0},
        compiler_params=_cparams(("arbitrary",)),
        name="dispatch_rows",
    )(dest_tiles, h_slab, xs)


def _expert_kernel(bi_ref, be_ref, nu_ref, x_ref, wg_ref, wu_ref, wd_ref, y_ref, wg_b, wu_b, wd_b):
    del bi_ref
    i = pl.program_id(0)

    @pl.when(i < nu_ref[0])
    def _():
        prev = be_ref[jnp.maximum(i - 1, 0)]

        @pl.when(jnp.logical_or(i == 0, be_ref[i] != prev))
        def _():
            wg_b[...] = wg_ref[0].astype(BF16)
            wu_b[...] = wu_ref[0].astype(BF16)
            wd_b[...] = wd_ref[0].astype(BF16)

        xb = jnp.concatenate([x_ref[pl.ds(c, EXPERT_BLOCK, stride=ROW_SLAB), :] for c in range(ROW_SLAB)],
                             axis=1).astype(BF16)
        g = jnp.dot(xb, wg_b[...], preferred_element_type=F32)
        u = jnp.dot(xb, wu_b[...], preferred_element_type=F32)
        a = (_silu(g) * u).astype(BF16)
        y = jnp.dot(a, wd_b[...], preferred_element_type=F32)
        for c in range(ROW_SLAB):
            y_ref[pl.ds(c, EXPERT_BLOCK, stride=ROW_SLAB), :] = y[:, c * LANES:(c + 1) * LANES]


def _experts(blk_idx, blk_exp, n_used, xs, wg, wu, wd):
    n_rows = xs.shape[0] // ROW_SLAB
    nb = n_rows // EXPERT_BLOCK
    rows = pl.BlockSpec((EXPERT_BLOCK * ROW_SLAB, LANES), lambda i, bi, be, nu: (bi[i], 0))
    return pl.pallas_call(
        _expert_kernel,
        grid_spec=pltpu.PrefetchScalarGridSpec(
            num_scalar_prefetch=3, grid=(nb,),
            in_specs=[rows,
                      pl.BlockSpec((1, D_MODEL, EXPERT_FF), lambda i, bi, be, nu: (be[i], 0, 0)),
                      pl.BlockSpec((1, D_MODEL, EXPERT_FF), lambda i, bi, be, nu: (be[i], 0, 0)),
                      pl.BlockSpec((1, EXPERT_FF, D_MODEL), lambda i, bi, be, nu: (be[i], 0, 0))],
            out_specs=rows,
            scratch_shapes=[pltpu.VMEM((D_MODEL, EXPERT_FF), BF16), pltpu.VMEM((D_MODEL, EXPERT_FF), BF16),
                            pltpu.VMEM((EXPERT_FF, D_MODEL), BF16)]),
        out_shape=jax.ShapeDtypeStruct(xs.shape, F32),
        input_output_aliases={3: 0},
        compiler_params=_cparams(("arbitrary",)),
        name="expert_ffn",
    )(blk_idx, blk_exp, n_used, xs, wg, wu, wd)


def _combine_kernel(dest_ref, h_ref, gt_ref, ys_ref, sg_ref, su_ref, sd_ref, g_ref, b_ref, o_ref, o2_ref,
                    idx0, idx1, gbuf, isem, sem, *, n_first):
    idx_smem = (idx0, idx1)
    i = pl.program_id(0)
    n = pl.num_programs(0)
    tm = h_ref.shape[0]

    def idx_copy(step, s):
        return pltpu.make_async_copy(dest_ref.at[step], idx_smem[s], isem.at[s])

    def row_copy(s, t, k, src_row):
        return pltpu.make_async_copy(_slab(ys_ref, src_row), _slab(gbuf.at[s, k], t), sem.at[s])

    def issue(s):
        def body(t, c):
            for k in range(TOP_K):
                row_copy(s, t, k, idx_smem[s][k * tm + t]).start(priority=k % 2)
            return c
        lax.fori_loop(0, tm, body, 0)

    def drain(s):
        def body(t, c):
            for k in range(TOP_K):
                row_copy(s, t, k, 0).wait()
            return c
        lax.fori_loop(0, tm, body, 0)

    @pl.when(i == 0)
    def _():
        first = idx_copy(0, 0)
        first.start()
        first.wait()
        issue(0)

        @pl.when(n > 1)
        def _():
            idx_copy(1, 1).start()

    def step(slot):
        nslot = 1 - slot

        @pl.when(i + 1 < n)
        def _():
            idx_copy(i + 1, nslot).wait()
            issue(nslot)

            @pl.when(i + 2 < n)
            def _():
                idx_copy(i + 2, slot).start()

        h = h_ref[...]
        hb = h.astype(BF16)
        g = jnp.dot(hb, sg_ref[...], preferred_element_type=F32)
        u = jnp.dot(hb, su_ref[...], preferred_element_type=F32)
        f = jnp.dot((_silu(g) * u).astype(BF16), sd_ref[...], preferred_element_type=F32)
        drain(slot)
        gt = gt_ref[...]
        cols = []
        for j in range(ROW_SLAB):
            fj = f[:, j * LANES:(j + 1) * LANES]
            for k in range(TOP_K):
                fj = fj + gbuf[slot, k, pl.ds(j, tm, stride=ROW_SLAB), :] * gt[:, k:k + 1]
            cols.append(fj)
        f = jnp.concatenate(cols, axis=1)
        out = _layer_norm(DEEPNORM_ALPHA * h + f, g_ref[...], b_ref[...])

        @pl.when(i < n_first)
        def _():
            o_ref[...] = out

        @pl.when(i == n_first)
        def _():
            o2_ref[...] = out

    for parity in (0, 1):
        pl.when((i & 1) == parity)(functools.partial(step, parity))


def _combine(dest_tiles, h_all, gates_t, ys, sg_bf, su_bf, sd_bf, g, b, n_first):
    t = h_all.shape[0]
    tm = COMBINE_TILE
    vec = pl.BlockSpec((1, D_MODEL), lambda i: (0, 0))
    return pl.pallas_call(
        functools.partial(_combine_kernel, n_first=n_first),
        grid=(t // tm,),
        in_specs=[pl.BlockSpec(dest_tiles.shape, lambda i: (0, 0)),
                  pl.BlockSpec((tm, D_MODEL), lambda i: (i, 0)),
                  pl.BlockSpec((tm, LANES), lambda i: (i, 0)),
                  pl.BlockSpec(memory_space=pl.ANY),
                  pl.BlockSpec((D_MODEL, SHARED_FF), lambda i: (0, 0)),
                  pl.BlockSpec((D_MODEL, SHARED_FF), lambda i: (0, 0)),
                  pl.BlockSpec((SHARED_FF, D_MODEL), lambda i: (0, 0)), vec, vec],
        out_specs=(pl.BlockSpec((tm, D_MODEL), lambda i: (jnp.minimum(i, n_first - 1), 0)),
                   pl.BlockSpec((tm, D_MODEL), lambda i: (0, 0))),
        out_shape=(jax.ShapeDtypeStruct((n_first * tm, D_MODEL), F32), jax.ShapeDtypeStruct((tm, D_MODEL), F32)),
        scratch_shapes=[pltpu.SMEM((tm * TOP_K,), I32), pltpu.SMEM((tm * TOP_K,), I32),
                        pltpu.VMEM((2, TOP_K, tm * ROW_SLAB, LANES), F32),
                        pltpu.SemaphoreType.DMA((2,)), pltpu.SemaphoreType.DMA((2,))],
        compiler_params=_cparams(("arbitrary",)),
        name="combine_ln2",
    )(dest_tiles, h_all, gates_t, ys, sg_bf, su_bf, sd_bf, g, b)


def _moe(h_all, h_slab, n_first, wr, rbias, wg, wu, wd, sg, su, sd, g2, b2):
    t = h_all.shape[0]
    top_e, rank, gates_t, cnt = _router(h_all, jnp.transpose(wr),
                                        jnp.broadcast_to(rbias[:, None], (N_EXPERTS, ROUTER_TILE)))
    counts = cnt[:, 0].astype(I32)
    pcounts = (counts + EXPERT_BLOCK - 1) // EXPERT_BLOCK * EXPERT_BLOCK
    pends = jnp.cumsum(pcounts)
    pstarts = pends - pcounts
    n_rows = (t * TOP_K // EXPERT_BLOCK + N_EXPERTS - 1) * EXPERT_BLOCK
    nb = n_rows // EXPERT_BLOCK
    n_used = (pends[-1] // EXPERT_BLOCK).astype(I32)
    blk_idx = jnp.minimum(jnp.arange(nb, dtype=I32), n_used - 1)
    blk_exp = jnp.minimum(jnp.sum((pends[None, :] <= (blk_idx * EXPERT_BLOCK)[:, None]).astype(I32), axis=1),
                          N_EXPERTS - 1)
    dest = _dest(top_e, rank, pstarts)
    tail = jnp.stack([pends[-1], (n_rows - pends[-1]) // (EXPERT_BLOCK // 2)]).astype(I32)
    xs = _padfill((pstarts + counts).astype(I32), (pcounts - counts).astype(I32), tail, n_rows)
    xs = _dispatch(_dest_tiles(dest, DISPATCH_TILE), h_slab, xs)
    ys = _experts(blk_idx, blk_exp, n_used.reshape(1), xs, wg, wu, wd)
    return _combine(_dest_tiles(dest, COMBINE_TILE), h_all, gates_t, ys,
                    sg.astype(BF16), su.astype(BF16), sd.astype(BF16), g2[None, :], b2[None, :], n_first)


def _expand_matrix():
    h = np.arange(LANES)[:, None]
    c = np.arange(SSD_WIDTH)[None, :]
    return jnp.asarray((c // SSD_HEAD_DIM == h).astype(np.float32), dtype=BF16)


def kernel(x_prompt, x_sample, cache_k, cache_v, state_ssm, state_conv, w_in, conv_w, conv_b, dt_bias, a_log, d_skip, ssd_norm_w, w_out, ln1_g, ln1_b, w_router, router_bias, w_exp_gate, w_exp_up, w_exp_down, w_sh_gate, w_sh_up, w_sh_down, ln2_g, ln2_b):
    bp, lp, _ = x_prompt.shape
    bs, ls, _ = x_sample.shape
    win = cache_k.shape[2]
    keep = min(MAX_WINDOW, lp)
    assert lp % SSD_CHUNK == 0 and ls <= SUBLANES and win % KEY_TILE == 0 and win >= MAX_WINDOW

    w_bf = jnp.pad(w_in[0], ((0, 0), (0, IN_COLS_PAD - IN_COLS))).astype(BF16)
    wo_bf = w_out[0].astype(BF16)
    pad_l = lambda v: jnp.pad(v, (0, LANES - v.shape[0]))[None, :]
    ssd_prm = (conv_w[0], conv_b[0][None, :], pad_l(dt_bias[0]), pad_l(a_log[0]),
               jnp.repeat(d_skip[0], SSD_HEAD_DIM)[None, :], ssd_norm_w[0][None, :], _expand_matrix())

    tp = bp * lp
    tm_p = 512 if lp % 512 == 0 else SSD_CHUNK
    tm_mix = 512 if tp % 512 == 0 else SSD_CHUNK
    cos_p, sin_p = _rope_tables(jnp.arange(lp, dtype=F32))
    xp2 = x_prompt.reshape(tp, D_MODEL)
    q, kf, vf, z, xbc, dtr = _inproj(xp2, w_bf, cos_p, sin_p, tm_p, lp // tm_p)
    att = _attention_window(q.reshape(bp, lp, ATT_WIDTH), kf.reshape(bp, lp, KV_WIDTH), vf.reshape(bp, lp, KV_WIDTH))
    ssd_y, ssm_p, conv_p = _ssd(xbc.reshape(bp, lp, CONV_DIM), dtr.reshape(bp, lp, LANES),
                                z.reshape(bp, lp, SSD_WIDTH),
                                jnp.zeros((bp, SSD_HEADS, SSD_HEAD_DIM, SSD_STATE), F32),
                                jnp.zeros((bp, CONV_W - 1, CONV_DIM), F32), ssd_prm, SSD_CHUNK, SSD_CHUNK)
    ts = bs * ls
    t_pad = -(-(tp + ts) // DISPATCH_TILE) * DISPATCH_TILE
    tail = t_pad - tp
    assert tp % tail == 0 and ts <= COMBINE_TILE and tp % COMBINE_TILE == 0
    h_bufs = _mix(att.reshape(tp, ATT_WIDTH), ssd_y.reshape(tp, SSD_WIDTH), xp2, wo_bf, ln1_g, ln1_b, tm_mix, t_pad)

    pos_s = (PAST_LEN + jnp.arange(ls, dtype=jnp.int32)).astype(F32)
    cos_s, sin_s = _rope_tables(jnp.tile(pos_s, bs))
    xs2 = x_sample.reshape(ts, D_MODEL)
    q_s, kf_s, vf_s, z_s, xbc_s, dtr_s = _inproj(xs2, w_bf, cos_s, sin_s, ts, 1)
    rows8 = lambda a, w: jnp.pad(a.reshape(bs, ls, w), ((0, 0), (0, SUBLANES - ls), (0, 0)))
    att_s = _attention_decode(rows8(q_s, ATT_WIDTH), cache_k[0].reshape(bs, win, KV_WIDTH),
                              cache_v[0].reshape(bs, win, KV_WIDTH), rows8(kf_s, KV_WIDTH),
                              rows8(vf_s, KV_WIDTH))[:, :ls]
    ssd_s, ssm_s, conv_s = _ssd(rows8(xbc_s, CONV_DIM), rows8(dtr_s, LANES), rows8(z_s, SSD_WIDTH),
                                state_ssm[0], state_conv[0], ssd_prm, SUBLANES, ls)
    tail_rows = lambda a: jnp.pad(a, ((0, tail - ts), (0, 0)))
    h_all, h_slab = _mix(tail_rows(att_s.reshape(ts, ATT_WIDTH)), tail_rows(ssd_s[:, :ls].reshape(ts, SSD_WIDTH)),
                         tail_rows(xs2), wo_bf, ln1_g, ln1_b, tail, t_pad, row0=tp, into=h_bufs)

    y_p, y_s = _moe(h_all, h_slab, tp // COMBINE_TILE, w_router[0], router_bias[0], w_exp_gate[0], w_exp_up[0],
                    w_exp_down[0], w_sh_gate[0], w_sh_up[0], w_sh_down[0], ln2_g[0], ln2_b[0])

    kv5 = lambda a, b, l: a.reshape(1, b, l, N_KV_HEADS, HEAD_DIM)
    return (y_p.reshape(bp, lp, D_MODEL), y_s[:ts].reshape(bs, ls, D_MODEL),
            kv5(kf.reshape(bp, lp, KV_WIDTH)[:, lp - keep:], bp, keep),
            kv5(vf.reshape(bp, lp, KV_WIDTH)[:, lp - keep:], bp, keep),
            ssm_p[None], conv_p[None],
            kv5(kf_s, bs, ls), kv5(vf_s, bs, ls), ssm_s[None], conv_s[None])
```
